```python
import math
import jax, jax.numpy as jnp
from jax import lax
import numpy as np

D_MODEL = 1024
BATCH = 4
SEQ = 4096
DEPTH = 2
DEC_BATCH = 32
DEC_SEQ = 4
PAST_LEN = 8192
PAGE_SIZE = 128

N_META = 16
CHUNK = 128
M_HEADS = 4
M_DK = 128
M_DV = 128
F_HEADS = 8
F_DH = 64
R_HEADS = 4
R_DK = 128
R_DV = 128
ROPE_BASE = 10000.0
LRU_WIDTH = 512
LRU_BLOCKS = 8
LRU_BDIM = LRU_WIDTH // LRU_BLOCKS
LRU_C = 8.0
CONV_W = 4
D_FF = ((8 * D_MODEL // 3 + 255) // 256) * 256
EVEN_COLS = (M_HEADS * M_DK, M_HEADS * M_DK, M_HEADS * M_DV, M_HEADS * M_DV, M_HEADS, M_HEADS,
             F_HEADS * F_DH, F_HEADS * F_DH, F_HEADS * F_DH, F_HEADS)
ODD_COLS = (R_HEADS * R_DK, R_HEADS * R_DK, R_HEADS * R_DV, R_HEADS * R_DV, LRU_WIDTH, LRU_WIDTH)
MIX_EVEN = M_HEADS * M_DV + F_HEADS * F_DH
MIX_ODD = R_HEADS * R_DV + LRU_WIDTH
EPS = 1e-6

kernel_name = "hybrid_mlstm_fox_retnet_rglru_decode_step"


def _split(z, widths):
    idx = [int(i) for i in np.cumsum(widths)[:-1]]
    return jnp.split(z, idx, axis=-1)


def rmsnorm(x, g):
    xf = x.astype(jnp.float32)
    y = xf * lax.rsqrt(jnp.mean(xf * xf, axis=-1, keepdims=True) + EPS) * g.astype(jnp.float32)
    return y.astype(x.dtype)


def swiglu(x, wg, wu, wd):
    return (jax.nn.silu(x @ wg) * (x @ wu)) @ wd


def to_heads(t, n_heads):
    b, s, _ = t.shape
    return t.reshape(b, s, n_heads, -1).transpose(0, 2, 1, 3)


def from_heads(t):
    b, h, s, d = t.shape
    return t.transpose(0, 2, 1, 3).reshape(b, s, h * d)


def run_chunks(step, carry, seqs, lead):
    carry, out0 = step(carry, tuple(s[:, :, :lead] for s in seqs))
    rest = tuple(s[:, :, lead:] for s in seqs)
    n = rest[0].shape[2] // CHUNK
    blocks = tuple(jnp.moveaxis(s.reshape(s.shape[:2] + (n, CHUNK) + s.shape[3:]), 2, 0) for s in rest)
    carry, outs = lax.scan(step, carry, blocks)
    outs = jnp.moveaxis(outs, 0, 2)
    outs = outs.reshape(outs.shape[:2] + (n * CHUNK,) + outs.shape[4:])
    return carry, jnp.concatenate([out0, outs], axis=2)


def mlstm_chunk(carry, seqs):
    C, n, m = carry
    q, k, v, logi, logf = seqs
    L = q.shape[2]
    b = jnp.cumsum(logf, axis=-1)
    causal = jnp.tril(jnp.ones((L, L), dtype=bool))
    logD = jnp.where(causal, b[..., :, None] - b[..., None, :] + logi[..., None, :], -jnp.inf)
    m_inter = b + m[..., None]
    m_t = jnp.maximum(m_inter, jnp.max(logD, axis=-1))
    w_intra = jnp.einsum('bhtd,bhsd->bhts', q, k) * jnp.exp(logD - m_t[..., None])
    w_inter = jnp.exp(m_inter - m_t)
    num = jnp.einsum('bhts,bhsv->bhtv', w_intra, v) + w_inter[..., None] * jnp.einsum('bhtd,bhdv->bhtv', q, C)
    den = jnp.sum(w_intra, axis=-1) + w_inter * jnp.einsum('bhtd,bhd->bht', q, n)
    h = num / jnp.maximum(jnp.abs(den), jnp.exp(-m_t))[..., None]
    m_new = m_t[..., -1]
    w_src = jnp.exp(b[..., -1:] - b + logi - m_new[..., None])
    decay = jnp.exp(b[..., -1] + m - m_new)
    C_new = decay[..., None, None] * C + jnp.einsum('bhs,bhsd,bhsv->bhdv', w_src, k, v)
    n_new = decay[..., None] * n + jnp.einsum('bhs,bhsd->bhd', w_src, k)
    return (C_new, n_new, m_new), h


def _ret_log_gamma():
    return jnp.log1p(-jnp.exp2(-5.0 - jnp.arange(R_HEADS, dtype=jnp.float32)))


def retention_chunk(S, seqs):
    q, k, v = seqs
    L = q.shape[2]
    lg = _ret_log_gamma()
    t = jnp.arange(L, dtype=jnp.float32)
    diff = t[:, None] - t[None, :]
    D = jnp.exp(jnp.where(diff >= 0, diff[None] * lg[:, None, None], -jnp.inf))
    intra = jnp.einsum('bhts,bhsv->bhtv', jnp.einsum('bhtd,bhsd->bhts', q, k) * D, v)
    inter = jnp.einsum('bhtd,bhdv->bhtv', q, S) * jnp.exp((t + 1.0)[None, :] * lg[:, None])[None, :, :, None]
    w_src = jnp.exp((L - 1.0 - t)[None, :] * lg[:, None])
    S_new = jnp.exp(L * lg)[None, :, None, None] * S + jnp.einsum('bhsd,hs,bhsv->bhdv', k, w_src, v)
    return S_new, intra + inter


def rotary(x, pos):
    half = x.shape[-1] // 2
    freq = ROPE_BASE ** (-jnp.arange(half, dtype=jnp.float32) / half)
    ang = pos.astype(jnp.float32)[:, None] * freq[None, :]
    cos, sin = jnp.cos(ang), jnp.sin(ang)
    x1, x2 = x[..., :half], x[..., half:]
    return jnp.concatenate([x1 * cos - x2 * sin, x1 * sin + x2 * cos], axis=-1)


def fox_attend(q, Fq, tq, k, v, Fk, tk):
    s = jnp.einsum('bhqd,bhkd->bhqk', q, k) * (F_DH ** -0.5) + Fq[..., :, None] - Fk[..., None, :]
    s = jnp.where(tk[None, :] <= tq[:, None], s, -jnp.inf)
    return jnp.einsum('bhqk,bhkd->bhqd', jax.nn.softmax(s, axis=-1), v)


def fox_prompt(q, k, v, F):
    b, h, T, d = q.shape
    tk = jnp.arange(T)
    out0 = fox_attend(q[:, :, :N_META], F[:, :, :N_META], tk[:N_META], k, v, F, tk)
    n = (T - N_META) // CHUNK
    qb = jnp.moveaxis(q[:, :, N_META:].reshape(b, h, n, CHUNK, d), 2, 0)
    Fb = jnp.moveaxis(F[:, :, N_META:].reshape(b, h, n, CHUNK), 2, 0)
    tb = tk[N_META:].reshape(n, CHUNK)
    outs = lax.map(lambda a: fox_attend(a[0], a[1], a[2], k, v, F, tk), (qb, Fb, tb))
    outs = jnp.moveaxis(outs, 0, 2).reshape(b, h, n * CHUNK, d)
    return jnp.concatenate([out0, outs], axis=2)


def mixer_even(xn, w_in, b_mi, b_mf, b_ff, w_out, m_state, fox_past, is_prompt):
    f32 = jnp.float32
    B, T, _ = xn.shape
    mq, mk, mv, mo, mi, mf, fq, fk, fv, ff = _split(xn @ w_in, EVEN_COLS)
    q = to_heads(mq, M_HEADS).astype(f32)
    k = to_heads(mk, M_HEADS).astype(f32) * (M_DK ** -0.5)
    v = to_heads(mv, M_HEADS).astype(f32)
    logi = (mi.astype(f32) + b_mi.astype(f32)).transpose(0, 2, 1)
    logf = jax.nn.log_sigmoid(mf.astype(f32) + b_mf.astype(f32)).transpose(0, 2, 1)
    seqs = (q, k, v, logi, logf)
    if is_prompt:
        m_new, hm = run_chunks(mlstm_chunk, m_state, seqs, N_META)
    else:
        m_new, hm = mlstm_chunk(m_state, seqs)
    hm = jax.nn.sigmoid(mo.astype(f32)) * from_heads(hm)
    fk_rows = fk.reshape(B, T, F_HEADS, F_DH)
    fv_rows = fv.reshape(B, T, F_HEADS, F_DH)
    lf_rows = jax.nn.log_sigmoid(ff.astype(f32) + b_ff.astype(f32))
    qf = to_heads(fq, F_HEADS).astype(f32)
    if is_prompt:
        kf = fk_rows.astype(f32).transpose(0, 2, 1, 3)
        vf = fv_rows.astype(f32).transpose(0, 2, 1, 3)
        F = jnp.cumsum(lf_rows, axis=1).transpose(0, 2, 1)
        hf = fox_prompt(qf, kf, vf, F)
    else:
        cache_k, cache_v, cache_lf, page_table = fox_past
        past = page_table.shape[1] * PAGE_SIZE
        gather = lambda c: c[page_table].reshape((B, past) + c.shape[2:]).astype(f32)
        kf = jnp.concatenate([gather(cache_k), fk_rows.astype(f32)], axis=1).transpose(0, 2, 1, 3)
        vf = jnp.concatenate([gather(cache_v), fv_rows.astype(f32)], axis=1).transpose(0, 2, 1, 3)
        F = jnp.cumsum(jnp.concatenate([gather(cache_lf), lf_rows], axis=1), axis=1).transpose(0, 2, 1)
        tk = jnp.arange(past + T)
        hf = fox_attend(qf, F[:, :, past:], tk[past:], kf, vf, F, tk)
    out = jnp.concatenate([hm, from_heads(hf)], axis=-1).astype(xn.dtype) @ w_out
    return out, m_new, (fk_rows, fv_rows, lf_rows)


def _lin_combine(c1, c2):
    a1, b1 = c1
    a2, b2 = c2
    return a1 * a2, a2 * b1 + b2


def mixer_odd(xn, pos, w_in, ret_ln_g, conv_w, conv_b, lru_wa, lru_ba, lru_wx, lru_bx, lru_lambda, w_out,
              S, h0, conv_buf, is_prompt):
    f32 = jnp.float32
    B, T, _ = xn.shape
    rq, rk, rv, rg, lx, lgate = _split(xn @ w_in, ODD_COLS)
    q = rotary(to_heads(rq, R_HEADS).astype(f32), pos)
    k = rotary(to_heads(rk, R_HEADS).astype(f32), pos) * (R_DK ** -0.5)
    v = to_heads(rv, R_HEADS).astype(f32)
    if is_prompt:
        S_new, hr = run_chunks(retention_chunk, S, (q, k, v), N_META)
    else:
        S_new, hr = retention_chunk(S, (q, k, v))
    mu = jnp.mean(hr, axis=-1, keepdims=True)
    var = jnp.mean(jnp.square(hr - mu), axis=-1, keepdims=True)
    hr = (hr - mu) * lax.rsqrt(var + EPS) * ret_ln_g.astype(f32)[None, :, None, :]
    hr = jax.nn.silu(rg.astype(f32)) * from_heads(hr)
    u = jnp.concatenate([conv_buf.astype(f32), lx.astype(f32)], axis=1)
    xc = conv_b.astype(f32) + sum(conv_w[j].astype(f32) * u[:, j:j + T] for j in range(CONV_W))
    xb = xc.reshape(B, T, LRU_BLOCKS, LRU_BDIM)
    r = jax.nn.sigmoid(jnp.einsum('btgi,gij->btgj', xb, lru_wa.astype(f32)).reshape(B, T, LRU_WIDTH) + lru_ba.astype(f32))
    ig = jax.nn.sigmoid(jnp.einsum('btgi,gij->btgj', xb, lru_wx.astype(f32)).reshape(B, T, LRU_WIDTH) + lru_bx.astype(f32))
    log_a = -LRU_C * r * jax.nn.softplus(-lru_lambda.astype(f32))
    a = jnp.exp(log_a)
    bx = jnp.sqrt(-jnp.expm1(2.0 * log_a)) * (ig * xc)
    bx = bx.at[:, 0].add(a[:, 0] * h0)
    _, hs = lax.associative_scan(_lin_combine, (a, bx), axis=1)
    hl = hs * jax.nn.gelu(lgate.astype(f32))
    out = jnp.concatenate([hr, hl], axis=-1).astype(xn.dtype) @ w_out
    return out, S_new, hs[:, -1], u[:, -(CONV_W - 1):]


def setup_inputs(seed: int = 0) -> dict:
    key = jax.random.key(seed)
    ks = jax.random.split(key, 40)
    f32 = jnp.float32
    nrm = lambda i, shape, s=1.0: jax.random.normal(ks[i], shape, f32) * s
    n_pages = PAST_LEN // PAGE_SIZE
    used = DEC_BATCH * n_pages
    n_phys = used + max(1, used // 4)
    perm = jax.random.permutation(ks[0], n_phys)
    page_table = perm[:used].reshape(DEC_BATCH, n_pages).astype(jnp.int32)
    lam_u = jax.random.uniform(ks[30], (LRU_WIDTH,), f32, minval=0.9, maxval=0.999)
    lam_a = lam_u ** (1.0 / LRU_C)
    return {
        "x_prompt": nrm(1, (BATCH, SEQ, D_MODEL)),
        "x_sample": nrm(2, (DEC_BATCH, DEC_SEQ, D_MODEL)),
        "cache_fox_k": nrm(3, (n_phys, PAGE_SIZE, F_HEADS, F_DH)),
        "cache_fox_v": nrm(4, (n_phys, PAGE_SIZE, F_HEADS, F_DH)),
        "cache_fox_logf": jax.nn.log_sigmoid(4.0 + nrm(5, (n_phys, PAGE_SIZE, F_HEADS))),
        "state_mlstm_C": nrm(6, (DEC_BATCH, M_HEADS, M_DK, M_DV), 0.5),
        "state_mlstm_n": nrm(7, (DEC_BATCH, M_HEADS, M_DK), 0.5),
        "state_mlstm_m": nrm(8, (DEC_BATCH, M_HEADS)),
        "state_ret_S": nrm(9, (DEC_BATCH, R_HEADS, R_DK, R_DV), 0.5),
        "state_lru_h": nrm(10, (DEC_BATCH, LRU_WIDTH), 0.5),
        "state_lru_conv": nrm(11, (DEC_BATCH, CONV_W - 1, LRU_WIDTH)),
        "page_table": page_table,
        "meta_tokens": nrm(12, (N_META, D_MODEL)),
        "w_in_even": nrm(13, (D_MODEL, sum(EVEN_COLS)), D_MODEL ** -0.5),
        "b_mlstm_i": nrm(14, (M_HEADS,), 0.1),
        "b_mlstm_f": jnp.linspace(3.0, 6.0, M_HEADS, dtype=f32) + nrm(15, (M_HEADS,), 0.1),
        "b_fox_f": jnp.linspace(2.0, 6.0, F_HEADS, dtype=f32) + nrm(16, (F_HEADS,), 0.1),
        "w_out_even": nrm(17, (MIX_EVEN, D_MODEL), MIX_EVEN ** -0.5),
        "w_in_odd": nrm(18, (D_MODEL, sum(ODD_COLS)), D_MODEL ** -0.5),
        "ret_ln_g": 1.0 + nrm(19, (R_HEADS, R_DV), 0.02),
        "conv_w": nrm(20, (CONV_W, LRU_WIDTH), CONV_W ** -0.5),
        "conv_b": nrm(21, (LRU_WIDTH,), 0.02),
        "lru_wa": nrm(22, (LRU_BLOCKS, LRU_BDIM, LRU_BDIM), LRU_BDIM ** -0.5),
        "lru_ba": nrm(23, (LRU_WIDTH,), 0.02),
        "lru_wx": nrm(24, (LRU_BLOCKS, LRU_BDIM, LRU_BDIM), LRU_BDIM ** -0.5),
        "lru_bx": nrm(25, (LRU_WIDTH,), 0.02),
        "lru_lambda": jnp.log(lam_a) - jnp.log1p(-lam_a),
        "w_out_odd": nrm(26, (MIX_ODD, D_MODEL), MIX_ODD ** -0.5),
        "norm_g": 1.0 + nrm(27, (DEPTH, 4, D_MODEL), 0.02),
        "ffn_wg": nrm(28, (DEPTH, D_MODEL, D_FF), D_MODEL ** -0.5),
        "ffn_wu": nrm(29, (DEPTH, D_MODEL, D_FF), D_MODEL ** -0.5),
        "ffn_wd": nrm(31, (DEPTH, D_FF, D_MODEL), D_FF ** -0.5),
    }


def reference(x_prompt, x_sample, cache_fox_k, cache_fox_v, cache_fox_logf, state_mlstm_C, state_mlstm_n,
              state_mlstm_m, state_ret_S, state_lru_h, state_lru_conv, page_table, meta_tokens, w_in_even,
              b_mlstm_i, b_mlstm_f, b_fox_f, w_out_even, w_in_odd, ret_ln_g, conv_w, conv_b, lru_wa, lru_ba,
              lru_wx, lru_bx, lru_lambda, w_out_odd, norm_g, ffn_wg, ffn_wu, ffn_wd):
    f32 = jnp.float32

    def trunk(h, pos, m_state, fox_past, S, h0, cbuf, is_prompt):
        fox_rows = None
        for layer in range(DEPTH):
            xn = rmsnorm(h, norm_g[layer, 0])
            if layer % 2 == 0:
                mix, m_state, fox_rows = mixer_even(xn, w_in_even, b_mlstm_i, b_mlstm_f, b_fox_f, w_out_even,
                                                    m_state, fox_past, is_prompt)
            else:
                mix, S, h0, cbuf = mixer_odd(xn, pos, w_in_odd, ret_ln_g, conv_w, conv_b, lru_wa, lru_ba,
                                             lru_wx, lru_bx, lru_lambda, w_out_odd, S, h0, cbuf, is_prompt)
            h = h + rmsnorm(mix, norm_g[layer, 1])
            ff = swiglu(rmsnorm(h, norm_g[layer, 2]), ffn_wg[layer], ffn_wu[layer], ffn_wd[layer])
            h = h + rmsnorm(ff, norm_g[layer, 3])
        return h, m_state, fox_rows, S, h0, cbuf

    B = x_prompt.shape[0]
    T = x_prompt.shape[1] + N_META
    hp = jnp.concatenate([jnp.broadcast_to(meta_tokens.astype(x_prompt.dtype)[None], (B, N_META, D_MODEL)),
                          x_prompt], axis=1)
    m0 = (jnp.zeros((B, M_HEADS, M_DK, M_DV), f32), jnp.zeros((B, M_HEADS, M_DK), f32),
          jnp.zeros((B, M_HEADS), f32))
    hp, (pC, pn, pm), (pk, pv, plf), pS, ph, pcb = trunk(
        hp, jnp.arange(T), m0, None, jnp.zeros((B, R_HEADS, R_DK, R_DV), f32),
        jnp.zeros((B, LRU_WIDTH), f32), jnp.zeros((B, CONV_W - 1, LRU_WIDTH), f32), True)
    y_prompt = hp[:, N_META:]

    past = page_table.shape[1] * PAGE_SIZE
    pos_s = past + jnp.arange(x_sample.shape[1])
    ms = (state_mlstm_C.astype(f32), state_mlstm_n.astype(f32), state_mlstm_m.astype(f32))
    y_sample, (sC, sn, sm), (sk, sv, slf), sS, sh, scb = trunk(
        x_sample, pos_s, ms, (cache_fox_k, cache_fox_v, cache_fox_logf, page_table),
        state_ret_S.astype(f32), state_lru_h.astype(f32), state_lru_conv, False)

    return (y_prompt, y_sample, pk, pv, plf, sk, sv, slf, pC, pn, pm, sC, sn, sm, pS, sS, ph, sh, pcb, scb)
```

```python
import functools
import math

import jax
import jax.numpy as jnp
import numpy as np
from jax import lax
from jax.experimental import pallas as pl
from jax.experimental.pallas import tpu as pltpu

F32 = jnp.float32
BF16 = jnp.bfloat16

D_MODEL = 1024
BATCH = 4
SEQ = 4096
DEC_BATCH = 32
DEC_SEQ = 4
PAGE_SIZE = 128
N_META = 16
CHUNK = 128
M_HEADS = 4
F_HEADS = 8
F_DH = 64
R_HEADS = 4
HEAD_DIM = 128
ROPE_BASE = 10000.0
LRU_WIDTH = 512
LRU_C = 8.0
CONV_W = 4
D_FF = 2816
EPS = 1e-6
NEG = -1e30

LANES = 128
PADF = CHUNK - N_META
TP = PADF + N_META + SEQ
NCH = TP // CHUNK
SROWS = 16
SPAD = SROWS - DEC_SEQ
SBASE = BATCH * TP
NP = SBASE + DEC_BATCH * SROWS
TM = 512
FFN_TF = D_FF // 2
FOX_TQ = 512
FOX_CK = 256
DEC_PG = 8
PRE_PG = 64


def _bf(x):
    return x.astype(BF16)


def _dot(a, b):
    return jnp.dot(a, b, preferred_element_type=F32)


def _dot_nt(a, b):
    return lax.dot_general(a, b, (((1,), (1,)), ((), ())), preferred_element_type=F32)


def _dot_tn(a, b):
    return lax.dot_general(a, b, (((0,), (0,)), ((), ())), preferred_element_type=F32)


def _split3(x):
    hi = _bf(x)
    r1 = x - hi.astype(F32)
    mid = _bf(r1)
    lo = _bf(r1 - mid.astype(F32))
    return hi, mid, lo


def _dot01_right(x, m01):
    hi, mid, lo = _split3(x)
    return _dot(hi, m01) + _dot(mid, m01) + _dot(lo, m01)


def _dot01_left(m01, x):
    hi, mid, lo = _split3(x)
    return _dot(m01, hi) + _dot(m01, mid) + _dot(m01, lo)


def _iota(shape, dim):
    return lax.broadcasted_iota(jnp.int32, shape, dim)


def _rms(x, g):
    ms = jnp.mean(x * x, axis=-1, keepdims=True)
    return x * lax.rsqrt(ms + EPS) * g


def _softplus(x):
    return jnp.maximum(x, 0.0) + jnp.log1p(jnp.exp(-jnp.abs(x)))


def _log_sigmoid(x):
    return -_softplus(-x)


def _sigmoid(x):
    return 1.0 / (1.0 + jnp.exp(-x))


def _nm_kernel(x_ref, g_ref, w_ref, *o_refs, splits):
    xn = _bf(_rms(x_ref[...], g_ref[...]))
    off = 0
    for o_ref, n in zip(o_refs, splits):
        o_ref[...] = _dot(xn, w_ref[:, off:off + n])
        off += n


def norm_matmul(x, g, w, splits):
    n_rows, k = x.shape
    return pl.pallas_call(
        functools.partial(_nm_kernel, splits=splits),
        grid=(n_rows // TM,),
        in_specs=[pl.BlockSpec((TM, k), lambda i: (i, 0)),
                  pl.BlockSpec((1, k), lambda i: (0, 0)),
                  pl.BlockSpec((k, sum(splits)), lambda i: (0, 0))],
        out_specs=[pl.BlockSpec((TM, n), lambda i: (i, 0)) for n in splits],
        out_shape=[jax.ShapeDtypeStruct((n_rows, n), F32) for n in splits],
        compiler_params=pltpu.CompilerParams(dimension_semantics=("parallel",)),
        name="norm_matmul",
    )(x, g, w)


def _proj_kernel(a1_ref, a2_ref, w1_ref, w2_ref, g_ref, h_ref, o_ref):
    mix = _dot(_bf(a1_ref[...]), w1_ref[...]) + _dot(_bf(a2_ref[...]), w2_ref[...])
    o_ref[...] = h_ref[...] + _rms(mix, g_ref[...])


def proj_residual(a1, a2, w1, w2, g, h):
    n_rows, k1 = a1.shape
    k2 = a2.shape[1]
    d = h.shape[1]
    return pl.pallas_call(
        _proj_kernel,
        grid=(n_rows // TM,),
        in_specs=[pl.BlockSpec((TM, k1), lambda i: (i, 0)),
                  pl.BlockSpec((TM, k2), lambda i: (i, 0)),
                  pl.BlockSpec((k1, d), lambda i: (0, 0)),
                  pl.BlockSpec((k2, d), lambda i: (0, 0)),
                  pl.BlockSpec((1, d), lambda i: (0, 0)),
                  pl.BlockSpec((TM, d), lambda i: (i, 0))],
        out_specs=pl.BlockSpec((TM, d), lambda i: (i, 0)),
        out_shape=jax.ShapeDtypeStruct((n_rows, d), F32),
        compiler_params=pltpu.CompilerParams(dimension_semantics=("parallel",)),
        name="proj_residual",
    )(a1, a2, w1, w2, g, h)


def _ffn_kernel(h_ref, g2_ref, g3_ref, wg_ref, wu_ref, wd_ref, o_ref, xn_ref, acc_ref):
    f = pl.program_id(1)

    @pl.when(f == 0)
    def _():
        xn_ref[...] = _bf(_rms(h_ref[...], g2_ref[...]))
        acc_ref[...] = jnp.zeros(acc_ref.shape, F32)

    xn = xn_ref[...]
    gate = _dot(xn, wg_ref[...])
    up = _dot(xn, wu_ref[...])
    act = gate * _sigmoid(gate) * up
    acc_ref[...] += _dot(_bf(act), wd_ref[...])

    @pl.when(f == pl.num_programs(1) - 1)
    def _():
        o_ref[...] = h_ref[...] + _rms(acc_ref[...], g3_ref[...])


def ffn_residual(h, g2, g3, wg, wu, wd):
    n_rows, d = h.shape
    dff = wg.shape[1]
    return pl.pallas_call(
        _ffn_kernel,
        grid=(n_rows // TM, dff // FFN_TF),
        in_specs=[pl.BlockSpec((TM, d), lambda i, f: (i, 0)),
                  pl.BlockSpec((1, d), lambda i, f: (0, 0)),
                  pl.BlockSpec((1, d), lambda i, f: (0, 0)),
                  pl.BlockSpec((d, FFN_TF), lambda i, f: (0, f)),
                  pl.BlockSpec((d, FFN_TF), lambda i, f: (0, f)),
                  pl.BlockSpec((FFN_TF, d), lambda i, f: (f, 0))],
        out_specs=pl.BlockSpec((TM, d), lambda i, f: (i, 0)),
        out_shape=jax.ShapeDtypeStruct((n_rows, d), F32),
        scratch_shapes=[pltpu.VMEM((TM, d), BF16), pltpu.VMEM((TM, d), F32)],
        compiler_params=pltpu.CompilerParams(dimension_semantics=("parallel", "arbitrary")),
        name="ffn_residual",
    )(h, g2, g3, wg, wu, wd)


def _gates_kernel(z_ref, zt_ref, brow_ref, bcol_ref, o_ref, ot_ref):
    x = z_ref[...] + brow_ref[...]
    o_ref[...] = jnp.where(_iota(x.shape, 1) < M_HEADS, x, _log_sigmoid(x))
    xt = zt_ref[...] + bcol_ref[...]
    ot_ref[...] = jnp.where(_iota(xt.shape, 0) < M_HEADS, xt, _log_sigmoid(xt))


def gates_activate(zg, zgt, brow, bcol):
    n_rows = zg.shape[0]
    return pl.pallas_call(
        _gates_kernel,
        grid=(n_rows // TM,),
        in_specs=[pl.BlockSpec((TM, LANES), lambda i: (i, 0)),
                  pl.BlockSpec((16, TM), lambda i: (0, i)),
                  pl.BlockSpec((1, LANES), lambda i: (0, 0)),
                  pl.BlockSpec((16, 1), lambda i: (0, 0))],
        out_specs=[pl.BlockSpec((TM, LANES), lambda i: (i, 0)),
                   pl.BlockSpec((16, TM), lambda i: (0, i))],
        out_shape=[jax.ShapeDtypeStruct((n_rows, LANES), F32),
                   jax.ShapeDtypeStruct((16, n_rows), F32)],
        compiler_params=pltpu.CompilerParams(dimension_semantics=("parallel",)),
        name="gates_activate",
    )(zg, zgt, brow, bcol)


def _mlstm_chunk(q, k, v, logi_c, lf_c, logi_r, lf_r, c_state, n_state, m_state, row0, rows):
    ri = _iota((rows, rows), 0)
    ci = _iota((rows, rows), 1)
    valid_c = _iota((rows, 1), 0) >= row0
    valid_r = _iota((1, rows), 1) >= row0
    lf_c = jnp.where(valid_c, lf_c, 0.0)
    lf_r = jnp.where(valid_r, lf_r, 0.0)
    logi_c = jnp.where(valid_c, logi_c, NEG)
    logi_r = jnp.where(valid_r, logi_r, NEG)
    causal = ci <= ri
    b_c = jnp.sum(jnp.where(causal, lf_r, 0.0), axis=1, keepdims=True)
    b_r = jnp.sum(jnp.where(ri <= ci, lf_c, 0.0), axis=0, keepdims=True)
    log_d = jnp.where(causal, b_c - b_r + logi_r, -jnp.inf)
    m_inter = b_c + m_state
    m_t = jnp.maximum(m_inter, jnp.max(log_d, axis=1, keepdims=True))
    qb = _bf(q)
    kb = _bf(k)
    vb = _bf(v)
    w_intra = _dot_nt(qb, kb) * jnp.exp(log_d - m_t)
    w_inter = jnp.exp(m_inter - m_t)
    num = _dot(_bf(w_intra), vb) + w_inter * _dot(qb, _bf(c_state))
    den = jnp.sum(w_intra, axis=1, keepdims=True) + w_inter * jnp.sum(q * n_state, axis=1, keepdims=True)
    h = num / jnp.maximum(jnp.abs(den), jnp.exp(-m_t))
    m_new = m_t[rows - 1:rows, :]
    b_last = b_c[rows - 1:rows, :]
    w_src = jnp.exp(b_last - b_c + logi_c - m_new)
    decay = jnp.exp(b_last + m_state - m_new)
    ks = k * w_src
    c_new = decay * c_state + _dot_tn(_bf(ks), vb)
    n_new = decay * n_state + jnp.sum(ks, axis=0, keepdims=True)
    return h, c_new, n_new, m_new


def _mlstm_kernel(z_ref, g_ref, gt_ref, c0_ref, n0_ref, m0_ref, o_ref, c_ref, n_ref, m_ref, *, rows, row0_first):
    c = pl.program_id(1)

    @pl.when(c == 0)
    def _():
        c_ref[...] = c0_ref[...]
        n_ref[...] = n0_ref[...]
        m_ref[...] = m0_ref[...]

    row0 = jnp.where(c == 0, row0_first, 0)
    g = g_ref[...]
    gt = gt_ref[...]
    outs = []
    for h in range(M_HEADS):
        lo = h * HEAD_DIM
        q = z_ref[:, lo:lo + HEAD_DIM]
        k = z_ref[:, 512 + lo:512 + lo + HEAD_DIM] * (HEAD_DIM ** -0.5)
        v = z_ref[:, 1024 + lo:1024 + lo + HEAD_DIM]
        og = z_ref[:, 1536 + lo:1536 + lo + HEAD_DIM]
        hh, c_new, n_new, m_new = _mlstm_chunk(
            q, k, v,
            g[:, h:h + 1], g[:, M_HEADS + h:M_HEADS + h + 1],
            gt[h:h + 1, :], gt[M_HEADS + h:M_HEADS + h + 1, :],
            c_ref[0, h], n_ref[0, h:h + 1, :], m_ref[0, h:h + 1, 0:1], row0, rows)
        c_ref[0, h] = c_new
        n_ref[0, h:h + 1, :] = n_new
        m_ref[0, h:h + 1, :] = jnp.broadcast_to(m_new, (1, LANES))
        outs.append(_sigmoid(og) * hh)
    o_ref[...] = jnp.concatenate(outs, axis=1)


def _state_specs(heads):
    return [pl.BlockSpec((1, heads, HEAD_DIM, HEAD_DIM), lambda b, c: (b, 0, 0, 0)),
            pl.BlockSpec((1, heads, HEAD_DIM), lambda b, c: (b, 0, 0)),
            pl.BlockSpec((1, heads, LANES), lambda b, c: (b, 0, 0))]


def mlstm_call(z, g2, g2t, c0, n0, m0, prev, *, rows, n_chunks, row0_first, block0):
    nb = c0.shape[0]
    blk = lambda b, c: (block0 + b * n_chunks + c, 0)
    in_specs = [pl.BlockSpec((rows, 2048), blk),
                pl.BlockSpec((rows, LANES), blk),
                pl.BlockSpec((None, 16, rows), lambda b, c: (b * n_chunks + c, 0, 0))] + _state_specs(M_HEADS)
    args = [z, g2, g2t, c0, n0, m0]
    aliases = {}
    if prev is not None:
        in_specs.append(pl.BlockSpec(memory_space=pl.ANY))
        args.append(prev)
        aliases = {6: 0}
    kern = functools.partial(_mlstm_kernel, rows=rows, row0_first=row0_first)
    if prev is not None:
        kern = functools.partial(_drop_arg, kern, 6)
    return pl.pallas_call(
        kern,
        grid=(nb, n_chunks),
        in_specs=in_specs,
        out_specs=[pl.BlockSpec((rows, 512), blk)] + _state_specs(M_HEADS),
        out_shape=[jax.ShapeDtypeStruct((NP, 512), F32),
                   jax.ShapeDtypeStruct((nb, M_HEADS, HEAD_DIM, HEAD_DIM), F32),
                   jax.ShapeDtypeStruct((nb, M_HEADS, HEAD_DIM), F32),
                   jax.ShapeDtypeStruct((nb, M_HEADS, LANES), F32)],
        input_output_aliases=aliases,
        compiler_params=pltpu.CompilerParams(dimension_semantics=("parallel", "arbitrary")),
        name="mlstm",
    )(*args)


def _drop_arg(kern, idx, *refs):
    return kern(*refs[:idx], *refs[idx + 1:])


def _ret_log_gamma(h):
    return math.log1p(-(2.0 ** (-5.0 - h)))


def _ret_kernel(z_ref, cos_ref, sin_ref, lng_ref, s0_ref, o_ref, s_ref, *, rows, row0_first):
    c = pl.program_id(1)

    @pl.when(c == 0)
    def _():
        s_ref[...] = s0_ref[...]

    row0 = jnp.where(c == 0, row0_first, 0)
    n_valid = (rows - row0).astype(F32)
    cosf = cos_ref[...]
    sinf = sin_ref[...]
    ri = _iota((rows, rows), 0)
    ci = _iota((rows, rows), 1)
    diff = (ri - ci).astype(F32)
    rowi = _iota((rows, 1), 0)
    te = (rowi - row0).astype(F32)
    valid = rowi >= row0
    outs = []
    for h in range(R_HEADS):
        lg = _ret_log_gamma(h)
        lo = h * HEAD_DIM
        q = z_ref[:, lo:lo + HEAD_DIM]
        k = z_ref[:, 512 + lo:512 + lo + HEAD_DIM]
        v = jnp.where(valid, z_ref[:, 1024 + lo:1024 + lo + HEAD_DIM], 0.0)
        rg = z_ref[:, 1536 + lo:1536 + lo + HEAD_DIM]
        q = q * cosf + pltpu.roll(q, HEAD_DIM // 2, 1) * sinf
        k = (k * cosf + pltpu.roll(k, HEAD_DIM // 2, 1) * sinf) * (HEAD_DIM ** -0.5)
        s_state = s_ref[0, h]
        qb = _bf(q)
        vb = _bf(v)
        decay_m = jnp.where(diff >= 0, jnp.exp(diff * lg), 0.0)
        intra = _dot(_bf(_dot_nt(qb, _bf(k)) * decay_m), vb)
        inter = _dot(qb, _bf(s_state)) * jnp.exp((te + 1.0) * lg)
        w_src = jnp.exp((n_valid - 1.0 - te) * lg)
        s_ref[0, h] = jnp.exp(n_valid * lg) * s_state + _dot_tn(_bf(k * w_src), vb)
        hr = intra + inter
        mu = jnp.mean(hr, axis=-1, keepdims=True)
        var = jnp.mean(jnp.square(hr - mu), axis=-1, keepdims=True)
        hr = (hr - mu) * lax.rsqrt(var + EPS) * lng_ref[h:h + 1, :]
        outs.append(rg * _sigmoid(rg) * hr)
    o_ref[...] = jnp.concatenate(outs, axis=1)


def retention_call(z, cosf, sinf, lng, s0, prev, *, rows, n_chunks, row0_first, block0):
    nb = s0.shape[0]
    blk = lambda b, c: (block0 + b * n_chunks + c, 0)
    sspec = pl.BlockSpec((1, R_HEADS, HEAD_DIM, HEAD_DIM), lambda b, c: (b, 0, 0, 0))
    in_specs = [pl.BlockSpec((rows, 2048), blk),
                pl.BlockSpec((rows, HEAD_DIM), lambda b, c: (c, 0)),
                pl.BlockSpec((rows, HEAD_DIM), lambda b, c: (c, 0)),
                pl.BlockSpec((R_HEADS, HEAD_DIM), lambda b, c: (0, 0)),
                sspec]
    args = [z, cosf, sinf, lng, s0]
    aliases = {}
    kern = functools.partial(_ret_kernel, rows=rows, row0_first=row0_first)
    if prev is not None:
        in_specs.append(pl.BlockSpec(memory_space=pl.ANY))
        args.append(prev)
        aliases = {5: 0}
        kern = functools.partial(_drop_arg, kern, 5)
    return pl.pallas_call(
        kern,
        grid=(nb, n_chunks),
        in_specs=in_specs,
        out_specs=[pl.BlockSpec((rows, 512), blk), sspec],
        out_shape=[jax.ShapeDtypeStruct((NP, 512), F32),
                   jax.ShapeDtypeStruct((nb, R_HEADS, HEAD_DIM, HEAD_DIM), F32)],
        input_output_aliases=aliases,
        compiler_params=pltpu.CompilerParams(dimension_semantics=("parallel", "arbitrary")),
        name="retention",
    )(*args)


def _lru_kernel(z_ref, pre_ref, h0_ref, cw_ref, cb_ref, wa_ref, ba_ref, wx_ref, bx_ref, lam_ref,
                o_ref, hlast_ref, tail_ref, *, rows, row0_first):
    c = pl.program_id(1)

    @pl.when(c == 0)
    def _():
        hlast_ref[...] = h0_ref[...]
        tail_ref[...] = jnp.zeros(tail_ref.shape, F32)

    row0 = jnp.where(c == 0, row0_first, 0)
    rowi = _iota((rows, 1), 0)
    valid = rowi >= row0
    lx = jnp.where(valid, z_ref[:, 0:LRU_WIDTH], pre_ref[...])
    gate = z_ref[:, LRU_WIDTH:2 * LRU_WIDTH]
    tail = tail_ref[0]
    row8 = _iota((8, 1), 0)
    xc = cb_ref[...] + cw_ref[CONV_W - 1:CONV_W, :] * lx
    for j in range(1, CONV_W):
        rolled = pltpu.roll(lx, j, 0)
        first = jnp.where(row8 < j, pltpu.roll(tail, j, 0), rolled[0:8])
        shifted = jnp.concatenate([first, rolled[8:]], axis=0)
        xc = xc + cw_ref[CONV_W - 1 - j:CONV_W - j, :] * shifted
    tail_ref[0] = lx[rows - 8:rows]
    xcb = _bf(xc)
    pre_a = []
    pre_x = []
    for p in range(LRU_WIDTH // LANES):
        xs = xcb[:, p * LANES:(p + 1) * LANES]
        pre_a.append(_dot(xs, wa_ref[p]))
        pre_x.append(_dot(xs, wx_ref[p]))
    r = _sigmoid(jnp.concatenate(pre_a, axis=1) + ba_ref[...])
    ig = _sigmoid(jnp.concatenate(pre_x, axis=1) + bx_ref[...])
    log_a = -LRU_C * r * _softplus(-lam_ref[...])
    a = jnp.where(valid, jnp.exp(log_a), 1.0)
    one_minus_a2 = -jnp.tanh(log_a) * (jnp.exp(2.0 * log_a) + 1.0)
    bx = jnp.where(valid, jnp.sqrt(one_minus_a2) * (ig * xc), 0.0)
    shift = 1
    while shift < rows:
        keep = rowi >= shift
        a_sh = jnp.where(keep, pltpu.roll(a, shift, 0), 1.0)
        b_sh = jnp.where(keep, pltpu.roll(bx, shift, 0), 0.0)
        bx = a * b_sh + bx
        a = a * a_sh
        shift *= 2
    hs = a * hlast_ref[0] + bx
    hlast_ref[0] = hs[rows - 1:rows, :]
    gl = 0.5 * gate * (1.0 + jnp.tanh(math.sqrt(2.0 / math.pi) * (gate + 0.044715 * gate * gate * gate)))
    o_ref[...] = hs * gl


def lru_call(z, pre, h0, cw, cb, wa2, ba, wx2, bx, lam, prev, *, rows, n_chunks, row0_first, block0):
    nb = h0.shape[0]
    blk = lambda b, c: (block0 + b * n_chunks + c, 0)
    blk_local = lambda b, c: (b * n_chunks + c, 0)
    const2 = lambda b, c: (0, 0)
    in_specs = [pl.BlockSpec((rows, 1024), blk),
                pl.BlockSpec((rows, LRU_WIDTH), const2 if pre.shape[0] == rows else blk_local),
                pl.BlockSpec((1, 1, LRU_WIDTH), lambda b, c: (b, 0, 0)),
                pl.BlockSpec((CONV_W, LRU_WIDTH), const2),
                pl.BlockSpec((1, LRU_WIDTH), const2),
                pl.BlockSpec((LRU_WIDTH // LANES, LANES, LANES), lambda b, c: (0, 0, 0)),
                pl.BlockSpec((1, LRU_WIDTH), const2),
                pl.BlockSpec((LRU_WIDTH // LANES, LANES, LANES), lambda b, c: (0, 0, 0)),
                pl.BlockSpec((1, LRU_WIDTH), const2),
                pl.BlockSpec((1, LRU_WIDTH), const2)]
    args = [z, pre, h0, cw, cb, wa2, ba, wx2, bx, lam]
    aliases = {}
    kern = functools.partial(_lru_kernel, rows=rows, row0_first=row0_first)
    if prev is not None:
        in_specs.append(pl.BlockSpec(memory_space=pl.ANY))
        args.append(prev)
        aliases = {10: 0}
        kern = functools.partial(_drop_arg, kern, 10)
    return pl.pallas_call(
        kern,
        grid=(nb, n_chunks),
        in_specs=in_specs,
        out_specs=[pl.BlockSpec((rows, LRU_WIDTH), blk),
                   pl.BlockSpec((1, 1, LRU_WIDTH), lambda b, c: (b, 0, 0)),
                   pl.BlockSpec((1, 8, LRU_WIDTH), lambda b, c: (b, 0, 0))],
        out_shape=[jax.ShapeDtypeStruct((NP, LRU_WIDTH), F32),
                   jax.ShapeDtypeStruct((nb, 1, LRU_WIDTH), F32),
                   jax.ShapeDtypeStruct((nb, 8, LRU_WIDTH), F32)],
        input_output_aliases=aliases,
        compiler_params=pltpu.CompilerParams(dimension_semantics=("parallel", "arbitrary")),
        name="rglru",
    )(*args)


def _fox_cumsum_kernel(g_ref, gt_ref, fc_ref, fr_ref):
    ri = _iota((CHUNK, CHUNK), 0)
    ci = _iota((CHUNK, CHUNK), 1)
    lower = jnp.where(ci <= ri, 1.0, 0.0).astype(BF16)
    upper = jnp.where(ri <= ci, 1.0, 0.0).astype(BF16)
    carry_r = jnp.zeros((1, LANES), F32)
    carry_c = jnp.zeros((16, 1), F32)
    for blk in range(NCH):
        x = g_ref[blk * CHUNK:(blk + 1) * CHUNK, :]
        xt = gt_ref[:, blk * CHUNK:(blk + 1) * CHUNK]
        if blk == 0:
            x = jnp.where(_iota((CHUNK, 1), 0) >= PADF, x, 0.0)
            xt = jnp.where(_iota((1, CHUNK), 1) >= PADF, xt, 0.0)
        cs = _dot01_left(lower, x) + carry_r
        cst = _dot01_right(xt, upper) + carry_c
        carry_r = cs[CHUNK - 1:CHUNK, :]
        carry_c = cst[:, CHUNK - 1:CHUNK]
        if blk == 0:
            cst = jnp.where(_iota((1, CHUNK), 1) >= PADF, cst, -NEG)
        fc_ref[0, blk * CHUNK:(blk + 1) * CHUNK, :] = cs
        fr_ref[0, :, blk * CHUNK:(blk + 1) * CHUNK] = cst


def fox_cumsum(g2, g2t):
    return pl.pallas_call(
        _fox_cumsum_kernel,
        grid=(BATCH,),
        in_specs=[pl.BlockSpec((TP, LANES), lambda b: (b, 0)),
                  pl.BlockSpec((16, TP), lambda b: (0, b))],
        out_specs=[pl.BlockSpec((1, TP, LANES), lambda b: (b, 0, 0)),
                   pl.BlockSpec((1, 16, TP), lambda b: (b, 0, 0))],
        out_shape=[jax.ShapeDtypeStruct((BATCH, TP, LANES), F32),
                   jax.ShapeDtypeStruct((BATCH, 16, TP), F32)],
        compiler_params=pltpu.CompilerParams(dimension_semantics=("parallel",)),
        name="fox_cumsum",
    )(g2, g2t)


def _fox_prompt_kernel(q_ref, k_ref, v_ref, fr_ref, fc_ref, o_ref, q_s, ka_s, kb_s, v_s, m_s, l_s, acc_s):
    i = pl.program_id(2)
    is_a = _iota((1, LANES), 1) < F_DH

    @pl.when(i == 0)
    def _():
        k = k_ref[...]
        ka_s[...] = _bf(jnp.where(is_a, k, 0.0))
        kb_s[...] = _bf(jnp.where(is_a, 0.0, k))
        v_s[...] = _bf(v_ref[...])
        q_s[...] = _bf(q_ref[...] * (F_DH ** -0.5))

    def init():
        m_s[...] = jnp.full(m_s.shape, NEG, F32)
        l_s[...] = jnp.zeros(l_s.shape, F32)
        acc_s[...] = jnp.zeros(acc_s.shape, F32)

    def chunk(q0, tq, k0, ck, causal):
        qb = q_s[pl.ds(q0, tq), :]
        vb = v_s[pl.ds(k0, ck), :]
        kblk = k0 // CHUNK
        for hh, k_s in enumerate((ka_s, kb_s)):
            s = _dot_nt(qb, k_s[pl.ds(k0, ck), :])
            fk = jnp.concatenate([fr_ref[0, 0, kblk + t, hh:hh + 1, :] for t in range(ck // CHUNK)], axis=1)
            u = s - fk
            if causal:
                u = jnp.where(k0 + _iota((tq, ck), 1) <= q0 + _iota((tq, ck), 0), u, NEG)
            fq = fc_ref[0, 0, pl.ds(q0, tq), hh:hh + 1]
            m_old = m_s[hh, 0:tq, :]
            m_new = jnp.maximum(m_old, jnp.max(u, axis=1, keepdims=True) + fq)
            alpha = jnp.exp(m_old - m_new)
            p = jnp.exp(u + (fq - m_new))
            l_s[hh, 0:tq, :] = alpha * l_s[hh, 0:tq, :] + jnp.sum(p, axis=1, keepdims=True)
            acc_s[hh, 0:tq, :] = alpha * acc_s[hh, 0:tq, :] + _dot(_bf(p), vb)
            m_s[hh, 0:tq, :] = m_new

    def finalize(q0, tq):
        oa = acc_s[0, 0:tq, :] / l_s[0, 0:tq, :]
        ob = acc_s[1, 0:tq, :] / l_s[1, 0:tq, :]
        o_ref[pl.ds(q0, tq), :] = jnp.where(is_a, oa, ob)

    @pl.when(i == 0)
    def _():
        init()
        chunk(0, CHUNK, 0, CHUNK, True)
        finalize(0, CHUNK)

    q0 = pl.multiple_of(CHUNK + i * FOX_TQ, CHUNK)
    init()
    chunk(q0, FOX_TQ, 0, CHUNK, False)

    def body(j, carry):
        chunk(q0, FOX_TQ, pl.multiple_of(CHUNK + j * FOX_CK, CHUNK), FOX_CK, False)
        return carry

    lax.fori_loop(0, (FOX_TQ // FOX_CK) * i, body, 0)
    for d in range(FOX_TQ // FOX_CK):
        chunk(q0, FOX_TQ, pl.multiple_of(q0 + d * FOX_CK, CHUNK), FOX_CK, True)
    finalize(q0, FOX_TQ)


def fox_prompt(zf, fr, fc):
    nq = (TP - CHUNK) // FOX_TQ
    pairs = F_HEADS // 2
    return pl.pallas_call(
        _fox_prompt_kernel,
        grid=(BATCH, pairs, nq),
        in_specs=[pl.BlockSpec((TP, LANES), lambda b, p, i: (b, p)),
                  pl.BlockSpec((TP, LANES), lambda b, p, i: (b, pairs + p)),
                  pl.BlockSpec((TP, LANES), lambda b, p, i: (b, 2 * pairs + p)),
                  pl.BlockSpec((1, 1, NCH, 2, LANES), lambda b, p, i: (b, p, 0, 0, 0)),
                  pl.BlockSpec((1, 1, TP, 2), lambda b, p, i: (b, p, 0, 0))],
        out_specs=pl.BlockSpec((TP, LANES), lambda b, p, i: (b, p)),
        out_shape=jax.ShapeDtypeStruct((NP, 512), F32),
        scratch_shapes=[pltpu.VMEM((TP, LANES), BF16)] * 4
        + [pltpu.VMEM((2, FOX_TQ, 1), F32), pltpu.VMEM((2, FOX_TQ, 1), F32),
           pltpu.VMEM((2, FOX_TQ, LANES), F32)],
        compiler_params=pltpu.CompilerParams(dimension_semantics=("parallel", "parallel", "arbitrary")),
        name="fox_prompt",
    )(zf, zf, zf, fr, fc)


def _lf_suffix_kernel(x_ref, suf_ref, tot_ref):
    ri = _iota((PAGE_SIZE, PAGE_SIZE), 0)
    ci = _iota((PAGE_SIZE, PAGE_SIZE), 1)
    after = jnp.where(ri > ci, 1.0, 0.0).astype(BF16)
    ones = jnp.ones((PAGE_SIZE, PAGE_SIZE), BF16)
    x = x_ref[...].reshape(PRE_PG * F_HEADS, PAGE_SIZE)
    suf_ref[...] = _dot01_right(x, after).reshape(PRE_PG, F_HEADS, PAGE_SIZE)
    tot_ref[...] = _dot01_right(x, ones).reshape(PRE_PG, F_HEADS, PAGE_SIZE)


def lf_suffix(lft):
    n_phys = lft.shape[0]
    spec = pl.BlockSpec((PRE_PG, F_HEADS, PAGE_SIZE), lambda i: (i, 0, 0))
    return pl.pallas_call(
        _lf_suffix_kernel,
        grid=(n_phys // PRE_PG,),
        in_specs=[spec],
        out_specs=[spec, spec],
        out_shape=[jax.ShapeDtypeStruct(lft.shape, F32)] * 2,
        compiler_params=pltpu.CompilerParams(dimension_semantics=("parallel",)),
        name="lf_suffix",
    )(lft)


def _fox_decode_kernel(pt_ref, zq_ref, zk_ref, zv_ref, gt_ref, *rest):
    k_refs = rest[0:DEC_PG]
    v_refs = rest[DEC_PG:2 * DEC_PG]
    suf_refs = rest[2 * DEC_PG:3 * DEC_PG]
    tot_refs = rest[3 * DEC_PG:4 * DEC_PG]
    o_ref = rest[4 * DEC_PG + 1]
    q2_s, m_s, l_s, acc_s, run_s, nc_s = rest[4 * DEC_PG + 2:]
    j = pl.program_id(1)
    n_rows = DEC_SEQ * F_HEADS
    width = F_HEADS * F_DH

    def update(s, bias, vb):
        u = s + bias
        m_old = m_s[...]
        m_new = jnp.maximum(m_old, jnp.max(u, axis=1, keepdims=True))
        alpha = jnp.exp(m_old - m_new)
        p = jnp.exp(u - m_new)
        l_s[...] = alpha * l_s[...] + jnp.sum(p, axis=1, keepdims=True)
        acc_s[...] = alpha * acc_s[...] + _dot(_bf(p), vb)
        m_s[...] = m_new

    @pl.when(j == 0)
    def _():
        m_s[...] = jnp.full(m_s.shape, NEG, F32)
        l_s[...] = jnp.zeros(l_s.shape, F32)
        acc_s[...] = jnp.zeros(acc_s.shape, F32)
        run_s[...] = jnp.zeros(run_s.shape, F32)
        head_mask = _iota((F_HEADS, width), 1) // F_DH == _iota((F_HEADS, width), 0)
        q16 = zq_ref[...] * (F_DH ** -0.5)
        q2 = jnp.concatenate(
            [jnp.where(head_mask, q16[SPAD + t:SPAD + t + 1, :], 0.0) for t in range(DEC_SEQ)], axis=0)
        q2_s[...] = _bf(q2)
        lf_new = gt_ref[F_HEADS:2 * F_HEADS, :]
        lane = _iota((1, SROWS), 1)
        cum = jnp.zeros((F_HEADS, SROWS), F32)
        for t in range(DEC_SEQ):
            cum = cum + jnp.where(lane >= SPAD + t, lf_new[:, SPAD + t:SPAD + t + 1], 0.0)
        nc = jnp.concatenate([cum[:, SPAD + t:SPAD + t + 1] for t in range(DEC_SEQ)], axis=0)
        nc_s[...] = nc
        cum4 = jnp.concatenate([cum] * DEC_SEQ, axis=0)
        qi = _iota((n_rows, SROWS), 0) // F_HEADS
        kj = _iota((n_rows, SROWS), 1) - SPAD
        bias = jnp.where((kj >= 0) & (kj <= qi), nc - cum4, NEG)
        s = _dot_nt(_bf(q2), _bf(zk_ref[...]))
        update(s, bias, _bf(zv_ref[...]))

    q2b = q2_s[...]
    nc = nc_s[...]
    for i in range(DEC_PG):
        s = _dot_nt(q2b, _bf(k_refs[i][0]))
        later = run_s[...] + suf_refs[i][0]
        bias = jnp.concatenate([later] * DEC_SEQ, axis=0) + nc
        update(s, bias, _bf(v_refs[i][0]))
        run_s[...] = run_s[...] + tot_refs[i][0]

    @pl.when(j == pl.num_programs(1) - 1)
    def _():
        head_mask = (_iota((n_rows, width), 1) // F_DH) == (_iota((n_rows, width), 0) % F_HEADS)
        o2 = jnp.where(head_mask, acc_s[...] / l_s[...], 0.0)
        o_ref[...] = jnp.zeros(o_ref.shape, F32)
        for t in range(DEC_SEQ):
            o_ref[SPAD + t:SPAD + t + 1, :] = jnp.sum(o2[t * F_HEADS:(t + 1) * F_HEADS, :], axis=0, keepdims=True)


def fox_decode(page_table, zf, g2t_s, cache_k, cache_v, suf, tot, prev):
    n_pages = page_table.shape[1]
    width = F_HEADS * F_DH
    sblk = SBASE // SROWS

    def page_map(i):
        return lambda b, j, pt: (pt[b, n_pages - 1 - (j * DEC_PG + i)], 0, 0)

    in_specs = [pl.BlockSpec((SROWS, width), lambda b, j, pt: (sblk + b, 0)),
                pl.BlockSpec((SROWS, width), lambda b, j, pt: (sblk + b, 1)),
                pl.BlockSpec((SROWS, width), lambda b, j, pt: (sblk + b, 2)),
                pl.BlockSpec((None, 16, SROWS), lambda b, j, pt: (b, 0, 0))]
    in_specs += [pl.BlockSpec((1, PAGE_SIZE, width), page_map(i)) for i in range(DEC_PG)]
    in_specs += [pl.BlockSpec((1, PAGE_SIZE, width), page_map(i)) for i in range(DEC_PG)]
    in_specs += [pl.BlockSpec((1, F_HEADS, PAGE_SIZE), page_map(i)) for i in range(DEC_PG)]
    in_specs += [pl.BlockSpec((1, F_HEADS, PAGE_SIZE), page_map(i)) for i in range(DEC_PG)]
    in_specs += [pl.BlockSpec(memory_space=pl.ANY)]
    args = [zf, zf, zf, g2t_s] + [cache_k] * DEC_PG + [cache_v] * DEC_PG + [suf] * DEC_PG + [tot] * DEC_PG + [prev]
    n_rows = DEC_SEQ * F_HEADS
    grid_spec = pltpu.PrefetchScalarGridSpec(
        num_scalar_prefetch=1,
        grid=(DEC_BATCH, n_pages // DEC_PG),
        in_specs=in_specs,
        out_specs=pl.BlockSpec((SROWS, width), lambda b, j, pt: (sblk + b, 0)),
        scratch_shapes=[pltpu.VMEM((n_rows, width), BF16),
                        pltpu.VMEM((n_rows, 1), F32),
                        pltpu.VMEM((n_rows, 1), F32),
                        pltpu.VMEM((n_rows, width), F32),
                        pltpu.VMEM((F_HEADS, PAGE_SIZE), F32),
                        pltpu.VMEM((n_rows, 1), F32)])
    return pl.pallas_call(
        _fox_decode_kernel,
        grid_spec=grid_spec,
        out_shape=jax.ShapeDtypeStruct((NP, width), F32),
        input_output_aliases={len(args): 0},
        compiler_params=pltpu.CompilerParams(dimension_semantics=("parallel", "arbitrary")),
        name="fox_decode",
    )(page_table, *args)


def _block_diag_pairs(w):
    z = jnp.zeros((LANES // 2, LANES // 2), w.dtype)
    pairs = [jnp.block([[w[2 * p], z], [z, w[2 * p + 1]]]) for p in range(w.shape[0] // 2)]
    return _bf(jnp.stack(pairs))


def _rope_tables(pos):
    half = HEAD_DIM // 2
    freq = ROPE_BASE ** (-jnp.arange(half, dtype=F32) / half)
    ang = pos.astype(F32)[:, None] * freq[None, :]
    cos, sin = jnp.cos(ang), jnp.sin(ang)
    return jnp.concatenate([cos, cos], axis=1), jnp.concatenate([-sin, sin], axis=1)


def _unpad_prompt(x):
    return x[:SBASE].reshape((BATCH, TP) + x.shape[1:])[:, PADF:]


def _unpad_sample(x):
    return x[SBASE:].reshape((DEC_BATCH, SROWS) + x.shape[1:])[:, SPAD:]


def kernel(x_prompt, x_sample, cache_fox_k, cache_fox_v, cache_fox_logf, state_mlstm_C, state_mlstm_n, state_mlstm_m, state_ret_S, state_lru_h, state_lru_conv, page_table, meta_tokens, w_in_even, b_mlstm_i, b_mlstm_f, b_fox_f, w_out_even, w_in_odd, ret_ln_g, conv_w, conv_b, lru_wa, lru_ba, lru_wx, lru_bx, lru_lambda, w_out_odd, norm_g, ffn_wg, ffn_wu, ffn_wd):
    n_pages = page_table.shape[1]
    past = n_pages * PAGE_SIZE
    n_phys = cache_fox_k.shape[0]

    hp = jnp.concatenate([jnp.zeros((BATCH, PADF, D_MODEL), F32),
                          jnp.broadcast_to(meta_tokens.astype(F32)[None], (BATCH, N_META, D_MODEL)),
                          x_prompt.astype(F32)], axis=1).reshape(SBASE, D_MODEL)
    hs = jnp.concatenate([jnp.zeros((DEC_BATCH, SPAD, D_MODEL), F32), x_sample.astype(F32)],
                         axis=1).reshape(DEC_BATCH * SROWS, D_MODEL)
    h = jnp.concatenate([hp, hs], axis=0)

    c = np.cumsum([0, 512, 512, 512, 512, 4, 4, 512, 512, 512, 8])
    gate_w = jnp.concatenate([w_in_even[:, c[4]:c[6]], w_in_even[:, c[9]:c[10]],
                              jnp.zeros((D_MODEL, LANES - 16), F32)], axis=1)
    w_even = _bf(jnp.concatenate([w_in_even[:, c[0]:c[4]], w_in_even[:, c[6]:c[9]], gate_w], axis=1))
    gate_b = jnp.concatenate([b_mlstm_i, b_mlstm_f, b_fox_f]).astype(F32)
    brow = jnp.concatenate([gate_b, jnp.zeros((LANES - 16,), F32)]).reshape(1, LANES)
    bcol = gate_b.reshape(16, 1)
    w_odd = _bf(w_in_odd)
    wo_even = _bf(w_out_even)
    wo_odd = _bf(w_out_odd)
    wg = _bf(ffn_wg)
    wu = _bf(ffn_wu)
    wd = _bf(ffn_wd)
    ng = norm_g.astype(F32).reshape(2, 4, 1, D_MODEL)

    zm, zf, zg = norm_matmul(h, ng[0, 0], w_even, (2048, 1536, LANES))
    g2, g2t = gates_activate(zg, zg[:, :16].T, brow, bcol)
    g2t_p = g2t[:, :SBASE].reshape(16, BATCH * NCH, CHUNK).transpose(1, 0, 2)
    g2t_s = g2t[:, SBASE:].reshape(16, DEC_BATCH, SROWS).transpose(1, 0, 2)

    zeros_c = jnp.zeros((BATCH, M_HEADS, HEAD_DIM, HEAD_DIM), F32)
    zeros_n = jnp.zeros((BATCH, M_HEADS, HEAD_DIM), F32)
    hm, p_c, p_n, p_m = mlstm_call(zm, g2, g2t_p, zeros_c, zeros_n, zeros_n, None,
                                   rows=CHUNK, n_chunks=NCH, row0_first=PADF, block0=0)
    m0_s = jnp.broadcast_to(state_mlstm_m.astype(F32)[:, :, None], (DEC_BATCH, M_HEADS, LANES))
    hm, s_c, s_n, s_m = mlstm_call(zm, g2, g2t_s, state_mlstm_C.astype(F32), state_mlstm_n.astype(F32), m0_s, hm,
                                   rows=SROWS, n_chunks=1, row0_first=SPAD, block0=SBASE // SROWS)

    fc, fr = fox_cumsum(g2, g2t)
    pairs = F_HEADS // 2
    fr_p = fr[:, 8:16].reshape(BATCH, pairs, 2, NCH, CHUNK).transpose(0, 1, 3, 2, 4)
    fc_p = fc[:, :, 8:16].reshape(BATCH, TP, pairs, 2).transpose(0, 2, 1, 3)
    hf = fox_prompt(zf, fr_p, fc_p)
    width = F_HEADS * F_DH
    suf, tot = lf_suffix(cache_fox_logf.astype(F32).transpose(0, 2, 1))
    hf = fox_decode(page_table, zf, g2t_s, cache_fox_k.reshape(n_phys, PAGE_SIZE, width),
                    cache_fox_v.reshape(n_phys, PAGE_SIZE, width), suf, tot, hf)

    h = proj_residual(hm, hf, wo_even[:512], wo_even[512:], ng[0, 1], h)
    h = ffn_residual(h, ng[0, 2], ng[0, 3], wg[0], wu[0], wd[0])

    zr, zl = norm_matmul(h, ng[1, 0], w_odd, (2048, 1024))
    cos_p, sin_p = _rope_tables(jnp.arange(TP) - PADF)
    cos_s, sin_s = _rope_tables(past + jnp.arange(SROWS) - SPAD)
    lng = ret_ln_g.astype(F32)
    hr, p_s = retention_call(zr, cos_p, sin_p, lng, jnp.zeros((BATCH, R_HEADS, HEAD_DIM, HEAD_DIM), F32), None,
                             rows=CHUNK, n_chunks=NCH, row0_first=PADF, block0=0)
    hr, s_s = retention_call(zr, cos_s, sin_s, lng, state_ret_S.astype(F32), hr,
                             rows=SROWS, n_chunks=1, row0_first=SPAD, block0=SBASE // SROWS)

    lru_args = (conv_w.astype(F32), conv_b.astype(F32).reshape(1, LRU_WIDTH),
                _block_diag_pairs(lru_wa), lru_ba.astype(F32).reshape(1, LRU_WIDTH),
                _block_diag_pairs(lru_wx), lru_bx.astype(F32).reshape(1, LRU_WIDTH),
                lru_lambda.astype(F32).reshape(1, LRU_WIDTH))
    hl, p_h, p_tail = lru_call(zl, jnp.zeros((CHUNK, LRU_WIDTH), F32), jnp.zeros((BATCH, 1, LRU_WIDTH), F32),
                               *lru_args, None, rows=CHUNK, n_chunks=NCH, row0_first=PADF, block0=0)
    pre_s = jnp.concatenate([jnp.zeros((DEC_BATCH, SPAD - (CONV_W - 1), LRU_WIDTH), F32),
                             state_lru_conv.astype(F32),
                             jnp.zeros((DEC_BATCH, DEC_SEQ, LRU_WIDTH), F32)], axis=1).reshape(-1, LRU_WIDTH)
    hl, s_h, s_tail = lru_call(zl, pre_s, state_lru_h.astype(F32).reshape(DEC_BATCH, 1, LRU_WIDTH),
                               *lru_args, hl, rows=SROWS, n_chunks=1, row0_first=SPAD, block0=SBASE // SROWS)

    h = proj_residual(hr, hl, wo_odd[:512], wo_odd[512:], ng[1, 1], h)
    h = ffn_residual(h, ng[1, 2], ng[1, 3], wg[1], wu[1], wd[1])

    y_prompt = h[:SBASE].reshape(BATCH, TP, D_MODEL)[:, CHUNK:]
    y_sample = _unpad_sample(h)
    fk = zf[:, 512:1024]
    fv = zf[:, 1024:1536]
    lf = g2[:, 8:16]
    heads = lambda x: x.reshape(x.shape[:2] + (F_HEADS, F_DH))
    nback = CONV_W - 1
    return (y_prompt, y_sample,
            heads(_unpad_prompt(fk)), heads(_unpad_prompt(fv)), _unpad_prompt(lf),
            heads(_unpad_sample(fk)), heads(_unpad_sample(fv)), _unpad_sample(lf),
            p_c, p_n, p_m[:, :, 0], s_c, s_n, s_m[:, :, 0],
            p_s, s_s,
            p_h[:, 0], s_h[:, 0], p_tail[:, 8 - nback:], s_tail[:, 8 - nback:])
```

```python
import functools
import math

import jax
import jax.numpy as jnp
import numpy as np
from jax import lax
from jax.experimental import pallas as pl
from jax.experimental.pallas import tpu as pltpu

F32 = jnp.float32
BF16 = jnp.bfloat16

D_MODEL = 1024
BATCH = 4
SEQ = 4096
DEC_BATCH = 32
DEC_SEQ = 4
PAGE_SIZE = 128
N_META = 16
CHUNK = 128
M_HEADS = 4
F_HEADS = 8
F_DH = 64
R_HEADS = 4
HEAD_DIM = 128
ROPE_BASE = 10000.0
LRU_WIDTH = 512
LRU_C = 8.0
CONV_W = 4
D_FF = 2816
EPS = 1e-6
NEG = -1e30

LANES = 128
PADF = CHUNK - N_META
TP = PADF + N_META + SEQ
NCH = TP // CHUNK
SROWS = 16
SPAD = SROWS - DEC_SEQ
SBASE = BATCH * TP
NP = SBASE + DEC_BATCH * SROWS
TM = 512
FFN_TF = D_FF // 2
FOX_TQ = 512
FOX_CK = 256
DEC_PG = 8
PRE_PG = 256


def _bf(x):
    return x.astype(BF16)


def _dot(a, b):
    return jnp.dot(a, b, preferred_element_type=F32)


def _dot_nt(a, b):
    return lax.dot_general(a, b, (((1,), (1,)), ((), ())), preferred_element_type=F32)


def _dot_tn(a, b):
    return lax.dot_general(a, b, (((0,), (0,)), ((), ())), preferred_element_type=F32)


def _split3(x):
    hi = _bf(x)
    r1 = x - hi.astype(F32)
    mid = _bf(r1)
    lo = _bf(r1 - mid.astype(F32))
    return hi, mid, lo


def _dot01_right(x, m01):
    hi, mid, lo = _split3(x)
    return _dot(hi, m01) + _dot(mid, m01) + _dot(lo, m01)


def _dot01_left(m01, x):
    hi, mid, lo = _split3(x)
    return _dot(m01, hi) + _dot(m01, mid) + _dot(m01, lo)


def _iota(shape, dim):
    return lax.broadcasted_iota(jnp.int32, shape, dim)


def _rms(x, g):
    ms = jnp.mean(x * x, axis=-1, keepdims=True)
    return x * lax.rsqrt(ms + EPS) * g


def _softplus(x):
    return jnp.maximum(x, 0.0) + jnp.log1p(jnp.exp(-jnp.abs(x)))


def _log_sigmoid(x):
    return -_softplus(-x)


def _sigmoid(x):
    return 1.0 / (1.0 + jnp.exp(-x))


def _nm_kernel(x_ref, g_ref, w_ref, *o_refs, splits):
    xn = _bf(_rms(x_ref[...], g_ref[...]))
    off = 0
    for o_ref, n in zip(o_refs, splits):
        o_ref[...] = _dot(xn, w_ref[:, off:off + n])
        off += n


def norm_matmul(x, g, w, splits):
    n_rows, k = x.shape
    return pl.pallas_call(
        functools.partial(_nm_kernel, splits=splits),
        grid=(n_rows // TM,),
        in_specs=[pl.BlockSpec((TM, k), lambda i: (i, 0)),
                  pl.BlockSpec((1, k), lambda i: (0, 0)),
                  pl.BlockSpec((k, sum(splits)), lambda i: (0, 0))],
        out_specs=[pl.BlockSpec((TM, n), lambda i: (i, 0)) for n in splits],
        out_shape=[jax.ShapeDtypeStruct((n_rows, n), F32) for n in splits],
        compiler_params=pltpu.CompilerParams(dimension_semantics=("parallel",)),
        name="norm_matmul",
    )(x, g, w)


def _proj_kernel(a1_ref, a2_ref, w1_ref, w2_ref, g_ref, h_ref, o_ref):
    mix = _dot(_bf(a1_ref[...]), w1_ref[...]) + _dot(_bf(a2_ref[...]), w2_ref[...])
    o_ref[...] = h_ref[...] + _rms(mix, g_ref[...])


def proj_residual(a1, a2, w1, w2, g, h):
    n_rows, k1 = a1.shape
    k2 = a2.shape[1]
    d = h.shape[1]
    return pl.pallas_call(
        _proj_kernel,
        grid=(n_rows // TM,),
        in_specs=[pl.BlockSpec((TM, k1), lambda i: (i, 0)),
                  pl.BlockSpec((TM, k2), lambda i: (i, 0)),
                  pl.BlockSpec((k1, d), lambda i: (0, 0)),
                  pl.BlockSpec((k2, d), lambda i: (0, 0)),
                  pl.BlockSpec((1, d), lambda i: (0, 0)),
                  pl.BlockSpec((TM, d), lambda i: (i, 0))],
        out_specs=pl.BlockSpec((TM, d), lambda i: (i, 0)),
        out_shape=jax.ShapeDtypeStruct((n_rows, d), F32),
        compiler_params=pltpu.CompilerParams(dimension_semantics=("parallel",)),
        name="proj_residual",
    )(a1, a2, w1, w2, g, h)


def _ffn_kernel(h_ref, g2_ref, g3_ref, wg_ref, wu_ref, wd_ref, o_ref, xn_ref, acc_ref):
    f = pl.program_id(1)

    @pl.when(f == 0)
    def _():
        xn_ref[...] = _bf(_rms(h_ref[...], g2_ref[...]))
        acc_ref[...] = jnp.zeros(acc_ref.shape, F32)

    xn = xn_ref[...]
    gate = _dot(xn, wg_ref[...])
    up = _dot(xn, wu_ref[...])
    act = gate * _sigmoid(gate) * up
    acc_ref[...] += _dot(_bf(act), wd_ref[...])

    @pl.when(f == pl.num_programs(1) - 1)
    def _():
        o_ref[...] = h_ref[...] + _rms(acc_ref[...], g3_ref[...])


def ffn_residual(h, g2, g3, wg, wu, wd):
    n_rows, d = h.shape
    dff = wg.shape[1]
    return pl.pallas_call(
        _ffn_kernel,
        grid=(n_rows // TM, dff // FFN_TF),
        in_specs=[pl.BlockSpec((TM, d), lambda i, f: (i, 0)),
                  pl.BlockSpec((1, d), lambda i, f: (0, 0)),
                  pl.BlockSpec((1, d), lambda i, f: (0, 0)),
                  pl.BlockSpec((d, FFN_TF), lambda i, f: (0, f)),
                  pl.BlockSpec((d, FFN_TF), lambda i, f: (0, f)),
                  pl.BlockSpec((FFN_TF, d), lambda i, f: (f, 0))],
        out_specs=pl.BlockSpec((TM, d), lambda i, f: (i, 0)),
        out_shape=jax.ShapeDtypeStruct((n_rows, d), F32),
        scratch_shapes=[pltpu.VMEM((TM, d), BF16), pltpu.VMEM((TM, d), F32)],
        compiler_params=pltpu.CompilerParams(dimension_semantics=("parallel", "arbitrary")),
        name="ffn_residual",
    )(h, g2, g3, wg, wu, wd)


def _gates_kernel(z_ref, zt_ref, brow_ref, bcol_ref, o_ref, ot_ref):
    x = z_ref[...] + brow_ref[...]
    o_ref[...] = jnp.where(_iota(x.shape, 1) < M_HEADS, x, _log_sigmoid(x))
    xt = zt_ref[...] + bcol_ref[...]
    ot_ref[...] = jnp.where(_iota(xt.shape, 0) < M_HEADS, xt, _log_sigmoid(xt))


def gates_activate(zg, zgt, brow, bcol):
    n_rows = zg.shape[0]
    return pl.pallas_call(
        _gates_kernel,
        grid=(n_rows // TM,),
        in_specs=[pl.BlockSpec((TM, LANES), lambda i: (i, 0)),
                  pl.BlockSpec((16, TM), lambda i: (0, i)),
                  pl.BlockSpec((1, LANES), lambda i: (0, 0)),
                  pl.BlockSpec((16, 1), lambda i: (0, 0))],
        out_specs=[pl.BlockSpec((TM, LANES), lambda i: (i, 0)),
                   pl.BlockSpec((16, TM), lambda i: (0, i))],
        out_shape=[jax.ShapeDtypeStruct((n_rows, LANES), F32),
                   jax.ShapeDtypeStruct((16, n_rows), F32)],
        compiler_params=pltpu.CompilerParams(dimension_semantics=("parallel",)),
        name="gates_activate",
    )(zg, zgt, brow, bcol)


def _mlstm_chunk(q, k, v, logi_c, lf_c, logi_r, lf_r, c_state, n_state, m_state, row0, rows):
    ri = _iota((rows, rows), 0)
    ci = _iota((rows, rows), 1)
    valid_c = _iota((rows, 1), 0) >= row0
    valid_r = _iota((1, rows), 1) >= row0
    lf_c = jnp.where(valid_c, lf_c, 0.0)
    lf_r = jnp.where(valid_r, lf_r, 0.0)
    logi_c = jnp.where(valid_c, logi_c, NEG)
    logi_r = jnp.where(valid_r, logi_r, NEG)
    causal = ci <= ri
    b_c = jnp.sum(jnp.where(causal, lf_r, 0.0), axis=1, keepdims=True)
    b_r = jnp.sum(jnp.where(ri <= ci, lf_c, 0.0), axis=0, keepdims=True)
    log_d = jnp.where(causal, b_c - b_r + logi_r, -jnp.inf)
    m_inter = b_c + m_state
    m_t = jnp.maximum(m_inter, jnp.max(log_d, axis=1, keepdims=True))
    qb = _bf(q)
    kb = _bf(k)
    vb = _bf(v)
    w_intra = _dot_nt(qb, kb) * jnp.exp(log_d - m_t)
    w_inter = jnp.exp(m_inter - m_t)
    num = _dot(_bf(w_intra), vb) + w_inter * _dot(qb, _bf(c_state))
    den = jnp.sum(w_intra, axis=1, keepdims=True) + w_inter * jnp.sum(q * n_state, axis=1, keepdims=True)
    h = num / jnp.maximum(jnp.abs(den), jnp.exp(-m_t))
    m_new = m_t[rows - 1:rows, :]
    b_last = b_c[rows - 1:rows, :]
    w_src = jnp.exp(b_last - b_c + logi_c - m_new)
    decay = jnp.exp(b_last + m_state - m_new)
    ks = k * w_src
    c_new = decay * c_state + _dot_tn(_bf(ks), vb)
    n_new = decay * n_state + jnp.sum(ks, axis=0, keepdims=True)
    return h, c_new, n_new, m_new


def _mlstm_kernel(z_ref, g_ref, gt_ref, c0_ref, n0_ref, m0_ref, o_ref, c_ref, n_ref, m_ref, *, rows, row0_first):
    c = pl.program_id(1)

    @pl.when(c == 0)
    def _():
        c_ref[...] = c0_ref[...]
        n_ref[...] = n0_ref[...]
        m_ref[...] = m0_ref[...]

    row0 = jnp.where(c == 0, row0_first, 0)
    g = g_ref[...]
    gt = gt_ref[...]
    outs = []
    for h in range(M_HEADS):
        lo = h * HEAD_DIM
        q = z_ref[:, lo:lo + HEAD_DIM]
        k = z_ref[:, 512 + lo:512 + lo + HEAD_DIM] * (HEAD_DIM ** -0.5)
        v = z_ref[:, 1024 + lo:1024 + lo + HEAD_DIM]
        og = z_ref[:, 1536 + lo:1536 + lo + HEAD_DIM]
        hh, c_new, n_new, m_new = _mlstm_chunk(
            q, k, v,
            g[:, h:h + 1], g[:, M_HEADS + h:M_HEADS + h + 1],
            gt[h:h + 1, :], gt[M_HEADS + h:M_HEADS + h + 1, :],
            c_ref[0, h], n_ref[0, h:h + 1, :], m_ref[0, h:h + 1, 0:1], row0, rows)
        c_ref[0, h] = c_new
        n_ref[0, h:h + 1, :] = n_new
        m_ref[0, h:h + 1, :] = jnp.broadcast_to(m_new, (1, LANES))
        outs.append(_sigmoid(og) * hh)
    o_ref[...] = jnp.concatenate(outs, axis=1)


def _state_specs(heads):
    return [pl.BlockSpec((1, heads, HEAD_DIM, HEAD_DIM), lambda b, c: (b, 0, 0, 0)),
            pl.BlockSpec((1, heads, HEAD_DIM), lambda b, c: (b, 0, 0)),
            pl.BlockSpec((1, heads, LANES), lambda b, c: (b, 0, 0))]


def mlstm_call(z, g2, g2t, c0, n0, m0, prev, *, rows, n_chunks, row0_first, block0):
    nb = c0.shape[0]
    blk = lambda b, c: (block0 + b * n_chunks + c, 0)
    in_specs = [pl.BlockSpec((rows, 2048), blk),
                pl.BlockSpec((rows, LANES), blk),
                pl.BlockSpec((None, 16, rows), lambda b, c: (b * n_chunks + c, 0, 0))] + _state_specs(M_HEADS)
    args = [z, g2, g2t, c0, n0, m0]
    aliases = {}
    if prev is not None:
        in_specs.append(pl.BlockSpec(memory_space=pl.ANY))
        args.append(prev)
        aliases = {6: 0}
    kern = functools.partial(_mlstm_kernel, rows=rows, row0_first=row0_first)
    if prev is not None:
        kern = functools.partial(_drop_arg, kern, 6)
    return pl.pallas_call(
        kern,
        grid=(nb, n_chunks),
        in_specs=in_specs,
        out_specs=[pl.BlockSpec((rows, 512), blk)] + _state_specs(M_HEADS),
        out_shape=[jax.ShapeDtypeStruct((NP, 512), F32),
                   jax.ShapeDtypeStruct((nb, M_HEADS, HEAD_DIM, HEAD_DIM), F32),
                   jax.ShapeDtypeStruct((nb, M_HEADS, HEAD_DIM), F32),
                   jax.ShapeDtypeStruct((nb, M_HEADS, LANES), F32)],
        input_output_aliases=aliases,
        compiler_params=pltpu.CompilerParams(dimension_semantics=("parallel", "arbitrary")),
        name="mlstm",
    )(*args)


def _drop_arg(kern, idx, *refs):
    return kern(*refs[:idx], *refs[idx + 1:])


def _ret_log_gamma(h):
    return math.log1p(-(2.0 ** (-5.0 - h)))


def _ret_kernel(z_ref, cos_ref, sin_ref, lng_ref, s0_ref, o_ref, s_ref, *, rows, row0_first):
    c = pl.program_id(1)

    @pl.when(c == 0)
    def _():
        s_ref[...] = s0_ref[...]

    row0 = jnp.where(c == 0, row0_first, 0)
    n_valid = (rows - row0).astype(F32)
    cosf = cos_ref[...]
    sinf = sin_ref[...]
    ri = _iota((rows, rows), 0)
    ci = _iota((rows, rows), 1)
    diff = (ri - ci).astype(F32)
    rowi = _iota((rows, 1), 0)
    te = (rowi - row0).astype(F32)
    valid = rowi >= row0
    outs = []
    for h in range(R_HEADS):
        lg = _ret_log_gamma(h)
        lo = h * HEAD_DIM
        q = z_ref[:, lo:lo + HEAD_DIM]
        k = z_ref[:, 512 + lo:512 + lo + HEAD_DIM]
        v = jnp.where(valid, z_ref[:, 1024 + lo:1024 + lo + HEAD_DIM], 0.0)
        rg = z_ref[:, 1536 + lo:1536 + lo + HEAD_DIM]
        q = q * cosf + pltpu.roll(q, HEAD_DIM // 2, 1) * sinf
        k = (k * cosf + pltpu.roll(k, HEAD_DIM // 2, 1) * sinf) * (HEAD_DIM ** -0.5)
        s_state = s_ref[0, h]
        qb = _bf(q)
        vb = _bf(v)
        decay_m = jnp.where(diff >= 0, jnp.exp(diff * lg), 0.0)
        intra = _dot(_bf(_dot_nt(qb, _bf(k)) * decay_m), vb)
        inter = _dot(qb, _bf(s_state)) * jnp.exp((te + 1.0) * lg)
        w_src = jnp.exp((n_valid - 1.0 - te) * lg)
        s_ref[0, h] = jnp.exp(n_valid * lg) * s_state + _dot_tn(_bf(k * w_src), vb)
        hr = intra + inter
        mu = jnp.mean(hr, axis=-1, keepdims=True)
        var = jnp.mean(jnp.square(hr - mu), axis=-1, keepdims=True)
        hr = (hr - mu) * lax.rsqrt(var + EPS) * lng_ref[h:h + 1, :]
        outs.append(rg * _sigmoid(rg) * hr)
    o_ref[...] = jnp.concatenate(outs, axis=1)


def retention_call(z, cosf, sinf, lng, s0, prev, *, rows, n_chunks, row0_first, block0):
    nb = s0.shape[0]
    blk = lambda b, c: (block0 + b * n_chunks + c, 0)
    sspec = pl.BlockSpec((1, R_HEADS, HEAD_DIM, HEAD_DIM), lambda b, c: (b, 0, 0, 0))
    in_specs = [pl.BlockSpec((rows, 2048), blk),
                pl.BlockSpec((rows, HEAD_DIM), lambda b, c: (c, 0)),
                pl.BlockSpec((rows, HEAD_DIM), lambda b, c: (c, 0)),
                pl.BlockSpec((R_HEADS, HEAD_DIM), lambda b, c: (0, 0)),
                sspec]
    args = [z, cosf, sinf, lng, s0]
    aliases = {}
    kern = functools.partial(_ret_kernel, rows=rows, row0_first=row0_first)
    if prev is not None:
        in_specs.append(pl.BlockSpec(memory_space=pl.ANY))
        args.append(prev)
        aliases = {5: 0}
        kern = functools.partial(_drop_arg, kern, 5)
    return pl.pallas_call(
        kern,
        grid=(nb, n_chunks),
        in_specs=in_specs,
        out_specs=[pl.BlockSpec((rows, 512), blk), sspec],
        out_shape=[jax.ShapeDtypeStruct((NP, 512), F32),
                   jax.ShapeDtypeStruct((nb, R_HEADS, HEAD_DIM, HEAD_DIM), F32)],
        input_output_aliases=aliases,
        compiler_params=pltpu.CompilerParams(dimension_semantics=("parallel", "arbitrary")),
        name="retention",
    )(*args)


def _lru_kernel(z_ref, pre_ref, h0_ref, cw_ref, cb_ref, wa_ref, ba_ref, wx_ref, bx_ref, lam_ref,
                o_ref, hlast_ref, tail_ref, *, rows, row0_first):
    c = pl.program_id(1)

    @pl.when(c == 0)
    def _():
        hlast_ref[...] = h0_ref[...]
        tail_ref[...] = jnp.zeros(tail_ref.shape, F32)

    row0 = jnp.where(c == 0, row0_first, 0)
    rowi = _iota((rows, 1), 0)
    valid = rowi >= row0
    lx = jnp.where(valid, z_ref[:, 0:LRU_WIDTH], pre_ref[...])
    gate = z_ref[:, LRU_WIDTH:2 * LRU_WIDTH]
    tail = tail_ref[0]
    row8 = _iota((8, 1), 0)
    xc = cb_ref[...] + cw_ref[CONV_W - 1:CONV_W, :] * lx
    for j in range(1, CONV_W):
        rolled = pltpu.roll(lx, j, 0)
        first = jnp.where(row8 < j, pltpu.roll(tail, j, 0), rolled[0:8])
        shifted = jnp.concatenate([first, rolled[8:]], axis=0)
        xc = xc + cw_ref[CONV_W - 1 - j:CONV_W - j, :] * shifted
    tail_ref[0] = lx[rows - 8:rows]
    xcb = _bf(xc)
    pre_a = []
    pre_x = []
    for p in range(LRU_WIDTH // LANES):
        xs = xcb[:, p * LANES:(p + 1) * LANES]
        pre_a.append(_dot(xs, wa_ref[p]))
        pre_x.append(_dot(xs, wx_ref[p]))
    r = _sigmoid(jnp.concatenate(pre_a, axis=1) + ba_ref[...])
    ig = _sigmoid(jnp.concatenate(pre_x, axis=1) + bx_ref[...])
    log_a = -LRU_C * r * _softplus(-lam_ref[...])
    a = jnp.where(valid, jnp.exp(log_a), 1.0)
    one_minus_a2 = -jnp.tanh(log_a) * (jnp.exp(2.0 * log_a) + 1.0)
    bx = jnp.where(valid, jnp.sqrt(one_minus_a2) * (ig * xc), 0.0)
    shift = 1
    while shift < rows:
        keep = rowi >= shift
        a_sh = jnp.where(keep, pltpu.roll(a, shift, 0), 1.0)
        b_sh = jnp.where(keep, pltpu.roll(bx, shift, 0), 0.0)
        bx = a * b_sh + bx
        a = a * a_sh
        shift *= 2
    hs = a * hlast_ref[0] + bx
    hlast_ref[0] = hs[rows - 1:rows, :]
    gl = 0.5 * gate * (1.0 + jnp.tanh(math.sqrt(2.0 / math.pi) * (gate + 0.044715 * gate * gate * gate)))
    o_ref[...] = hs * gl


def lru_call(z, pre, h0, cw, cb, wa2, ba, wx2, bx, lam, prev, *, rows, n_chunks, row0_first, block0):
    nb = h0.shape[0]
    blk = lambda b, c: (block0 + b * n_chunks + c, 0)
    blk_local = lambda b, c: (b * n_chunks + c, 0)
    const2 = lambda b, c: (0, 0)
    in_specs = [pl.BlockSpec((rows, 1024), blk),
                pl.BlockSpec((rows, LRU_WIDTH), const2 if pre.shape[0] == rows else blk_local),
                pl.BlockSpec((1, 1, LRU_WIDTH), lambda b, c: (b, 0, 0)),
                pl.BlockSpec((CONV_W, LRU_WIDTH), const2),
                pl.BlockSpec((1, LRU_WIDTH), const2),
                pl.BlockSpec((LRU_WIDTH // LANES, LANES, LANES), lambda b, c: (0, 0, 0)),
                pl.BlockSpec((1, LRU_WIDTH), const2),
                pl.BlockSpec((LRU_WIDTH // LANES, LANES, LANES), lambda b, c: (0, 0, 0)),
                pl.BlockSpec((1, LRU_WIDTH), const2),
                pl.BlockSpec((1, LRU_WIDTH), const2)]
    args = [z, pre, h0, cw, cb, wa2, ba, wx2, bx, lam]
    aliases = {}
    kern = functools.partial(_lru_kernel, rows=rows, row0_first=row0_first)
    if prev is not None:
        in_specs.append(pl.BlockSpec(memory_space=pl.ANY))
        args.append(prev)
        aliases = {10: 0}
        kern = functools.partial(_drop_arg, kern, 10)
    return pl.pallas_call(
        kern,
        grid=(nb, n_chunks),
        in_specs=in_specs,
        out_specs=[pl.BlockSpec((rows, LRU_WIDTH), blk),
                   pl.BlockSpec((1, 1, LRU_WIDTH), lambda b, c: (b, 0, 0)),
                   pl.BlockSpec((1, 8, LRU_WIDTH), lambda b, c: (b, 0, 0))],
        out_shape=[jax.ShapeDtypeStruct((NP, LRU_WIDTH), F32),
                   jax.ShapeDtypeStruct((nb, 1, LRU_WIDTH), F32),
                   jax.ShapeDtypeStruct((nb, 8, LRU_WIDTH), F32)],
        input_output_aliases=aliases,
        compiler_params=pltpu.CompilerParams(dimension_semantics=("parallel", "arbitrary")),
        name="rglru",
    )(*args)


def _fox_cumsum_kernel(g_ref, gt_ref, fc_ref, fr_ref):
    ri = _iota((CHUNK, CHUNK), 0)
    ci = _iota((CHUNK, CHUNK), 1)
    lower = jnp.where(ci <= ri, 1.0, 0.0).astype(BF16)
    upper = jnp.where(ri <= ci, 1.0, 0.0).astype(BF16)
    carry_r = jnp.zeros((1, LANES), F32)
    carry_c = jnp.zeros((16, 1), F32)
    for blk in range(NCH):
        x = g_ref[blk * CHUNK:(blk + 1) * CHUNK, :]
        xt = gt_ref[:, blk * CHUNK:(blk + 1) * CHUNK]
        if blk == 0:
            x = jnp.where(_iota((CHUNK, 1), 0) >= PADF, x, 0.0)
            xt = jnp.where(_iota((1, CHUNK), 1) >= PADF, xt, 0.0)
        cs = _dot01_left(lower, x) + carry_r
        cst = _dot01_right(xt, upper) + carry_c
        carry_r = cs[CHUNK - 1:CHUNK, :]
        carry_c = cst[:, CHUNK - 1:CHUNK]
        if blk == 0:
            cst = jnp.where(_iota((1, CHUNK), 1) >= PADF, cst, -NEG)
        fc_ref[0, blk * CHUNK:(blk + 1) * CHUNK, :] = cs
        fr_ref[0, :, blk * CHUNK:(blk + 1) * CHUNK] = cst


def fox_cumsum(g2, g2t):
    return pl.pallas_call(
        _fox_cumsum_kernel,
        grid=(BATCH,),
        in_specs=[pl.BlockSpec((TP, LANES), lambda b: (b, 0)),
                  pl.BlockSpec((16, TP), lambda b: (0, b))],
        out_specs=[pl.BlockSpec((1, TP, LANES), lambda b: (b, 0, 0)),
                   pl.BlockSpec((1, 16, TP), lambda b: (b, 0, 0))],
        out_shape=[jax.ShapeDtypeStruct((BATCH, TP, LANES), F32),
                   jax.ShapeDtypeStruct((BATCH, 16, TP), F32)],
        compiler_params=pltpu.CompilerParams(dimension_semantics=("parallel",)),
        name="fox_cumsum",
    )(g2, g2t)


def _fox_prompt_kernel(q_ref, k_ref, v_ref, fr_ref, fc_ref, o_ref, q_s, kt_s, v_s, fq_s, m_s, l_s, acc_s):
    i = pl.program_id(2)
    is_a = _iota((1, LANES), 1) < F_DH
    row_a = _iota((LANES, 1), 0) < F_DH

    @pl.when(i == 0)
    def _():
        def fill(blk, carry):
            kt = k_ref[pl.ds(pl.multiple_of(blk * CHUNK, CHUNK), CHUNK), :].T
            kt_s[0, blk] = _bf(jnp.where(row_a, kt, 0.0))
            kt_s[1, blk] = _bf(jnp.where(row_a, 0.0, kt))
            return carry

        lax.fori_loop(0, NCH, fill, 0)
        v_s[...] = _bf(v_ref[...])
        q_s[...] = _bf(q_ref[...] * (F_DH ** -0.5))
        for hh in range(2):
            fq_s[hh] = jnp.broadcast_to(fc_ref[0, 0, :, hh:hh + 1], (TP, LANES))

    def init():
        m_s[...] = jnp.full(m_s.shape, NEG, F32)
        l_s[...] = jnp.zeros(l_s.shape, F32)
        acc_s[...] = jnp.zeros(acc_s.shape, F32)

    def chunk(q0, tq, kblk, nblk, causal):
        ck = nblk * CHUNK
        k0 = kblk * CHUNK if isinstance(kblk, int) else pl.multiple_of(kblk * CHUNK, CHUNK)
        qb = q_s[pl.ds(q0, tq), :]
        vb = v_s[pl.ds(k0, ck), :]
        for hh in range(2):
            kt = jnp.concatenate([kt_s[hh, kblk + t] for t in range(nblk)], axis=1)
            fk = jnp.concatenate([fr_ref[0, 0, kblk + t, hh:hh + 1, :] for t in range(nblk)], axis=1)
            u = _dot(qb, kt) - fk
            if causal:
                u = jnp.where(k0 + _iota((tq, ck), 1) <= q0 + _iota((tq, ck), 0), u, NEG)
            fq = fq_s[hh, pl.ds(q0, tq), :]
            m_old = m_s[hh, 0:tq, :]
            m_new = jnp.maximum(m_old, jnp.max(u, axis=1, keepdims=True) + fq)
            alpha = jnp.exp(m_old - m_new)
            p = jnp.exp(u + jnp.concatenate([fq - m_new] * nblk, axis=1))
            l_s[hh, 0:tq, :] = alpha * l_s[hh, 0:tq, :] + jnp.sum(p, axis=1, keepdims=True)
            acc_s[hh, 0:tq, :] = alpha * acc_s[hh, 0:tq, :] + _dot(_bf(p), vb)
            m_s[hh, 0:tq, :] = m_new

    def finalize(q0, tq):
        oa = acc_s[0, 0:tq, :] / l_s[0, 0:tq, :]
        ob = acc_s[1, 0:tq, :] / l_s[1, 0:tq, :]
        o_ref[pl.ds(q0, tq), :] = jnp.where(is_a, oa, ob)

    @pl.when(i == 0)
    def _():
        init()
        chunk(0, CHUNK, 0, 1, True)
        finalize(0, CHUNK)

    nck = FOX_CK // CHUNK
    nqb = FOX_TQ // CHUNK
    qblk = 1 + i * nqb
    q0 = pl.multiple_of(qblk * CHUNK, CHUNK)
    init()
    chunk(q0, FOX_TQ, 0, 1, False)

    def body(j, carry):
        chunk(q0, FOX_TQ, 1 + j * nck, nck, False)
        return carry

    lax.fori_loop(0, (FOX_TQ // FOX_CK) * i, body, 0)
    for d in range(FOX_TQ // FOX_CK):
        chunk(q0, FOX_TQ, qblk + d * nck, nck, True)
    finalize(q0, FOX_TQ)


def fox_prompt(zf, fr, fc):
    nq = (TP - CHUNK) // FOX_TQ
    pairs = F_HEADS // 2
    return pl.pallas_call(
        _fox_prompt_kernel,
        grid=(BATCH, pairs, nq),
        in_specs=[pl.BlockSpec((TP, LANES), lambda b, p, i: (b, p)),
                  pl.BlockSpec((TP, LANES), lambda b, p, i: (b, pairs + p)),
                  pl.BlockSpec((TP, LANES), lambda b, p, i: (b, 2 * pairs + p)),
                  pl.BlockSpec((1, 1, NCH, 2, LANES), lambda b, p, i: (b, p, 0, 0, 0)),
                  pl.BlockSpec((1, 1, TP, 2), lambda b, p, i: (b, p, 0, 0))],
        out_specs=pl.BlockSpec((TP, LANES), lambda b, p, i: (b, p)),
        out_shape=jax.ShapeDtypeStruct((NP, 512), F32),
        scratch_shapes=[pltpu.VMEM((TP, LANES), BF16),
                        pltpu.VMEM((2, NCH, LANES, CHUNK), BF16),
                        pltpu.VMEM((TP, LANES), BF16),
                        pltpu.VMEM((2, TP, LANES), F32),
                        pltpu.VMEM((2, FOX_TQ, LANES), F32),
                        pltpu.VMEM((2, FOX_TQ, LANES), F32),
                        pltpu.VMEM((2, FOX_TQ, LANES), F32)],
        compiler_params=pltpu.CompilerParams(dimension_semantics=("parallel", "parallel", "arbitrary")),
        name="fox_prompt",
    )(zf, zf, zf, fr, fc)


def _lf_suffix_kernel(x_ref, suf_ref, tot_ref):
    n = PAGE_SIZE * F_HEADS
    r = _iota((n, n), 0)
    c = _iota((n, n), 1)
    same = (r % F_HEADS) == (c % F_HEADS)
    after = jnp.where(same & (r > c), 1.0, 0.0).astype(BF16)
    total = jnp.where(same, 1.0, 0.0).astype(BF16)
    x = x_ref[...]
    suf_ref[...] = _dot01_right(x, after)
    tot_ref[...] = _dot01_right(x, total)


def lf_suffix(lff):
    n_phys = lff.shape[0]
    spec = pl.BlockSpec((PRE_PG, lff.shape[1]), lambda i: (i, 0))
    return pl.pallas_call(
        _lf_suffix_kernel,
        grid=(n_phys // PRE_PG,),
        in_specs=[spec],
        out_specs=[spec, spec],
        out_shape=[jax.ShapeDtypeStruct(lff.shape, F32)] * 2,
        compiler_params=pltpu.CompilerParams(dimension_semantics=("parallel",)),
        name="lf_suffix",
    )(lff)


def _fox_decode_kernel(pt_ref, q_ref, kn_ref, vn_ref, lfn_ref, *rest):
    k_refs = rest[0:DEC_PG]
    v_refs = rest[DEC_PG:2 * DEC_PG]
    suf_refs = rest[2 * DEC_PG:3 * DEC_PG]
    tot_refs = rest[3 * DEC_PG:4 * DEC_PG]
    o_ref = rest[4 * DEC_PG]
    q2_s, m_s, l_s, acc_s, run_s, nc_s = rest[4 * DEC_PG + 1:]
    b = pl.program_id(0)
    j = pl.program_id(1)
    n_pages = pl.num_programs(1) * DEC_PG
    n_rows = DEC_SEQ * F_HEADS
    flat = PAGE_SIZE * F_HEADS
    new_flat = SROWS * F_HEADS

    def update(u, vb):
        reps = u.shape[1] // LANES
        m_old = m_s[...]
        m_new = jnp.maximum(m_old, jnp.max(u, axis=1, keepdims=True))
        alpha = jnp.exp(m_old - m_new)
        p = jnp.exp(u - jnp.concatenate([m_new] * reps, axis=1))
        l_s[...] = alpha * l_s[...] + jnp.sum(p, axis=1, keepdims=True)
        acc_s[...] = alpha[:, 0:F_DH] * acc_s[...] + _dot(_bf(p), vb)
        m_s[...] = m_new

    @pl.when(j == 0)
    def _():
        m_s[...] = jnp.full(m_s.shape, NEG, F32)
        l_s[...] = jnp.zeros(l_s.shape, F32)
        acc_s[...] = jnp.zeros(acc_s.shape, F32)
        run_s[...] = jnp.zeros(run_s.shape, F32)
        q2 = _bf(q_ref[...] * (F_DH ** -0.5))
        q2_s[...] = q2
        r = _iota((new_flat, new_flat), 0)
        c = _iota((new_flat, new_flat), 1)
        inc = jnp.where(((r % F_HEADS) == (c % F_HEADS)) & (r <= c) & (r >= SPAD * F_HEADS), 1.0, 0.0).astype(BF16)
        cum = _dot01_right(lfn_ref[...], inc)[0:1, :]
        row_q = _iota((n_rows, new_flat), 0) // F_HEADS
        row_h = _iota((n_rows, new_flat), 0) % F_HEADS
        key = _iota((n_rows, new_flat), 1) // F_HEADS - SPAD
        same = (_iota((n_rows, new_flat), 1) % F_HEADS) == row_h
        nc = jnp.sum(jnp.where(same & (key == row_q), cum, 0.0), axis=1, keepdims=True)
        nc_s[...] = jnp.broadcast_to(nc, nc_s.shape)
        s = _dot_nt(q2, _bf(kn_ref[...]))
        u = jnp.where(same & (key >= 0) & (key <= row_q), s + (nc - cum), NEG)
        update(u, _bf(vn_ref[...]))

    q2 = q2_s[...]
    same_head = (_iota((n_rows, flat), 1) % F_HEADS) == (_iota((n_rows, flat), 0) % F_HEADS)
    ncw = jnp.concatenate([nc_s[...]] * (flat // LANES), axis=1)
    for i in range(DEC_PG):
        row = pt_ref[b, n_pages - 1 - (j * DEC_PG + i)] % 8
        kf = _bf(k_refs[i][0].reshape(flat, F_DH))
        vf = _bf(v_refs[i][0].reshape(flat, F_DH))
        later = run_s[...] + suf_refs[i][pl.ds(row, 1), :]
        u = jnp.where(same_head, _dot_nt(q2, kf) + (later + ncw), NEG)
        update(u, vf)
        run_s[...] = run_s[...] + tot_refs[i][pl.ds(row, 1), :]

    @pl.when(j == pl.num_programs(1) - 1)
    def _():
        o_ref[...] = acc_s[...] / l_s[:, 0:F_DH]


def fox_decode(page_table, q2, kn, vn, lfn, cache_k, cache_v, suf, tot):
    n_pages = page_table.shape[1]
    n_rows = DEC_SEQ * F_HEADS
    flat = PAGE_SIZE * F_HEADS

    def page_map(i):
        return lambda b, j, pt: (pt[b, n_pages - 1 - (j * DEC_PG + i)], 0, 0, 0)

    def row_map(i):
        return lambda b, j, pt: (pt[b, n_pages - 1 - (j * DEC_PG + i)] // 8, 0)

    per_b = lambda b, j, pt: (b, 0, 0)
    in_specs = [pl.BlockSpec((None, n_rows, F_DH), per_b),
                pl.BlockSpec((None, SROWS * F_HEADS, F_DH), per_b),
                pl.BlockSpec((None, SROWS * F_HEADS, F_DH), per_b),
                pl.BlockSpec((None, 8, SROWS * F_HEADS), per_b)]
    in_specs += [pl.BlockSpec((1, PAGE_SIZE, F_HEADS, F_DH), page_map(i)) for i in range(DEC_PG)]
    in_specs += [pl.BlockSpec((1, PAGE_SIZE, F_HEADS, F_DH), page_map(i)) for i in range(DEC_PG)]
    in_specs += [pl.BlockSpec((8, flat), row_map(i)) for i in range(DEC_PG)]
    in_specs += [pl.BlockSpec((8, flat), row_map(i)) for i in range(DEC_PG)]
    args = [q2, kn, vn, lfn] + [cache_k] * DEC_PG + [cache_v] * DEC_PG + [suf] * DEC_PG + [tot] * DEC_PG
    grid_spec = pltpu.PrefetchScalarGridSpec(
        num_scalar_prefetch=1,
        grid=(DEC_BATCH, n_pages // DEC_PG),
        in_specs=in_specs,
        out_specs=pl.BlockSpec((None, n_rows, F_DH), per_b),
        scratch_shapes=[pltpu.VMEM((n_rows, F_DH), BF16),
                        pltpu.VMEM((n_rows, LANES), F32),
                        pltpu.VMEM((n_rows, LANES), F32),
                        pltpu.VMEM((n_rows, F_DH), F32),
                        pltpu.VMEM((1, flat), F32),
                        pltpu.VMEM((n_rows, LANES), F32)])
    return pl.pallas_call(
        _fox_decode_kernel,
        grid_spec=grid_spec,
        out_shape=jax.ShapeDtypeStruct((DEC_BATCH, n_rows, F_DH), F32),
        compiler_params=pltpu.CompilerParams(dimension_semantics=("parallel", "arbitrary")),
        name="fox_decode",
    )(page_table, *args)


def _block_diag_pairs(w):
    z = jnp.zeros((LANES // 2, LANES // 2), w.dtype)
    pairs = [jnp.block([[w[2 * p], z], [z, w[2 * p + 1]]]) for p in range(w.shape[0] // 2)]
    return _bf(jnp.stack(pairs))


def _rope_tables(pos):
    half = HEAD_DIM // 2
    freq = ROPE_BASE ** (-jnp.arange(half, dtype=F32) / half)
    ang = pos.astype(F32)[:, None] * freq[None, :]
    cos, sin = jnp.cos(ang), jnp.sin(ang)
    return jnp.concatenate([cos, cos], axis=1), jnp.concatenate([-sin, sin], axis=1)


def _unpad_prompt(x):
    return x[:SBASE].reshape((BATCH, TP) + x.shape[1:])[:, PADF:]


def _unpad_sample(x):
    return x[SBASE:].reshape((DEC_BATCH, SROWS) + x.shape[1:])[:, SPAD:]


def kernel(x_prompt, x_sample, cache_fox_k, cache_fox_v, cache_fox_logf, state_mlstm_C, state_mlstm_n, state_mlstm_m, state_ret_S, state_lru_h, state_lru_conv, page_table, meta_tokens, w_in_even, b_mlstm_i, b_mlstm_f, b_fox_f, w_out_even, w_in_odd, ret_ln_g, conv_w, conv_b, lru_wa, lru_ba, lru_wx, lru_bx, lru_lambda, w_out_odd, norm_g, ffn_wg, ffn_wu, ffn_wd):
    n_pages = page_table.shape[1]
    past = n_pages * PAGE_SIZE
    n_phys = cache_fox_k.shape[0]

    hp = jnp.concatenate([jnp.zeros((BATCH, PADF, D_MODEL), F32),
                          jnp.broadcast_to(meta_tokens.astype(F32)[None], (BATCH, N_META, D_MODEL)),
                          x_prompt.astype(F32)], axis=1).reshape(SBASE, D_MODEL)
    hs = jnp.concatenate([jnp.zeros((DEC_BATCH, SPAD, D_MODEL), F32), x_sample.astype(F32)],
                         axis=1).reshape(DEC_BATCH * SROWS, D_MODEL)
    h = jnp.concatenate([hp, hs], axis=0)

    c = np.cumsum([0, 512, 512, 512, 512, 4, 4, 512, 512, 512, 8])
    gate_w = jnp.concatenate([w_in_even[:, c[4]:c[6]], w_in_even[:, c[9]:c[10]],
                              jnp.zeros((D_MODEL, LANES - 16), F32)], axis=1)
    w_even = _bf(jnp.concatenate([w_in_even[:, c[0]:c[4]], w_in_even[:, c[6]:c[9]], gate_w], axis=1))
    gate_b = jnp.concatenate([b_mlstm_i, b_mlstm_f, b_fox_f]).astype(F32)
    brow = jnp.concatenate([gate_b, jnp.zeros((LANES - 16,), F32)]).reshape(1, LANES)
    bcol = gate_b.reshape(16, 1)
    w_odd = _bf(w_in_odd)
    wo_even = _bf(w_out_even)
    wo_odd = _bf(w_out_odd)
    wg = _bf(ffn_wg)
    wu = _bf(ffn_wu)
    wd = _bf(ffn_wd)
    ng = norm_g.astype(F32).reshape(2, 4, 1, D_MODEL)

    zm, zf, zg = norm_matmul(h, ng[0, 0], w_even, (2048, 1536, LANES))
    g2, g2t = gates_activate(zg, zg[:, :16].T, brow, bcol)
    g2t_p = g2t[:, :SBASE].reshape(16, BATCH * NCH, CHUNK).transpose(1, 0, 2)
    g2t_s = g2t[:, SBASE:].reshape(16, DEC_BATCH, SROWS).transpose(1, 0, 2)

    zeros_c = jnp.zeros((BATCH, M_HEADS, HEAD_DIM, HEAD_DIM), F32)
    zeros_n = jnp.zeros((BATCH, M_HEADS, HEAD_DIM), F32)
    hm, p_c, p_n, p_m = mlstm_call(zm, g2, g2t_p, zeros_c, zeros_n, zeros_n, None,
                                   rows=CHUNK, n_chunks=NCH, row0_first=PADF, block0=0)
    m0_s = jnp.broadcast_to(state_mlstm_m.astype(F32)[:, :, None], (DEC_BATCH, M_HEADS, LANES))
    hm, s_c, s_n, s_m = mlstm_call(zm, g2, g2t_s, state_mlstm_C.astype(F32), state_mlstm_n.astype(F32), m0_s, hm,
                                   rows=SROWS, n_chunks=1, row0_first=SPAD, block0=SBASE // SROWS)

    fc, fr = fox_cumsum(g2, g2t)
    pairs = F_HEADS // 2
    fr_p = fr[:, 8:16].reshape(BATCH, pairs, 2, NCH, CHUNK).transpose(0, 1, 3, 2, 4)
    fc_p = fc[:, :, 8:16].reshape(BATCH, TP, pairs, 2).transpose(0, 2, 1, 3)
    hf = fox_prompt(zf, fr_p, fc_p)
    width = F_HEADS * F_DH
    suf, tot = lf_suffix(cache_fox_logf.astype(F32).reshape(n_phys, PAGE_SIZE * F_HEADS))
    zs = zf[SBASE:].reshape(DEC_BATCH, SROWS, 3, F_HEADS, F_DH)
    q2 = zs[:, SPAD:, 0].reshape(DEC_BATCH, DEC_SEQ * F_HEADS, F_DH)
    kn = zs[:, :, 1].reshape(DEC_BATCH, SROWS * F_HEADS, F_DH)
    vn = zs[:, :, 2].reshape(DEC_BATCH, SROWS * F_HEADS, F_DH)
    lfn = jnp.broadcast_to(g2[SBASE:, 8:16].reshape(DEC_BATCH, 1, SROWS * F_HEADS), (DEC_BATCH, 8, SROWS * F_HEADS))
    od = fox_decode(page_table, q2, kn, vn, lfn, cache_fox_k.astype(F32), cache_fox_v.astype(F32), suf, tot)
    hf_s = jnp.pad(od.reshape(DEC_BATCH, DEC_SEQ, width), ((0, 0), (SPAD, 0), (0, 0))).reshape(DEC_BATCH * SROWS, width)
    hf = lax.dynamic_update_slice(hf, hf_s, (SBASE, 0))

    h = proj_residual(hm, hf, wo_even[:512], wo_even[512:], ng[0, 1], h)
    h = ffn_residual(h, ng[0, 2], ng[0, 3], wg[0], wu[0], wd[0])

    zr, zl = norm_matmul(h, ng[1, 0], w_odd, (2048, 1024))
    cos_p, sin_p = _rope_tables(jnp.arange(TP) - PADF)
    cos_s, sin_s = _rope_tables(past + jnp.arange(SROWS) - SPAD)
    lng = ret_ln_g.astype(F32)
    hr, p_s = retention_call(zr, cos_p, sin_p, lng, jnp.zeros((BATCH, R_HEADS, HEAD_DIM, HEAD_DIM), F32), None,
                             rows=CHUNK, n_chunks=NCH, row0_first=PADF, block0=0)
    hr, s_s = retention_call(zr, cos_s, sin_s, lng, state_ret_S.astype(F32), hr,
                             rows=SROWS, n_chunks=1, row0_first=SPAD, block0=SBASE // SROWS)

    lru_args = (conv_w.astype(F32), conv_b.astype(F32).reshape(1, LRU_WIDTH),
                _block_diag_pairs(lru_wa), lru_ba.astype(F32).reshape(1, LRU_WIDTH),
                _block_diag_pairs(lru_wx), lru_bx.astype(F32).reshape(1, LRU_WIDTH),
                lru_lambda.astype(F32).reshape(1, LRU_WIDTH))
    hl, p_h, p_tail = lru_call(zl, jnp.zeros((CHUNK, LRU_WIDTH), F32), jnp.zeros((BATCH, 1, LRU_WIDTH), F32),
                               *lru_args, None, rows=CHUNK, n_chunks=NCH, row0_first=PADF, block0=0)
    pre_s = jnp.concatenate([jnp.zeros((DEC_BATCH, SPAD - (CONV_W - 1), LRU_WIDTH), F32),
                             state_lru_conv.astype(F32),
                             jnp.zeros((DEC_BATCH, DEC_SEQ, LRU_WIDTH), F32)], axis=1).reshape(-1, LRU_WIDTH)
    hl, s_h, s_tail = lru_call(zl, pre_s, state_lru_h.astype(F32).reshape(DEC_BATCH, 1, LRU_WIDTH),
                               *lru_args, hl, rows=SROWS, n_chunks=1, row0_first=SPAD, block0=SBASE // SROWS)

    h = proj_residual(hr, hl, wo_odd[:512], wo_odd[512:], ng[1, 1], h)
    h = ffn_residual(h, ng[1, 2], ng[1, 3], wg[1], wu[1], wd[1])

    y_prompt = h[:SBASE].reshape(BATCH, TP, D_MODEL)[:, CHUNK:]
    y_sample = _unpad_sample(h)
    fk = zf[:, 512:1024]
    fv = zf[:, 1024:1536]
    lf = g2[:, 8:16]
    heads = lambda x: x.reshape(x.shape[:2] + (F_HEADS, F_DH))
    nback = CONV_W - 1
    return (y_prompt, y_sample,
            heads(_unpad_prompt(fk)), heads(_unpad_prompt(fv)), _unpad_prompt(lf),
            heads(_unpad_sample(fk)), heads(_unpad_sample(fv)), _unpad_sample(lf),
            p_c, p_n, p_m[:, :, 0], s_c, s_n, s_m[:, :, 0],
            p_s, s_s,
            p_h[:, 0], s_h[:, 0], p_tail[:, 8 - nback:], s_tail[:, 8 - nback:])
```

```python
import functools
import math

import jax
import jax.numpy as jnp
import numpy as np
from jax import lax
from jax.experimental import pallas as pl
from jax.experimental.pallas import tpu as pltpu

F32 = jnp.float32
BF16 = jnp.bfloat16

D_MODEL = 1024
BATCH = 4
SEQ = 4096
DEC_BATCH = 32
DEC_SEQ = 4
PAGE_SIZE = 128
N_META = 16
CHUNK = 128
M_HEADS = 4
F_HEADS = 8
F_DH = 64
R_HEADS = 4
HEAD_DIM = 128
ROPE_BASE = 10000.0
LRU_WIDTH = 512
LRU_C = 8.0
CONV_W = 4
D_FF = 2816
EPS = 1e-6
NEG = -1e30

LANES = 128
PADF = CHUNK - N_META
TP = PADF + N_META + SEQ
NCH = TP // CHUNK
SROWS = 16
SPAD = SROWS - DEC_SEQ
SBASE = BATCH * TP
NP = SBASE + DEC_BATCH * SROWS
TM = 512
FFN_TF = D_FF // 2
FOX_TQ = 512
FOX_CK = 512
LOG2E = math.log2(math.e)
DEC_PG = 16
PRE_PG = 128


def _bf(x):
    return x.astype(BF16)


def _dot(a, b):
    return jnp.dot(a, b, preferred_element_type=F32)


def _dot_nt(a, b):
    return lax.dot_general(a, b, (((1,), (1,)), ((), ())), preferred_element_type=F32)


def _dot_tn(a, b):
    return lax.dot_general(a, b, (((0,), (0,)), ((), ())), preferred_element_type=F32)


def _split3(x):
    hi = _bf(x)
    r1 = x - hi.astype(F32)
    mid = _bf(r1)
    lo = _bf(r1 - mid.astype(F32))
    return hi, mid, lo


def _dot01_right(x, m01):
    hi, mid, lo = _split3(x)
    return _dot(hi, m01) + _dot(mid, m01) + _dot(lo, m01)


def _dot01_left(m01, x):
    hi, mid, lo = _split3(x)
    return _dot(m01, hi) + _dot(m01, mid) + _dot(m01, lo)


def _iota(shape, dim):
    return lax.broadcasted_iota(jnp.int32, shape, dim)


def _rms(x, g):
    ms = jnp.mean(x * x, axis=-1, keepdims=True)
    return x * lax.rsqrt(ms + EPS) * g


def _softplus(x):
    return jnp.maximum(x, 0.0) + jnp.log1p(jnp.exp(-jnp.abs(x)))


def _log_sigmoid(x):
    return -_softplus(-x)


def _sigmoid(x):
    return 1.0 / (1.0 + jnp.exp(-x))


def _nm_kernel(x_ref, g_ref, w_ref, *o_refs, splits):
    xn = _bf(_rms(x_ref[...], g_ref[...]))
    off = 0
    for o_ref, n in zip(o_refs, splits):
        o_ref[...] = _dot(xn, w_ref[:, off:off + n])
        off += n


def norm_matmul(x, g, w, splits):
    n_rows, k = x.shape
    return pl.pallas_call(
        functools.partial(_nm_kernel, splits=splits),
        grid=(n_rows // TM,),
        in_specs=[pl.BlockSpec((TM, k), lambda i: (i, 0)),
                  pl.BlockSpec((1, k), lambda i: (0, 0)),
                  pl.BlockSpec((k, sum(splits)), lambda i: (0, 0))],
        out_specs=[pl.BlockSpec((TM, n), lambda i: (i, 0)) for n in splits],
        out_shape=[jax.ShapeDtypeStruct((n_rows, n), F32) for n in splits],
        compiler_params=pltpu.CompilerParams(dimension_semantics=("parallel",)),
        name="norm_matmul",
    )(x, g, w)


def _proj_kernel(a1p_ref, a1s_ref, a2p_ref, a2s_ref, w1_ref, w2_ref, g_ref, h_ref, o_ref):
    last = pl.program_id(0) == pl.num_programs(0) - 1
    a1 = jnp.where(last, a1s_ref[...], a1p_ref[...])
    a2 = jnp.where(last, a2s_ref[...], a2p_ref[...])
    mix = _dot(_bf(a1), w1_ref[...]) + _dot(_bf(a2), w2_ref[...])
    o_ref[...] = h_ref[...] + _rms(mix, g_ref[...])


def proj_residual(a1p, a1s, a2p, a2s, w1, w2, g, h):
    assert a1p.shape[0] == SBASE and a1s.shape[0] == TM and SBASE % TM == 0 and h.shape[0] == SBASE + TM
    k1 = a1p.shape[1]
    k2 = a2p.shape[1]
    n_rows, d = h.shape
    n_prompt = SBASE // TM
    prompt_blk = lambda i: (jnp.minimum(i, n_prompt - 1), 0)
    const = lambda i: (0, 0)
    return pl.pallas_call(
        _proj_kernel,
        grid=(n_rows // TM,),
        in_specs=[pl.BlockSpec((TM, k1), prompt_blk),
                  pl.BlockSpec((TM, k1), const),
                  pl.BlockSpec((TM, k2), prompt_blk),
                  pl.BlockSpec((TM, k2), const),
                  pl.BlockSpec((k1, d), const),
                  pl.BlockSpec((k2, d), const),
                  pl.BlockSpec((1, d), const),
                  pl.BlockSpec((TM, d), lambda i: (i, 0))],
        out_specs=pl.BlockSpec((TM, d), lambda i: (i, 0)),
        out_shape=jax.ShapeDtypeStruct((n_rows, d), F32),
        compiler_params=pltpu.CompilerParams(dimension_semantics=("arbitrary",)),
        name="proj_residual",
    )(a1p, a1s, a2p, a2s, w1, w2, g, h)


def _ffn_kernel(h_ref, g2_ref, g3_ref, wg_ref, wu_ref, wd_ref, o_ref, xn_ref, acc_ref):
    f = pl.program_id(1)

    @pl.when(f == 0)
    def _():
        xn_ref[...] = _bf(_rms(h_ref[...], g2_ref[...]))
        acc_ref[...] = jnp.zeros(acc_ref.shape, F32)

    xn = xn_ref[...]
    gate = _dot(xn, wg_ref[...])
    up = _dot(xn, wu_ref[...])
    act = gate * _sigmoid(gate) * up
    acc_ref[...] += _dot(_bf(act), wd_ref[...])

    @pl.when(f == pl.num_programs(1) - 1)
    def _():
        o_ref[...] = h_ref[...] + _rms(acc_ref[...], g3_ref[...])


def ffn_residual(h, g2, g3, wg, wu, wd):
    n_rows, d = h.shape
    dff = wg.shape[1]
    return pl.pallas_call(
        _ffn_kernel,
        grid=(n_rows // TM, dff // FFN_TF),
        in_specs=[pl.BlockSpec((TM, d), lambda i, f: (i, 0)),
                  pl.BlockSpec((1, d), lambda i, f: (0, 0)),
                  pl.BlockSpec((1, d), lambda i, f: (0, 0)),
                  pl.BlockSpec((d, FFN_TF), lambda i, f: (0, f)),
                  pl.BlockSpec((d, FFN_TF), lambda i, f: (0, f)),
                  pl.BlockSpec((FFN_TF, d), lambda i, f: (f, 0))],
        out_specs=pl.BlockSpec((TM, d), lambda i, f: (i, 0)),
        out_shape=jax.ShapeDtypeStruct((n_rows, d), F32),
        scratch_shapes=[pltpu.VMEM((TM, d), BF16), pltpu.VMEM((TM, d), F32)],
        compiler_params=pltpu.CompilerParams(dimension_semantics=("parallel", "arbitrary")),
        name="ffn_residual",
    )(h, g2, g3, wg, wu, wd)


def _gates_kernel(z_ref, zt_ref, brow_ref, bcol_ref, o_ref, ot_ref):
    x = z_ref[...] + brow_ref[...]
    o_ref[...] = jnp.where(_iota(x.shape, 1) < M_HEADS, x, _log_sigmoid(x))
    xt = zt_ref[...] + bcol_ref[...]
    ot_ref[...] = jnp.where(_iota(xt.shape, 0) < M_HEADS, xt, _log_sigmoid(xt))


def gates_activate(zg, zgt, brow, bcol):
    n_rows = zg.shape[0]
    return pl.pallas_call(
        _gates_kernel,
        grid=(n_rows // TM,),
        in_specs=[pl.BlockSpec((TM, LANES), lambda i: (i, 0)),
                  pl.BlockSpec((16, TM), lambda i: (0, i)),
                  pl.BlockSpec((1, LANES), lambda i: (0, 0)),
                  pl.BlockSpec((16, 1), lambda i: (0, 0))],
        out_specs=[pl.BlockSpec((TM, LANES), lambda i: (i, 0)),
                   pl.BlockSpec((16, TM), lambda i: (0, i))],
        out_shape=[jax.ShapeDtypeStruct((n_rows, LANES), F32),
                   jax.ShapeDtypeStruct((16, n_rows), F32)],
        compiler_params=pltpu.CompilerParams(dimension_semantics=("parallel",)),
        name="gates_activate",
    )(zg, zgt, brow, bcol)


def _mlstm_chunk(q, k, v, logi_c, lf_c, logi_r, lf_r, c_state, n_state, m_state, row0, rows):
    ri = _iota((rows, rows), 0)
    ci = _iota((rows, rows), 1)
    valid_c = _iota((rows, 1), 0) >= row0
    valid_r = _iota((1, rows), 1) >= row0
    lf_c = jnp.where(valid_c, lf_c, 0.0)
    lf_r = jnp.where(valid_r, lf_r, 0.0)
    logi_c = jnp.where(valid_c, logi_c, NEG)
    logi_r = jnp.where(valid_r, logi_r, NEG)
    causal = ci <= ri
    b_c = jnp.sum(jnp.where(causal, lf_r, 0.0), axis=1, keepdims=True)
    b_r = jnp.sum(jnp.where(ri <= ci, lf_c, 0.0), axis=0, keepdims=True)
    log_d = jnp.where(causal, b_c - b_r + logi_r, -jnp.inf)
    m_inter = b_c + m_state
    m_t = jnp.maximum(m_inter, jnp.max(log_d, axis=1, keepdims=True))
    qb = _bf(q)
    kb = _bf(k)
    vb = _bf(v)
    w_intra = _dot_nt(qb, kb) * jnp.exp(log_d - m_t)
    w_inter = jnp.exp(m_inter - m_t)
    num = _dot(_bf(w_intra), vb) + w_inter * _dot(qb, _bf(c_state))
    den = jnp.sum(w_intra, axis=1, keepdims=True) + w_inter * jnp.sum(q * n_state, axis=1, keepdims=True)
    h = num / jnp.maximum(jnp.abs(den), jnp.exp(-m_t))
    m_new = m_t[rows - 1:rows, :]
    b_last = b_c[rows - 1:rows, :]
    w_src = jnp.exp(b_last - b_c + logi_c - m_new)
    decay = jnp.exp(b_last + m_state - m_new)
    ks = k * w_src
    c_new = decay * c_state + _dot_tn(_bf(ks), vb)
    n_new = decay * n_state + jnp.sum(ks, axis=0, keepdims=True)
    return h, c_new, n_new, m_new


def _mlstm_kernel(z_ref, g_ref, gt_ref, c0_ref, n0_ref, m0_ref, o_ref, c_ref, n_ref, m_ref, *, rows, row0_first):
    c = pl.program_id(1)

    @pl.when(c == 0)
    def _():
        c_ref[...] = c0_ref[...]
        n_ref[...] = n0_ref[...]
        m_ref[...] = m0_ref[...]

    row0 = jnp.where(c == 0, row0_first, 0)
    g = g_ref[...]
    gt = gt_ref[...]
    outs = []
    for h in range(M_HEADS):
        lo = h * HEAD_DIM
        q = z_ref[:, lo:lo + HEAD_DIM]
        k = z_ref[:, 512 + lo:512 + lo + HEAD_DIM] * (HEAD_DIM ** -0.5)
        v = z_ref[:, 1024 + lo:1024 + lo + HEAD_DIM]
        og = z_ref[:, 1536 + lo:1536 + lo + HEAD_DIM]
        hh, c_new, n_new, m_new = _mlstm_chunk(
            q, k, v,
            g[:, h:h + 1], g[:, M_HEADS + h:M_HEADS + h + 1],
            gt[h:h + 1, :], gt[M_HEADS + h:M_HEADS + h + 1, :],
            c_ref[0, h], n_ref[0, h:h + 1, :], m_ref[0, h:h + 1, 0:1], row0, rows)
        c_ref[0, h] = c_new
        n_ref[0, h:h + 1, :] = n_new
        m_ref[0, h:h + 1, :] = jnp.broadcast_to(m_new, (1, LANES))
        outs.append(_sigmoid(og) * hh)
    o_ref[...] = jnp.concatenate(outs, axis=1)


def _state_specs(heads):
    return [pl.BlockSpec((1, heads, HEAD_DIM, HEAD_DIM), lambda b, c: (b, 0, 0, 0)),
            pl.BlockSpec((1, heads, HEAD_DIM), lambda b, c: (b, 0, 0)),
            pl.BlockSpec((1, heads, LANES), lambda b, c: (b, 0, 0))]


def mlstm_call(z, g2, g2t, c0, n0, m0, *, rows, n_chunks, row0_first, block0):
    nb = c0.shape[0]
    blk = lambda b, c: (block0 + b * n_chunks + c, 0)
    blk_local = lambda b, c: (b * n_chunks + c, 0)
    return pl.pallas_call(
        functools.partial(_mlstm_kernel, rows=rows, row0_first=row0_first),
        grid=(nb, n_chunks),
        in_specs=[pl.BlockSpec((rows, 2048), blk),
                  pl.BlockSpec((rows, LANES), blk),
                  pl.BlockSpec((None, 16, rows), lambda b, c: (b * n_chunks + c, 0, 0))] + _state_specs(M_HEADS),
        out_specs=[pl.BlockSpec((rows, 512), blk_local)] + _state_specs(M_HEADS),
        out_shape=[jax.ShapeDtypeStruct((nb * n_chunks * rows, 512), F32),
                   jax.ShapeDtypeStruct((nb, M_HEADS, HEAD_DIM, HEAD_DIM), F32),
                   jax.ShapeDtypeStruct((nb, M_HEADS, HEAD_DIM), F32),
                   jax.ShapeDtypeStruct((nb, M_HEADS, LANES), F32)],
        compiler_params=pltpu.CompilerParams(dimension_semantics=("parallel", "arbitrary")),
        name="mlstm",
    )(z, g2, g2t, c0, n0, m0)


def _ret_log_gamma(h):
    return math.log1p(-(2.0 ** (-5.0 - h)))


def _ret_kernel(z_ref, cos_ref, sin_ref, lng_ref, s0_ref, o_ref, s_ref, *, rows, row0_first):
    c = pl.program_id(1)

    @pl.when(c == 0)
    def _():
        s_ref[...] = s0_ref[...]

    row0 = jnp.where(c == 0, row0_first, 0)
    n_valid = (rows - row0).astype(F32)
    cosf = cos_ref[...]
    sinf = sin_ref[...]
    ri = _iota((rows, rows), 0)
    ci = _iota((rows, rows), 1)
    diff = (ri - ci).astype(F32)
    rowi = _iota((rows, 1), 0)
    te = (rowi - row0).astype(F32)
    valid = rowi >= row0
    outs = []
    for h in range(R_HEADS):
        lg = _ret_log_gamma(h)
        lo = h * HEAD_DIM
        q = z_ref[:, lo:lo + HEAD_DIM]
        k = z_ref[:, 512 + lo:512 + lo + HEAD_DIM]
        v = jnp.where(valid, z_ref[:, 1024 + lo:1024 + lo + HEAD_DIM], 0.0)
        rg = z_ref[:, 1536 + lo:1536 + lo + HEAD_DIM]
        q = q * cosf + pltpu.roll(q, HEAD_DIM // 2, 1) * sinf
        k = (k * cosf + pltpu.roll(k, HEAD_DIM // 2, 1) * sinf) * (HEAD_DIM ** -0.5)
        s_state = s_ref[0, h]
        qb = _bf(q)
        vb = _bf(v)
        decay_m = jnp.where(diff >= 0, jnp.exp(diff * lg), 0.0)
        intra = _dot(_bf(_dot_nt(qb, _bf(k)) * decay_m), vb)
        inter = _dot(qb, _bf(s_state)) * jnp.exp((te + 1.0) * lg)
        w_src = jnp.exp((n_valid - 1.0 - te) * lg)
        s_ref[0, h] = jnp.exp(n_valid * lg) * s_state + _dot_tn(_bf(k * w_src), vb)
        hr = intra + inter
        mu = jnp.mean(hr, axis=-1, keepdims=True)
        var = jnp.mean(jnp.square(hr - mu), axis=-1, keepdims=True)
        hr = (hr - mu) * lax.rsqrt(var + EPS) * lng_ref[h:h + 1, :]
        outs.append(rg * _sigmoid(rg) * hr)
    o_ref[...] = jnp.concatenate(outs, axis=1)


def retention_call(z, cosf, sinf, lng, s0, *, rows, n_chunks, row0_first, block0):
    nb = s0.shape[0]
    blk = lambda b, c: (block0 + b * n_chunks + c, 0)
    sspec = pl.BlockSpec((1, R_HEADS, HEAD_DIM, HEAD_DIM), lambda b, c: (b, 0, 0, 0))
    return pl.pallas_call(
        functools.partial(_ret_kernel, rows=rows, row0_first=row0_first),
        grid=(nb, n_chunks),
        in_specs=[pl.BlockSpec((rows, 2048), blk),
                  pl.BlockSpec((rows, HEAD_DIM), lambda b, c: (c, 0)),
                  pl.BlockSpec((rows, HEAD_DIM), lambda b, c: (c, 0)),
                  pl.BlockSpec((R_HEADS, HEAD_DIM), lambda b, c: (0, 0)),
                  sspec],
        out_specs=[pl.BlockSpec((rows, 512), lambda b, c: (b * n_chunks + c, 0)), sspec],
        out_shape=[jax.ShapeDtypeStruct((nb * n_chunks * rows, 512), F32),
                   jax.ShapeDtypeStruct((nb, R_HEADS, HEAD_DIM, HEAD_DIM), F32)],
        compiler_params=pltpu.CompilerParams(dimension_semantics=("parallel", "arbitrary")),
        name="retention",
    )(z, cosf, sinf, lng, s0)


def _lru_kernel(z_ref, pre_ref, h0_ref, cw_ref, cb_ref, wa_ref, ba_ref, wx_ref, bx_ref, lam_ref,
                o_ref, hlast_ref, tail_ref, *, rows, row0_first):
    c = pl.program_id(1)

    @pl.when(c == 0)
    def _():
        hlast_ref[...] = h0_ref[...]
        tail_ref[...] = jnp.zeros(tail_ref.shape, F32)

    row0 = jnp.where(c == 0, row0_first, 0)
    rowi = _iota((rows, 1), 0)
    valid = rowi >= row0
    lx = jnp.where(valid, z_ref[:, 0:LRU_WIDTH], pre_ref[...])
    gate = z_ref[:, LRU_WIDTH:2 * LRU_WIDTH]
    tail = tail_ref[0]
    row8 = _iota((8, 1), 0)
    xc = cb_ref[...] + cw_ref[CONV_W - 1:CONV_W, :] * lx
    for j in range(1, CONV_W):
        rolled = pltpu.roll(lx, j, 0)
        first = jnp.where(row8 < j, pltpu.roll(tail, j, 0), rolled[0:8])
        shifted = jnp.concatenate([first, rolled[8:]], axis=0)
        xc = xc + cw_ref[CONV_W - 1 - j:CONV_W - j, :] * shifted
    tail_ref[0] = lx[rows - 8:rows]
    xcb = _bf(xc)
    pre_a = []
    pre_x = []
    for p in range(LRU_WIDTH // LANES):
        xs = xcb[:, p * LANES:(p + 1) * LANES]
        pre_a.append(_dot(xs, wa_ref[p]))
        pre_x.append(_dot(xs, wx_ref[p]))
    r = _sigmoid(jnp.concatenate(pre_a, axis=1) + ba_ref[...])
    ig = _sigmoid(jnp.concatenate(pre_x, axis=1) + bx_ref[...])
    log_a = -LRU_C * r * _softplus(-lam_ref[...])
    a = jnp.where(valid, jnp.exp(log_a), 1.0)
    one_minus_a2 = -jnp.tanh(log_a) * (jnp.exp(2.0 * log_a) + 1.0)
    bx = jnp.where(valid, jnp.sqrt(one_minus_a2) * (ig * xc), 0.0)
    shift = 1
    while shift < rows:
        keep = rowi >= shift
        a_sh = jnp.where(keep, pltpu.roll(a, shift, 0), 1.0)
        b_sh = jnp.where(keep, pltpu.roll(bx, shift, 0), 0.0)
        bx = a * b_sh + bx
        a = a * a_sh
        shift *= 2
    hs = a * hlast_ref[0] + bx
    hlast_ref[0] = hs[rows - 1:rows, :]
    gl = 0.5 * gate * (1.0 + jnp.tanh(math.sqrt(2.0 / math.pi) * (gate + 0.044715 * gate * gate * gate)))
    o_ref[...] = hs * gl


def lru_call(z, pre, h0, cw, cb, wa2, ba, wx2, bx, lam, *, rows, n_chunks, row0_first, block0):
    nb = h0.shape[0]
    blk = lambda b, c: (block0 + b * n_chunks + c, 0)
    blk_local = lambda b, c: (b * n_chunks + c, 0)
    const2 = lambda b, c: (0, 0)
    in_specs = [pl.BlockSpec((rows, 1024), blk),
                pl.BlockSpec((rows, LRU_WIDTH), const2 if pre.shape[0] == rows else blk_local),
                pl.BlockSpec((1, 1, LRU_WIDTH), lambda b, c: (b, 0, 0)),
                pl.BlockSpec((CONV_W, LRU_WIDTH), const2),
                pl.BlockSpec((1, LRU_WIDTH), const2),
                pl.BlockSpec((LRU_WIDTH // LANES, LANES, LANES), lambda b, c: (0, 0, 0)),
                pl.BlockSpec((1, LRU_WIDTH), const2),
                pl.BlockSpec((LRU_WIDTH // LANES, LANES, LANES), lambda b, c: (0, 0, 0)),
                pl.BlockSpec((1, LRU_WIDTH), const2),
                pl.BlockSpec((1, LRU_WIDTH), const2)]
    return pl.pallas_call(
        functools.partial(_lru_kernel, rows=rows, row0_first=row0_first),
        grid=(nb, n_chunks),
        in_specs=in_specs,
        out_specs=[pl.BlockSpec((rows, LRU_WIDTH), blk_local),
                   pl.BlockSpec((1, 1, LRU_WIDTH), lambda b, c: (b, 0, 0)),
                   pl.BlockSpec((1, 8, LRU_WIDTH), lambda b, c: (b, 0, 0))],
        out_shape=[jax.ShapeDtypeStruct((nb * n_chunks * rows, LRU_WIDTH), F32),
                   jax.ShapeDtypeStruct((nb, 1, LRU_WIDTH), F32),
                   jax.ShapeDtypeStruct((nb, 8, LRU_WIDTH), F32)],
        compiler_params=pltpu.CompilerParams(dimension_semantics=("parallel", "arbitrary")),
        name="rglru",
    )(z, pre, h0, cw, cb, wa2, ba, wx2, bx, lam)


def _fox_cumsum_kernel(g_ref, gt_ref, fc_ref, fr_ref):
    ri = _iota((CHUNK, CHUNK), 0)
    ci = _iota((CHUNK, CHUNK), 1)
    lower = jnp.where(ci <= ri, 1.0, 0.0).astype(BF16)
    upper = jnp.where(ri <= ci, 1.0, 0.0).astype(BF16)
    carry_r = jnp.zeros((1, LANES), F32)
    carry_c = jnp.zeros((16, 1), F32)
    for blk in range(NCH):
        x = g_ref[blk * CHUNK:(blk + 1) * CHUNK, :]
        xt = gt_ref[:, blk * CHUNK:(blk + 1) * CHUNK]
        if blk == 0:
            x = jnp.where(_iota((CHUNK, 1), 0) >= PADF, x, 0.0)
            xt = jnp.where(_iota((1, CHUNK), 1) >= PADF, xt, 0.0)
        cs = _dot01_left(lower, x) + carry_r
        cst = _dot01_right(xt, upper) + carry_c
        carry_r = cs[CHUNK - 1:CHUNK, :]
        carry_c = cst[:, CHUNK - 1:CHUNK]
        if blk == 0:
            cst = jnp.where(_iota((1, CHUNK), 1) >= PADF, cst, -NEG)
        fc_ref[0, blk * CHUNK:(blk + 1) * CHUNK, :] = cs
        fr_ref[0, :, blk * CHUNK:(blk + 1) * CHUNK] = cst


def fox_cumsum(g2, g2t):
    return pl.pallas_call(
        _fox_cumsum_kernel,
        grid=(BATCH,),
        in_specs=[pl.BlockSpec((TP, LANES), lambda b: (b, 0)),
                  pl.BlockSpec((16, TP), lambda b: (0, b))],
        out_specs=[pl.BlockSpec((1, TP, LANES), lambda b: (b, 0, 0)),
                   pl.BlockSpec((1, 16, TP), lambda b: (b, 0, 0))],
        out_shape=[jax.ShapeDtypeStruct((BATCH, TP, LANES), F32),
                   jax.ShapeDtypeStruct((BATCH, 16, TP), F32)],
        compiler_params=pltpu.CompilerParams(dimension_semantics=("parallel",)),
        name="fox_cumsum",
    )(g2, g2t)


def _fox_prompt_kernel(q_ref, k_ref, v_ref, fr_ref, fc_ref, o_ref, q_s, kt_s, v_s, fq_s, m_s, l_s, acc_s):
    i = pl.program_id(2)
    is_a = _iota((1, LANES), 1) < F_DH
    row_a = _iota((LANES, 1), 0) < F_DH

    @pl.when(i == 0)
    def _():
        def fill(blk, carry):
            kt = k_ref[pl.ds(pl.multiple_of(blk * CHUNK, CHUNK), CHUNK), :].T
            kt_s[0, blk] = _bf(jnp.where(row_a, kt, 0.0))
            kt_s[1, blk] = _bf(jnp.where(row_a, 0.0, kt))
            return carry

        lax.fori_loop(0, NCH, fill, 0)
        v_s[...] = _bf(v_ref[...])
        q_s[...] = _bf(q_ref[...] * (F_DH ** -0.5 * LOG2E))
        for hh in range(2):
            fq_s[hh] = jnp.broadcast_to(fc_ref[0, 0, :, hh:hh + 1] * LOG2E, (TP, LANES))

    def init():
        m_s[...] = jnp.full(m_s.shape, NEG, F32)
        l_s[...] = jnp.zeros(l_s.shape, F32)
        acc_s[...] = jnp.zeros(acc_s.shape, F32)

    def chunk(q0, tq, kblk, nblk, causal):
        ck = nblk * CHUNK
        k0 = kblk * CHUNK if isinstance(kblk, int) else pl.multiple_of(kblk * CHUNK, CHUNK)
        qb = q_s[pl.ds(q0, tq), :]
        vb = v_s[pl.ds(k0, ck), :]
        for hh in range(2):
            kt = jnp.concatenate([kt_s[hh, kblk + t] for t in range(nblk)], axis=1)
            fk = jnp.concatenate([fr_ref[0, 0, kblk + t, hh:hh + 1, :] for t in range(nblk)], axis=1)
            u = _dot(qb, kt) - fk * LOG2E
            if causal:
                u = jnp.where(k0 + _iota((tq, ck), 1) <= q0 + _iota((tq, ck), 0), u, NEG)
            fq = fq_s[hh, pl.ds(q0, tq), :]
            m_old = m_s[hh, 0:tq, :]
            m_new = jnp.maximum(m_old, jnp.max(u, axis=1, keepdims=True) + fq)
            alpha = jnp.exp2(m_old - m_new)
            p = jnp.exp2(u + jnp.concatenate([fq - m_new] * nblk, axis=1))
            l_s[hh, 0:tq, :] = alpha * l_s[hh, 0:tq, :] + jnp.sum(p, axis=1, keepdims=True)
            acc_s[hh, 0:tq, :] = alpha * acc_s[hh, 0:tq, :] + _dot(_bf(p), vb)
            m_s[hh, 0:tq, :] = m_new

    def finalize(q0, tq):
        oa = acc_s[0, 0:tq, :] / l_s[0, 0:tq, :]
        ob = acc_s[1, 0:tq, :] / l_s[1, 0:tq, :]
        o_ref[pl.ds(q0, tq), :] = jnp.where(is_a, oa, ob)

    @pl.when(i == 0)
    def _():
        init()
        chunk(0, CHUNK, 0, 1, True)
        finalize(0, CHUNK)

    nck = FOX_CK // CHUNK
    nqb = FOX_TQ // CHUNK
    qblk = 1 + i * nqb
    q0 = pl.multiple_of(qblk * CHUNK, CHUNK)
    init()
    chunk(q0, FOX_TQ, 0, 1, False)

    def body(j, carry):
        chunk(q0, FOX_TQ, 1 + j * nck, nck, False)
        return carry

    lax.fori_loop(0, (FOX_TQ // FOX_CK) * i, body, 0)
    for d in range(FOX_TQ // FOX_CK):
        chunk(q0, FOX_TQ, qblk + d * nck, nck, True)
    finalize(q0, FOX_TQ)


def fox_prompt(zf, fr, fc):
    nq = (TP - CHUNK) // FOX_TQ
    pairs = F_HEADS // 2
    return pl.pallas_call(
        _fox_prompt_kernel,
        grid=(BATCH, pairs, nq),
        in_specs=[pl.BlockSpec((TP, LANES), lambda b, p, i: (b, p)),
                  pl.BlockSpec((TP, LANES), lambda b, p, i: (b, pairs + p)),
                  pl.BlockSpec((TP, LANES), lambda b, p, i: (b, 2 * pairs + p)),
                  pl.BlockSpec((1, 1, NCH, 2, LANES), lambda b, p, i: (b, p, 0, 0, 0)),
                  pl.BlockSpec((1, 1, TP, 2), lambda b, p, i: (b, p, 0, 0))],
        out_specs=pl.BlockSpec((TP, LANES), lambda b, p, i: (b, p)),
        out_shape=jax.ShapeDtypeStruct((SBASE, 512), F32),
        scratch_shapes=[pltpu.VMEM((TP, LANES), BF16),
                        pltpu.VMEM((2, NCH, LANES, CHUNK), BF16),
                        pltpu.VMEM((TP, LANES), BF16),
                        pltpu.VMEM((2, TP, LANES), F32),
                        pltpu.VMEM((2, FOX_TQ, LANES), F32),
                        pltpu.VMEM((2, FOX_TQ, LANES), F32),
                        pltpu.VMEM((2, FOX_TQ, LANES), F32)],
        compiler_params=pltpu.CompilerParams(dimension_semantics=("parallel", "parallel", "arbitrary")),
        name="fox_prompt",
    )(zf, zf, zf, fr, fc)


def _lf_suffix_kernel(x_ref, suf_ref, tot_ref):
    ri = _iota((PAGE_SIZE, PAGE_SIZE), 0)
    ci = _iota((PAGE_SIZE, PAGE_SIZE), 1)
    after = jnp.where(ri > ci, 1.0, 0.0).astype(BF16)
    ones = jnp.ones((PAGE_SIZE, PAGE_SIZE), BF16)
    x = x_ref[...].reshape(PRE_PG * F_HEADS, PAGE_SIZE)
    suf_ref[...] = _dot01_right(x, after).reshape(PRE_PG, F_HEADS, PAGE_SIZE)
    tot_ref[...] = _dot01_right(x, ones).reshape(PRE_PG, F_HEADS, PAGE_SIZE)


def lf_suffix(lft):
    n_phys = lft.shape[0]
    spec = pl.BlockSpec((PRE_PG, F_HEADS, PAGE_SIZE), lambda i: (i, 0, 0))
    return pl.pallas_call(
        _lf_suffix_kernel,
        grid=(n_phys // PRE_PG,),
        in_specs=[spec],
        out_specs=[spec, spec],
        out_shape=[jax.ShapeDtypeStruct(lft.shape, F32)] * 2,
        compiler_params=pltpu.CompilerParams(dimension_semantics=("parallel",)),
        name="lf_suffix",
    )(lft)


def _fox_decode_kernel(pt_ref, zq_ref, zk_ref, zv_ref, gt_ref, *rest):
    k_refs = rest[0:DEC_PG]
    v_refs = rest[DEC_PG:2 * DEC_PG]
    suf_refs = rest[2 * DEC_PG:3 * DEC_PG]
    tot_refs = rest[3 * DEC_PG:4 * DEC_PG]
    o_ref = rest[4 * DEC_PG]
    q2_s, m_s, l_s, acc_s, run_s, nc_s = rest[4 * DEC_PG + 1:]
    j = pl.program_id(1)
    n_rows = DEC_SEQ * F_HEADS
    width = F_HEADS * F_DH

    def update(u, v, v_feature_major):
        m_old = m_s[...]
        m_new = jnp.maximum(m_old, jnp.max(u, axis=1, keepdims=True))
        alpha = jnp.exp(m_old - m_new)
        p = jnp.exp(u - m_new[:, 0:u.shape[1]])
        pv = _dot_nt(_bf(p), v) if v_feature_major else _dot(_bf(p), v)
        l_s[...] = alpha * l_s[...] + jnp.sum(p, axis=1, keepdims=True)
        acc_s[...] = jnp.concatenate([alpha] * (width // LANES), axis=1) * acc_s[...] + pv
        m_s[...] = m_new

    @pl.when(j == 0)
    def _():
        m_s[...] = jnp.full(m_s.shape, NEG, F32)
        l_s[...] = jnp.zeros(l_s.shape, F32)
        acc_s[...] = jnp.zeros(acc_s.shape, F32)
        run_s[...] = jnp.zeros(run_s.shape, F32)
        head_mask = _iota((F_HEADS, width), 1) // F_DH == _iota((F_HEADS, width), 0)
        q16 = zq_ref[...] * (F_DH ** -0.5)
        q2 = _bf(jnp.concatenate(
            [jnp.where(head_mask, q16[SPAD + t:SPAD + t + 1, :], 0.0) for t in range(DEC_SEQ)], axis=0))
        q2_s[...] = q2
        lf_new = gt_ref[F_HEADS:2 * F_HEADS, :]
        lane = _iota((1, SROWS), 1)
        cum = jnp.zeros((F_HEADS, SROWS), F32)
        for t in range(DEC_SEQ):
            cum = cum + jnp.where(lane >= SPAD + t, lf_new[:, SPAD + t:SPAD + t + 1], 0.0)
        nc = jnp.concatenate([cum[:, SPAD + t:SPAD + t + 1] for t in range(DEC_SEQ)], axis=0)
        nc_s[...] = jnp.broadcast_to(nc, nc_s.shape)
        cum4 = jnp.concatenate([cum] * DEC_SEQ, axis=0)
        qi = _iota((n_rows, SROWS), 0) // F_HEADS
        kj = _iota((n_rows, SROWS), 1) - SPAD
        u = jnp.where((kj >= 0) & (kj <= qi), _dot_nt(q2, _bf(zk_ref[...])) + (nc - cum4), NEG)
        update(u, _bf(zv_ref[...]), False)

    q2 = q2_s[...]
    nc = nc_s[...]
    run = run_s[...]
    us = []
    for i in range(DEC_PG):
        later = run + suf_refs[i][0]
        bias = jnp.concatenate([later] * DEC_SEQ, axis=0) + nc
        us.append(_dot(q2, _bf(k_refs[i][0].reshape(width, PAGE_SIZE))) + bias)
        run = run + tot_refs[i][0]
    run_s[...] = run
    u = jnp.concatenate(us, axis=1)
    m_old = m_s[...]
    m_new = jnp.maximum(m_old, jnp.max(u, axis=1, keepdims=True))
    alpha = jnp.exp(m_old - m_new)
    p = jnp.exp(u - jnp.concatenate([m_new] * DEC_PG, axis=1))
    l_s[...] = alpha * l_s[...] + jnp.sum(p, axis=1, keepdims=True)
    pv = _dot_nt(_bf(p[:, 0:PAGE_SIZE]), _bf(v_refs[0][0].reshape(width, PAGE_SIZE)))
    for i in range(1, DEC_PG):
        pv = pv + _dot_nt(_bf(p[:, i * PAGE_SIZE:(i + 1) * PAGE_SIZE]), _bf(v_refs[i][0].reshape(width, PAGE_SIZE)))
    acc_s[...] = jnp.concatenate([alpha] * (width // LANES), axis=1) * acc_s[...] + pv
    m_s[...] = m_new

    @pl.when(j == pl.num_programs(1) - 1)
    def _():
        head_mask = (_iota((n_rows, width), 1) // F_DH) == (_iota((n_rows, width), 0) % F_HEADS)
        o2 = jnp.where(head_mask, acc_s[...] / jnp.concatenate([l_s[...]] * (width // LANES), axis=1), 0.0)
        o_ref[...] = jnp.zeros(o_ref.shape, F32)
        for t in range(DEC_SEQ):
            o_ref[SPAD + t:SPAD + t + 1, :] = jnp.sum(o2[t * F_HEADS:(t + 1) * F_HEADS, :], axis=0, keepdims=True)


def fox_decode(page_table, zf, g2t_s, cache_kt, cache_vt, suf, tot):
    n_pages = page_table.shape[1]
    width = F_HEADS * F_DH
    sblk = SBASE // SROWS

    def page_map(i):
        return lambda b, j, pt: (pt[b, n_pages - 1 - (j * DEC_PG + i)], 0, 0, 0)

    def page_map3(i):
        return lambda b, j, pt: (pt[b, n_pages - 1 - (j * DEC_PG + i)], 0, 0)

    in_specs = [pl.BlockSpec((SROWS, width), lambda b, j, pt: (sblk + b, 0)),
                pl.BlockSpec((SROWS, width), lambda b, j, pt: (sblk + b, 1)),
                pl.BlockSpec((SROWS, width), lambda b, j, pt: (sblk + b, 2)),
                pl.BlockSpec((None, 16, SROWS), lambda b, j, pt: (b, 0, 0))]
    in_specs += [pl.BlockSpec((1, F_HEADS, F_DH, PAGE_SIZE), page_map(i)) for i in range(DEC_PG)]
    in_specs += [pl.BlockSpec((1, F_HEADS, F_DH, PAGE_SIZE), page_map(i)) for i in range(DEC_PG)]
    in_specs += [pl.BlockSpec((1, F_HEADS, PAGE_SIZE), page_map3(i)) for i in range(DEC_PG)]
    in_specs += [pl.BlockSpec((1, F_HEADS, PAGE_SIZE), page_map3(i)) for i in range(DEC_PG)]
    args = [zf, zf, zf, g2t_s] + [cache_kt] * DEC_PG + [cache_vt] * DEC_PG + [suf] * DEC_PG + [tot] * DEC_PG
    n_rows = DEC_SEQ * F_HEADS
    grid_spec = pltpu.PrefetchScalarGridSpec(
        num_scalar_prefetch=1,
        grid=(DEC_BATCH, n_pages // DEC_PG),
        in_specs=in_specs,
        out_specs=pl.BlockSpec((SROWS, width), lambda b, j, pt: (b, 0)),
        scratch_shapes=[pltpu.VMEM((n_rows, width), BF16),
                        pltpu.VMEM((n_rows, LANES), F32),
                        pltpu.VMEM((n_rows, LANES), F32),
                        pltpu.VMEM((n_rows, width), F32),
                        pltpu.VMEM((F_HEADS, PAGE_SIZE), F32),
                        pltpu.VMEM((n_rows, LANES), F32)])
    return pl.pallas_call(
        _fox_decode_kernel,
        grid_spec=grid_spec,
        out_shape=jax.ShapeDtypeStruct((DEC_BATCH * SROWS, width), F32),
        compiler_params=pltpu.CompilerParams(dimension_semantics=("parallel", "arbitrary")),
        name="fox_decode",
    )(page_table, *args)


def _block_diag_pairs(w):
    z = jnp.zeros((LANES // 2, LANES // 2), w.dtype)
    pairs = [jnp.block([[w[2 * p], z], [z, w[2 * p + 1]]]) for p in range(w.shape[0] // 2)]
    return _bf(jnp.stack(pairs))


def _rope_tables(pos):
    half = HEAD_DIM // 2
    freq = ROPE_BASE ** (-jnp.arange(half, dtype=F32) / half)
    ang = pos.astype(F32)[:, None] * freq[None, :]
    cos, sin = jnp.cos(ang), jnp.sin(ang)
    return jnp.concatenate([cos, cos], axis=1), jnp.concatenate([-sin, sin], axis=1)


def _unpad_prompt(x):
    return x[:SBASE].reshape((BATCH, TP) + x.shape[1:])[:, PADF:]


def _unpad_sample(x):
    return x[SBASE:].reshape((DEC_BATCH, SROWS) + x.shape[1:])[:, SPAD:]


def kernel(x_prompt, x_sample, cache_fox_k, cache_fox_v, cache_fox_logf, state_mlstm_C, state_mlstm_n, state_mlstm_m, state_ret_S, state_lru_h, state_lru_conv, page_table, meta_tokens, w_in_even, b_mlstm_i, b_mlstm_f, b_fox_f, w_out_even, w_in_odd, ret_ln_g, conv_w, conv_b, lru_wa, lru_ba, lru_wx, lru_bx, lru_lambda, w_out_odd, norm_g, ffn_wg, ffn_wu, ffn_wd):
    n_pages = page_table.shape[1]
    past = n_pages * PAGE_SIZE
    n_phys = cache_fox_k.shape[0]

    hp = jnp.concatenate([jnp.zeros((BATCH, PADF, D_MODEL), F32),
                          jnp.broadcast_to(meta_tokens.astype(F32)[None], (BATCH, N_META, D_MODEL)),
                          x_prompt.astype(F32)], axis=1).reshape(SBASE, D_MODEL)
    hs = jnp.concatenate([jnp.zeros((DEC_BATCH, SPAD, D_MODEL), F32), x_sample.astype(F32)],
                         axis=1).reshape(DEC_BATCH * SROWS, D_MODEL)
    h = jnp.concatenate([hp, hs], axis=0)

    c = np.cumsum([0, 512, 512, 512, 512, 4, 4, 512, 512, 512, 8])
    gate_w = jnp.concatenate([w_in_even[:, c[4]:c[6]], w_in_even[:, c[9]:c[10]],
                              jnp.zeros((D_MODEL, LANES - 16), F32)], axis=1)
    w_even = _bf(jnp.concatenate([w_in_even[:, c[0]:c[4]], w_in_even[:, c[6]:c[9]], gate_w], axis=1))
    gate_b = jnp.concatenate([b_mlstm_i, b_mlstm_f, b_fox_f]).astype(F32)
    brow = jnp.concatenate([gate_b, jnp.zeros((LANES - 16,), F32)]).reshape(1, LANES)
    bcol = gate_b.reshape(16, 1)
    w_odd = _bf(w_in_odd)
    wo_even = _bf(w_out_even)
    wo_odd = _bf(w_out_odd)
    wg = _bf(ffn_wg)
    wu = _bf(ffn_wu)
    wd = _bf(ffn_wd)
    ng = norm_g.astype(F32).reshape(2, 4, 1, D_MODEL)

    zm, zf, zg = norm_matmul(h, ng[0, 0], w_even, (2048, 1536, LANES))
    g2, g2t = gates_activate(zg, zg[:, :16].T, brow, bcol)
    g2t_p = g2t[:, :SBASE].reshape(16, BATCH * NCH, CHUNK).transpose(1, 0, 2)
    g2t_s = g2t[:, SBASE:].reshape(16, DEC_BATCH, SROWS).transpose(1, 0, 2)

    zeros_c = jnp.zeros((BATCH, M_HEADS, HEAD_DIM, HEAD_DIM), F32)
    zeros_n = jnp.zeros((BATCH, M_HEADS, HEAD_DIM), F32)
    hm_p, p_c, p_n, p_m = mlstm_call(zm, g2, g2t_p, zeros_c, zeros_n, zeros_n,
                                     rows=CHUNK, n_chunks=NCH, row0_first=PADF, block0=0)
    m0_s = jnp.broadcast_to(state_mlstm_m.astype(F32)[:, :, None], (DEC_BATCH, M_HEADS, LANES))
    hm_s, s_c, s_n, s_m = mlstm_call(zm, g2, g2t_s, state_mlstm_C.astype(F32), state_mlstm_n.astype(F32), m0_s,
                                     rows=SROWS, n_chunks=1, row0_first=SPAD, block0=SBASE // SROWS)

    fc, fr = fox_cumsum(g2, g2t)
    pairs = F_HEADS // 2
    fr_p = fr[:, 8:16].reshape(BATCH, pairs, 2, NCH, CHUNK).transpose(0, 1, 3, 2, 4)
    fc_p = fc[:, :, 8:16].reshape(BATCH, TP, pairs, 2).transpose(0, 2, 1, 3)
    hf_p = fox_prompt(zf, fr_p, fc_p)
    suf, tot = lf_suffix(cache_fox_logf.astype(F32).transpose(0, 2, 1))
    hf_s = fox_decode(page_table, zf, g2t_s, cache_fox_k.astype(F32).transpose(0, 2, 3, 1),
                      cache_fox_v.astype(F32).transpose(0, 2, 3, 1), suf, tot)

    h = proj_residual(hm_p, hm_s, hf_p, hf_s, wo_even[:512], wo_even[512:], ng[0, 1], h)
    h = ffn_residual(h, ng[0, 2], ng[0, 3], wg[0], wu[0], wd[0])

    zr, zl = norm_matmul(h, ng[1, 0], w_odd, (2048, 1024))
    cos_p, sin_p = _rope_tables(jnp.arange(TP) - PADF)
    cos_s, sin_s = _rope_tables(past + jnp.arange(SROWS) - SPAD)
    lng = ret_ln_g.astype(F32)
    hr_p, p_s = retention_call(zr, cos_p, sin_p, lng, jnp.zeros((BATCH, R_HEADS, HEAD_DIM, HEAD_DIM), F32),
                               rows=CHUNK, n_chunks=NCH, row0_first=PADF, block0=0)
    hr_s, s_s = retention_call(zr, cos_s, sin_s, lng, state_ret_S.astype(F32),
                               rows=SROWS, n_chunks=1, row0_first=SPAD, block0=SBASE // SROWS)

    lru_args = (conv_w.astype(F32), conv_b.astype(F32).reshape(1, LRU_WIDTH),
                _block_diag_pairs(lru_wa), lru_ba.astype(F32).reshape(1, LRU_WIDTH),
                _block_diag_pairs(lru_wx), lru_bx.astype(F32).reshape(1, LRU_WIDTH),
                lru_lambda.astype(F32).reshape(1, LRU_WIDTH))
    hl_p, p_h, p_tail = lru_call(zl, jnp.zeros((CHUNK, LRU_WIDTH), F32), jnp.zeros((BATCH, 1, LRU_WIDTH), F32),
                                 *lru_args, rows=CHUNK, n_chunks=NCH, row0_first=PADF, block0=0)
    pre_s = jnp.concatenate([jnp.zeros((DEC_BATCH, SPAD - (CONV_W - 1), LRU_WIDTH), F32),
                             state_lru_conv.astype(F32),
                             jnp.zeros((DEC_BATCH, DEC_SEQ, LRU_WIDTH), F32)], axis=1).reshape(-1, LRU_WIDTH)
    hl_s, s_h, s_tail = lru_call(zl, pre_s, state_lru_h.astype(F32).reshape(DEC_BATCH, 1, LRU_WIDTH),
                                 *lru_args, rows=SROWS, n_chunks=1, row0_first=SPAD, block0=SBASE // SROWS)

    h = proj_residual(hr_p, hr_s, hl_p, hl_s, wo_odd[:512], wo_odd[512:], ng[1, 1], h)
    h = ffn_residual(h, ng[1, 2], ng[1, 3], wg[1], wu[1], wd[1])

    y_prompt = h[:SBASE].reshape(BATCH, TP, D_MODEL)[:, CHUNK:]
    y_sample = _unpad_sample(h)
    fk = zf[:, 512:1024]
    fv = zf[:, 1024:1536]
    lf = g2[:, 8:16]
    heads = lambda x: x.reshape(x.shape[:2] + (F_HEADS, F_DH))
    nback = CONV_W - 1
    return (y_prompt, y_sample,
            heads(_unpad_prompt(fk)), heads(_unpad_prompt(fv)), _unpad_prompt(lf),
            heads(_unpad_sample(fk)), heads(_unpad_sample(fv)), _unpad_sample(lf),
            p_c, p_n, p_m[:, :, 0], s_c, s_n, s_m[:, :, 0],
            p_s, s_s,
            p_h[:, 0], s_h[:, 0], p_tail[:, 8 - nback:], s_tail[:, 8 - nback:])
```

```python
import functools
import math

import jax
import jax.numpy as jnp
import numpy as np
from jax import lax
from jax.experimental import pallas as pl
from jax.experimental.pallas import tpu as pltpu

F32 = jnp.float32
BF16 = jnp.bfloat16

D_MODEL = 1024
BATCH = 4
SEQ = 4096
DEC_BATCH = 32
DEC_SEQ = 4
PAGE_SIZE = 128
N_META = 16
CHUNK = 128
M_HEADS = 4
F_HEADS = 8
F_DH = 64
R_HEADS = 4
HEAD_DIM = 128
ROPE_BASE = 10000.0
LRU_WIDTH = 512
LRU_C = 8.0
CONV_W = 4
D_FF = 2816
EPS = 1e-6
NEG = -1e30

LANES = 128
PADF = CHUNK - N_META
TP = PADF + N_META + SEQ
NCH = TP // CHUNK
SROWS = 16
SPAD = SROWS - DEC_SEQ
SBASE = BATCH * TP
NP = SBASE + DEC_BATCH * SROWS
TM = 512
FFN_TF = D_FF // 2
FOX_TQ = 512
FOX_CK = 512
LOG2E = math.log2(math.e)
DEC_PG = 16
PRE_PG = 128


def _bf(x):
    return x.astype(BF16)


def _dot(a, b):
    return jnp.dot(a, b, preferred_element_type=F32)


def _dot_nt(a, b):
    return lax.dot_general(a, b, (((1,), (1,)), ((), ())), preferred_element_type=F32)


def _dot_tn(a, b):
    return lax.dot_general(a, b, (((0,), (0,)), ((), ())), preferred_element_type=F32)


def _split3(x):
    hi = _bf(x)
    r1 = x - hi.astype(F32)
    mid = _bf(r1)
    lo = _bf(r1 - mid.astype(F32))
    return hi, mid, lo


def _dot01_right(x, m01):
    hi, mid, lo = _split3(x)
    return _dot(hi, m01) + _dot(mid, m01) + _dot(lo, m01)


def _dot01_left(m01, x):
    hi, mid, lo = _split3(x)
    return _dot(m01, hi) + _dot(m01, mid) + _dot(m01, lo)


def _iota(shape, dim):
    return lax.broadcasted_iota(jnp.int32, shape, dim)


def _rms(x, g):
    ms = jnp.mean(x * x, axis=-1, keepdims=True)
    return x * lax.rsqrt(ms + EPS) * g


def _softplus(x):
    return jnp.maximum(x, 0.0) + jnp.log1p(jnp.exp(-jnp.abs(x)))


def _log_sigmoid(x):
    return -_softplus(-x)


def _sigmoid(x):
    return 1.0 / (1.0 + jnp.exp(-x))


def _nm_kernel(x_ref, g_ref, w_ref, *o_refs, splits):
    xn = _bf(_rms(x_ref[...], g_ref[...]))
    off = 0
    for o_ref, n in zip(o_refs, splits):
        o_ref[...] = _dot(xn, w_ref[:, off:off + n])
        off += n


def norm_matmul(x, g, w, splits):
    n_rows, k = x.shape
    return pl.pallas_call(
        functools.partial(_nm_kernel, splits=splits),
        grid=(n_rows // TM,),
        in_specs=[pl.BlockSpec((TM, k), lambda i: (i, 0)),
                  pl.BlockSpec((1, k), lambda i: (0, 0)),
                  pl.BlockSpec((k, sum(splits)), lambda i: (0, 0))],
        out_specs=[pl.BlockSpec((TM, n), lambda i: (i, 0)) for n in splits],
        out_shape=[jax.ShapeDtypeStruct((n_rows, n), F32) for n in splits],
        compiler_params=pltpu.CompilerParams(dimension_semantics=("parallel",)),
        name="norm_matmul",
    )(x, g, w)


def _proj_kernel(a1p_ref, a1s_ref, a2p_ref, a2s_ref, w1_ref, w2_ref, g_ref, h_ref, o_ref):
    last = pl.program_id(0) == pl.num_programs(0) - 1
    a1 = jnp.where(last, a1s_ref[...], a1p_ref[...])
    a2 = jnp.where(last, a2s_ref[...], a2p_ref[...])
    mix = _dot(_bf(a1), w1_ref[...]) + _dot(_bf(a2), w2_ref[...])
    o_ref[...] = h_ref[...] + _rms(mix, g_ref[...])


def proj_residual(a1p, a1s, a2p, a2s, w1, w2, g, h):
    assert a1p.shape[0] == SBASE and a1s.shape[0] == TM and SBASE % TM == 0 and h.shape[0] == SBASE + TM
    k1 = a1p.shape[1]
    k2 = a2p.shape[1]
    n_rows, d = h.shape
    n_prompt = SBASE // TM
    prompt_blk = lambda i: (jnp.minimum(i, n_prompt - 1), 0)
    const = lambda i: (0, 0)
    return pl.pallas_call(
        _proj_kernel,
        grid=(n_rows // TM,),
        in_specs=[pl.BlockSpec((TM, k1), prompt_blk),
                  pl.BlockSpec((TM, k1), const),
                  pl.BlockSpec((TM, k2), prompt_blk),
                  pl.BlockSpec((TM, k2), const),
                  pl.BlockSpec((k1, d), const),
                  pl.BlockSpec((k2, d), const),
                  pl.BlockSpec((1, d), const),
                  pl.BlockSpec((TM, d), lambda i: (i, 0))],
        out_specs=pl.BlockSpec((TM, d), lambda i: (i, 0)),
        out_shape=jax.ShapeDtypeStruct((n_rows, d), F32),
        compiler_params=pltpu.CompilerParams(dimension_semantics=("arbitrary",)),
        name="proj_residual",
    )(a1p, a1s, a2p, a2s, w1, w2, g, h)


def _ffn_kernel(h_ref, g2_ref, g3_ref, wg_ref, wu_ref, wd_ref, *rest, split):
    if split:
        op_ref, os_ref, xn_ref, acc_ref = rest
    else:
        o_ref, xn_ref, acc_ref = rest
    f = pl.program_id(1)

    @pl.when(f == 0)
    def _():
        xn_ref[...] = _bf(_rms(h_ref[...], g2_ref[...]))
        acc_ref[...] = jnp.zeros(acc_ref.shape, F32)

    xn = xn_ref[...]
    gate = _dot(xn, wg_ref[...])
    up = _dot(xn, wu_ref[...])
    act = gate * _sigmoid(gate) * up
    acc_ref[...] += _dot(_bf(act), wd_ref[...])

    @pl.when(f == pl.num_programs(1) - 1)
    def _():
        res = h_ref[...] + _rms(acc_ref[...], g3_ref[...])
        if split:
            is_sample = pl.program_id(0) == pl.num_programs(0) - 1

            @pl.when(is_sample)
            def _():
                os_ref[...] = res

            @pl.when(jnp.logical_not(is_sample))
            def _():
                op_ref[...] = res
        else:
            o_ref[...] = res


def ffn_residual(h, g2, g3, wg, wu, wd, split=False):
    n_rows, d = h.shape
    dff = wg.shape[1]
    if split:
        assert n_rows == SBASE + TM and SBASE % TM == 0
        n_prompt = SBASE // TM
        out_specs = [pl.BlockSpec((TM, d), lambda i, f: (jnp.minimum(i, n_prompt - 1), 0)),
                     pl.BlockSpec((TM, d), lambda i, f: (0, 0))]
        out_shape = [jax.ShapeDtypeStruct((SBASE, d), F32), jax.ShapeDtypeStruct((TM, d), F32)]
    else:
        out_specs = pl.BlockSpec((TM, d), lambda i, f: (i, 0))
        out_shape = jax.ShapeDtypeStruct((n_rows, d), F32)
    return pl.pallas_call(
        functools.partial(_ffn_kernel, split=split),
        grid=(n_rows // TM, dff // FFN_TF),
        in_specs=[pl.BlockSpec((TM, d), lambda i, f: (i, 0)),
                  pl.BlockSpec((1, d), lambda i, f: (0, 0)),
                  pl.BlockSpec((1, d), lambda i, f: (0, 0)),
                  pl.BlockSpec((d, FFN_TF), lambda i, f: (0, f)),
                  pl.BlockSpec((d, FFN_TF), lambda i, f: (0, f)),
                  pl.BlockSpec((FFN_TF, d), lambda i, f: (f, 0))],
        out_specs=out_specs,
        out_shape=out_shape,
        scratch_shapes=[pltpu.VMEM((TM, d), BF16), pltpu.VMEM((TM, d), F32)],
        compiler_params=pltpu.CompilerParams(
            dimension_semantics=("arbitrary" if split else "parallel", "arbitrary")),
        name="ffn_residual",
    )(h, g2, g3, wg, wu, wd)


def _gates_kernel(z_ref, zt_ref, brow_ref, bcol_ref, o_ref, ot_ref):
    x = z_ref[...] + brow_ref[...]
    o_ref[...] = jnp.where(_iota(x.shape, 1) < M_HEADS, x, _log_sigmoid(x))
    xt = zt_ref[...] + bcol_ref[...]
    ot_ref[...] = jnp.where(_iota(xt.shape, 0) < M_HEADS, xt, _log_sigmoid(xt))


def gates_activate(zg, zgt, brow, bcol):
    n_rows = zg.shape[0]
    return pl.pallas_call(
        _gates_kernel,
        grid=(n_rows // TM,),
        in_specs=[pl.BlockSpec((TM, LANES), lambda i: (i, 0)),
                  pl.BlockSpec((16, TM), lambda i: (0, i)),
                  pl.BlockSpec((1, LANES), lambda i: (0, 0)),
                  pl.BlockSpec((16, 1), lambda i: (0, 0))],
        out_specs=[pl.BlockSpec((TM, LANES), lambda i: (i, 0)),
                   pl.BlockSpec((16, TM), lambda i: (0, i))],
        out_shape=[jax.ShapeDtypeStruct((n_rows, LANES), F32),
                   jax.ShapeDtypeStruct((16, n_rows), F32)],
        compiler_params=pltpu.CompilerParams(dimension_semantics=("parallel",)),
        name="gates_activate",
    )(zg, zgt, brow, bcol)


def _mlstm_chunk(q, k, v, logi_c, lf_c, logi_r, lf_r, c_state, n_state, m_state, row0, rows):
    ri = _iota((rows, rows), 0)
    ci = _iota((rows, rows), 1)
    valid_c = _iota((rows, 1), 0) >= row0
    valid_r = _iota((1, rows), 1) >= row0
    lf_c = jnp.where(valid_c, lf_c, 0.0)
    lf_r = jnp.where(valid_r, lf_r, 0.0)
    logi_c = jnp.where(valid_c, logi_c, NEG)
    logi_r = jnp.where(valid_r, logi_r, NEG)
    causal = ci <= ri
    b_c = jnp.sum(jnp.where(causal, lf_r, 0.0), axis=1, keepdims=True)
    b_r = jnp.sum(jnp.where(ri <= ci, lf_c, 0.0), axis=0, keepdims=True)
    log_d = jnp.where(causal, b_c - b_r + logi_r, -jnp.inf)
    m_inter = b_c + m_state
    m_t = jnp.maximum(m_inter, jnp.max(log_d, axis=1, keepdims=True))
    qb = _bf(q)
    kb = _bf(k)
    vb = _bf(v)
    w_intra = _dot_nt(qb, kb) * jnp.exp(log_d - m_t)
    w_inter = jnp.exp(m_inter - m_t)
    num = _dot(_bf(w_intra), vb) + w_inter * _dot(qb, _bf(c_state))
    den = jnp.sum(w_intra, axis=1, keepdims=True) + w_inter * jnp.sum(q * n_state, axis=1, keepdims=True)
    h = num / jnp.maximum(jnp.abs(den), jnp.exp(-m_t))
    m_new = m_t[rows - 1:rows, :]
    b_last = b_c[rows - 1:rows, :]
    w_src = jnp.exp(b_last - b_c + logi_c - m_new)
    decay = jnp.exp(b_last + m_state - m_new)
    ks = k * w_src
    c_new = decay * c_state + _dot_tn(_bf(ks), vb)
    n_new = decay * n_state + jnp.sum(ks, axis=0, keepdims=True)
    return h, c_new, n_new, m_new


def _mlstm_kernel(z_ref, g_ref, gt_ref, c0_ref, n0_ref, m0_ref, o_ref, c_ref, n_ref, m_ref, *, rows, row0_first):
    c = pl.program_id(1)

    @pl.when(c == 0)
    def _():
        c_ref[...] = c0_ref[...]
        n_ref[...] = n0_ref[...]
        m_ref[...] = m0_ref[...]

    row0 = jnp.where(c == 0, row0_first, 0)
    g = g_ref[...]
    gt = gt_ref[...]
    outs = []
    for h in range(M_HEADS):
        lo = h * HEAD_DIM
        q = z_ref[:, lo:lo + HEAD_DIM]
        k = z_ref[:, 512 + lo:512 + lo + HEAD_DIM] * (HEAD_DIM ** -0.5)
        v = z_ref[:, 1024 + lo:1024 + lo + HEAD_DIM]
        og = z_ref[:, 1536 + lo:1536 + lo + HEAD_DIM]
        hh, c_new, n_new, m_new = _mlstm_chunk(
            q, k, v,
            g[:, h:h + 1], g[:, M_HEADS + h:M_HEADS + h + 1],
            gt[h:h + 1, :], gt[M_HEADS + h:M_HEADS + h + 1, :],
            c_ref[0, h], n_ref[0, h:h + 1, :], m_ref[0, h:h + 1, 0:1], row0, rows)
        c_ref[0, h] = c_new
        n_ref[0, h:h + 1, :] = n_new
        m_ref[0, h:h + 1, :] = jnp.broadcast_to(m_new, (1, LANES))
        outs.append(_sigmoid(og) * hh)
    o_ref[...] = jnp.concatenate(outs, axis=1)


def _state_specs(heads):
    return [pl.BlockSpec((1, heads, HEAD_DIM, HEAD_DIM), lambda b, c: (b, 0, 0, 0)),
            pl.BlockSpec((1, heads, HEAD_DIM), lambda b, c: (b, 0, 0)),
            pl.BlockSpec((1, heads, LANES), lambda b, c: (b, 0, 0))]


def mlstm_call(z, g2, g2t, c0, n0, m0, *, rows, n_chunks, row0_first, block0):
    nb = c0.shape[0]
    blk = lambda b, c: (block0 + b * n_chunks + c, 0)
    blk_local = lambda b, c: (b * n_chunks + c, 0)
    return pl.pallas_call(
        functools.partial(_mlstm_kernel, rows=rows, row0_first=row0_first),
        grid=(nb, n_chunks),
        in_specs=[pl.BlockSpec((rows, 2048), blk),
                  pl.BlockSpec((rows, LANES), blk),
                  pl.BlockSpec((None, 16, rows), lambda b, c: (b * n_chunks + c, 0, 0))] + _state_specs(M_HEADS),
        out_specs=[pl.BlockSpec((rows, 512), blk_local)] + _state_specs(M_HEADS),
        out_shape=[jax.ShapeDtypeStruct((nb * n_chunks * rows, 512), F32),
                   jax.ShapeDtypeStruct((nb, M_HEADS, HEAD_DIM, HEAD_DIM), F32),
                   jax.ShapeDtypeStruct((nb, M_HEADS, HEAD_DIM), F32),
                   jax.ShapeDtypeStruct((nb, M_HEADS, LANES), F32)],
        compiler_params=pltpu.CompilerParams(dimension_semantics=("parallel", "arbitrary")),
        name="mlstm",
    )(z, g2, g2t, c0, n0, m0)


def _ret_log_gamma(h):
    return math.log1p(-(2.0 ** (-5.0 - h)))


def _ret_kernel(z_ref, cos_ref, sin_ref, lng_ref, s0_ref, o_ref, s_ref, *, rows, row0_first):
    c = pl.program_id(1)

    @pl.when(c == 0)
    def _():
        s_ref[...] = s0_ref[...]

    row0 = jnp.where(c == 0, row0_first, 0)
    n_valid = (rows - row0).astype(F32)
    cosf = cos_ref[...]
    sinf = sin_ref[...]
    ri = _iota((rows, rows), 0)
    ci = _iota((rows, rows), 1)
    diff = (ri - ci).astype(F32)
    rowi = _iota((rows, 1), 0)
    te = (rowi - row0).astype(F32)
    valid = rowi >= row0
    outs = []
    for h in range(R_HEADS):
        lg = _ret_log_gamma(h)
        lo = h * HEAD_DIM
        q = z_ref[:, lo:lo + HEAD_DIM]
        k = z_ref[:, 512 + lo:512 + lo + HEAD_DIM]
        v = jnp.where(valid, z_ref[:, 1024 + lo:1024 + lo + HEAD_DIM], 0.0)
        rg = z_ref[:, 1536 + lo:1536 + lo + HEAD_DIM]
        q = q * cosf + pltpu.roll(q, HEAD_DIM // 2, 1) * sinf
        k = (k * cosf + pltpu.roll(k, HEAD_DIM // 2, 1) * sinf) * (HEAD_DIM ** -0.5)
        s_state = s_ref[0, h]
        qb = _bf(q)
        vb = _bf(v)
        decay_m = jnp.where(diff >= 0, jnp.exp(diff * lg), 0.0)
        intra = _dot(_bf(_dot_nt(qb, _bf(k)) * decay_m), vb)
        inter = _dot(qb, _bf(s_state)) * jnp.exp((te + 1.0) * lg)
        w_src = jnp.exp((n_valid - 1.0 - te) * lg)
        s_ref[0, h] = jnp.exp(n_valid * lg) * s_state + _dot_tn(_bf(k * w_src), vb)
        hr = intra + inter
        mu = jnp.mean(hr, axis=-1, keepdims=True)
        var = jnp.mean(jnp.square(hr - mu), axis=-1, keepdims=True)
        hr = (hr - mu) * lax.rsqrt(var + EPS) * lng_ref[h:h + 1, :]
        outs.append(rg * _sigmoid(rg) * hr)
    o_ref[...] = jnp.concatenate(outs, axis=1)


def retention_call(z, cosf, sinf, lng, s0, *, rows, n_chunks, row0_first, block0):
    nb = s0.shape[0]
    blk = lambda b, c: (block0 + b * n_chunks + c, 0)
    sspec = pl.BlockSpec((1, R_HEADS, HEAD_DIM, HEAD_DIM), lambda b, c: (b, 0, 0, 0))
    return pl.pallas_call(
        functools.partial(_ret_kernel, rows=rows, row0_first=row0_first),
        grid=(nb, n_chunks),
        in_specs=[pl.BlockSpec((rows, 2048), blk),
                  pl.BlockSpec((rows, HEAD_DIM), lambda b, c: (c, 0)),
                  pl.BlockSpec((rows, HEAD_DIM), lambda b, c: (c, 0)),
                  pl.BlockSpec((R_HEADS, HEAD_DIM), lambda b, c: (0, 0)),
                  sspec],
        out_specs=[pl.BlockSpec((rows, 512), lambda b, c: (b * n_chunks + c, 0)), sspec],
        out_shape=[jax.ShapeDtypeStruct((nb * n_chunks * rows, 512), F32),
                   jax.ShapeDtypeStruct((nb, R_HEADS, HEAD_DIM, HEAD_DIM), F32)],
        compiler_params=pltpu.CompilerParams(dimension_semantics=("parallel", "arbitrary")),
        name="retention",
    )(z, cosf, sinf, lng, s0)


def _lru_kernel(z_ref, pre_ref, h0_ref, cw_ref, cb_ref, wa_ref, ba_ref, wx_ref, bx_ref, lam_ref,
                o_ref, hlast_ref, tail_ref, *, rows, row0_first):
    c = pl.program_id(1)

    @pl.when(c == 0)
    def _():
        hlast_ref[...] = h0_ref[...]
        tail_ref[...] = jnp.zeros(tail_ref.shape, F32)

    row0 = jnp.where(c == 0, row0_first, 0)
    rowi = _iota((rows, 1), 0)
    valid = rowi >= row0
    lx = jnp.where(valid, z_ref[:, 0:LRU_WIDTH], pre_ref[...])
    gate = z_ref[:, LRU_WIDTH:2 * LRU_WIDTH]
    tail = tail_ref[0]
    row8 = _iota((8, 1), 0)
    xc = cb_ref[...] + cw_ref[CONV_W - 1:CONV_W, :] * lx
    for j in range(1, CONV_W):
        rolled = pltpu.roll(lx, j, 0)
        first = jnp.where(row8 < j, pltpu.roll(tail, j, 0), rolled[0:8])
        shifted = jnp.concatenate([first, rolled[8:]], axis=0)
        xc = xc + cw_ref[CONV_W - 1 - j:CONV_W - j, :] * shifted
    tail_ref[0] = lx[rows - 8:rows]
    xcb = _bf(xc)
    pre_a = []
    pre_x = []
    for p in range(LRU_WIDTH // LANES):
        xs = xcb[:, p * LANES:(p + 1) * LANES]
        pre_a.append(_dot(xs, wa_ref[p]))
        pre_x.append(_dot(xs, wx_ref[p]))
    r = _sigmoid(jnp.concatenate(pre_a, axis=1) + ba_ref[...])
    ig = _sigmoid(jnp.concatenate(pre_x, axis=1) + bx_ref[...])
    log_a = -LRU_C * r * _softplus(-lam_ref[...])
    a = jnp.where(valid, jnp.exp(log_a), 1.0)
    one_minus_a2 = -jnp.tanh(log_a) * (jnp.exp(2.0 * log_a) + 1.0)
    bx = jnp.where(valid, jnp.sqrt(one_minus_a2) * (ig * xc), 0.0)
    shift = 1
    while shift < rows:
        keep = rowi >= shift
        a_sh = jnp.where(keep, pltpu.roll(a, shift, 0), 1.0)
        b_sh = jnp.where(keep, pltpu.roll(bx, shift, 0), 0.0)
        bx = a * b_sh + bx
        a = a * a_sh
        shift *= 2
    hs = a * hlast_ref[0] + bx
    hlast_ref[0] = hs[rows - 1:rows, :]
    gl = 0.5 * gate * (1.0 + jnp.tanh(math.sqrt(2.0 / math.pi) * (gate + 0.044715 * gate * gate * gate)))
    o_ref[...] = hs * gl


def lru_call(z, pre, h0, cw, cb, wa2, ba, wx2, bx, lam, *, rows, n_chunks, row0_first, block0):
    nb = h0.shape[0]
    blk = lambda b, c: (block0 + b * n_chunks + c, 0)
    blk_local = lambda b, c: (b * n_chunks + c, 0)
    const2 = lambda b, c: (0, 0)
    in_specs = [pl.BlockSpec((rows, 1024), blk),
                pl.BlockSpec((rows, LRU_WIDTH), const2 if pre.shape[0] == rows else blk_local),
                pl.BlockSpec((1, 1, LRU_WIDTH), lambda b, c: (b, 0, 0)),
                pl.BlockSpec((CONV_W, LRU_WIDTH), const2),
                pl.BlockSpec((1, LRU_WIDTH), const2),
                pl.BlockSpec((LRU_WIDTH // LANES, LANES, LANES), lambda b, c: (0, 0, 0)),
                pl.BlockSpec((1, LRU_WIDTH), const2),
                pl.BlockSpec((LRU_WIDTH // LANES, LANES, LANES), lambda b, c: (0, 0, 0)),
                pl.BlockSpec((1, LRU_WIDTH), const2),
                pl.BlockSpec((1, LRU_WIDTH), const2)]
    return pl.pallas_call(
        functools.partial(_lru_kernel, rows=rows, row0_first=row0_first),
        grid=(nb, n_chunks),
        in_specs=in_specs,
        out_specs=[pl.BlockSpec((rows, LRU_WIDTH), blk_local),
                   pl.BlockSpec((1, 1, LRU_WIDTH), lambda b, c: (b, 0, 0)),
                   pl.BlockSpec((1, 8, LRU_WIDTH), lambda b, c: (b, 0, 0))],
        out_shape=[jax.ShapeDtypeStruct((nb * n_chunks * rows, LRU_WIDTH), F32),
                   jax.ShapeDtypeStruct((nb, 1, LRU_WIDTH), F32),
                   jax.ShapeDtypeStruct((nb, 8, LRU_WIDTH), F32)],
        compiler_params=pltpu.CompilerParams(dimension_semantics=("parallel", "arbitrary")),
        name="rglru",
    )(z, pre, h0, cw, cb, wa2, ba, wx2, bx, lam)


def _fox_cumsum_kernel(g_ref, gt_ref, fc_ref, fr_ref):
    ri = _iota((CHUNK, CHUNK), 0)
    ci = _iota((CHUNK, CHUNK), 1)
    lower = jnp.where(ci <= ri, 1.0, 0.0).astype(BF16)
    upper = jnp.where(ri <= ci, 1.0, 0.0).astype(BF16)
    carry_r = jnp.zeros((1, LANES), F32)
    carry_c = jnp.zeros((16, 1), F32)
    for blk in range(NCH):
        x = g_ref[blk * CHUNK:(blk + 1) * CHUNK, :]
        xt = gt_ref[:, blk * CHUNK:(blk + 1) * CHUNK]
        if blk == 0:
            x = jnp.where(_iota((CHUNK, 1), 0) >= PADF, x, 0.0)
            xt = jnp.where(_iota((1, CHUNK), 1) >= PADF, xt, 0.0)
        cs = _dot01_left(lower, x) + carry_r
        cst = _dot01_right(xt, upper) + carry_c
        carry_r = cs[CHUNK - 1:CHUNK, :]
        carry_c = cst[:, CHUNK - 1:CHUNK]
        if blk == 0:
            cst = jnp.where(_iota((1, CHUNK), 1) >= PADF, cst, -NEG)
        fc_ref[0, blk * CHUNK:(blk + 1) * CHUNK, :] = cs
        fr_ref[0, :, blk * CHUNK:(blk + 1) * CHUNK] = cst


def fox_cumsum(g2, g2t):
    return pl.pallas_call(
        _fox_cumsum_kernel,
        grid=(BATCH,),
        in_specs=[pl.BlockSpec((TP, LANES), lambda b: (b, 0)),
                  pl.BlockSpec((16, TP), lambda b: (0, b))],
        out_specs=[pl.BlockSpec((1, TP, LANES), lambda b: (b, 0, 0)),
                   pl.BlockSpec((1, 16, TP), lambda b: (b, 0, 0))],
        out_shape=[jax.ShapeDtypeStruct((BATCH, TP, LANES), F32),
                   jax.ShapeDtypeStruct((BATCH, 16, TP), F32)],
        compiler_params=pltpu.CompilerParams(dimension_semantics=("parallel",)),
        name="fox_cumsum",
    )(g2, g2t)


def _fox_prompt_kernel(q_ref, k_ref, v_ref, fr_ref, fc_ref, o_ref,
                       qt_s, ka_s, kb_s, vt_s, fk_s, m_s, l_s, acc_s, ua_s):
    i = pl.program_id(2)
    is_a = _iota((1, LANES), 1) < F_DH
    row_a = _iota((LANES, 1), 0) < F_DH
    k_heads = (ka_s, kb_s)

    @pl.when(i == 0)
    def _():
        k = k_ref[...]
        ka_s[...] = _bf(jnp.where(is_a, k, 0.0))
        kb_s[...] = _bf(jnp.where(is_a, 0.0, k))

        def fill(blk, carry):
            rows = pl.ds(pl.multiple_of(blk * CHUNK, CHUNK), CHUNK)
            qt_s[blk] = _bf((q_ref[rows, :] * (F_DH ** -0.5 * LOG2E)).T)
            vt_s[blk] = _bf(v_ref[rows, :].T)
            return carry

        lax.fori_loop(0, NCH, fill, 0)
        is_pad = _iota((TP, 1), 0) < PADF
        for hh in range(2):
            fk = jnp.where(is_pad, -NEG, fc_ref[0, 0, :, hh:hh + 1] * LOG2E)
            fk_s[hh] = jnp.broadcast_to(fk, (TP, LANES))

    def init():
        m_s[...] = jnp.full(m_s.shape, NEG, F32)
        l_s[...] = jnp.zeros(l_s.shape, F32)
        acc_s[...] = jnp.zeros(acc_s.shape, F32)

    def scores(hh, qt, kblk, nkb):
        ck = nkb * CHUNK
        k0 = kblk * CHUNK if isinstance(kblk, int) else pl.multiple_of(kblk * CHUNK, CHUNK)
        fk = jnp.concatenate([fk_s[hh, pl.ds(k0, ck), :]] * (qt.shape[1] // LANES), axis=1)
        return _dot(k_heads[hh][pl.ds(k0, ck), :], qt) - fk

    def step(hh, u, qblk, nqb, kblk, nkb, causal):
        tq = nqb * CHUNK
        ck = nkb * CHUNK
        fq = jnp.concatenate([fr_ref[0, 0, qblk + t, hh:hh + 1, :] for t in range(nqb)], axis=1)
        fq = jnp.where(fq > 0.5 * -NEG, 0.0, fq * LOG2E)
        if causal:
            u = jnp.where(kblk * CHUNK + _iota((ck, tq), 0) <= qblk * CHUNK + _iota((ck, tq), 1), u, NEG)
        m_old = m_s[hh, :, 0:tq]
        m_new = jnp.maximum(m_old, jnp.max(u, axis=0, keepdims=True) + fq)
        alpha = jnp.exp2(m_old - m_new)
        p = jnp.exp2(u + (fq - m_new))
        l_s[hh, :, 0:tq] = alpha * l_s[hh, :, 0:tq] + jnp.sum(p, axis=0, keepdims=True)
        m_s[hh, :, 0:tq] = m_new
        feat = slice(hh * F_DH, (hh + 1) * F_DH)
        vt = jnp.concatenate([vt_s[kblk + t, feat, :] for t in range(nkb)], axis=1)
        acc_s[feat, 0:tq] = alpha * acc_s[feat, 0:tq] + _dot(vt, _bf(p))

    def finalize(qblk, nqb):
        tq = nqb * CHUNK
        o_t = acc_s[:, 0:tq] / jnp.where(row_a, l_s[0, :, 0:tq], l_s[1, :, 0:tq])
        for t in range(nqb):
            rows = pl.ds(pl.multiple_of((qblk + t) * CHUNK, CHUNK), CHUNK)
            o_ref[rows, :] = o_t[:, t * CHUNK:(t + 1) * CHUNK].T

    @pl.when(i == 0)
    def _():
        init()
        qt0 = qt_s[0]
        for hh in range(2):
            step(hh, scores(hh, qt0, 0, 1), 0, 1, 0, 1, True)
        finalize(0, 1)

    nck = FOX_CK // CHUNK
    nqb = FOX_TQ // CHUNK
    qblk = 1 + i * nqb
    qt = jnp.concatenate([qt_s[qblk + t] for t in range(nqb)], axis=1)
    init()
    for hh in range(2):
        step(hh, scores(hh, qt, 0, 1), qblk, nqb, 0, 1, False)

    ua_s[...] = scores(0, qt, 1, nck)

    def body(j, carry):
        kblk = 1 + j * nck
        ub = scores(1, qt, kblk, nck)
        step(0, ua_s[...], qblk, nqb, kblk, nck, False)
        ua_s[...] = scores(0, qt, kblk + nck, nck)
        step(1, ub, qblk, nqb, kblk, nck, False)
        return carry

    lax.fori_loop(0, i, body, 0)
    ub = scores(1, qt, qblk, nck)
    step(0, ua_s[...], qblk, nqb, qblk, nck, True)
    step(1, ub, qblk, nqb, qblk, nck, True)
    finalize(qblk, nqb)


def fox_prompt(zq, zk, zv, fr, fc):
    nq = (TP - CHUNK) // FOX_TQ
    pairs = F_HEADS // 2
    return pl.pallas_call(
        _fox_prompt_kernel,
        grid=(BATCH, pairs, nq),
        in_specs=[pl.BlockSpec((TP, LANES), lambda b, p, i: (b, p)),
                  pl.BlockSpec((TP, LANES), lambda b, p, i: (b, p)),
                  pl.BlockSpec((TP, LANES), lambda b, p, i: (b, p)),
                  pl.BlockSpec((1, 1, NCH, 2, LANES), lambda b, p, i: (b, p, 0, 0, 0)),
                  pl.BlockSpec((1, 1, TP, 2), lambda b, p, i: (b, p, 0, 0))],
        out_specs=pl.BlockSpec((TP, LANES), lambda b, p, i: (b, p)),
        out_shape=jax.ShapeDtypeStruct((SBASE, 512), F32),
        scratch_shapes=[pltpu.VMEM((NCH, LANES, CHUNK), BF16),
                        pltpu.VMEM((TP, LANES), BF16),
                        pltpu.VMEM((TP, LANES), BF16),
                        pltpu.VMEM((NCH, LANES, CHUNK), BF16),
                        pltpu.VMEM((2, TP, LANES), F32),
                        pltpu.VMEM((2, 1, FOX_TQ), F32),
                        pltpu.VMEM((2, 1, FOX_TQ), F32),
                        pltpu.VMEM((LANES, FOX_TQ), F32),
                        pltpu.VMEM((FOX_CK, FOX_TQ), F32)],
        compiler_params=pltpu.CompilerParams(dimension_semantics=("parallel", "parallel", "arbitrary")),
        name="fox_prompt",
    )(zq, zk, zv, fr, fc)


def _lf_suffix_kernel(x_ref, suf_ref, tot_ref):
    ri = _iota((PAGE_SIZE, PAGE_SIZE), 0)
    ci = _iota((PAGE_SIZE, PAGE_SIZE), 1)
    after = jnp.where(ri > ci, 1.0, 0.0).astype(BF16)
    ones = jnp.ones((PAGE_SIZE, PAGE_SIZE), BF16)
    x = x_ref[...].reshape(PRE_PG * F_HEADS, PAGE_SIZE)
    suf_ref[...] = _dot01_right(x, after).reshape(PRE_PG, F_HEADS, PAGE_SIZE)
    tot_ref[...] = _dot01_right(x, ones).reshape(PRE_PG, F_HEADS, PAGE_SIZE)


def lf_suffix(lft):
    n_phys = lft.shape[0]
    spec = pl.BlockSpec((PRE_PG, F_HEADS, PAGE_SIZE), lambda i: (i, 0, 0))
    return pl.pallas_call(
        _lf_suffix_kernel,
        grid=(n_phys // PRE_PG,),
        in_specs=[spec],
        out_specs=[spec, spec],
        out_shape=[jax.ShapeDtypeStruct(lft.shape, F32)] * 2,
        compiler_params=pltpu.CompilerParams(dimension_semantics=("parallel",)),
        name="lf_suffix",
    )(lft)


def _fox_decode_kernel(pt_ref, zq_ref, zk_ref, zv_ref, gt_ref, *rest):
    k_refs = rest[0:DEC_PG]
    v_refs = rest[DEC_PG:2 * DEC_PG]
    suf_refs = rest[2 * DEC_PG:3 * DEC_PG]
    tot_refs = rest[3 * DEC_PG:4 * DEC_PG]
    o_ref = rest[4 * DEC_PG]
    q2_s, m_s, l_s, acc_s, run_s, nc_s = rest[4 * DEC_PG + 1:]
    j = pl.program_id(1)
    n_rows = DEC_SEQ * F_HEADS
    width = F_HEADS * F_DH

    def update(u, v, v_feature_major):
        m_old = m_s[...]
        m_new = jnp.maximum(m_old, jnp.max(u, axis=1, keepdims=True))
        alpha = jnp.exp(m_old - m_new)
        p = jnp.exp(u - m_new[:, 0:u.shape[1]])
        pv = _dot_nt(_bf(p), v) if v_feature_major else _dot(_bf(p), v)
        l_s[...] = alpha * l_s[...] + jnp.sum(p, axis=1, keepdims=True)
        acc_s[...] = jnp.concatenate([alpha] * (width // LANES), axis=1) * acc_s[...] + pv
        m_s[...] = m_new

    @pl.when(j == 0)
    def _():
        m_s[...] = jnp.full(m_s.shape, NEG, F32)
        l_s[...] = jnp.zeros(l_s.shape, F32)
        acc_s[...] = jnp.zeros(acc_s.shape, F32)
        run_s[...] = jnp.zeros(run_s.shape, F32)
        head_mask = _iota((F_HEADS, width), 1) // F_DH == _iota((F_HEADS, width), 0)
        q16 = zq_ref[...] * (F_DH ** -0.5)
        q2 = _bf(jnp.concatenate(
            [jnp.where(head_mask, q16[SPAD + t:SPAD + t + 1, :], 0.0) for t in range(DEC_SEQ)], axis=0))
        q2_s[...] = q2
        lf_new = gt_ref[F_HEADS:2 * F_HEADS, :]
        lane = _iota((1, SROWS), 1)
        cum = jnp.zeros((F_HEADS, SROWS), F32)
        for t in range(DEC_SEQ):
            cum = cum + jnp.where(lane >= SPAD + t, lf_new[:, SPAD + t:SPAD + t + 1], 0.0)
        nc = jnp.concatenate([cum[:, SPAD + t:SPAD + t + 1] for t in range(DEC_SEQ)], axis=0)
        nc_s[...] = jnp.broadcast_to(nc, nc_s.shape)
        cum4 = jnp.concatenate([cum] * DEC_SEQ, axis=0)
        qi = _iota((n_rows, SROWS), 0) // F_HEADS
        kj = _iota((n_rows, SROWS), 1) - SPAD
        u = jnp.where((kj >= 0) & (kj <= qi), _dot_nt(q2, _bf(zk_ref[...])) + (nc - cum4), NEG)
        update(u, _bf(zv_ref[...]), False)

    q2 = q2_s[...]
    nc = nc_s[...]
    run = run_s[...]
    us = []
    for i in range(DEC_PG):
        later = run + suf_refs[i][0]
        bias = jnp.concatenate([later] * DEC_SEQ, axis=0) + nc
        us.append(_dot(q2, _bf(k_refs[i][0].reshape(width, PAGE_SIZE))) + bias)
        run = run + tot_refs[i][0]
    run_s[...] = run
    u = jnp.concatenate(us, axis=1)
    m_old = m_s[...]
    m_new = jnp.maximum(m_old, jnp.max(u, axis=1, keepdims=True))
    alpha = jnp.exp(m_old - m_new)
    p = jnp.exp(u - jnp.concatenate([m_new] * DEC_PG, axis=1))
    l_s[...] = alpha * l_s[...] + jnp.sum(p, axis=1, keepdims=True)
    pv = _dot_nt(_bf(p[:, 0:PAGE_SIZE]), _bf(v_refs[0][0].reshape(width, PAGE_SIZE)))
    for i in range(1, DEC_PG):
        pv = pv + _dot_nt(_bf(p[:, i * PAGE_SIZE:(i + 1) * PAGE_SIZE]), _bf(v_refs[i][0].reshape(width, PAGE_SIZE)))
    acc_s[...] = jnp.concatenate([alpha] * (width // LANES), axis=1) * acc_s[...] + pv
    m_s[...] = m_new

    @pl.when(j == pl.num_programs(1) - 1)
    def _():
        head_mask = (_iota((n_rows, width), 1) // F_DH) == (_iota((n_rows, width), 0) % F_HEADS)
        o2 = jnp.where(head_mask, acc_s[...] / jnp.concatenate([l_s[...]] * (width // LANES), axis=1), 0.0)
        o_ref[...] = jnp.zeros(o_ref.shape, F32)
        for t in range(DEC_SEQ):
            o_ref[SPAD + t:SPAD + t + 1, :] = jnp.sum(o2[t * F_HEADS:(t + 1) * F_HEADS, :], axis=0, keepdims=True)


def fox_decode(page_table, zq, zk, zv, g2t_s, cache_kt, cache_vt, suf, tot):
    n_pages = page_table.shape[1]
    width = F_HEADS * F_DH
    sblk = SBASE // SROWS

    def page_map(i):
        return lambda b, j, pt: (pt[b, n_pages - 1 - (j * DEC_PG + i)], 0, 0, 0)

    def page_map3(i):
        return lambda b, j, pt: (pt[b, n_pages - 1 - (j * DEC_PG + i)], 0, 0)

    in_specs = [pl.BlockSpec((SROWS, width), lambda b, j, pt: (sblk + b, 0)),
                pl.BlockSpec((SROWS, width), lambda b, j, pt: (sblk + b, 0)),
                pl.BlockSpec((SROWS, width), lambda b, j, pt: (sblk + b, 0)),
                pl.BlockSpec((None, 16, SROWS), lambda b, j, pt: (b, 0, 0))]
    in_specs += [pl.BlockSpec((1, F_HEADS, F_DH, PAGE_SIZE), page_map(i)) for i in range(DEC_PG)]
    in_specs += [pl.BlockSpec((1, F_HEADS, F_DH, PAGE_SIZE), page_map(i)) for i in range(DEC_PG)]
    in_specs += [pl.BlockSpec((1, F_HEADS, PAGE_SIZE), page_map3(i)) for i in range(DEC_PG)]
    in_specs += [pl.BlockSpec((1, F_HEADS, PAGE_SIZE), page_map3(i)) for i in range(DEC_PG)]
    args = [zq, zk, zv, g2t_s] + [cache_kt] * DEC_PG + [cache_vt] * DEC_PG + [suf] * DEC_PG + [tot] * DEC_PG
    n_rows = DEC_SEQ * F_HEADS
    grid_spec = pltpu.PrefetchScalarGridSpec(
        num_scalar_prefetch=1,
        grid=(DEC_BATCH, n_pages // DEC_PG),
        in_specs=in_specs,
        out_specs=pl.BlockSpec((SROWS, width), lambda b, j, pt: (b, 0)),
        scratch_shapes=[pltpu.VMEM((n_rows, width), BF16),
                        pltpu.VMEM((n_rows, LANES), F32),
                        pltpu.VMEM((n_rows, LANES), F32),
                        pltpu.VMEM((n_rows, width), F32),
                        pltpu.VMEM((F_HEADS, PAGE_SIZE), F32),
                        pltpu.VMEM((n_rows, LANES), F32)])
    return pl.pallas_call(
        _fox_decode_kernel,
        grid_spec=grid_spec,
        out_shape=jax.ShapeDtypeStruct((DEC_BATCH * SROWS, width), F32),
        compiler_params=pltpu.CompilerParams(dimension_semantics=("parallel", "arbitrary")),
        name="fox_decode",
    )(page_table, *args)


def _block_diag_pairs(w):
    z = jnp.zeros((LANES // 2, LANES // 2), w.dtype)
    pairs = [jnp.block([[w[2 * p], z], [z, w[2 * p + 1]]]) for p in range(w.shape[0] // 2)]
    return _bf(jnp.stack(pairs))


def _rope_tables(pos):
    half = HEAD_DIM // 2
    freq = ROPE_BASE ** (-jnp.arange(half, dtype=F32) / half)
    ang = pos.astype(F32)[:, None] * freq[None, :]
    cos, sin = jnp.cos(ang), jnp.sin(ang)
    return jnp.concatenate([cos, cos], axis=1), jnp.concatenate([-sin, sin], axis=1)


def _unpad_prompt(x):
    return x[:SBASE].reshape((BATCH, TP) + x.shape[1:])[:, PADF:]


def _unpad_sample(x):
    return x[SBASE:].reshape((DEC_BATCH, SROWS) + x.shape[1:])[:, SPAD:]


def kernel(x_prompt, x_sample, cache_fox_k, cache_fox_v, cache_fox_logf, state_mlstm_C, state_mlstm_n, state_mlstm_m, state_ret_S, state_lru_h, state_lru_conv, page_table, meta_tokens, w_in_even, b_mlstm_i, b_mlstm_f, b_fox_f, w_out_even, w_in_odd, ret_ln_g, conv_w, conv_b, lru_wa, lru_ba, lru_wx, lru_bx, lru_lambda, w_out_odd, norm_g, ffn_wg, ffn_wu, ffn_wd):
    n_pages = page_table.shape[1]
    past = n_pages * PAGE_SIZE
    n_phys = cache_fox_k.shape[0]

    head = jnp.concatenate([jnp.zeros((PADF, D_MODEL), F32), meta_tokens.astype(F32)], axis=0)
    hs = jnp.concatenate([jnp.zeros((DEC_BATCH, SPAD, D_MODEL), F32), x_sample.astype(F32)],
                         axis=1).reshape(DEC_BATCH * SROWS, D_MODEL)
    pieces = []
    for b in range(BATCH):
        pieces += [head, x_prompt[b].astype(F32)]
    h = jnp.concatenate(pieces + [hs], axis=0)

    c = np.cumsum([0, 512, 512, 512, 512, 4, 4, 512, 512, 512, 8])
    gate_w = jnp.concatenate([w_in_even[:, c[4]:c[6]], w_in_even[:, c[9]:c[10]],
                              jnp.zeros((D_MODEL, LANES - 16), F32)], axis=1)
    w_even = _bf(jnp.concatenate([w_in_even[:, c[0]:c[4]], w_in_even[:, c[6]:c[9]], gate_w], axis=1))
    gate_b = jnp.concatenate([b_mlstm_i, b_mlstm_f, b_fox_f]).astype(F32)
    brow = jnp.concatenate([gate_b, jnp.zeros((LANES - 16,), F32)]).reshape(1, LANES)
    bcol = gate_b.reshape(16, 1)
    w_odd = _bf(w_in_odd)
    wo_even = _bf(w_out_even)
    wo_odd = _bf(w_out_odd)
    wg = _bf(ffn_wg)
    wu = _bf(ffn_wu)
    wd = _bf(ffn_wd)
    ng = norm_g.astype(F32).reshape(2, 4, 1, D_MODEL)

    zm, zq, zk, zv, zg = norm_matmul(h, ng[0, 0], w_even, (2048, 512, 512, 512, LANES))
    g2, g2t = gates_activate(zg, zg[:, :16].T, brow, bcol)
    g2t_p = g2t[:, :SBASE].reshape(16, BATCH * NCH, CHUNK).transpose(1, 0, 2)
    g2t_s = g2t[:, SBASE:].reshape(16, DEC_BATCH, SROWS).transpose(1, 0, 2)

    zeros_c = jnp.zeros((BATCH, M_HEADS, HEAD_DIM, HEAD_DIM), F32)
    zeros_n = jnp.zeros((BATCH, M_HEADS, HEAD_DIM), F32)
    hm_p, p_c, p_n, p_m = mlstm_call(zm, g2, g2t_p, zeros_c, zeros_n, zeros_n,
                                     rows=CHUNK, n_chunks=NCH, row0_first=PADF, block0=0)
    m0_s = jnp.broadcast_to(state_mlstm_m.astype(F32)[:, :, None], (DEC_BATCH, M_HEADS, LANES))
    hm_s, s_c, s_n, s_m = mlstm_call(zm, g2, g2t_s, state_mlstm_C.astype(F32), state_mlstm_n.astype(F32), m0_s,
                                     rows=SROWS, n_chunks=1, row0_first=SPAD, block0=SBASE // SROWS)

    fc, fr = fox_cumsum(g2, g2t)
    pairs = F_HEADS // 2
    fr_p = fr[:, 8:16].reshape(BATCH, pairs, 2, NCH, CHUNK).transpose(0, 1, 3, 2, 4)
    fc_p = fc[:, :, 8:16].reshape(BATCH, TP, pairs, 2).transpose(0, 2, 1, 3)
    hf_p = fox_prompt(zq, zk, zv, fr_p, fc_p)
    suf, tot = lf_suffix(cache_fox_logf.astype(F32).transpose(0, 2, 1))
    hf_s = fox_decode(page_table, zq, zk, zv, g2t_s, cache_fox_k.astype(F32).transpose(0, 2, 3, 1),
                      cache_fox_v.astype(F32).transpose(0, 2, 3, 1), suf, tot)

    h = proj_residual(hm_p, hm_s, hf_p, hf_s, wo_even[:512], wo_even[512:], ng[0, 1], h)
    h = ffn_residual(h, ng[0, 2], ng[0, 3], wg[0], wu[0], wd[0])

    zr, zl = norm_matmul(h, ng[1, 0], w_odd, (2048, 1024))
    cos_p, sin_p = _rope_tables(jnp.arange(TP) - PADF)
    cos_s, sin_s = _rope_tables(past + jnp.arange(SROWS) - SPAD)
    lng = ret_ln_g.astype(F32)
    hr_p, p_s = retention_call(zr, cos_p, sin_p, lng, jnp.zeros((BATCH, R_HEADS, HEAD_DIM, HEAD_DIM), F32),
                               rows=CHUNK, n_chunks=NCH, row0_first=PADF, block0=0)
    hr_s, s_s = retention_call(zr, cos_s, sin_s, lng, state_ret_S.astype(F32),
                               rows=SROWS, n_chunks=1, row0_first=SPAD, block0=SBASE // SROWS)

    lru_args = (conv_w.astype(F32), conv_b.astype(F32).reshape(1, LRU_WIDTH),
                _block_diag_pairs(lru_wa), lru_ba.astype(F32).reshape(1, LRU_WIDTH),
                _block_diag_pairs(lru_wx), lru_bx.astype(F32).reshape(1, LRU_WIDTH),
                lru_lambda.astype(F32).reshape(1, LRU_WIDTH))
    hl_p, p_h, p_tail = lru_call(zl, jnp.zeros((CHUNK, LRU_WIDTH), F32), jnp.zeros((BATCH, 1, LRU_WIDTH), F32),
                                 *lru_args, rows=CHUNK, n_chunks=NCH, row0_first=PADF, block0=0)
    pre_s = jnp.concatenate([jnp.zeros((DEC_BATCH, SPAD - (CONV_W - 1), LRU_WIDTH), F32),
                             state_lru_conv.astype(F32),
                             jnp.zeros((DEC_BATCH, DEC_SEQ, LRU_WIDTH), F32)], axis=1).reshape(-1, LRU_WIDTH)
    hl_s, s_h, s_tail = lru_call(zl, pre_s, state_lru_h.astype(F32).reshape(DEC_BATCH, 1, LRU_WIDTH),
                                 *lru_args, rows=SROWS, n_chunks=1, row0_first=SPAD, block0=SBASE // SROWS)

    h = proj_residual(hr_p, hr_s, hl_p, hl_s, wo_odd[:512], wo_odd[512:], ng[1, 1], h)
    h_p, h_s = ffn_residual(h, ng[1, 2], ng[1, 3], wg[1], wu[1], wd[1], split=True)

    y_prompt = h_p.reshape(BATCH, TP, D_MODEL)[:, CHUNK:]
    y_sample = h_s.reshape(DEC_BATCH, SROWS, D_MODEL)[:, SPAD:]
    lf = g2[:, 8:16]
    heads = lambda x: x.reshape(x.shape[:2] + (F_HEADS, F_DH))
    nback = CONV_W - 1
    return (y_prompt, y_sample,
            heads(_unpad_prompt(zk)), heads(_unpad_prompt(zv)), _unpad_prompt(lf),
            heads(_unpad_sample(zk)), heads(_unpad_sample(zv)), _unpad_sample(lf),
            p_c, p_n, p_m[:, :, 0], s_c, s_n, s_m[:, :, 0],
            p_s, s_s,
            p_h[:, 0], s_h[:, 0], p_tail[:, 8 - nback:], s_tail[:, 8 - nback:])
```

```python
import functools
import math

import jax
import jax.numpy as jnp
import numpy as np
from jax import lax
from jax.experimental import pallas as pl
from jax.experimental.pallas import tpu as pltpu

F32 = jnp.float32
BF16 = jnp.bfloat16

D_MODEL = 1024
BATCH = 4
SEQ = 4096
DEC_BATCH = 32
DEC_SEQ = 4
PAGE_SIZE = 128
N_META = 16
CHUNK = 128
M_HEADS = 4
F_HEADS = 8
F_DH = 64
R_HEADS = 4
HEAD_DIM = 128
ROPE_BASE = 10000.0
LRU_WIDTH = 512
LRU_C = 8.0
CONV_W = 4
D_FF = 2816
EPS = 1e-6
NEG = -1e30

LANES = 128
PADF = CHUNK - N_META
TP = PADF + N_META + SEQ
NCH = TP // CHUNK
SROWS = 16
SPAD = SROWS - DEC_SEQ
SBASE = BATCH * TP
NP = SBASE + DEC_BATCH * SROWS
TM = 512
SEQ_GROUP = 4
MXU_WIDTH = 256
MIX_FFN_VMEM_BYTES = 56 * 1024 * 1024
FOX_TQ = 512
FOX_CK = 512
LOG2E = math.log2(math.e)
DEC_PG = 16
PRE_PG = 128


def _bf(x):
    return x.astype(BF16)


def _dot(a, b):
    return jnp.dot(a, b, preferred_element_type=F32)


def _dot_nt(a, b):
    return lax.dot_general(a, b, (((1,), (1,)), ((), ())), preferred_element_type=F32)


def _dot_tn(a, b):
    return lax.dot_general(a, b, (((0,), (0,)), ((), ())), preferred_element_type=F32)


def _split3(x):
    hi = _bf(x)
    r1 = x - hi.astype(F32)
    mid = _bf(r1)
    lo = _bf(r1 - mid.astype(F32))
    return hi, mid, lo


def _dot01_right(x, m01):
    hi, mid, lo = _split3(x)
    return _dot(hi, m01) + _dot(mid, m01) + _dot(lo, m01)


def _dot01_left(m01, x):
    hi, mid, lo = _split3(x)
    return _dot(m01, hi) + _dot(m01, mid) + _dot(m01, lo)


def _iota(shape, dim):
    return lax.broadcasted_iota(jnp.int32, shape, dim)


def _rms(x, g):
    ms = jnp.mean(x * x, axis=-1, keepdims=True)
    return x * lax.rsqrt(ms + EPS) * g


def _softplus(x):
    return jnp.maximum(x, 0.0) + jnp.log1p(jnp.exp(-jnp.abs(x)))


def _log_sigmoid(x):
    return -_softplus(-x)


def _sigmoid(x):
    return 0.5 * jnp.tanh(0.5 * x) + 0.5


def _nm_kernel(x_ref, g_ref, w_ref, *o_refs, splits):
    last = pl.program_id(0) == pl.num_programs(0) - 1
    xn = _bf(_rms(x_ref[...], g_ref[...]))
    off = 0
    for k, n in enumerate(splits):
        res = _dot(xn, w_ref[:, off:off + n])
        off += n

        @pl.when(last)
        def _(res=res, k=k):
            o_refs[2 * k + 1][...] = res

        @pl.when(jnp.logical_not(last))
        def _(res=res, k=k):
            o_refs[2 * k][...] = res


def norm_matmul(x, g, w, splits):
    n_rows, k = x.shape
    assert n_rows == SBASE + TM and SBASE % TM == 0
    n_prompt = SBASE // TM
    prompt_blk = lambda i: (jnp.minimum(i, n_prompt - 1), 0)
    const = lambda i: (0, 0)
    out_specs = []
    out_shape = []
    for n in splits:
        out_specs += [pl.BlockSpec((TM, n), prompt_blk), pl.BlockSpec((TM, n), const)]
        out_shape += [jax.ShapeDtypeStruct((SBASE, n), F32), jax.ShapeDtypeStruct((TM, n), F32)]
    return pl.pallas_call(
        functools.partial(_nm_kernel, splits=splits),
        grid=(n_rows // TM,),
        in_specs=[pl.BlockSpec((TM, k), lambda i: (i, 0)),
                  pl.BlockSpec((1, k), const),
                  pl.BlockSpec((k, sum(splits)), const, pipeline_mode=pl.Buffered(1))],
        out_specs=out_specs,
        out_shape=out_shape,
        compiler_params=pltpu.CompilerParams(dimension_semantics=("arbitrary",)),
        name="norm_matmul",
    )(x, g, w)


def _mix_ffn_kernel(a1p_ref, a1s_ref, a2p_ref, a2s_ref, h_ref, w1_ref, w2_ref, g_ref, wg_ref, wu_ref, wd_ref,
                    *o_refs, split):
    last = pl.program_id(0) == pl.num_programs(0) - 1
    a1 = jnp.where(last, a1s_ref[...], a1p_ref[...])
    a2 = jnp.where(last, a2s_ref[...], a2p_ref[...])
    mix = _dot(_bf(a1), w1_ref[...]) + _dot(_bf(a2), w2_ref[...])
    h1 = h_ref[...] + _rms(mix, g_ref[0:1, :])
    xn = _bf(_rms(h1, g_ref[1:2, :]))
    acts = []
    for c in range(wg_ref.shape[1] // MXU_WIDTH):
        cols = slice(c * MXU_WIDTH, (c + 1) * MXU_WIDTH)
        gate = _dot(xn, wg_ref[:, cols])
        up = _dot(xn, wu_ref[:, cols])
        acts.append(_bf(gate * _sigmoid(gate) * up))
    ff = _dot(jnp.concatenate(acts, axis=1), wd_ref[...])
    res = h1 + _rms(ff, g_ref[2:3, :])
    if split:
        op_ref, os_ref = o_refs

        @pl.when(last)
        def _():
            os_ref[...] = res

        @pl.when(jnp.logical_not(last))
        def _():
            op_ref[...] = res
    else:
        o_refs[0][...] = res


def mix_ffn(a1p, a1s, a2p, a2s, h, w1, w2, g, wg, wu, wd, split=False):
    assert a1p.shape[0] == SBASE and a1s.shape[0] == TM and SBASE % TM == 0 and h.shape[0] == SBASE + TM
    k1 = a1p.shape[1]
    k2 = a2p.shape[1]
    n_rows, d = h.shape
    dff = wg.shape[1]
    assert dff % MXU_WIDTH == 0
    n_prompt = SBASE // TM
    prompt_blk = lambda i: (jnp.minimum(i, n_prompt - 1), 0)
    const = lambda i: (0, 0)
    resident = lambda shape: pl.BlockSpec(shape, const, pipeline_mode=pl.Buffered(1))
    if split:
        out_specs = [pl.BlockSpec((TM, d), prompt_blk), pl.BlockSpec((TM, d), const)]
        out_shape = [jax.ShapeDtypeStruct((SBASE, d), F32), jax.ShapeDtypeStruct((TM, d), F32)]
    else:
        out_specs = pl.BlockSpec((TM, d), lambda i: (i, 0))
        out_shape = jax.ShapeDtypeStruct((n_rows, d), F32)
    return pl.pallas_call(
        functools.partial(_mix_ffn_kernel, split=split),
        grid=(n_rows // TM,),
        in_specs=[pl.BlockSpec((TM, k1), prompt_blk),
                  resident((TM, k1)),
                  pl.BlockSpec((TM, k2), prompt_blk),
                  resident((TM, k2)),
                  pl.BlockSpec((TM, d), lambda i: (i, 0)),
                  resident((k1, d)),
                  resident((k2, d)),
                  resident((3, d)),
                  resident((d, dff)),
                  resident((d, dff)),
                  resident((dff, d))],
        out_specs=out_specs,
        out_shape=out_shape,
        compiler_params=pltpu.CompilerParams(dimension_semantics=("arbitrary",),
                                             vmem_limit_bytes=MIX_FFN_VMEM_BYTES),
        name="mix_ffn",
    )(a1p, a1s, a2p, a2s, h, w1, w2, g, wg, wu, wd)


def _gates_kernel(z_ref, zt_ref, brow_ref, bcol_ref, o_ref, ot_ref):
    x = z_ref[...] + brow_ref[...]
    o_ref[...] = jnp.where(_iota(x.shape, 1) < M_HEADS, x, _log_sigmoid(x))
    xt = zt_ref[...] + bcol_ref[...]
    ot_ref[...] = jnp.where(_iota(xt.shape, 0) < M_HEADS, xt, _log_sigmoid(xt))


def gates_activate(zg, zgt, brow, bcol):
    n_rows = zg.shape[0]
    return pl.pallas_call(
        _gates_kernel,
        grid=(n_rows // TM,),
        in_specs=[pl.BlockSpec((TM, LANES), lambda i: (i, 0)),
                  pl.BlockSpec((16, TM), lambda i: (0, i)),
                  pl.BlockSpec((1, LANES), lambda i: (0, 0)),
                  pl.BlockSpec((16, 1), lambda i: (0, 0))],
        out_specs=[pl.BlockSpec((TM, LANES), lambda i: (i, 0)),
                   pl.BlockSpec((16, TM), lambda i: (0, i))],
        out_shape=[jax.ShapeDtypeStruct((n_rows, LANES), F32),
                   jax.ShapeDtypeStruct((16, n_rows), F32)],
        compiler_params=pltpu.CompilerParams(dimension_semantics=("parallel",)),
        name="gates_activate",
    )(zg, zgt, brow, bcol)


def _mlstm_chunk(q, k, v, logi_c, lf_c, logi_r, lf_r, c_state, n_state, m_state, row0, rows):
    ri = _iota((rows, rows), 0)
    ci = _iota((rows, rows), 1)
    valid_c = _iota((rows, 1), 0) >= row0
    valid_r = _iota((1, rows), 1) >= row0
    lf_c = jnp.where(valid_c, lf_c, 0.0)
    lf_r = jnp.where(valid_r, lf_r, 0.0)
    logi_c = jnp.where(valid_c, logi_c, NEG)
    logi_r = jnp.where(valid_r, logi_r, NEG)
    causal = ci <= ri
    b_c = jnp.sum(jnp.where(causal, lf_r, 0.0), axis=1, keepdims=True)
    b_r = jnp.sum(jnp.where(ri <= ci, lf_c, 0.0), axis=0, keepdims=True)
    log_d = jnp.where(causal, b_c - b_r + logi_r, -jnp.inf)
    m_inter = b_c + m_state
    m_t = jnp.maximum(m_inter, jnp.max(log_d, axis=1, keepdims=True))
    qb = _bf(q)
    kb = _bf(k)
    vb = _bf(v)
    w_intra = _dot_nt(qb, kb) * jnp.exp(log_d - m_t)
    w_inter = jnp.exp(m_inter - m_t)
    num = _dot(_bf(w_intra), vb) + w_inter * _dot(qb, _bf(c_state))
    den = jnp.sum(w_intra, axis=1, keepdims=True) + w_inter * jnp.sum(q * n_state, axis=1, keepdims=True)
    h = num / jnp.maximum(jnp.abs(den), jnp.exp(-m_t))
    m_new = m_t[rows - 1:rows, :]
    b_last = b_c[rows - 1:rows, :]
    w_src = jnp.exp(b_last - b_c + logi_c - m_new)
    decay = jnp.exp(b_last + m_state - m_new)
    ks = k * w_src
    c_new = decay * c_state + _dot_tn(_bf(ks), vb)
    n_new = decay * n_state + jnp.sum(ks, axis=0, keepdims=True)
    return h, c_new, n_new, m_new


def _mlstm_kernel(z_ref, g_ref, gt_ref, c0_ref, n0_ref, m0_ref, o_ref, c_ref, n_ref, m_ref, *, rows, row0_first):
    c = pl.program_id(1)

    @pl.when(c == 0)
    def _():
        c_ref[...] = c0_ref[...]
        n_ref[...] = n0_ref[...]
        m_ref[...] = m0_ref[...]

    row0 = jnp.where(c == 0, row0_first, 0)
    for s in range(SEQ_GROUP):
        g = g_ref[s]
        gt = gt_ref[s, 0]
        outs = []
        for h in range(M_HEADS):
            lo = h * HEAD_DIM
            q = z_ref[s, :, lo:lo + HEAD_DIM]
            k = z_ref[s, :, 512 + lo:512 + lo + HEAD_DIM] * (HEAD_DIM ** -0.5)
            v = z_ref[s, :, 1024 + lo:1024 + lo + HEAD_DIM]
            og = z_ref[s, :, 1536 + lo:1536 + lo + HEAD_DIM]
            hh, c_new, n_new, m_new = _mlstm_chunk(
                q, k, v,
                g[:, h:h + 1], g[:, M_HEADS + h:M_HEADS + h + 1],
                gt[h:h + 1, :], gt[M_HEADS + h:M_HEADS + h + 1, :],
                c_ref[s, h], n_ref[s, h:h + 1, :], m_ref[s, h:h + 1, 0:1], row0, rows)
            c_ref[s, h] = c_new
            n_ref[s, h:h + 1, :] = n_new
            m_ref[s, h:h + 1, :] = jnp.broadcast_to(m_new, (1, LANES))
            outs.append(_sigmoid(og) * hh)
        o_ref[s] = jnp.concatenate(outs, axis=1)


def _state_specs(heads):
    return [pl.BlockSpec((SEQ_GROUP, heads, HEAD_DIM, HEAD_DIM), lambda b, c: (b, 0, 0, 0)),
            pl.BlockSpec((SEQ_GROUP, heads, HEAD_DIM), lambda b, c: (b, 0, 0)),
            pl.BlockSpec((SEQ_GROUP, heads, LANES), lambda b, c: (b, 0, 0))]


def mlstm_call(z, g2, g2t, c0, n0, m0, *, row0_first):
    nb, t_len, _ = z.shape
    n_chunks, rows = g2t.shape[1], g2t.shape[3]
    assert nb % SEQ_GROUP == 0 and n_chunks * rows == t_len
    blk = lambda b, c: (b, c, 0)
    return pl.pallas_call(
        functools.partial(_mlstm_kernel, rows=rows, row0_first=row0_first),
        grid=(nb // SEQ_GROUP, n_chunks),
        in_specs=[pl.BlockSpec((SEQ_GROUP, rows, 2048), blk),
                  pl.BlockSpec((SEQ_GROUP, rows, LANES), blk),
                  pl.BlockSpec((SEQ_GROUP, 1, 16, rows), lambda b, c: (b, c, 0, 0))] + _state_specs(M_HEADS),
        out_specs=[pl.BlockSpec((SEQ_GROUP, rows, 512), blk)] + _state_specs(M_HEADS),
        out_shape=[jax.ShapeDtypeStruct((nb, t_len, 512), F32),
                   jax.ShapeDtypeStruct((nb, M_HEADS, HEAD_DIM, HEAD_DIM), F32),
                   jax.ShapeDtypeStruct((nb, M_HEADS, HEAD_DIM), F32),
                   jax.ShapeDtypeStruct((nb, M_HEADS, LANES), F32)],
        compiler_params=pltpu.CompilerParams(dimension_semantics=("parallel", "arbitrary")),
        name="mlstm",
    )(z, g2, g2t, c0, n0, m0)


def _ret_log_gamma(h):
    return math.log1p(-(2.0 ** (-5.0 - h)))


def _ret_kernel(z_ref, cos_ref, sin_ref, lng_ref, s0_ref, o_ref, s_ref, *, rows, row0_first):
    c = pl.program_id(1)

    @pl.when(c == 0)
    def _():
        s_ref[...] = s0_ref[...]

    row0 = jnp.where(c == 0, row0_first, 0)
    n_valid = (rows - row0).astype(F32)
    cosf = cos_ref[...]
    sinf = sin_ref[...]
    ri = _iota((rows, rows), 0)
    ci = _iota((rows, rows), 1)
    diff = (ri - ci).astype(F32)
    rowi = _iota((rows, 1), 0)
    te = (rowi - row0).astype(F32)
    valid = rowi >= row0
    for h in range(R_HEADS):
        lg = _ret_log_gamma(h)
        lo = h * HEAD_DIM
        decay_m = jnp.where(diff >= 0, jnp.exp(diff * lg), 0.0)
        w_inter = jnp.exp((te + 1.0) * lg)
        w_src = jnp.exp((n_valid - 1.0 - te) * lg)
        s_decay = jnp.exp(n_valid * lg)
        for s in range(SEQ_GROUP):
            q = z_ref[s, :, lo:lo + HEAD_DIM]
            k = z_ref[s, :, 512 + lo:512 + lo + HEAD_DIM]
            v = jnp.where(valid, z_ref[s, :, 1024 + lo:1024 + lo + HEAD_DIM], 0.0)
            rg = z_ref[s, :, 1536 + lo:1536 + lo + HEAD_DIM]
            q = q * cosf + pltpu.roll(q, HEAD_DIM // 2, 1) * sinf
            k = (k * cosf + pltpu.roll(k, HEAD_DIM // 2, 1) * sinf) * (HEAD_DIM ** -0.5)
            s_state = s_ref[s, h]
            qb = _bf(q)
            vb = _bf(v)
            intra = _dot(_bf(_dot_nt(qb, _bf(k)) * decay_m), vb)
            inter = _dot(qb, _bf(s_state)) * w_inter
            s_ref[s, h] = s_decay * s_state + _dot_tn(_bf(k * w_src), vb)
            hr = intra + inter
            mu = jnp.mean(hr, axis=-1, keepdims=True)
            var = jnp.mean(jnp.square(hr - mu), axis=-1, keepdims=True)
            hr = (hr - mu) * lax.rsqrt(var + EPS) * lng_ref[h:h + 1, :]
            o_ref[s, :, lo:lo + HEAD_DIM] = rg * _sigmoid(rg) * hr


def retention_call(z, cosf, sinf, lng, s0, *, rows, row0_first):
    nb, t_len, _ = z.shape
    assert nb % SEQ_GROUP == 0 and t_len % rows == 0
    blk = lambda b, c: (b, c, 0)
    sspec = pl.BlockSpec((SEQ_GROUP, R_HEADS, HEAD_DIM, HEAD_DIM), lambda b, c: (b, 0, 0, 0))
    return pl.pallas_call(
        functools.partial(_ret_kernel, rows=rows, row0_first=row0_first),
        grid=(nb // SEQ_GROUP, t_len // rows),
        in_specs=[pl.BlockSpec((SEQ_GROUP, rows, 2048), blk),
                  pl.BlockSpec((rows, HEAD_DIM), lambda b, c: (c, 0)),
                  pl.BlockSpec((rows, HEAD_DIM), lambda b, c: (c, 0)),
                  pl.BlockSpec((R_HEADS, HEAD_DIM), lambda b, c: (0, 0)),
                  sspec],
        out_specs=[pl.BlockSpec((SEQ_GROUP, rows, 512), blk), sspec],
        out_shape=[jax.ShapeDtypeStruct((nb, t_len, 512), F32),
                   jax.ShapeDtypeStruct((nb, R_HEADS, HEAD_DIM, HEAD_DIM), F32)],
        compiler_params=pltpu.CompilerParams(dimension_semantics=("parallel", "arbitrary")),
        name="retention",
    )(z, cosf, sinf, lng, s0)


def _lru_kernel(z_ref, pre_ref, h0_ref, cw_ref, cb_ref, wa_ref, ba_ref, wx_ref, bx_ref, lam_ref,
                o_ref, hlast_ref, tail_ref, cbuf_s, a_s, b_s, *, rows, row0_first):
    c = pl.program_id(1)

    @pl.when(c == 0)
    def _():
        hlast_ref[...] = h0_ref[...]
        tail_ref[...] = jnp.zeros(tail_ref.shape, F32)

    row0 = jnp.where(c == 0, row0_first, 0)
    rowi = _iota((rows, 1), 0)
    valid = rowi >= row0
    sp_lam = _softplus(-lam_ref[...])
    for s in range(SEQ_GROUP):
        lx = jnp.where(valid, z_ref[s, :, 0:LRU_WIDTH], pre_ref[min(s, pre_ref.shape[0] - 1)])
        cbuf_s[s, 0:8, :] = tail_ref[s]
        cbuf_s[s, 8:8 + rows, :] = lx
        xc = cb_ref[...] + cw_ref[CONV_W - 1:CONV_W, :] * lx
        for j in range(1, CONV_W):
            xc = xc + cw_ref[CONV_W - 1 - j:CONV_W - j, :] * cbuf_s[s, 8 - j:8 - j + rows, :]
        tail_ref[s] = lx[rows - 8:rows]
        xcb = _bf(xc)
        pre_a = []
        pre_x = []
        for p in range(LRU_WIDTH // LANES):
            xs = xcb[:, p * LANES:(p + 1) * LANES]
            pre_a.append(_dot(xs, wa_ref[p]))
            pre_x.append(_dot(xs, wx_ref[p]))
        r = _sigmoid(jnp.concatenate(pre_a, axis=1) + ba_ref[...])
        ig = _sigmoid(jnp.concatenate(pre_x, axis=1) + bx_ref[...])
        log_a = -LRU_C * r * sp_lam
        a = jnp.where(valid, jnp.exp(log_a), 1.0)
        one_minus_a2 = -jnp.tanh(log_a) * (jnp.exp(2.0 * log_a) + 1.0)
        bx = jnp.where(valid, jnp.sqrt(one_minus_a2) * (ig * xc), 0.0)
        a_s[s] = a
        b_s[s] = bx
    carries = [hlast_ref[s] for s in range(SEQ_GROUP)]
    for t in range(rows):
        for s in range(SEQ_GROUP):
            carries[s] = a_s[s, t:t + 1, :] * carries[s] + b_s[s, t:t + 1, :]
            b_s[s, t:t + 1, :] = carries[s]
    for s in range(SEQ_GROUP):
        hlast_ref[s] = carries[s]
        gate = z_ref[s, :, LRU_WIDTH:2 * LRU_WIDTH]
        gl = 0.5 * gate * (1.0 + jnp.tanh(math.sqrt(2.0 / math.pi) * (gate + 0.044715 * gate * gate * gate)))
        o_ref[s] = b_s[s] * gl


def lru_call(z, pre, h0, cw, cb, wa2, ba, wx2, bx, lam, *, rows, row0_first):
    nb, t_len, _ = z.shape
    assert nb % SEQ_GROUP == 0 and t_len % rows == 0
    blk = lambda b, c: (b, c, 0)
    per_seq = lambda b, c: (b, 0, 0)
    const2 = lambda b, c: (0, 0)
    const3 = lambda b, c: (0, 0, 0)
    shared_pre = pre.shape[0] == 1
    in_specs = [pl.BlockSpec((SEQ_GROUP, rows, 1024), blk),
                pl.BlockSpec((1 if shared_pre else SEQ_GROUP, rows, LRU_WIDTH), const3 if shared_pre else per_seq),
                pl.BlockSpec((SEQ_GROUP, 1, LRU_WIDTH), per_seq),
                pl.BlockSpec((CONV_W, LRU_WIDTH), const2),
                pl.BlockSpec((1, LRU_WIDTH), const2),
                pl.BlockSpec((LRU_WIDTH // LANES, LANES, LANES), const3),
                pl.BlockSpec((1, LRU_WIDTH), const2),
                pl.BlockSpec((LRU_WIDTH // LANES, LANES, LANES), const3),
                pl.BlockSpec((1, LRU_WIDTH), const2),
                pl.BlockSpec((1, LRU_WIDTH), const2)]
    return pl.pallas_call(
        functools.partial(_lru_kernel, rows=rows, row0_first=row0_first),
        grid=(nb // SEQ_GROUP, t_len // rows),
        in_specs=in_specs,
        out_specs=[pl.BlockSpec((SEQ_GROUP, rows, LRU_WIDTH), blk),
                   pl.BlockSpec((SEQ_GROUP, 1, LRU_WIDTH), per_seq),
                   pl.BlockSpec((SEQ_GROUP, 8, LRU_WIDTH), per_seq)],
        out_shape=[jax.ShapeDtypeStruct((nb, t_len, LRU_WIDTH), F32),
                   jax.ShapeDtypeStruct((nb, 1, LRU_WIDTH), F32),
                   jax.ShapeDtypeStruct((nb, 8, LRU_WIDTH), F32)],
        scratch_shapes=[pltpu.VMEM((SEQ_GROUP, rows + 8, LRU_WIDTH), F32),
                        pltpu.VMEM((SEQ_GROUP, rows, LRU_WIDTH), F32),
                        pltpu.VMEM((SEQ_GROUP, rows, LRU_WIDTH), F32)],
        compiler_params=pltpu.CompilerParams(dimension_semantics=("parallel", "arbitrary")),
        name="rglru",
    )(z, pre, h0, cw, cb, wa2, ba, wx2, bx, lam)


def _fox_cumsum_kernel(g_ref, gt_ref, fc_ref, fr_ref):
    ri = _iota((CHUNK, CHUNK), 0)
    ci = _iota((CHUNK, CHUNK), 1)
    lower = jnp.where(ci <= ri, 1.0, 0.0).astype(BF16)
    upper = jnp.where(ri <= ci, 1.0, 0.0).astype(BF16)
    carry_r = jnp.zeros((1, LANES), F32)
    carry_c = jnp.zeros((16, 1), F32)
    for blk in range(NCH):
        x = g_ref[blk * CHUNK:(blk + 1) * CHUNK, :]
        xt = gt_ref[:, blk * CHUNK:(blk + 1) * CHUNK]
        if blk == 0:
            x = jnp.where(_iota((CHUNK, 1), 0) >= PADF, x, 0.0)
            xt = jnp.where(_iota((1, CHUNK), 1) >= PADF, xt, 0.0)
        cs = _dot01_left(lower, x) + carry_r
        cst = _dot01_right(xt, upper) + carry_c
        carry_r = cs[CHUNK - 1:CHUNK, :]
        carry_c = cst[:, CHUNK - 1:CHUNK]
        if blk == 0:
            cst = jnp.where(_iota((1, CHUNK), 1) >= PADF, cst, -NEG)
        fc_ref[0, blk * CHUNK:(blk + 1) * CHUNK, :] = cs
        fr_ref[0, :, blk * CHUNK:(blk + 1) * CHUNK] = cst


def fox_cumsum(g2, g2t):
    return pl.pallas_call(
        _fox_cumsum_kernel,
        grid=(BATCH,),
        in_specs=[pl.BlockSpec((TP, LANES), lambda b: (b, 0)),
                  pl.BlockSpec((16, TP), lambda b: (0, b))],
        out_specs=[pl.BlockSpec((1, TP, LANES), lambda b: (b, 0, 0)),
                   pl.BlockSpec((1, 16, TP), lambda b: (b, 0, 0))],
        out_shape=[jax.ShapeDtypeStruct((BATCH, TP, LANES), F32),
                   jax.ShapeDtypeStruct((BATCH, 16, TP), F32)],
        compiler_params=pltpu.CompilerParams(dimension_semantics=("parallel",)),
        name="fox_cumsum",
    )(g2, g2t)


def _fox_prompt_kernel(q_ref, k_ref, v_ref, fr_ref, fc_ref, o_ref,
                       qt_s, ka_s, kb_s, vt_s, fk_s, m_s, l_s, acc_s, ua_s):
    i = pl.program_id(2)
    is_a = _iota((1, LANES), 1) < F_DH
    row_a = _iota((LANES, 1), 0) < F_DH
    k_heads = (ka_s, kb_s)

    @pl.when(i == 0)
    def _():
        k = k_ref[...]
        ka_s[...] = _bf(jnp.where(is_a, k, 0.0))
        kb_s[...] = _bf(jnp.where(is_a, 0.0, k))

        def fill(blk, carry):
            rows = pl.ds(pl.multiple_of(blk * CHUNK, CHUNK), CHUNK)
            qt_s[blk] = _bf((q_ref[rows, :] * (F_DH ** -0.5 * LOG2E)).T)
            vt_s[blk] = _bf(v_ref[rows, :].T)
            return carry

        lax.fori_loop(0, NCH, fill, 0)
        is_pad = _iota((TP, 1), 0) < PADF
        for hh in range(2):
            fk = jnp.where(is_pad, -NEG, fc_ref[0, 0, :, hh:hh + 1] * LOG2E)
            fk_s[hh] = jnp.broadcast_to(fk, (TP, LANES))

    def init():
        m_s[...] = jnp.full(m_s.shape, NEG, F32)
        l_s[...] = jnp.zeros(l_s.shape, F32)
        acc_s[...] = jnp.zeros(acc_s.shape, F32)

    def scores(hh, qt, kblk, nkb):
        ck = nkb * CHUNK
        k0 = kblk * CHUNK if isinstance(kblk, int) else pl.multiple_of(kblk * CHUNK, CHUNK)
        fk = jnp.concatenate([fk_s[hh, pl.ds(k0, ck), :]] * (qt.shape[1] // LANES), axis=1)
        return _dot(k_heads[hh][pl.ds(k0, ck), :], qt) - fk

    def step(hh, u, qblk, nqb, kblk, nkb, causal):
        tq = nqb * CHUNK
        ck = nkb * CHUNK
        fq = jnp.concatenate([fr_ref[0, 0, qblk + t, hh:hh + 1, :] for t in range(nqb)], axis=1)
        fq = jnp.where(fq > 0.5 * -NEG, 0.0, fq * LOG2E)
        if causal:
            u = jnp.where(kblk * CHUNK + _iota((ck, tq), 0) <= qblk * CHUNK + _iota((ck, tq), 1), u, NEG)
        m_old = m_s[hh, :, 0:tq]
        m_new = jnp.maximum(m_old, jnp.max(u, axis=0, keepdims=True) + fq)
        alpha = jnp.exp2(m_old - m_new)
        p = jnp.exp2(u + (fq - m_new))
        l_s[hh, :, 0:tq] = alpha * l_s[hh, :, 0:tq] + jnp.sum(p, axis=0, keepdims=True)
        m_s[hh, :, 0:tq] = m_new
        feat = slice(hh * F_DH, (hh + 1) * F_DH)
        vt = jnp.concatenate([vt_s[kblk + t, feat, :] for t in range(nkb)], axis=1)
        acc_s[feat, 0:tq] = alpha * acc_s[feat, 0:tq] + _dot(vt, _bf(p))

    def finalize(qblk, nqb):
        tq = nqb * CHUNK
        o_t = acc_s[:, 0:tq] / jnp.where(row_a, l_s[0, :, 0:tq], l_s[1, :, 0:tq])
        for t in range(nqb):
            rows = pl.ds(pl.multiple_of((qblk + t) * CHUNK, CHUNK), CHUNK)
            o_ref[rows, :] = o_t[:, t * CHUNK:(t + 1) * CHUNK].T

    @pl.when(i == 0)
    def _():
        init()
        qt0 = qt_s[0]
        for hh in range(2):
            step(hh, scores(hh, qt0, 0, 1), 0, 1, 0, 1, True)
        finalize(0, 1)

    nck = FOX_CK // CHUNK
    nqb = FOX_TQ // CHUNK
    qblk = 1 + i * nqb
    qt = jnp.concatenate([qt_s[qblk + t] for t in range(nqb)], axis=1)
    init()
    for hh in range(2):
        step(hh, scores(hh, qt, 0, 1), qblk, nqb, 0, 1, False)

    ua_s[...] = scores(0, qt, 1, nck)

    def body(j, carry):
        kblk = 1 + j * nck
        ub = scores(1, qt, kblk, nck)
        step(0, ua_s[...], qblk, nqb, kblk, nck, False)
        ua_s[...] = scores(0, qt, kblk + nck, nck)
        step(1, ub, qblk, nqb, kblk, nck, False)
        return carry

    lax.fori_loop(0, i, body, 0)
    ub = scores(1, qt, qblk, nck)
    step(0, ua_s[...], qblk, nqb, qblk, nck, True)
    step(1, ub, qblk, nqb, qblk, nck, True)
    finalize(qblk, nqb)


def fox_prompt(zq, zk, zv, fr, fc):
    nq = (TP - CHUNK) // FOX_TQ
    pairs = F_HEADS // 2
    return pl.pallas_call(
        _fox_prompt_kernel,
        grid=(BATCH, pairs, nq),
        in_specs=[pl.BlockSpec((TP, LANES), lambda b, p, i: (b, p)),
                  pl.BlockSpec((TP, LANES), lambda b, p, i: (b, p)),
                  pl.BlockSpec((TP, LANES), lambda b, p, i: (b, p)),
                  pl.BlockSpec((1, 1, NCH, 2, LANES), lambda b, p, i: (b, p, 0, 0, 0)),
                  pl.BlockSpec((1, 1, TP, 2), lambda b, p, i: (b, p, 0, 0))],
        out_specs=pl.BlockSpec((TP, LANES), lambda b, p, i: (b, p)),
        out_shape=jax.ShapeDtypeStruct((SBASE, 512), F32),
        scratch_shapes=[pltpu.VMEM((NCH, LANES, CHUNK), BF16),
                        pltpu.VMEM((TP, LANES), BF16),
                        pltpu.VMEM((TP, LANES), BF16),
                        pltpu.VMEM((NCH, LANES, CHUNK), BF16),
                        pltpu.VMEM((2, TP, LANES), F32),
                        pltpu.VMEM((2, 1, FOX_TQ), F32),
                        pltpu.VMEM((2, 1, FOX_TQ), F32),
                        pltpu.VMEM((LANES, FOX_TQ), F32),
                        pltpu.VMEM((FOX_CK, FOX_TQ), F32)],
        compiler_params=pltpu.CompilerParams(dimension_semantics=("parallel", "parallel", "arbitrary")),
        name="fox_prompt",
    )(zq, zk, zv, fr, fc)


def _lf_suffix_kernel(x_ref, suf_ref, tot_ref):
    ri = _iota((PAGE_SIZE, PAGE_SIZE), 0)
    ci = _iota((PAGE_SIZE, PAGE_SIZE), 1)
    after = jnp.where(ri > ci, 1.0, 0.0).astype(BF16)
    ones = jnp.ones((PAGE_SIZE, PAGE_SIZE), BF16)
    x = x_ref[...].reshape(PRE_PG * F_HEADS, PAGE_SIZE)
    suf_ref[...] = _dot01_right(x, after).reshape(PRE_PG, F_HEADS, PAGE_SIZE)
    tot_ref[...] = _dot01_right(x, ones).reshape(PRE_PG, F_HEADS, PAGE_SIZE)


def lf_suffix(lft):
    n_phys = lft.shape[0]
    spec = pl.BlockSpec((PRE_PG, F_HEADS, PAGE_SIZE), lambda i: (i, 0, 0))
    return pl.pallas_call(
        _lf_suffix_kernel,
        grid=(n_phys // PRE_PG,),
        in_specs=[spec],
        out_specs=[spec, spec],
        out_shape=[jax.ShapeDtypeStruct(lft.shape, F32)] * 2,
        compiler_params=pltpu.CompilerParams(dimension_semantics=("parallel",)),
        name="lf_suffix",
    )(lft)


def _fox_decode_kernel(pt_ref, zq_ref, zk_ref, zv_ref, gt_ref, *rest):
    k_refs = rest[0:DEC_PG]
    v_refs = rest[DEC_PG:2 * DEC_PG]
    suf_refs = rest[2 * DEC_PG:3 * DEC_PG]
    tot_refs = rest[3 * DEC_PG:4 * DEC_PG]
    o_ref = rest[4 * DEC_PG]
    q2_s, m_s, l_s, acc_s, run_s, nc_s = rest[4 * DEC_PG + 1:]
    j = pl.program_id(1)
    n_rows = DEC_SEQ * F_HEADS
    width = F_HEADS * F_DH

    def update(u, v, v_feature_major):
        m_old = m_s[...]
        m_new = jnp.maximum(m_old, jnp.max(u, axis=1, keepdims=True))
        alpha = jnp.exp(m_old - m_new)
        p = jnp.exp(u - m_new[:, 0:u.shape[1]])
        pv = _dot_nt(_bf(p), v) if v_feature_major else _dot(_bf(p), v)
        l_s[...] = alpha * l_s[...] + jnp.sum(p, axis=1, keepdims=True)
        acc_s[...] = jnp.concatenate([alpha] * (width // LANES), axis=1) * acc_s[...] + pv
        m_s[...] = m_new

    @pl.when(j == 0)
    def _():
        m_s[...] = jnp.full(m_s.shape, NEG, F32)
        l_s[...] = jnp.zeros(l_s.shape, F32)
        acc_s[...] = jnp.zeros(acc_s.shape, F32)
        run_s[...] = jnp.zeros(run_s.shape, F32)
        head_mask = _iota((F_HEADS, width), 1) // F_DH == _iota((F_HEADS, width), 0)
        q16 = zq_ref[...] * (F_DH ** -0.5)
        q2 = _bf(jnp.concatenate(
            [jnp.where(head_mask, q16[SPAD + t:SPAD + t + 1, :], 0.0) for t in range(DEC_SEQ)], axis=0))
        q2_s[...] = q2
        lf_new = gt_ref[F_HEADS:2 * F_HEADS, :]
        lane = _iota((1, SROWS), 1)
        cum = jnp.zeros((F_HEADS, SROWS), F32)
        for t in range(DEC_SEQ):
            cum = cum + jnp.where(lane >= SPAD + t, lf_new[:, SPAD + t:SPAD + t + 1], 0.0)
        nc = jnp.concatenate([cum[:, SPAD + t:SPAD + t + 1] for t in range(DEC_SEQ)], axis=0)
        nc_s[...] = jnp.broadcast_to(nc, nc_s.shape)
        cum4 = jnp.concatenate([cum] * DEC_SEQ, axis=0)
        qi = _iota((n_rows, SROWS), 0) // F_HEADS
        kj = _iota((n_rows, SROWS), 1) - SPAD
        u = jnp.where((kj >= 0) & (kj <= qi), _dot_nt(q2, _bf(zk_ref[...])) + (nc - cum4), NEG)
        update(u, _bf(zv_ref[...]), False)

    q2 = q2_s[...]
    nc = nc_s[...]
    run = run_s[...]
    us = []
    for i in range(DEC_PG):
        later = run + suf_refs[i][0]
        bias = jnp.concatenate([later] * DEC_SEQ, axis=0) + nc
        us.append(_dot(q2, _bf(k_refs[i][0].reshape(width, PAGE_SIZE))) + bias)
        run = run + tot_refs[i][0]
    run_s[...] = run
    u = jnp.concatenate(us, axis=1)
    m_old = m_s[...]
    m_new = jnp.maximum(m_old, jnp.max(u, axis=1, keepdims=True))
    alpha = jnp.exp(m_old - m_new)
    p = jnp.exp(u - jnp.concatenate([m_new] * DEC_PG, axis=1))
    l_s[...] = alpha * l_s[...] + jnp.sum(p, axis=1, keepdims=True)
    pv = _dot_nt(_bf(p[:, 0:PAGE_SIZE]), _bf(v_refs[0][0].reshape(width, PAGE_SIZE)))
    for i in range(1, DEC_PG):
        pv = pv + _dot_nt(_bf(p[:, i * PAGE_SIZE:(i + 1) * PAGE_SIZE]), _bf(v_refs[i][0].reshape(width, PAGE_SIZE)))
    acc_s[...] = jnp.concatenate([alpha] * (width // LANES), axis=1) * acc_s[...] + pv
    m_s[...] = m_new

    @pl.when(j == pl.num_programs(1) - 1)
    def _():
        head_mask = (_iota((n_rows, width), 1) // F_DH) == (_iota((n_rows, width), 0) % F_HEADS)
        o2 = jnp.where(head_mask, acc_s[...] / jnp.concatenate([l_s[...]] * (width // LANES), axis=1), 0.0)
        o_ref[...] = jnp.zeros(o_ref.shape, F32)
        for t in range(DEC_SEQ):
            o_ref[SPAD + t:SPAD + t + 1, :] = jnp.sum(o2[t * F_HEADS:(t + 1) * F_HEADS, :], axis=0, keepdims=True)


def fox_decode(page_table, zq, zk, zv, g2t_s, cache_kt, cache_vt, suf, tot):
    n_pages = page_table.shape[1]
    width = F_HEADS * F_DH

    def page_map(i):
        return lambda b, j, pt: (pt[b, n_pages - 1 - (j * DEC_PG + i)], 0, 0, 0)

    def page_map3(i):
        return lambda b, j, pt: (pt[b, n_pages - 1 - (j * DEC_PG + i)], 0, 0)

    in_specs = [pl.BlockSpec((SROWS, width), lambda b, j, pt: (b, 0)),
                pl.BlockSpec((SROWS, width), lambda b, j, pt: (b, 0)),
                pl.BlockSpec((SROWS, width), lambda b, j, pt: (b, 0)),
                pl.BlockSpec((None, 16, SROWS), lambda b, j, pt: (b, 0, 0))]
    in_specs += [pl.BlockSpec((1, F_HEADS, F_DH, PAGE_SIZE), page_map(i)) for i in range(DEC_PG)]
    in_specs += [pl.BlockSpec((1, F_HEADS, F_DH, PAGE_SIZE), page_map(i)) for i in range(DEC_PG)]
    in_specs += [pl.BlockSpec((1, F_HEADS, PAGE_SIZE), page_map3(i)) for i in range(DEC_PG)]
    in_specs += [pl.BlockSpec((1, F_HEADS, PAGE_SIZE), page_map3(i)) for i in range(DEC_PG)]
    args = [zq, zk, zv, g2t_s] + [cache_kt] * DEC_PG + [cache_vt] * DEC_PG + [suf] * DEC_PG + [tot] * DEC_PG
    n_rows = DEC_SEQ * F_HEADS
    grid_spec = pltpu.PrefetchScalarGridSpec(
        num_scalar_prefetch=1,
        grid=(DEC_BATCH, n_pages // DEC_PG),
        in_specs=in_specs,
        out_specs=pl.BlockSpec((SROWS, width), lambda b, j, pt: (b, 0)),
        scratch_shapes=[pltpu.VMEM((n_rows, width), BF16),
                        pltpu.VMEM((n_rows, LANES), F32),
                        pltpu.VMEM((n_rows, LANES), F32),
                        pltpu.VMEM((n_rows, width), F32),
                        pltpu.VMEM((F_HEADS, PAGE_SIZE), F32),
                        pltpu.VMEM((n_rows, LANES), F32)])
    return pl.pallas_call(
        _fox_decode_kernel,
        grid_spec=grid_spec,
        out_shape=jax.ShapeDtypeStruct((DEC_BATCH * SROWS, width), F32),
        compiler_params=pltpu.CompilerParams(dimension_semantics=("parallel", "arbitrary")),
        name="fox_decode",
    )(page_table, *args)


def _block_diag_pairs(w):
    z = jnp.zeros((LANES // 2, LANES // 2), w.dtype)
    pairs = [jnp.block([[w[2 * p], z], [z, w[2 * p + 1]]]) for p in range(w.shape[0] // 2)]
    return _bf(jnp.stack(pairs))


def _rope_tables(pos):
    half = HEAD_DIM // 2
    freq = ROPE_BASE ** (-jnp.arange(half, dtype=F32) / half)
    ang = pos.astype(F32)[:, None] * freq[None, :]
    cos, sin = jnp.cos(ang), jnp.sin(ang)
    return jnp.concatenate([cos, cos], axis=1), jnp.concatenate([-sin, sin], axis=1)


def _unpad_prompt(x):
    return x.reshape((BATCH, TP) + x.shape[1:])[:, PADF:]


def _unpad_sample(x):
    return x.reshape((DEC_BATCH, SROWS) + x.shape[1:])[:, SPAD:]


def kernel(x_prompt, x_sample, cache_fox_k, cache_fox_v, cache_fox_logf, state_mlstm_C, state_mlstm_n, state_mlstm_m, state_ret_S, state_lru_h, state_lru_conv, page_table, meta_tokens, w_in_even, b_mlstm_i, b_mlstm_f, b_fox_f, w_out_even, w_in_odd, ret_ln_g, conv_w, conv_b, lru_wa, lru_ba, lru_wx, lru_bx, lru_lambda, w_out_odd, norm_g, ffn_wg, ffn_wu, ffn_wd):
    n_pages = page_table.shape[1]
    past = n_pages * PAGE_SIZE
    n_phys = cache_fox_k.shape[0]

    head = jnp.concatenate([jnp.zeros((PADF, D_MODEL), F32), meta_tokens.astype(F32)], axis=0)
    hs = jnp.concatenate([jnp.zeros((DEC_BATCH, SPAD, D_MODEL), F32), x_sample.astype(F32)],
                         axis=1).reshape(DEC_BATCH * SROWS, D_MODEL)
    pieces = []
    for b in range(BATCH):
        pieces += [head, x_prompt[b].astype(F32)]
    h = jnp.concatenate(pieces + [hs], axis=0)

    c = np.cumsum([0, 512, 512, 512, 512, 4, 4, 512, 512, 512, 8])
    gate_w = jnp.concatenate([w_in_even[:, c[4]:c[6]], w_in_even[:, c[9]:c[10]],
                              jnp.zeros((D_MODEL, LANES - 16), F32)], axis=1)
    w_even = _bf(jnp.concatenate([w_in_even[:, c[0]:c[4]], w_in_even[:, c[6]:c[9]], gate_w], axis=1))
    gate_b = jnp.concatenate([b_mlstm_i, b_mlstm_f, b_fox_f]).astype(F32)
    brow = jnp.concatenate([gate_b, jnp.zeros((LANES - 16,), F32)]).reshape(1, LANES)
    bcol = gate_b.reshape(16, 1)
    w_odd = _bf(w_in_odd)
    wo_even = _bf(w_out_even)
    wo_odd = _bf(w_out_odd)
    wg = _bf(ffn_wg)
    wu = _bf(ffn_wu)
    wd = _bf(ffn_wd)
    ng = norm_g.astype(F32).reshape(2, 4, 1, D_MODEL)

    (zm_p, zm_s, zq_p, zq_s, zk_p, zk_s, zv_p, zv_s, zg_p, zg_s) = norm_matmul(
        h, ng[0, 0], w_even, (2048, 512, 512, 512, LANES))
    g2_p, g2t_p = gates_activate(zg_p, zg_p[:, :16].T, brow, bcol)
    g2_s, g2t_s = gates_activate(zg_s, zg_s[:, :16].T, brow, bcol)
    prompt3 = lambda x: x.reshape(BATCH, TP, x.shape[-1])
    sample3 = lambda x: x.reshape(DEC_BATCH, SROWS, x.shape[-1])
    flat2 = lambda x: x.reshape(-1, x.shape[-1])

    zeros_c = jnp.zeros((BATCH, M_HEADS, HEAD_DIM, HEAD_DIM), F32)
    zeros_n = jnp.zeros((BATCH, M_HEADS, HEAD_DIM), F32)
    hm_p, p_c, p_n, p_m = mlstm_call(prompt3(zm_p), prompt3(g2_p),
                                     g2t_p.reshape(16, BATCH, NCH, CHUNK).transpose(1, 2, 0, 3),
                                     zeros_c, zeros_n, zeros_n, row0_first=PADF)
    m0_s = jnp.broadcast_to(state_mlstm_m.astype(F32)[:, :, None], (DEC_BATCH, M_HEADS, LANES))
    hm_s, s_c, s_n, s_m = mlstm_call(sample3(zm_s), sample3(g2_s),
                                     g2t_s.reshape(16, DEC_BATCH, 1, SROWS).transpose(1, 2, 0, 3),
                                     state_mlstm_C.astype(F32), state_mlstm_n.astype(F32), m0_s, row0_first=SPAD)

    fc, fr = fox_cumsum(g2_p, g2t_p)
    pairs = F_HEADS // 2
    fr_p = fr[:, 8:16].reshape(BATCH, pairs, 2, NCH, CHUNK).transpose(0, 1, 3, 2, 4)
    fc_p = fc[:, :, 8:16].reshape(BATCH, TP, pairs, 2).transpose(0, 2, 1, 3)
    hf_p = fox_prompt(zq_p, zk_p, zv_p, fr_p, fc_p)
    suf, tot = lf_suffix(cache_fox_logf.astype(F32).transpose(0, 2, 1))
    hf_s = fox_decode(page_table, zq_s, zk_s, zv_s, g2t_s.reshape(16, DEC_BATCH, SROWS).transpose(1, 0, 2),
                      cache_fox_k.astype(F32).transpose(0, 2, 3, 1), cache_fox_v.astype(F32).transpose(0, 2, 3, 1),
                      suf, tot)

    gains = norm_g.astype(F32)
    h = mix_ffn(flat2(hm_p), flat2(hm_s), hf_p, hf_s, h, wo_even[:512], wo_even[512:], gains[0, 1:4],
                wg[0], wu[0], wd[0])

    zr_p, zr_s, zl_p, zl_s = norm_matmul(h, ng[1, 0], w_odd, (2048, 1024))
    cos_p, sin_p = _rope_tables(jnp.arange(TP) - PADF)
    cos_s, sin_s = _rope_tables(past + jnp.arange(SROWS) - SPAD)
    lng = ret_ln_g.astype(F32)
    hr_p, p_s = retention_call(prompt3(zr_p), cos_p, sin_p, lng,
                               jnp.zeros((BATCH, R_HEADS, HEAD_DIM, HEAD_DIM), F32), rows=CHUNK, row0_first=PADF)
    hr_s, s_s = retention_call(sample3(zr_s), cos_s, sin_s, lng, state_ret_S.astype(F32),
                               rows=SROWS, row0_first=SPAD)

    lru_args = (conv_w.astype(F32), conv_b.astype(F32).reshape(1, LRU_WIDTH),
                _block_diag_pairs(lru_wa), lru_ba.astype(F32).reshape(1, LRU_WIDTH),
                _block_diag_pairs(lru_wx), lru_bx.astype(F32).reshape(1, LRU_WIDTH),
                lru_lambda.astype(F32).reshape(1, LRU_WIDTH))
    hl_p, p_h, p_tail = lru_call(prompt3(zl_p), jnp.zeros((1, CHUNK, LRU_WIDTH), F32),
                                 jnp.zeros((BATCH, 1, LRU_WIDTH), F32), *lru_args, rows=CHUNK, row0_first=PADF)
    pre_s = jnp.concatenate([jnp.zeros((DEC_BATCH, SPAD - (CONV_W - 1), LRU_WIDTH), F32),
                             state_lru_conv.astype(F32),
                             jnp.zeros((DEC_BATCH, DEC_SEQ, LRU_WIDTH), F32)], axis=1)
    hl_s, s_h, s_tail = lru_call(sample3(zl_s), pre_s, state_lru_h.astype(F32).reshape(DEC_BATCH, 1, LRU_WIDTH),
                                 *lru_args, rows=SROWS, row0_first=SPAD)

    h_p, h_s = mix_ffn(flat2(hr_p), flat2(hr_s), flat2(hl_p), flat2(hl_s), h, wo_odd[:512], wo_odd[512:],
                       gains[1, 1:4], wg[1], wu[1], wd[1], split=True)

    y_prompt = h_p.reshape(BATCH, TP, D_MODEL)[:, CHUNK:]
    y_sample = h_s.reshape(DEC_BATCH, SROWS, D_MODEL)[:, SPAD:]
    heads = lambda x: x.reshape(x.shape[:2] + (F_HEADS, F_DH))
    nback = CONV_W - 1
    return (y_prompt, y_sample,
            heads(_unpad_prompt(zk_p)), heads(_unpad_prompt(zv_p)), _unpad_prompt(g2_p[:, 8:16]),
            heads(_unpad_sample(zk_s)), heads(_unpad_sample(zv_s)), _unpad_sample(g2_s[:, 8:16]),
            p_c, p_n, p_m[:, :, 0], s_c, s_n, s_m[:, :, 0],
            p_s, s_s,
            p_h[:, 0], s_h[:, 0], p_tail[:, 8 - nback:], s_tail[:, 8 - nback:])
```

```python
import functools
import math

import jax
import jax.numpy as jnp
import numpy as np
from jax import lax
from jax.experimental import pallas as pl
from jax.experimental.pallas import tpu as pltpu

F32 = jnp.float32
BF16 = jnp.bfloat16

D_MODEL = 1024
BATCH = 4
SEQ = 4096
DEC_BATCH = 32
DEC_SEQ = 4
PAGE_SIZE = 128
N_META = 16
CHUNK = 128
M_HEADS = 4
F_HEADS = 8
F_DH = 64
R_HEADS = 4
HEAD_DIM = 128
ROPE_BASE = 10000.0
LRU_WIDTH = 512
LRU_C = 8.0
CONV_W = 4
D_FF = 2816
EPS = 1e-6
NEG = -1e30

LANES = 128
PADF = CHUNK - N_META
TP = PADF + N_META + SEQ
NCH = TP // CHUNK
SROWS = 16
SPAD = SROWS - DEC_SEQ
SBASE = BATCH * TP
NP = SBASE + DEC_BATCH * SROWS
TM = 512
SEQ_GROUP = 4
MXU_WIDTH = 256
MIX_FFN_VMEM_BYTES = 56 * 1024 * 1024
FOX_TQ = 512
FOX_CK = 512
LOG2E = math.log2(math.e)
DEC_PG = 16
PRE_PG = 128


def _bf(x):
    return x.astype(BF16)


def _dot(a, b):
    return jnp.dot(a, b, preferred_element_type=F32)


def _dot_nt(a, b):
    return lax.dot_general(a, b, (((1,), (1,)), ((), ())), preferred_element_type=F32)


def _dot_tn(a, b):
    return lax.dot_general(a, b, (((0,), (0,)), ((), ())), preferred_element_type=F32)


def _split3(x):
    hi = _bf(x)
    r1 = x - hi.astype(F32)
    mid = _bf(r1)
    lo = _bf(r1 - mid.astype(F32))
    return hi, mid, lo


def _dot01_right(x, m01):
    hi, mid, lo = _split3(x)
    return _dot(hi, m01) + _dot(mid, m01) + _dot(lo, m01)


def _dot01_left(m01, x):
    hi, mid, lo = _split3(x)
    return _dot(m01, hi) + _dot(m01, mid) + _dot(m01, lo)


def _iota(shape, dim):
    return lax.broadcasted_iota(jnp.int32, shape, dim)


def _rms(x, g):
    ms = jnp.mean(x * x, axis=-1, keepdims=True)
    return x * lax.rsqrt(ms + EPS) * g


def _softplus(x):
    return jnp.maximum(x, 0.0) + jnp.log1p(jnp.exp(-jnp.abs(x)))


def _log_sigmoid(x):
    return -_softplus(-x)


def _sigmoid(x):
    return 0.5 * jnp.tanh(0.5 * x) + 0.5


def _nm_kernel(x_ref, g_ref, w_ref, *o_refs, splits):
    xn = _bf(_rms(x_ref[...], g_ref[...]))
    off = 0
    for o_ref, n in zip(o_refs, splits):
        o_ref[...] = _dot(xn, w_ref[:, off:off + n])
        off += n


def norm_matmul(x, g, w, splits):
    n_rows, k = x.shape
    const = lambda i: (0, 0)
    return pl.pallas_call(
        functools.partial(_nm_kernel, splits=splits),
        grid=(n_rows // TM,),
        in_specs=[pl.BlockSpec((TM, k), lambda i: (i, 0)),
                  pl.BlockSpec((1, k), const),
                  pl.BlockSpec((k, sum(splits)), const, pipeline_mode=pl.Buffered(1))],
        out_specs=[pl.BlockSpec((TM, n), lambda i: (i, 0)) for n in splits],
        out_shape=[jax.ShapeDtypeStruct((n_rows, n), F32) for n in splits],
        compiler_params=pltpu.CompilerParams(dimension_semantics=("parallel",)),
        name="norm_matmul",
    )(x, g, w)


def _mix_ffn_kernel(a1_ref, a2_ref, h_ref, w1_ref, w2_ref, g_ref, wg_ref, wu_ref, wd_ref, o_ref):
    mix = _dot(_bf(a1_ref[...]), w1_ref[...]) + _dot(_bf(a2_ref[...]), w2_ref[...])
    h1 = h_ref[...] + _rms(mix, g_ref[0:1, :])
    xn = _bf(_rms(h1, g_ref[1:2, :]))
    acts = []
    for c in range(wg_ref.shape[1] // MXU_WIDTH):
        cols = slice(c * MXU_WIDTH, (c + 1) * MXU_WIDTH)
        gate = _dot(xn, wg_ref[:, cols])
        up = _dot(xn, wu_ref[:, cols])
        acts.append(_bf(gate * _sigmoid(gate) * up))
    ff = _dot(jnp.concatenate(acts, axis=1), wd_ref[...])
    o_ref[...] = h1 + _rms(ff, g_ref[2:3, :])


def mix_ffn(a1, a2, h, w1, w2, g, wg, wu, wd):
    k1 = a1.shape[1]
    k2 = a2.shape[1]
    n_rows, d = h.shape
    dff = wg.shape[1]
    assert dff % MXU_WIDTH == 0 and n_rows % TM == 0
    row_blk = lambda i: (i, 0)
    resident = lambda shape: pl.BlockSpec(shape, lambda i: (0, 0), pipeline_mode=pl.Buffered(1))
    return pl.pallas_call(
        _mix_ffn_kernel,
        grid=(n_rows // TM,),
        in_specs=[pl.BlockSpec((TM, k1), row_blk),
                  pl.BlockSpec((TM, k2), row_blk),
                  pl.BlockSpec((TM, d), row_blk),
                  resident((k1, d)),
                  resident((k2, d)),
                  resident((3, d)),
                  resident((d, dff)),
                  resident((d, dff)),
                  resident((dff, d))],
        out_specs=pl.BlockSpec((TM, d), row_blk),
        out_shape=jax.ShapeDtypeStruct((n_rows, d), F32),
        compiler_params=pltpu.CompilerParams(dimension_semantics=("parallel",),
                                             vmem_limit_bytes=MIX_FFN_VMEM_BYTES),
        name="mix_ffn",
    )(a1, a2, h, w1, w2, g, wg, wu, wd)


def _gates_kernel(z_ref, zt_ref, brow_ref, bcol_ref, o_ref, ot_ref):
    x = z_ref[...] + brow_ref[...]
    o_ref[...] = jnp.where(_iota(x.shape, 1) < M_HEADS, x, _log_sigmoid(x))
    xt = zt_ref[...] + bcol_ref[...]
    ot_ref[...] = jnp.where(_iota(xt.shape, 0) < M_HEADS, xt, _log_sigmoid(xt))


def gates_activate(zg, zgt, brow, bcol):
    n_rows = zg.shape[0]
    return pl.pallas_call(
        _gates_kernel,
        grid=(n_rows // TM,),
        in_specs=[pl.BlockSpec((TM, LANES), lambda i: (i, 0)),
                  pl.BlockSpec((16, TM), lambda i: (0, i)),
                  pl.BlockSpec((1, LANES), lambda i: (0, 0)),
                  pl.BlockSpec((16, 1), lambda i: (0, 0))],
        out_specs=[pl.BlockSpec((TM, LANES), lambda i: (i, 0)),
                   pl.BlockSpec((16, TM), lambda i: (0, i))],
        out_shape=[jax.ShapeDtypeStruct((n_rows, LANES), F32),
                   jax.ShapeDtypeStruct((16, n_rows), F32)],
        compiler_params=pltpu.CompilerParams(dimension_semantics=("parallel",)),
        name="gates_activate",
    )(zg, zgt, brow, bcol)


def _mlstm_chunk(q, k, v, logi_c, lf_c, logi_r, lf_r, c_state, n_state, m_state, row0, rows):
    ri = _iota((rows, rows), 0)
    ci = _iota((rows, rows), 1)
    valid_c = _iota((rows, 1), 0) >= row0
    valid_r = _iota((1, rows), 1) >= row0
    lf_c = jnp.where(valid_c, lf_c, 0.0)
    lf_r = jnp.where(valid_r, lf_r, 0.0)
    logi_c = jnp.where(valid_c, logi_c, NEG)
    logi_r = jnp.where(valid_r, logi_r, NEG)
    causal = ci <= ri
    b_c = jnp.sum(jnp.where(causal, lf_r, 0.0), axis=1, keepdims=True)
    b_r = jnp.sum(jnp.where(ri <= ci, lf_c, 0.0), axis=0, keepdims=True)
    log_d = jnp.where(causal, b_c - b_r + logi_r, -jnp.inf)
    m_inter = b_c + m_state
    m_t = jnp.maximum(m_inter, jnp.max(log_d, axis=1, keepdims=True))
    qb = _bf(q)
    kb = _bf(k)
    vb = _bf(v)
    w_intra = _dot_nt(qb, kb) * jnp.exp(log_d - m_t)
    w_inter = jnp.exp(m_inter - m_t)
    num = _dot(_bf(w_intra), vb) + w_inter * _dot(qb, _bf(c_state))
    den = jnp.sum(w_intra, axis=1, keepdims=True) + w_inter * jnp.sum(q * n_state, axis=1, keepdims=True)
    h = num / jnp.maximum(jnp.abs(den), jnp.exp(-m_t))
    m_new = m_t[rows - 1:rows, :]
    b_last = b_c[rows - 1:rows, :]
    w_src = jnp.exp(b_last - b_c + logi_c - m_new)
    decay = jnp.exp(b_last + m_state - m_new)
    ks = k * w_src
    c_new = decay * c_state + _dot_tn(_bf(ks), vb)
    n_new = decay * n_state + jnp.sum(ks, axis=0, keepdims=True)
    return h, c_new, n_new, m_new


def _mlstm_kernel(z_ref, g_ref, gt_ref, c0_ref, n0_ref, m0_ref, o_ref, c_ref, n_ref, m_ref, *, rows, row0_first):
    c = pl.program_id(1)

    @pl.when(c == 0)
    def _():
        c_ref[...] = c0_ref[...]
        n_ref[...] = n0_ref[...]
        m_ref[...] = m0_ref[...]

    row0 = jnp.where(c == 0, row0_first, 0)
    for s in range(SEQ_GROUP):
        g = g_ref[s]
        gt = gt_ref[s, 0]
        outs = []
        for h in range(M_HEADS):
            lo = h * HEAD_DIM
            q = z_ref[s, :, lo:lo + HEAD_DIM]
            k = z_ref[s, :, 512 + lo:512 + lo + HEAD_DIM] * (HEAD_DIM ** -0.5)
            v = z_ref[s, :, 1024 + lo:1024 + lo + HEAD_DIM]
            og = z_ref[s, :, 1536 + lo:1536 + lo + HEAD_DIM]
            hh, c_new, n_new, m_new = _mlstm_chunk(
                q, k, v,
                g[:, h:h + 1], g[:, M_HEADS + h:M_HEADS + h + 1],
                gt[h:h + 1, :], gt[M_HEADS + h:M_HEADS + h + 1, :],
                c_ref[s, h], n_ref[s, h:h + 1, :], m_ref[s, h:h + 1, 0:1], row0, rows)
            c_ref[s, h] = c_new
            n_ref[s, h:h + 1, :] = n_new
            m_ref[s, h:h + 1, :] = jnp.broadcast_to(m_new, (1, LANES))
            outs.append(_sigmoid(og) * hh)
        o_ref[s] = jnp.concatenate(outs, axis=1)


def _state_specs(heads):
    return [pl.BlockSpec((SEQ_GROUP, heads, HEAD_DIM, HEAD_DIM), lambda b, c: (b, 0, 0, 0)),
            pl.BlockSpec((SEQ_GROUP, heads, HEAD_DIM), lambda b, c: (b, 0, 0)),
            pl.BlockSpec((SEQ_GROUP, heads, LANES), lambda b, c: (b, 0, 0))]


def mlstm_call(z, g2, g2t, c0, n0, m0, *, row0_first):
    nb, t_len, _ = z.shape
    n_chunks, rows = g2t.shape[1], g2t.shape[3]
    assert nb % SEQ_GROUP == 0 and n_chunks * rows == t_len
    blk = lambda b, c: (b, c, 0)
    return pl.pallas_call(
        functools.partial(_mlstm_kernel, rows=rows, row0_first=row0_first),
        grid=(nb // SEQ_GROUP, n_chunks),
        in_specs=[pl.BlockSpec((SEQ_GROUP, rows, 2048), blk),
                  pl.BlockSpec((SEQ_GROUP, rows, LANES), blk),
                  pl.BlockSpec((SEQ_GROUP, 1, 16, rows), lambda b, c: (b, c, 0, 0))] + _state_specs(M_HEADS),
        out_specs=[pl.BlockSpec((SEQ_GROUP, rows, 512), blk)] + _state_specs(M_HEADS),
        out_shape=[jax.ShapeDtypeStruct((nb, t_len, 512), F32),
                   jax.ShapeDtypeStruct((nb, M_HEADS, HEAD_DIM, HEAD_DIM), F32),
                   jax.ShapeDtypeStruct((nb, M_HEADS, HEAD_DIM), F32),
                   jax.ShapeDtypeStruct((nb, M_HEADS, LANES), F32)],
        compiler_params=pltpu.CompilerParams(dimension_semantics=("parallel", "arbitrary")),
        name="mlstm",
    )(z, g2, g2t, c0, n0, m0)


def _ret_log_gamma(h):
    return math.log1p(-(2.0 ** (-5.0 - h)))


def _ret_kernel(z_ref, cos_ref, sin_ref, lng_ref, s0_ref, o_ref, s_ref, *, rows, row0_first):
    c = pl.program_id(1)

    @pl.when(c == 0)
    def _():
        s_ref[...] = s0_ref[...]

    row0 = jnp.where(c == 0, row0_first, 0)
    n_valid = (rows - row0).astype(F32)
    cosf = cos_ref[...]
    sinf = sin_ref[...]
    ri = _iota((rows, rows), 0)
    ci = _iota((rows, rows), 1)
    diff = (ri - ci).astype(F32)
    rowi = _iota((rows, 1), 0)
    te = (rowi - row0).astype(F32)
    valid = rowi >= row0
    for h in range(R_HEADS):
        lg = _ret_log_gamma(h)
        lo = h * HEAD_DIM
        decay_m = jnp.where(diff >= 0, jnp.exp(diff * lg), 0.0)
        w_inter = jnp.exp((te + 1.0) * lg)
        w_src = jnp.exp((n_valid - 1.0 - te) * lg)
        s_decay = jnp.exp(n_valid * lg)
        for s in range(SEQ_GROUP):
            q = z_ref[s, :, lo:lo + HEAD_DIM]
            k = z_ref[s, :, 512 + lo:512 + lo + HEAD_DIM]
            v = jnp.where(valid, z_ref[s, :, 1024 + lo:1024 + lo + HEAD_DIM], 0.0)
            rg = z_ref[s, :, 1536 + lo:1536 + lo + HEAD_DIM]
            q = q * cosf + pltpu.roll(q, HEAD_DIM // 2, 1) * sinf
            k = (k * cosf + pltpu.roll(k, HEAD_DIM // 2, 1) * sinf) * (HEAD_DIM ** -0.5)
            s_state = s_ref[s, h]
            qb = _bf(q)
            vb = _bf(v)
            intra = _dot(_bf(_dot_nt(qb, _bf(k)) * decay_m), vb)
            inter = _dot(qb, _bf(s_state)) * w_inter
            s_ref[s, h] = s_decay * s_state + _dot_tn(_bf(k * w_src), vb)
            hr = intra + inter
            mu = jnp.mean(hr, axis=-1, keepdims=True)
            var = jnp.mean(jnp.square(hr - mu), axis=-1, keepdims=True)
            hr = (hr - mu) * lax.rsqrt(var + EPS) * lng_ref[h:h + 1, :]
            o_ref[s, :, lo:lo + HEAD_DIM] = rg * _sigmoid(rg) * hr


def retention_call(z, cosf, sinf, lng, s0, *, rows, row0_first):
    nb, t_len, _ = z.shape
    assert nb % SEQ_GROUP == 0 and t_len % rows == 0
    blk = lambda b, c: (b, c, 0)
    sspec = pl.BlockSpec((SEQ_GROUP, R_HEADS, HEAD_DIM, HEAD_DIM), lambda b, c: (b, 0, 0, 0))
    return pl.pallas_call(
        functools.partial(_ret_kernel, rows=rows, row0_first=row0_first),
        grid=(nb // SEQ_GROUP, t_len // rows),
        in_specs=[pl.BlockSpec((SEQ_GROUP, rows, 2048), blk),
                  pl.BlockSpec((rows, HEAD_DIM), lambda b, c: (c, 0)),
                  pl.BlockSpec((rows, HEAD_DIM), lambda b, c: (c, 0)),
                  pl.BlockSpec((R_HEADS, HEAD_DIM), lambda b, c: (0, 0)),
                  sspec],
        out_specs=[pl.BlockSpec((SEQ_GROUP, rows, 512), blk), sspec],
        out_shape=[jax.ShapeDtypeStruct((nb, t_len, 512), F32),
                   jax.ShapeDtypeStruct((nb, R_HEADS, HEAD_DIM, HEAD_DIM), F32)],
        compiler_params=pltpu.CompilerParams(dimension_semantics=("parallel", "arbitrary")),
        name="retention",
    )(z, cosf, sinf, lng, s0)


def _lru_kernel(z_ref, pre_ref, h0_ref, cw_ref, cb_ref, wa_ref, ba_ref, wx_ref, bx_ref, lam_ref,
                o_ref, hlast_ref, tail_ref, cbuf_s, a_s, b_s, *, rows, row0_first):
    c = pl.program_id(1)

    @pl.when(c == 0)
    def _():
        hlast_ref[...] = h0_ref[...]
        tail_ref[...] = jnp.zeros(tail_ref.shape, F32)

    row0 = jnp.where(c == 0, row0_first, 0)
    rowi = _iota((rows, 1), 0)
    valid = rowi >= row0
    sp_lam = _softplus(-lam_ref[...])
    for s in range(SEQ_GROUP):
        lx = jnp.where(valid, z_ref[s, :, 0:LRU_WIDTH], pre_ref[min(s, pre_ref.shape[0] - 1)])
        cbuf_s[s, 0:8, :] = tail_ref[s]
        cbuf_s[s, 8:8 + rows, :] = lx
        xc = cb_ref[...] + cw_ref[CONV_W - 1:CONV_W, :] * lx
        for j in range(1, CONV_W):
            xc = xc + cw_ref[CONV_W - 1 - j:CONV_W - j, :] * cbuf_s[s, 8 - j:8 - j + rows, :]
        tail_ref[s] = lx[rows - 8:rows]
        xcb = _bf(xc)
        pre_a = []
        pre_x = []
        for p in range(LRU_WIDTH // LANES):
            xs = xcb[:, p * LANES:(p + 1) * LANES]
            pre_a.append(_dot(xs, wa_ref[p]))
            pre_x.append(_dot(xs, wx_ref[p]))
        r = _sigmoid(jnp.concatenate(pre_a, axis=1) + ba_ref[...])
        ig = _sigmoid(jnp.concatenate(pre_x, axis=1) + bx_ref[...])
        log_a = -LRU_C * r * sp_lam
        a = jnp.where(valid, jnp.exp(log_a), 1.0)
        one_minus_a2 = -jnp.tanh(log_a) * (jnp.exp(2.0 * log_a) + 1.0)
        bx = jnp.where(valid, jnp.sqrt(one_minus_a2) * (ig * xc), 0.0)
        a_s[s] = a
        b_s[s] = bx
    carries = [hlast_ref[s] for s in range(SEQ_GROUP)]
    for t in range(rows):
        for s in range(SEQ_GROUP):
            carries[s] = a_s[s, t:t + 1, :] * carries[s] + b_s[s, t:t + 1, :]
            b_s[s, t:t + 1, :] = carries[s]
    for s in range(SEQ_GROUP):
        hlast_ref[s] = carries[s]
        gate = z_ref[s, :, LRU_WIDTH:2 * LRU_WIDTH]
        gl = 0.5 * gate * (1.0 + jnp.tanh(math.sqrt(2.0 / math.pi) * (gate + 0.044715 * gate * gate * gate)))
        o_ref[s] = b_s[s] * gl


def lru_call(z, pre, h0, cw, cb, wa2, ba, wx2, bx, lam, *, rows, row0_first):
    nb, t_len, _ = z.shape
    assert nb % SEQ_GROUP == 0 and t_len % rows == 0
    blk = lambda b, c: (b, c, 0)
    per_seq = lambda b, c: (b, 0, 0)
    const2 = lambda b, c: (0, 0)
    const3 = lambda b, c: (0, 0, 0)
    shared_pre = pre.shape[0] == 1
    in_specs = [pl.BlockSpec((SEQ_GROUP, rows, 1024), blk),
                pl.BlockSpec((1 if shared_pre else SEQ_GROUP, rows, LRU_WIDTH), const3 if shared_pre else per_seq),
                pl.BlockSpec((SEQ_GROUP, 1, LRU_WIDTH), per_seq),
                pl.BlockSpec((CONV_W, LRU_WIDTH), const2),
                pl.BlockSpec((1, LRU_WIDTH), const2),
                pl.BlockSpec((LRU_WIDTH // LANES, LANES, LANES), const3),
                pl.BlockSpec((1, LRU_WIDTH), const2),
                pl.BlockSpec((LRU_WIDTH // LANES, LANES, LANES), const3),
                pl.BlockSpec((1, LRU_WIDTH), const2),
                pl.BlockSpec((1, LRU_WIDTH), const2)]
    return pl.pallas_call(
        functools.partial(_lru_kernel, rows=rows, row0_first=row0_first),
        grid=(nb // SEQ_GROUP, t_len // rows),
        in_specs=in_specs,
        out_specs=[pl.BlockSpec((SEQ_GROUP, rows, LRU_WIDTH), blk),
                   pl.BlockSpec((SEQ_GROUP, 1, LRU_WIDTH), per_seq),
                   pl.BlockSpec((SEQ_GROUP, 8, LRU_WIDTH), per_seq)],
        out_shape=[jax.ShapeDtypeStruct((nb, t_len, LRU_WIDTH), F32),
                   jax.ShapeDtypeStruct((nb, 1, LRU_WIDTH), F32),
                   jax.ShapeDtypeStruct((nb, 8, LRU_WIDTH), F32)],
        scratch_shapes=[pltpu.VMEM((SEQ_GROUP, rows + 8, LRU_WIDTH), F32),
                        pltpu.VMEM((SEQ_GROUP, rows, LRU_WIDTH), F32),
                        pltpu.VMEM((SEQ_GROUP, rows, LRU_WIDTH), F32)],
        compiler_params=pltpu.CompilerParams(dimension_semantics=("parallel", "arbitrary")),
        name="rglru",
    )(z, pre, h0, cw, cb, wa2, ba, wx2, bx, lam)


def _fox_cumsum_kernel(g_ref, gt_ref, fc_ref, fr_ref):
    ri = _iota((CHUNK, CHUNK), 0)
    ci = _iota((CHUNK, CHUNK), 1)
    lower = jnp.where(ci <= ri, 1.0, 0.0).astype(BF16)
    upper = jnp.where(ri <= ci, 1.0, 0.0).astype(BF16)
    carry_r = jnp.zeros((1, LANES), F32)
    carry_c = jnp.zeros((16, 1), F32)
    for blk in range(NCH):
        x = g_ref[blk * CHUNK:(blk + 1) * CHUNK, :]
        xt = gt_ref[:, blk * CHUNK:(blk + 1) * CHUNK]
        if blk == 0:
            x = jnp.where(_iota((CHUNK, 1), 0) >= PADF, x, 0.0)
            xt = jnp.where(_iota((1, CHUNK), 1) >= PADF, xt, 0.0)
        cs = _dot01_left(lower, x) + carry_r
        cst = _dot01_right(xt, upper) + carry_c
        carry_r = cs[CHUNK - 1:CHUNK, :]
        carry_c = cst[:, CHUNK - 1:CHUNK]
        if blk == 0:
            cst = jnp.where(_iota((1, CHUNK), 1) >= PADF, cst, -NEG)
        fc_ref[0, blk * CHUNK:(blk + 1) * CHUNK, :] = cs
        fr_ref[0, :, blk * CHUNK:(blk + 1) * CHUNK] = cst


def fox_cumsum(g2, g2t):
    return pl.pallas_call(
        _fox_cumsum_kernel,
        grid=(BATCH,),
        in_specs=[pl.BlockSpec((TP, LANES), lambda b: (b, 0)),
                  pl.BlockSpec((16, TP), lambda b: (0, b))],
        out_specs=[pl.BlockSpec((1, TP, LANES), lambda b: (b, 0, 0)),
                   pl.BlockSpec((1, 16, TP), lambda b: (b, 0, 0))],
        out_shape=[jax.ShapeDtypeStruct((BATCH, TP, LANES), F32),
                   jax.ShapeDtypeStruct((BATCH, 16, TP), F32)],
        compiler_params=pltpu.CompilerParams(dimension_semantics=("parallel",)),
        name="fox_cumsum",
    )(g2, g2t)


def _fox_prompt_kernel(q_ref, k_ref, v_ref, fr_ref, fc_ref, o_ref,
                       qta_s, qtb_s, ka_s, kb_s, vt_s, m_s, l_s, acc_s, ua_s):
    i = pl.program_id(2)
    lane = _iota((1, LANES), 1)
    is_a = lane < F_DH
    row = _iota((LANES, 1), 0)
    row_a = row < F_DH
    k_heads = (ka_s, kb_s)
    qt_heads = (qta_s, qtb_s)

    @pl.when(i == 0)
    def _():
        k = k_ref[...]
        is_pad = _iota((TP, 1), 0) < PADF
        for hh, k_s in enumerate(k_heads):
            fk = jnp.where(is_pad, -NEG, fc_ref[0, 0, :, hh:hh + 1] * LOG2E)
            hi, mid, lo = (t.astype(F32) for t in _split3(fk))
            base = (1 - hh) * F_DH
            extra = jnp.where(lane == base, hi, jnp.where(lane == base + 1, mid, jnp.where(lane == base + 2, lo, 0.0)))
            own = is_a if hh == 0 else jnp.logical_not(is_a)
            k_s[...] = _bf(jnp.where(own, k, extra))

        minus_a = jnp.where((row >= F_DH) & (row < F_DH + 3), -1.0, 0.0)
        minus_b = jnp.where(row < 3, -1.0, 0.0)

        def fill(blk, carry):
            rows = pl.ds(pl.multiple_of(blk * CHUNK, CHUNK), CHUNK)
            qt = (q_ref[rows, :] * (F_DH ** -0.5 * LOG2E)).T
            qta_s[blk] = _bf(jnp.where(row_a, qt, minus_a))
            qtb_s[blk] = _bf(jnp.where(row_a, minus_b, qt))
            vt_s[blk] = _bf(v_ref[rows, :].T)
            return carry

        lax.fori_loop(0, NCH, fill, 0)

    def init():
        m_s[...] = jnp.full(m_s.shape, NEG, F32)
        l_s[...] = jnp.zeros(l_s.shape, F32)
        acc_s[...] = jnp.zeros(acc_s.shape, F32)

    def scores(hh, qt, kblk, nkb):
        ck = nkb * CHUNK
        k0 = kblk * CHUNK if isinstance(kblk, int) else pl.multiple_of(kblk * CHUNK, CHUNK)
        return _dot(k_heads[hh][pl.ds(k0, ck), :], qt[hh])

    def step(hh, u, qblk, nqb, kblk, nkb, causal):
        tq = nqb * CHUNK
        ck = nkb * CHUNK
        fq = jnp.concatenate([fr_ref[0, 0, qblk + t, hh:hh + 1, :] for t in range(nqb)], axis=1)
        fq = jnp.where(fq > 0.5 * -NEG, 0.0, fq * LOG2E)
        if causal:
            u = jnp.where(kblk * CHUNK + _iota((ck, tq), 0) <= qblk * CHUNK + _iota((ck, tq), 1), u, NEG)
        m_old = m_s[hh, :, 0:tq]
        m_new = jnp.maximum(m_old, jnp.max(u, axis=0, keepdims=True) + fq)
        alpha = jnp.exp2(m_old - m_new)
        p = _bf(jnp.exp2(u + (fq - m_new)))
        m_s[hh, :, 0:tq] = m_new
        feat = slice(hh * F_DH, (hh + 1) * F_DH)
        vt = jnp.concatenate([vt_s[kblk + t, feat, :] for t in range(nkb)], axis=1)
        pv = _dot(jnp.concatenate([vt, jnp.ones((16, ck), BF16)], axis=0), p)
        l_s[hh, :, 0:tq] = alpha * l_s[hh, :, 0:tq] + pv[F_DH:F_DH + 1, :]
        acc_s[feat, 0:tq] = alpha * acc_s[feat, 0:tq] + pv[0:F_DH, :]

    def finalize(qblk, nqb):
        tq = nqb * CHUNK
        o_t = acc_s[:, 0:tq] / jnp.where(row_a, l_s[0, :, 0:tq], l_s[1, :, 0:tq])
        for t in range(nqb):
            rows = pl.ds(pl.multiple_of((qblk + t) * CHUNK, CHUNK), CHUNK)
            o_ref[rows, :] = o_t[:, t * CHUNK:(t + 1) * CHUNK].T

    @pl.when(i == 0)
    def _():
        init()
        qt0 = (qta_s[0], qtb_s[0])
        for hh in range(2):
            step(hh, scores(hh, qt0, 0, 1), 0, 1, 0, 1, True)
        finalize(0, 1)

    nck = FOX_CK // CHUNK
    nqb = FOX_TQ // CHUNK
    qblk = 1 + i * nqb
    qt = tuple(jnp.concatenate([qt_h[qblk + t] for t in range(nqb)], axis=1) for qt_h in qt_heads)
    init()
    for hh in range(2):
        step(hh, scores(hh, qt, 0, 1), qblk, nqb, 0, 1, False)

    ua_s[...] = scores(0, qt, 1, nck)

    def body(j, carry):
        kblk = 1 + j * nck
        ub = scores(1, qt, kblk, nck)
        step(0, ua_s[...], qblk, nqb, kblk, nck, False)
        ua_s[...] = scores(0, qt, kblk + nck, nck)
        step(1, ub, qblk, nqb, kblk, nck, False)
        return carry

    lax.fori_loop(0, i, body, 0)
    ub = scores(1, qt, qblk, nck)
    step(0, ua_s[...], qblk, nqb, qblk, nck, True)
    step(1, ub, qblk, nqb, qblk, nck, True)
    finalize(qblk, nqb)


def fox_prompt(zq, zk, zv, fr, fc):
    nq = (TP - CHUNK) // FOX_TQ
    pairs = F_HEADS // 2
    return pl.pallas_call(
        _fox_prompt_kernel,
        grid=(BATCH, pairs, nq),
        in_specs=[pl.BlockSpec((TP, LANES), lambda b, p, i: (b, p)),
                  pl.BlockSpec((TP, LANES), lambda b, p, i: (b, p)),
                  pl.BlockSpec((TP, LANES), lambda b, p, i: (b, p)),
                  pl.BlockSpec((1, 1, NCH, 2, LANES), lambda b, p, i: (b, p, 0, 0, 0)),
                  pl.BlockSpec((1, 1, TP, 2), lambda b, p, i: (b, p, 0, 0))],
        out_specs=pl.BlockSpec((TP, LANES), lambda b, p, i: (b, p)),
        out_shape=jax.ShapeDtypeStruct((SBASE, 512), F32),
        scratch_shapes=[pltpu.VMEM((NCH, LANES, CHUNK), BF16),
                        pltpu.VMEM((NCH, LANES, CHUNK), BF16),
                        pltpu.VMEM((TP, LANES), BF16),
                        pltpu.VMEM((TP, LANES), BF16),
                        pltpu.VMEM((NCH, LANES, CHUNK), BF16),
                        pltpu.VMEM((2, 1, FOX_TQ), F32),
                        pltpu.VMEM((2, 1, FOX_TQ), F32),
                        pltpu.VMEM((LANES, FOX_TQ), F32),
                        pltpu.VMEM((FOX_CK, FOX_TQ), F32)],
        compiler_params=pltpu.CompilerParams(dimension_semantics=("parallel", "parallel", "arbitrary")),
        name="fox_prompt",
    )(zq, zk, zv, fr, fc)


def _lf_suffix_kernel(x_ref, suf_ref, tot_ref):
    ri = _iota((PAGE_SIZE, PAGE_SIZE), 0)
    ci = _iota((PAGE_SIZE, PAGE_SIZE), 1)
    after = jnp.where(ri > ci, 1.0, 0.0).astype(BF16)
    ones = jnp.ones((PAGE_SIZE, PAGE_SIZE), BF16)
    x = x_ref[...].reshape(PRE_PG * F_HEADS, PAGE_SIZE)
    suf_ref[...] = _dot01_right(x, after).reshape(PRE_PG, F_HEADS, PAGE_SIZE)
    tot_ref[...] = _dot01_right(x, ones).reshape(PRE_PG, F_HEADS, PAGE_SIZE)


def lf_suffix(lft):
    n_phys = lft.shape[0]
    spec = pl.BlockSpec((PRE_PG, F_HEADS, PAGE_SIZE), lambda i: (i, 0, 0))
    return pl.pallas_call(
        _lf_suffix_kernel,
        grid=(n_phys // PRE_PG,),
        in_specs=[spec],
        out_specs=[spec, spec],
        out_shape=[jax.ShapeDtypeStruct(lft.shape, F32)] * 2,
        compiler_params=pltpu.CompilerParams(dimension_semantics=("parallel",)),
        name="lf_suffix",
    )(lft)


def _fox_decode_kernel(pt_ref, zq_ref, zk_ref, zv_ref, gt_ref, *rest):
    k_refs = rest[0:DEC_PG]
    v_refs = rest[DEC_PG:2 * DEC_PG]
    suf_refs = rest[2 * DEC_PG:3 * DEC_PG]
    tot_refs = rest[3 * DEC_PG:4 * DEC_PG]
    o_ref = rest[4 * DEC_PG]
    q2_s, m_s, l_s, acc_s, run_s, nc_s = rest[4 * DEC_PG + 1:]
    j = pl.program_id(1)
    n_rows = DEC_SEQ * F_HEADS
    width = F_HEADS * F_DH

    def update(u, v, v_feature_major):
        m_old = m_s[...]
        m_new = jnp.maximum(m_old, jnp.max(u, axis=1, keepdims=True))
        alpha = jnp.exp(m_old - m_new)
        p = jnp.exp(u - m_new[:, 0:u.shape[1]])
        pv = _dot_nt(_bf(p), v) if v_feature_major else _dot(_bf(p), v)
        l_s[...] = alpha * l_s[...] + jnp.sum(p, axis=1, keepdims=True)
        acc_s[...] = jnp.concatenate([alpha] * (width // LANES), axis=1) * acc_s[...] + pv
        m_s[...] = m_new

    @pl.when(j == 0)
    def _():
        m_s[...] = jnp.full(m_s.shape, NEG, F32)
        l_s[...] = jnp.zeros(l_s.shape, F32)
        acc_s[...] = jnp.zeros(acc_s.shape, F32)
        run_s[...] = jnp.zeros(run_s.shape, F32)
        head_mask = _iota((F_HEADS, width), 1) // F_DH == _iota((F_HEADS, width), 0)
        q16 = zq_ref[...] * (F_DH ** -0.5)
        q2 = _bf(jnp.concatenate(
            [jnp.where(head_mask, q16[SPAD + t:SPAD + t + 1, :], 0.0) for t in range(DEC_SEQ)], axis=0))
        q2_s[...] = q2
        lf_new = gt_ref[F_HEADS:2 * F_HEADS, :]
        lane = _iota((1, SROWS), 1)
        cum = jnp.zeros((F_HEADS, SROWS), F32)
        for t in range(DEC_SEQ):
            cum = cum + jnp.where(lane >= SPAD + t, lf_new[:, SPAD + t:SPAD + t + 1], 0.0)
        nc = jnp.concatenate([cum[:, SPAD + t:SPAD + t + 1] for t in range(DEC_SEQ)], axis=0)
        nc_s[...] = jnp.broadcast_to(nc, nc_s.shape)
        cum4 = jnp.concatenate([cum] * DEC_SEQ, axis=0)
        qi = _iota((n_rows, SROWS), 0) // F_HEADS
        kj = _iota((n_rows, SROWS), 1) - SPAD
        u = jnp.where((kj >= 0) & (kj <= qi), _dot_nt(q2, _bf(zk_ref[...])) + (nc - cum4), NEG)
        update(u, _bf(zv_ref[...]), False)

    q2 = q2_s[...]
    nc = nc_s[...]
    run = run_s[...]
    us = []
    for i in range(DEC_PG):
        later = run + suf_refs[i][0]
        bias = jnp.concatenate([later] * DEC_SEQ, axis=0) + nc
        us.append(_dot(q2, _bf(k_refs[i][0].reshape(width, PAGE_SIZE))) + bias)
        run = run + tot_refs[i][0]
    run_s[...] = run
    u = jnp.concatenate(us, axis=1)
    m_old = m_s[...]
    m_new = jnp.maximum(m_old, jnp.max(u, axis=1, keepdims=True))
    alpha = jnp.exp(m_old - m_new)
    p = jnp.exp(u - jnp.concatenate([m_new] * DEC_PG, axis=1))
    l_s[...] = alpha * l_s[...] + jnp.sum(p, axis=1, keepdims=True)
    pv = _dot_nt(_bf(p[:, 0:PAGE_SIZE]), _bf(v_refs[0][0].reshape(width, PAGE_SIZE)))
    for i in range(1, DEC_PG):
        pv = pv + _dot_nt(_bf(p[:, i * PAGE_SIZE:(i + 1) * PAGE_SIZE]), _bf(v_refs[i][0].reshape(width, PAGE_SIZE)))
    acc_s[...] = jnp.concatenate([alpha] * (width // LANES), axis=1) * acc_s[...] + pv
    m_s[...] = m_new

    @pl.when(j == pl.num_programs(1) - 1)
    def _():
        head_mask = (_iota((n_rows, width), 1) // F_DH) == (_iota((n_rows, width), 0) % F_HEADS)
        o2 = jnp.where(head_mask, acc_s[...] / jnp.concatenate([l_s[...]] * (width // LANES), axis=1), 0.0)
        o_ref[...] = jnp.zeros(o_ref.shape, F32)
        for t in range(DEC_SEQ):
            o_ref[SPAD + t:SPAD + t + 1, :] = jnp.sum(o2[t * F_HEADS:(t + 1) * F_HEADS, :], axis=0, keepdims=True)


def fox_decode(page_table, zq, zk, zv, g2t_s, cache_kt, cache_vt, suf, tot):
    n_pages = page_table.shape[1]
    width = F_HEADS * F_DH

    def page_map(i):
        return lambda b, j, pt: (pt[b, n_pages - 1 - (j * DEC_PG + i)], 0, 0, 0)

    def page_map3(i):
        return lambda b, j, pt: (pt[b, n_pages - 1 - (j * DEC_PG + i)], 0, 0)

    in_specs = [pl.BlockSpec((SROWS, width), lambda b, j, pt: (b, 0)),
                pl.BlockSpec((SROWS, width), lambda b, j, pt: (b, 0)),
                pl.BlockSpec((SROWS, width), lambda b, j, pt: (b, 0)),
                pl.BlockSpec((None, 16, SROWS), lambda b, j, pt: (b, 0, 0))]
    in_specs += [pl.BlockSpec((1, F_HEADS, F_DH, PAGE_SIZE), page_map(i)) for i in range(DEC_PG)]
    in_specs += [pl.BlockSpec((1, F_HEADS, F_DH, PAGE_SIZE), page_map(i)) for i in range(DEC_PG)]
    in_specs += [pl.BlockSpec((1, F_HEADS, PAGE_SIZE), page_map3(i)) for i in range(DEC_PG)]
    in_specs += [pl.BlockSpec((1, F_HEADS, PAGE_SIZE), page_map3(i)) for i in range(DEC_PG)]
    args = [zq, zk, zv, g2t_s] + [cache_kt] * DEC_PG + [cache_vt] * DEC_PG + [suf] * DEC_PG + [tot] * DEC_PG
    n_rows = DEC_SEQ * F_HEADS
    grid_spec = pltpu.PrefetchScalarGridSpec(
        num_scalar_prefetch=1,
        grid=(DEC_BATCH, n_pages // DEC_PG),
        in_specs=in_specs,
        out_specs=pl.BlockSpec((SROWS, width), lambda b, j, pt: (b, 0)),
        scratch_shapes=[pltpu.VMEM((n_rows, width), BF16),
                        pltpu.VMEM((n_rows, LANES), F32),
                        pltpu.VMEM((n_rows, LANES), F32),
                        pltpu.VMEM((n_rows, width), F32),
                        pltpu.VMEM((F_HEADS, PAGE_SIZE), F32),
                        pltpu.VMEM((n_rows, LANES), F32)])
    return pl.pallas_call(
        _fox_decode_kernel,
        grid_spec=grid_spec,
        out_shape=jax.ShapeDtypeStruct((DEC_BATCH * SROWS, width), F32),
        compiler_params=pltpu.CompilerParams(dimension_semantics=("parallel", "arbitrary")),
        name="fox_decode",
    )(page_table, *args)


def _block_diag_pairs(w):
    z = jnp.zeros((LANES // 2, LANES // 2), w.dtype)
    pairs = [jnp.block([[w[2 * p], z], [z, w[2 * p + 1]]]) for p in range(w.shape[0] // 2)]
    return _bf(jnp.stack(pairs))


def _rope_tables(pos):
    half = HEAD_DIM // 2
    freq = ROPE_BASE ** (-jnp.arange(half, dtype=F32) / half)
    ang = pos.astype(F32)[:, None] * freq[None, :]
    cos, sin = jnp.cos(ang), jnp.sin(ang)
    return jnp.concatenate([cos, cos], axis=1), jnp.concatenate([-sin, sin], axis=1)


def _unpad_prompt(x):
    return x.reshape((BATCH, TP) + x.shape[1:])[:, PADF:]


def _unpad_sample(x):
    return x.reshape((DEC_BATCH, SROWS) + x.shape[1:])[:, SPAD:]


def kernel(x_prompt, x_sample, cache_fox_k, cache_fox_v, cache_fox_logf, state_mlstm_C, state_mlstm_n, state_mlstm_m, state_ret_S, state_lru_h, state_lru_conv, page_table, meta_tokens, w_in_even, b_mlstm_i, b_mlstm_f, b_fox_f, w_out_even, w_in_odd, ret_ln_g, conv_w, conv_b, lru_wa, lru_ba, lru_wx, lru_bx, lru_lambda, w_out_odd, norm_g, ffn_wg, ffn_wu, ffn_wd):
    n_pages = page_table.shape[1]
    past = n_pages * PAGE_SIZE
    n_phys = cache_fox_k.shape[0]

    head = jnp.concatenate([jnp.zeros((PADF, D_MODEL), F32), meta_tokens.astype(F32)], axis=0)
    hs = jnp.concatenate([jnp.zeros((DEC_BATCH, SPAD, D_MODEL), F32), x_sample.astype(F32)],
                         axis=1).reshape(DEC_BATCH * SROWS, D_MODEL)
    h_s = hs
    h_p = jnp.concatenate([jnp.broadcast_to(head[None], (BATCH, CHUNK, D_MODEL)), x_prompt.astype(F32)],
                          axis=1).reshape(SBASE, D_MODEL)

    c = np.cumsum([0, 512, 512, 512, 512, 4, 4, 512, 512, 512, 8])
    gate_w = jnp.concatenate([w_in_even[:, c[4]:c[6]], w_in_even[:, c[9]:c[10]],
                              jnp.zeros((D_MODEL, LANES - 16), F32)], axis=1)
    w_even = _bf(jnp.concatenate([w_in_even[:, c[0]:c[4]], w_in_even[:, c[6]:c[9]], gate_w], axis=1))
    gate_b = jnp.concatenate([b_mlstm_i, b_mlstm_f, b_fox_f]).astype(F32)
    brow = jnp.concatenate([gate_b, jnp.zeros((LANES - 16,), F32)]).reshape(1, LANES)
    bcol = gate_b.reshape(16, 1)
    w_odd = _bf(w_in_odd)
    wo_even = _bf(w_out_even)
    wo_odd = _bf(w_out_odd)
    wg = _bf(ffn_wg)
    wu = _bf(ffn_wu)
    wd = _bf(ffn_wd)
    ng = norm_g.astype(F32).reshape(2, 4, 1, D_MODEL)

    even_cols = (2048, 512, 512, 512, LANES)
    zm_p, zq_p, zk_p, zv_p, zg_p = norm_matmul(h_p, ng[0, 0], w_even, even_cols)
    zm_s, zq_s, zk_s, zv_s, zg_s = norm_matmul(h_s, ng[0, 0], w_even, even_cols)
    g2_p, g2t_p = gates_activate(zg_p, zg_p[:, :16].T, brow, bcol)
    g2_s, g2t_s = gates_activate(zg_s, zg_s[:, :16].T, brow, bcol)
    prompt3 = lambda x: x.reshape(BATCH, TP, x.shape[-1])
    sample3 = lambda x: x.reshape(DEC_BATCH, SROWS, x.shape[-1])
    flat2 = lambda x: x.reshape(-1, x.shape[-1])

    zeros_c = jnp.zeros((BATCH, M_HEADS, HEAD_DIM, HEAD_DIM), F32)
    zeros_n = jnp.zeros((BATCH, M_HEADS, HEAD_DIM), F32)
    hm_p, p_c, p_n, p_m = mlstm_call(prompt3(zm_p), prompt3(g2_p),
                                     g2t_p.reshape(16, BATCH, NCH, CHUNK).transpose(1, 2, 0, 3),
                                     zeros_c, zeros_n, zeros_n, row0_first=PADF)
    m0_s = jnp.broadcast_to(state_mlstm_m.astype(F32)[:, :, None], (DEC_BATCH, M_HEADS, LANES))
    hm_s, s_c, s_n, s_m = mlstm_call(sample3(zm_s), sample3(g2_s),
                                     g2t_s.reshape(16, DEC_BATCH, 1, SROWS).transpose(1, 2, 0, 3),
                                     state_mlstm_C.astype(F32), state_mlstm_n.astype(F32), m0_s, row0_first=SPAD)

    fc, fr = fox_cumsum(g2_p, g2t_p)
    pairs = F_HEADS // 2
    fr_p = fr[:, 8:16].reshape(BATCH, pairs, 2, NCH, CHUNK).transpose(0, 1, 3, 2, 4)
    fc_p = fc[:, :, 8:16].reshape(BATCH, TP, pairs, 2).transpose(0, 2, 1, 3)
    hf_p = fox_prompt(zq_p, zk_p, zv_p, fr_p, fc_p)
    suf, tot = lf_suffix(cache_fox_logf.astype(F32).transpose(0, 2, 1))
    hf_s = fox_decode(page_table, zq_s, zk_s, zv_s, g2t_s.reshape(16, DEC_BATCH, SROWS).transpose(1, 0, 2),
                      cache_fox_k.astype(F32).transpose(0, 2, 3, 1), cache_fox_v.astype(F32).transpose(0, 2, 3, 1),
                      suf, tot)

    gains = norm_g.astype(F32)
    even_w = (wo_even[:512], wo_even[512:], gains[0, 1:4], wg[0], wu[0], wd[0])
    h_p = mix_ffn(flat2(hm_p), hf_p, h_p, *even_w)
    h_s = mix_ffn(flat2(hm_s), hf_s, h_s, *even_w)

    zr_p, zl_p = norm_matmul(h_p, ng[1, 0], w_odd, (2048, 1024))
    zr_s, zl_s = norm_matmul(h_s, ng[1, 0], w_odd, (2048, 1024))
    cos_p, sin_p = _rope_tables(jnp.arange(TP) - PADF)
    cos_s, sin_s = _rope_tables(past + jnp.arange(SROWS) - SPAD)
    lng = ret_ln_g.astype(F32)
    hr_p, p_s = retention_call(prompt3(zr_p), cos_p, sin_p, lng,
                               jnp.zeros((BATCH, R_HEADS, HEAD_DIM, HEAD_DIM), F32), rows=CHUNK, row0_first=PADF)
    hr_s, s_s = retention_call(sample3(zr_s), cos_s, sin_s, lng, state_ret_S.astype(F32),
                               rows=SROWS, row0_first=SPAD)

    lru_args = (conv_w.astype(F32), conv_b.astype(F32).reshape(1, LRU_WIDTH),
                _block_diag_pairs(lru_wa), lru_ba.astype(F32).reshape(1, LRU_WIDTH),
                _block_diag_pairs(lru_wx), lru_bx.astype(F32).reshape(1, LRU_WIDTH),
                lru_lambda.astype(F32).reshape(1, LRU_WIDTH))
    hl_p, p_h, p_tail = lru_call(prompt3(zl_p), jnp.zeros((1, CHUNK, LRU_WIDTH), F32),
                                 jnp.zeros((BATCH, 1, LRU_WIDTH), F32), *lru_args, rows=CHUNK, row0_first=PADF)
    pre_s = jnp.concatenate([jnp.zeros((DEC_BATCH, SPAD - (CONV_W - 1), LRU_WIDTH), F32),
                             state_lru_conv.astype(F32),
                             jnp.zeros((DEC_BATCH, DEC_SEQ, LRU_WIDTH), F32)], axis=1)
    hl_s, s_h, s_tail = lru_call(sample3(zl_s), pre_s, state_lru_h.astype(F32).reshape(DEC_BATCH, 1, LRU_WIDTH),
                                 *lru_args, rows=SROWS, row0_first=SPAD)

    odd_w = (wo_odd[:512], wo_odd[512:], gains[1, 1:4], wg[1], wu[1], wd[1])
    h_p = mix_ffn(flat2(hr_p), flat2(hl_p), h_p, *odd_w)
    h_s = mix_ffn(flat2(hr_s), flat2(hl_s), h_s, *odd_w)

    y_prompt = h_p.reshape(BATCH, TP, D_MODEL)[:, CHUNK:]
    y_sample = h_s.reshape(DEC_BATCH, SROWS, D_MODEL)[:, SPAD:]
    heads = lambda x: x.reshape(x.shape[:2] + (F_HEADS, F_DH))
    nback = CONV_W - 1
    return (y_prompt, y_sample,
            heads(_unpad_prompt(zk_p)), heads(_unpad_prompt(zv_p)), _unpad_prompt(g2_p[:, 8:16]),
            heads(_unpad_sample(zk_s)), heads(_unpad_sample(zv_s)), _unpad_sample(g2_s[:, 8:16]),
            p_c, p_n, p_m[:, :, 0], s_c, s_n, s_m[:, :, 0],
            p_s, s_s,
            p_h[:, 0], s_h[:, 0], p_tail[:, 8 - nback:], s_tail[:, 8 - nback:])
```

```python
import functools
import math

import jax
import jax.numpy as jnp
import numpy as np
from jax import lax
from jax.experimental import pallas as pl
from jax.experimental.pallas import tpu as pltpu

F32 = jnp.float32
BF16 = jnp.bfloat16

D_MODEL = 1024
BATCH = 4
SEQ = 4096
DEC_BATCH = 32
DEC_SEQ = 4
PAGE_SIZE = 128
N_META = 16
CHUNK = 128
M_HEADS = 4
F_HEADS = 8
F_DH = 64
R_HEADS = 4
HEAD_DIM = 128
ROPE_BASE = 10000.0
LRU_WIDTH = 512
LRU_C = 8.0
CONV_W = 4
D_FF = 2816
EPS = 1e-6
NEG = -1e30

LANES = 128
PADF = CHUNK - N_META
TP = PADF + N_META + SEQ
NCH = TP // CHUNK
SROWS = 16
SPAD = SROWS - DEC_SEQ
SBASE = BATCH * TP
NP = SBASE + DEC_BATCH * SROWS
TM = 512
SEQ_GROUP = 4
MXU_WIDTH = 256
MIX_FFN_VMEM_BYTES = 56 * 1024 * 1024
FOX_TQ = 512
FOX_CK = 512
LOG2E = math.log2(math.e)
DEC_PG = 32
PRE_PG = 128


def _bf(x):
    return x.astype(BF16)


def _dot(a, b):
    return jnp.dot(a, b, preferred_element_type=F32)


def _dot_nt(a, b):
    return lax.dot_general(a, b, (((1,), (1,)), ((), ())), preferred_element_type=F32)


def _dot_tn(a, b):
    return lax.dot_general(a, b, (((0,), (0,)), ((), ())), preferred_element_type=F32)


def _split3(x):
    hi = _bf(x)
    r1 = x - hi.astype(F32)
    mid = _bf(r1)
    lo = _bf(r1 - mid.astype(F32))
    return hi, mid, lo


def _dot01_right(x, m01):
    hi, mid, lo = _split3(x)
    return _dot(hi, m01) + _dot(mid, m01) + _dot(lo, m01)


def _dot01_left(m01, x):
    hi, mid, lo = _split3(x)
    return _dot(m01, hi) + _dot(m01, mid) + _dot(m01, lo)


def _iota(shape, dim):
    return lax.broadcasted_iota(jnp.int32, shape, dim)


def _rms(x, g):
    ms = jnp.mean(x * x, axis=-1, keepdims=True)
    return x * lax.rsqrt(ms + EPS) * g


def _softplus(x):
    return jnp.maximum(x, 0.0) + jnp.log1p(jnp.exp(-jnp.abs(x)))


def _log_sigmoid(x):
    return -_softplus(-x)


def _sigmoid(x):
    return 0.5 * jnp.tanh(0.5 * x) + 0.5


def _nm_kernel(x_ref, g_ref, w_ref, *o_refs, splits):
    xn = _bf(_rms(x_ref[...], g_ref[...]))
    off = 0
    for o_ref, n in zip(o_refs, splits):
        o_ref[...] = _dot(xn, w_ref[:, off:off + n])
        off += n


def norm_matmul(x, g, w, splits):
    n_rows, k = x.shape
    const = lambda i: (0, 0)
    return pl.pallas_call(
        functools.partial(_nm_kernel, splits=splits),
        grid=(n_rows // TM,),
        in_specs=[pl.BlockSpec((TM, k), lambda i: (i, 0)),
                  pl.BlockSpec((1, k), const),
                  pl.BlockSpec((k, sum(splits)), const, pipeline_mode=pl.Buffered(1))],
        out_specs=[pl.BlockSpec((TM, n), lambda i: (i, 0)) for n in splits],
        out_shape=[jax.ShapeDtypeStruct((n_rows, n), F32) for n in splits],
        compiler_params=pltpu.CompilerParams(dimension_semantics=("parallel",)),
        name="norm_matmul",
    )(x, g, w)


def _mix_ffn_kernel(a1_ref, a2_ref, h_ref, w1_ref, w2_ref, g_ref, wg_ref, wu_ref, wd_ref, o_ref):
    mix = _dot(_bf(a1_ref[...]), w1_ref[...]) + _dot(_bf(a2_ref[...]), w2_ref[...])
    h1 = h_ref[...] + _rms(mix, g_ref[0:1, :])
    xn = _bf(_rms(h1, g_ref[1:2, :]))
    acts = []
    for c in range(wg_ref.shape[1] // MXU_WIDTH):
        cols = slice(c * MXU_WIDTH, (c + 1) * MXU_WIDTH)
        gate = _dot(xn, wg_ref[:, cols])
        up = _dot(xn, wu_ref[:, cols])
        acts.append(_bf(gate * _sigmoid(gate) * up))
    ff = _dot(jnp.concatenate(acts, axis=1), wd_ref[...])
    o_ref[...] = h1 + _rms(ff, g_ref[2:3, :])


def mix_ffn(a1, a2, h, w1, w2, g, wg, wu, wd):
    k1 = a1.shape[1]
    k2 = a2.shape[1]
    n_rows, d = h.shape
    dff = wg.shape[1]
    assert dff % MXU_WIDTH == 0 and n_rows % TM == 0
    row_blk = lambda i: (i, 0)
    resident = lambda shape: pl.BlockSpec(shape, lambda i: (0, 0), pipeline_mode=pl.Buffered(1))
    return pl.pallas_call(
        _mix_ffn_kernel,
        grid=(n_rows // TM,),
        in_specs=[pl.BlockSpec((TM, k1), row_blk),
                  pl.BlockSpec((TM, k2), row_blk),
                  pl.BlockSpec((TM, d), row_blk),
                  resident((k1, d)),
                  resident((k2, d)),
                  resident((3, d)),
                  resident((d, dff)),
                  resident((d, dff)),
                  resident((dff, d))],
        out_specs=pl.BlockSpec((TM, d), row_blk),
        out_shape=jax.ShapeDtypeStruct((n_rows, d), F32),
        compiler_params=pltpu.CompilerParams(dimension_semantics=("parallel",),
                                             vmem_limit_bytes=MIX_FFN_VMEM_BYTES),
        name="mix_ffn",
    )(a1, a2, h, w1, w2, g, wg, wu, wd)


def _gates_kernel(z_ref, zt_ref, brow_ref, bcol_ref, o_ref, ot_ref):
    x = z_ref[...] + brow_ref[...]
    o_ref[...] = jnp.where(_iota(x.shape, 1) < M_HEADS, x, _log_sigmoid(x))
    xt = zt_ref[...] + bcol_ref[...]
    ot_ref[...] = jnp.where(_iota(xt.shape, 0) < M_HEADS, xt, _log_sigmoid(xt))


def gates_activate(zg, zgt, brow, bcol):
    n_rows = zg.shape[0]
    return pl.pallas_call(
        _gates_kernel,
        grid=(n_rows // TM,),
        in_specs=[pl.BlockSpec((TM, LANES), lambda i: (i, 0)),
                  pl.BlockSpec((16, TM), lambda i: (0, i)),
                  pl.BlockSpec((1, LANES), lambda i: (0, 0)),
                  pl.BlockSpec((16, 1), lambda i: (0, 0))],
        out_specs=[pl.BlockSpec((TM, LANES), lambda i: (i, 0)),
                   pl.BlockSpec((16, TM), lambda i: (0, i))],
        out_shape=[jax.ShapeDtypeStruct((n_rows, LANES), F32),
                   jax.ShapeDtypeStruct((16, n_rows), F32)],
        compiler_params=pltpu.CompilerParams(dimension_semantics=("parallel",)),
        name="gates_activate",
    )(zg, zgt, brow, bcol)


def _mlstm_chunk(q, k, v, logi_c, lf_c, logi_r, lf_r, c_state, n_state, m_state, row0, rows):
    ri = _iota((rows, rows), 0)
    ci = _iota((rows, rows), 1)
    valid_c = _iota((rows, 1), 0) >= row0
    valid_r = _iota((1, rows), 1) >= row0
    lf_c = jnp.where(valid_c, lf_c, 0.0)
    lf_r = jnp.where(valid_r, lf_r, 0.0)
    logi_c = jnp.where(valid_c, logi_c, NEG)
    logi_r = jnp.where(valid_r, logi_r, NEG)
    causal = ci <= ri
    b_c = jnp.sum(jnp.where(causal, lf_r, 0.0), axis=1, keepdims=True)
    b_r = jnp.sum(jnp.where(ri <= ci, lf_c, 0.0), axis=0, keepdims=True)
    log_d = jnp.where(causal, b_c - b_r + logi_r, -jnp.inf)
    m_inter = b_c + m_state
    m_t = jnp.maximum(m_inter, jnp.max(log_d, axis=1, keepdims=True))
    qb = _bf(q)
    kb = _bf(k)
    vb = _bf(v)
    w_intra = _dot_nt(qb, kb) * jnp.exp(log_d - m_t)
    w_inter = jnp.exp(m_inter - m_t)
    num = _dot(_bf(w_intra), vb) + w_inter * _dot(qb, _bf(c_state))
    den = jnp.sum(w_intra, axis=1, keepdims=True) + w_inter * jnp.sum(q * n_state, axis=1, keepdims=True)
    h = num / jnp.maximum(jnp.abs(den), jnp.exp(-m_t))
    m_new = m_t[rows - 1:rows, :]
    b_last = b_c[rows - 1:rows, :]
    w_src = jnp.exp(b_last - b_c + logi_c - m_new)
    decay = jnp.exp(b_last + m_state - m_new)
    ks = k * w_src
    c_new = decay * c_state + _dot_tn(_bf(ks), vb)
    n_new = decay * n_state + jnp.sum(ks, axis=0, keepdims=True)
    return h, c_new, n_new, m_new


def _mlstm_chunk_t(q, k, v, logi_r, lf_r, ct_state, n_state, m_state, row0, rows):
    ri = _iota((rows, rows), 0)
    ci = _iota((rows, rows), 1)
    valid_r = _iota((1, rows), 1) >= row0
    lf_r = jnp.where(valid_r, lf_r, 0.0)
    logi_r = jnp.where(valid_r, logi_r, NEG)
    upper = jnp.where(ri <= ci, 1.0, 0.0).astype(BF16)
    a = jnp.where(ci > ri, lf_r, jnp.where(ci == ri, logi_r, 0.0))
    pre = _dot01_right(jnp.concatenate([a, jnp.broadcast_to(lf_r, (8, rows))], axis=0), upper)
    b_r = pre[rows:rows + 1, :]
    log_d = jnp.where(ri <= ci, pre[0:rows, :], -jnp.inf)
    m_inter = b_r + m_state
    m_t = jnp.maximum(m_inter, jnp.max(log_d, axis=0, keepdims=True))
    qb = _bf(q)
    kb = _bf(k)
    w = _dot_nt(kb, qb) * jnp.exp(log_d - m_t)
    w_inter = jnp.exp(m_inter - m_t)
    vt = v.T
    num_t = _dot(_bf(vt), _bf(w)) + w_inter * _dot_nt(_bf(ct_state), qb)
    nq = _dot_nt(_bf(jnp.broadcast_to(n_state, (8, HEAD_DIM))), qb)[0:1, :]
    den = jnp.sum(w, axis=0, keepdims=True) + w_inter * nq
    h_t = num_t / jnp.maximum(jnp.abs(den), jnp.exp(-m_t))
    m_new = m_t[:, rows - 1:rows]
    b_last = b_r[:, rows - 1:rows]
    w_src = jnp.exp(b_last - b_r + logi_r - m_new)
    decay = jnp.exp(b_last + m_state - m_new)
    ct_new = decay * ct_state + _dot(_bf(vt * w_src), kb)
    n_new = decay * n_state + _dot(_bf(jnp.broadcast_to(w_src, (8, rows))), kb)[0:1, :]
    return h_t.T, ct_new, n_new, m_new


def _mlstm_kernel(z_ref, g_ref, gt_ref, c0_ref, n0_ref, m0_ref, o_ref, c_ref, n_ref, m_ref, *,
                  rows, row0_first, transposed):
    c = pl.program_id(1)

    @pl.when(c == 0)
    def _():
        c_ref[...] = c0_ref[...]
        n_ref[...] = n0_ref[...]
        m_ref[...] = m0_ref[...]

    row0 = jnp.where(c == 0, row0_first, 0)
    for s in range(SEQ_GROUP):
        g = g_ref[s]
        gt = gt_ref[s, 0]
        outs = []
        for h in range(M_HEADS):
            lo = h * HEAD_DIM
            q = z_ref[s, :, lo:lo + HEAD_DIM]
            k = z_ref[s, :, 512 + lo:512 + lo + HEAD_DIM] * (HEAD_DIM ** -0.5)
            v = z_ref[s, :, 1024 + lo:1024 + lo + HEAD_DIM]
            og = z_ref[s, :, 1536 + lo:1536 + lo + HEAD_DIM]
            state = (c_ref[s, h], n_ref[s, h:h + 1, :], m_ref[s, h:h + 1, 0:1])
            if transposed:
                hh, c_new, n_new, m_new = _mlstm_chunk_t(
                    q, k, v, gt[h:h + 1, :], gt[M_HEADS + h:M_HEADS + h + 1, :], *state, row0, rows)
            else:
                hh, c_new, n_new, m_new = _mlstm_chunk(
                    q, k, v,
                    g[:, h:h + 1], g[:, M_HEADS + h:M_HEADS + h + 1],
                    gt[h:h + 1, :], gt[M_HEADS + h:M_HEADS + h + 1, :], *state, row0, rows)
            c_ref[s, h] = c_new
            n_ref[s, h:h + 1, :] = n_new
            m_ref[s, h:h + 1, :] = jnp.broadcast_to(m_new, (1, LANES))
            outs.append(_sigmoid(og) * hh)
        o_ref[s] = jnp.concatenate(outs, axis=1)


def _state_specs(heads):
    return [pl.BlockSpec((SEQ_GROUP, heads, HEAD_DIM, HEAD_DIM), lambda b, c: (b, 0, 0, 0)),
            pl.BlockSpec((SEQ_GROUP, heads, HEAD_DIM), lambda b, c: (b, 0, 0)),
            pl.BlockSpec((SEQ_GROUP, heads, LANES), lambda b, c: (b, 0, 0))]


def mlstm_call(z, g2, g2t, c0, n0, m0, *, row0_first, transposed):
    nb, t_len, _ = z.shape
    n_chunks, rows = g2t.shape[1], g2t.shape[3]
    assert nb % SEQ_GROUP == 0 and n_chunks * rows == t_len
    blk = lambda b, c: (b, c, 0)
    return pl.pallas_call(
        functools.partial(_mlstm_kernel, rows=rows, row0_first=row0_first, transposed=transposed),
        grid=(nb // SEQ_GROUP, n_chunks),
        in_specs=[pl.BlockSpec((SEQ_GROUP, rows, 2048), blk),
                  pl.BlockSpec((SEQ_GROUP, rows, LANES), blk),
                  pl.BlockSpec((SEQ_GROUP, 1, 16, rows), lambda b, c: (b, c, 0, 0))] + _state_specs(M_HEADS),
        out_specs=[pl.BlockSpec((SEQ_GROUP, rows, 512), blk)] + _state_specs(M_HEADS),
        out_shape=[jax.ShapeDtypeStruct((nb, t_len, 512), F32),
                   jax.ShapeDtypeStruct((nb, M_HEADS, HEAD_DIM, HEAD_DIM), F32),
                   jax.ShapeDtypeStruct((nb, M_HEADS, HEAD_DIM), F32),
                   jax.ShapeDtypeStruct((nb, M_HEADS, LANES), F32)],
        compiler_params=pltpu.CompilerParams(dimension_semantics=("parallel", "arbitrary")),
        name="mlstm",
    )(z, g2, g2t, c0, n0, m0)


def _ret_log_gamma(h):
    return math.log1p(-(2.0 ** (-5.0 - h)))


def _ret_kernel(z_ref, cos_ref, sin_ref, lng_ref, s0_ref, o_ref, s_ref, *, rows, row0_first):
    c = pl.program_id(1)

    @pl.when(c == 0)
    def _():
        s_ref[...] = s0_ref[...]

    row0 = jnp.where(c == 0, row0_first, 0)
    n_valid = (rows - row0).astype(F32)
    cosf = cos_ref[...]
    sinf = sin_ref[...]
    ri = _iota((rows, rows), 0)
    ci = _iota((rows, rows), 1)
    diff = (ri - ci).astype(F32)
    rowi = _iota((rows, 1), 0)
    te = (rowi - row0).astype(F32)
    valid = rowi >= row0
    for h in range(R_HEADS):
        lg = _ret_log_gamma(h)
        lo = h * HEAD_DIM
        decay_m = jnp.where(diff >= 0, jnp.exp(diff * lg), 0.0)
        w_inter = jnp.exp((te + 1.0) * lg)
        w_src = jnp.exp((n_valid - 1.0 - te) * lg)
        s_decay = jnp.exp(n_valid * lg)
        for s in range(SEQ_GROUP):
            q = z_ref[s, :, lo:lo + HEAD_DIM]
            k = z_ref[s, :, 512 + lo:512 + lo + HEAD_DIM]
            v = jnp.where(valid, z_ref[s, :, 1024 + lo:1024 + lo + HEAD_DIM], 0.0)
            rg = z_ref[s, :, 1536 + lo:1536 + lo + HEAD_DIM]
            q = q * cosf + pltpu.roll(q, HEAD_DIM // 2, 1) * sinf
            k = (k * cosf + pltpu.roll(k, HEAD_DIM // 2, 1) * sinf) * (HEAD_DIM ** -0.5)
            s_state = s_ref[s, h]
            qb = _bf(q)
            vb = _bf(v)
            intra = _dot(_bf(_dot_nt(qb, _bf(k)) * decay_m), vb)
            inter = _dot(qb, _bf(s_state)) * w_inter
            s_ref[s, h] = s_decay * s_state + _dot_tn(_bf(k * w_src), vb)
            hr = intra + inter
            mu = jnp.mean(hr, axis=-1, keepdims=True)
            var = jnp.mean(jnp.square(hr - mu), axis=-1, keepdims=True)
            hr = (hr - mu) * lax.rsqrt(var + EPS) * lng_ref[h:h + 1, :]
            o_ref[s, :, lo:lo + HEAD_DIM] = rg * _sigmoid(rg) * hr


def retention_call(z, cosf, sinf, lng, s0, *, rows, row0_first):
    nb, t_len, _ = z.shape
    assert nb % SEQ_GROUP == 0 and t_len % rows == 0
    blk = lambda b, c: (b, c, 0)
    sspec = pl.BlockSpec((SEQ_GROUP, R_HEADS, HEAD_DIM, HEAD_DIM), lambda b, c: (b, 0, 0, 0))
    return pl.pallas_call(
        functools.partial(_ret_kernel, rows=rows, row0_first=row0_first),
        grid=(nb // SEQ_GROUP, t_len // rows),
        in_specs=[pl.BlockSpec((SEQ_GROUP, rows, 2048), blk),
                  pl.BlockSpec((rows, HEAD_DIM), lambda b, c: (c, 0)),
                  pl.BlockSpec((rows, HEAD_DIM), lambda b, c: (c, 0)),
                  pl.BlockSpec((R_HEADS, HEAD_DIM), lambda b, c: (0, 0)),
                  sspec],
        out_specs=[pl.BlockSpec((SEQ_GROUP, rows, 512), blk), sspec],
        out_shape=[jax.ShapeDtypeStruct((nb, t_len, 512), F32),
                   jax.ShapeDtypeStruct((nb, R_HEADS, HEAD_DIM, HEAD_DIM), F32)],
        compiler_params=pltpu.CompilerParams(dimension_semantics=("parallel", "arbitrary")),
        name="retention",
    )(z, cosf, sinf, lng, s0)


def _lru_kernel(z_ref, pre_ref, h0_ref, cw_ref, cb_ref, wa_ref, ba_ref, wx_ref, bx_ref, lam_ref,
                o_ref, hlast_ref, tail_ref, cbuf_s, a_s, b_s, *, rows, row0_first):
    c = pl.program_id(1)

    @pl.when(c == 0)
    def _():
        hlast_ref[...] = h0_ref[...]
        tail_ref[...] = jnp.zeros(tail_ref.shape, F32)

    row0 = jnp.where(c == 0, row0_first, 0)
    rowi = _iota((rows, 1), 0)
    valid = rowi >= row0
    sp_lam = _softplus(-lam_ref[...])
    for s in range(SEQ_GROUP):
        lx = jnp.where(valid, z_ref[s, :, 0:LRU_WIDTH], pre_ref[min(s, pre_ref.shape[0] - 1)])
        cbuf_s[s, 0:8, :] = tail_ref[s]
        cbuf_s[s, 8:8 + rows, :] = lx
        xc = cb_ref[...] + cw_ref[CONV_W - 1:CONV_W, :] * lx
        for j in range(1, CONV_W):
            xc = xc + cw_ref[CONV_W - 1 - j:CONV_W - j, :] * cbuf_s[s, 8 - j:8 - j + rows, :]
        tail_ref[s] = lx[rows - 8:rows]
        xcb = _bf(xc)
        pre_a = []
        pre_x = []
        for p in range(LRU_WIDTH // LANES):
            xs = xcb[:, p * LANES:(p + 1) * LANES]
            pre_a.append(_dot(xs, wa_ref[p]))
            pre_x.append(_dot(xs, wx_ref[p]))
        r = _sigmoid(jnp.concatenate(pre_a, axis=1) + ba_ref[...])
        ig = _sigmoid(jnp.concatenate(pre_x, axis=1) + bx_ref[...])
        log_a = -LRU_C * r * sp_lam
        a = jnp.where(valid, jnp.exp(log_a), 1.0)
        one_minus_a2 = -jnp.tanh(log_a) * (jnp.exp(2.0 * log_a) + 1.0)
        bx = jnp.where(valid, jnp.sqrt(one_minus_a2) * (ig * xc), 0.0)
        a_s[s] = a
        b_s[s] = bx
    carries = [hlast_ref[s] for s in range(SEQ_GROUP)]
    for t in range(rows):
        for s in range(SEQ_GROUP):
            carries[s] = a_s[s, t:t + 1, :] * carries[s] + b_s[s, t:t + 1, :]
            b_s[s, t:t + 1, :] = carries[s]
    for s in range(SEQ_GROUP):
        hlast_ref[s] = carries[s]
        gate = z_ref[s, :, LRU_WIDTH:2 * LRU_WIDTH]
        gl = 0.5 * gate * (1.0 + jnp.tanh(math.sqrt(2.0 / math.pi) * (gate + 0.044715 * gate * gate * gate)))
        o_ref[s] = b_s[s] * gl


def lru_call(z, pre, h0, cw, cb, wa2, ba, wx2, bx, lam, *, rows, row0_first):
    nb, t_len, _ = z.shape
    assert nb % SEQ_GROUP == 0 and t_len % rows == 0
    blk = lambda b, c: (b, c, 0)
    per_seq = lambda b, c: (b, 0, 0)
    const2 = lambda b, c: (0, 0)
    const3 = lambda b, c: (0, 0, 0)
    shared_pre = pre.shape[0] == 1
    in_specs = [pl.BlockSpec((SEQ_GROUP, rows, 1024), blk),
                pl.BlockSpec((1 if shared_pre else SEQ_GROUP, rows, LRU_WIDTH), const3 if shared_pre else per_seq),
                pl.BlockSpec((SEQ_GROUP, 1, LRU_WIDTH), per_seq),
                pl.BlockSpec((CONV_W, LRU_WIDTH), const2),
                pl.BlockSpec((1, LRU_WIDTH), const2),
                pl.BlockSpec((LRU_WIDTH // LANES, LANES, LANES), const3),
                pl.BlockSpec((1, LRU_WIDTH), const2),
                pl.BlockSpec((LRU_WIDTH // LANES, LANES, LANES), const3),
                pl.BlockSpec((1, LRU_WIDTH), const2),
                pl.BlockSpec((1, LRU_WIDTH), const2)]
    return pl.pallas_call(
        functools.partial(_lru_kernel, rows=rows, row0_first=row0_first),
        grid=(nb // SEQ_GROUP, t_len // rows),
        in_specs=in_specs,
        out_specs=[pl.BlockSpec((SEQ_GROUP, rows, LRU_WIDTH), blk),
                   pl.BlockSpec((SEQ_GROUP, 1, LRU_WIDTH), per_seq),
                   pl.BlockSpec((SEQ_GROUP, 8, LRU_WIDTH), per_seq)],
        out_shape=[jax.ShapeDtypeStruct((nb, t_len, LRU_WIDTH), F32),
                   jax.ShapeDtypeStruct((nb, 1, LRU_WIDTH), F32),
                   jax.ShapeDtypeStruct((nb, 8, LRU_WIDTH), F32)],
        scratch_shapes=[pltpu.VMEM((SEQ_GROUP, rows + 8, LRU_WIDTH), F32),
                        pltpu.VMEM((SEQ_GROUP, rows, LRU_WIDTH), F32),
                        pltpu.VMEM((SEQ_GROUP, rows, LRU_WIDTH), F32)],
        compiler_params=pltpu.CompilerParams(dimension_semantics=("parallel", "arbitrary")),
        name="rglru",
    )(z, pre, h0, cw, cb, wa2, ba, wx2, bx, lam)


def _fox_cumsum_kernel(g_ref, gt_ref, fc_ref, fr_ref):
    ri = _iota((CHUNK, CHUNK), 0)
    ci = _iota((CHUNK, CHUNK), 1)
    lower = jnp.where(ci <= ri, 1.0, 0.0).astype(BF16)
    upper = jnp.where(ri <= ci, 1.0, 0.0).astype(BF16)
    carry_r = jnp.zeros((1, LANES), F32)
    carry_c = jnp.zeros((16, 1), F32)
    for blk in range(NCH):
        x = g_ref[blk * CHUNK:(blk + 1) * CHUNK, :]
        xt = gt_ref[:, blk * CHUNK:(blk + 1) * CHUNK]
        if blk == 0:
            x = jnp.where(_iota((CHUNK, 1), 0) >= PADF, x, 0.0)
            xt = jnp.where(_iota((1, CHUNK), 1) >= PADF, xt, 0.0)
        cs = _dot01_left(lower, x) + carry_r
        cst = _dot01_right(xt, upper) + carry_c
        carry_r = cs[CHUNK - 1:CHUNK, :]
        carry_c = cst[:, CHUNK - 1:CHUNK]
        if blk == 0:
            cst = jnp.where(_iota((1, CHUNK), 1) >= PADF, cst, -NEG)
        fc_ref[0, blk * CHUNK:(blk + 1) * CHUNK, :] = cs
        fr_ref[0, :, blk * CHUNK:(blk + 1) * CHUNK] = cst


def fox_cumsum(g2, g2t):
    return pl.pallas_call(
        _fox_cumsum_kernel,
        grid=(BATCH,),
        in_specs=[pl.BlockSpec((TP, LANES), lambda b: (b, 0)),
                  pl.BlockSpec((16, TP), lambda b: (0, b))],
        out_specs=[pl.BlockSpec((1, TP, LANES), lambda b: (b, 0, 0)),
                   pl.BlockSpec((1, 16, TP), lambda b: (b, 0, 0))],
        out_shape=[jax.ShapeDtypeStruct((BATCH, TP, LANES), F32),
                   jax.ShapeDtypeStruct((BATCH, 16, TP), F32)],
        compiler_params=pltpu.CompilerParams(dimension_semantics=("parallel",)),
        name="fox_cumsum",
    )(g2, g2t)


def _fox_prompt_kernel(q_ref, k_ref, v_ref, fr_ref, fc_ref, o_ref,
                       qta_s, qtb_s, ka_s, kb_s, vt_s, m_s, l_s, acc_s, ua_s):
    i = pl.program_id(2)
    lane = _iota((1, LANES), 1)
    is_a = lane < F_DH
    row = _iota((LANES, 1), 0)
    row_a = row < F_DH
    k_heads = (ka_s, kb_s)
    qt_heads = (qta_s, qtb_s)

    @pl.when(i == 0)
    def _():
        k = k_ref[...]
        is_pad = _iota((TP, 1), 0) < PADF
        for hh, k_s in enumerate(k_heads):
            fk = jnp.where(is_pad, -NEG, fc_ref[0, 0, :, hh:hh + 1] * LOG2E)
            hi, mid, lo = (t.astype(F32) for t in _split3(fk))
            base = (1 - hh) * F_DH
            extra = jnp.where(lane == base, hi, jnp.where(lane == base + 1, mid, jnp.where(lane == base + 2, lo, 0.0)))
            own = is_a if hh == 0 else jnp.logical_not(is_a)
            k_s[...] = _bf(jnp.where(own, k, extra))

        minus_a = jnp.where((row >= F_DH) & (row < F_DH + 3), -1.0, 0.0)
        minus_b = jnp.where(row < 3, -1.0, 0.0)

        def fill(blk, carry):
            rows = pl.ds(pl.multiple_of(blk * CHUNK, CHUNK), CHUNK)
            qt = (q_ref[rows, :] * (F_DH ** -0.5 * LOG2E)).T
            qta_s[blk] = _bf(jnp.where(row_a, qt, minus_a))
            qtb_s[blk] = _bf(jnp.where(row_a, minus_b, qt))
            vt_s[blk] = _bf(v_ref[rows, :].T)
            return carry

        lax.fori_loop(0, NCH, fill, 0)

    def init():
        m_s[...] = jnp.full(m_s.shape, NEG, F32)
        l_s[...] = jnp.zeros(l_s.shape, F32)
        acc_s[...] = jnp.zeros(acc_s.shape, F32)

    def scores(hh, qt, kblk, nkb):
        ck = nkb * CHUNK
        k0 = kblk * CHUNK if isinstance(kblk, int) else pl.multiple_of(kblk * CHUNK, CHUNK)
        return _dot(k_heads[hh][pl.ds(k0, ck), :], qt[hh])

    def step(hh, u, qblk, nqb, kblk, nkb, causal):
        tq = nqb * CHUNK
        ck = nkb * CHUNK
        fq = jnp.concatenate([fr_ref[0, 0, qblk + t, hh:hh + 1, :] for t in range(nqb)], axis=1)
        fq = jnp.where(fq > 0.5 * -NEG, 0.0, fq * LOG2E)
        if causal:
            u = jnp.where(kblk * CHUNK + _iota((ck, tq), 0) <= qblk * CHUNK + _iota((ck, tq), 1), u, NEG)
        m_old = m_s[hh, :, 0:tq]
        m_new = jnp.maximum(m_old, jnp.max(u, axis=0, keepdims=True) + fq)
        alpha = jnp.exp2(m_old - m_new)
        p = _bf(jnp.exp2(u + (fq - m_new)))
        m_s[hh, :, 0:tq] = m_new
        feat = slice(hh * F_DH, (hh + 1) * F_DH)
        vt = jnp.concatenate([vt_s[kblk + t, feat, :] for t in range(nkb)], axis=1)
        pv = _dot(jnp.concatenate([vt, jnp.ones((16, ck), BF16)], axis=0), p)
        l_s[hh, :, 0:tq] = alpha * l_s[hh, :, 0:tq] + pv[F_DH:F_DH + 1, :]
        acc_s[feat, 0:tq] = alpha * acc_s[feat, 0:tq] + pv[0:F_DH, :]

    def finalize(qblk, nqb):
        tq = nqb * CHUNK
        o_t = acc_s[:, 0:tq] / jnp.where(row_a, l_s[0, :, 0:tq], l_s[1, :, 0:tq])
        for t in range(nqb):
            rows = pl.ds(pl.multiple_of((qblk + t) * CHUNK, CHUNK), CHUNK)
            o_ref[rows, :] = o_t[:, t * CHUNK:(t + 1) * CHUNK].T

    @pl.when(i == 0)
    def _():
        init()
        qt0 = (qta_s[0], qtb_s[0])
        for hh in range(2):
            step(hh, scores(hh, qt0, 0, 1), 0, 1, 0, 1, True)
        finalize(0, 1)

    nck = FOX_CK // CHUNK
    nqb = FOX_TQ // CHUNK
    qblk = 1 + i * nqb
    qt = tuple(jnp.concatenate([qt_h[qblk + t] for t in range(nqb)], axis=1) for qt_h in qt_heads)
    init()
    for hh in range(2):
        step(hh, scores(hh, qt, 0, 1), qblk, nqb, 0, 1, False)

    ua_s[...] = scores(0, qt, 1, nck)

    def body(j, carry):
        kblk = 1 + j * nck
        ub = scores(1, qt, kblk, nck)
        step(0, ua_s[...], qblk, nqb, kblk, nck, False)
        ua_s[...] = scores(0, qt, kblk + nck, nck)
        step(1, ub, qblk, nqb, kblk, nck, False)
        return carry

    lax.fori_loop(0, i, body, 0)
    ub = scores(1, qt, qblk, nck)
    step(0, ua_s[...], qblk, nqb, qblk, nck, True)
    step(1, ub, qblk, nqb, qblk, nck, True)
    finalize(qblk, nqb)


def fox_prompt(zq, zk, zv, fr, fc):
    nq = (TP - CHUNK) // FOX_TQ
    pairs = F_HEADS // 2
    return pl.pallas_call(
        _fox_prompt_kernel,
        grid=(BATCH, pairs, nq),
        in_specs=[pl.BlockSpec((TP, LANES), lambda b, p, i: (b, p)),
                  pl.BlockSpec((TP, LANES), lambda b, p, i: (b, p)),
                  pl.BlockSpec((TP, LANES), lambda b, p, i: (b, p)),
                  pl.BlockSpec((1, 1, NCH, 2, LANES), lambda b, p, i: (b, p, 0, 0, 0)),
                  pl.BlockSpec((1, 1, TP, 2), lambda b, p, i: (b, p, 0, 0))],
        out_specs=pl.BlockSpec((TP, LANES), lambda b, p, i: (b, p)),
        out_shape=jax.ShapeDtypeStruct((SBASE, 512), F32),
        scratch_shapes=[pltpu.VMEM((NCH, LANES, CHUNK), BF16),
                        pltpu.VMEM((NCH, LANES, CHUNK), BF16),
                        pltpu.VMEM((TP, LANES), BF16),
                        pltpu.VMEM((TP, LANES), BF16),
                        pltpu.VMEM((NCH, LANES, CHUNK), BF16),
                        pltpu.VMEM((2, 1, FOX_TQ), F32),
                        pltpu.VMEM((2, 1, FOX_TQ), F32),
                        pltpu.VMEM((LANES, FOX_TQ), F32),
                        pltpu.VMEM((FOX_CK, FOX_TQ), F32)],
        compiler_params=pltpu.CompilerParams(dimension_semantics=("parallel", "parallel", "arbitrary")),
        name="fox_prompt",
    )(zq, zk, zv, fr, fc)


def _lf_suffix_kernel(x_ref, suf_ref, tot_ref):
    ri = _iota((PAGE_SIZE, PAGE_SIZE), 0)
    ci = _iota((PAGE_SIZE, PAGE_SIZE), 1)
    after = jnp.where(ri > ci, 1.0, 0.0).astype(BF16)
    ones = jnp.ones((PAGE_SIZE, PAGE_SIZE), BF16)
    x = x_ref[...].reshape(PRE_PG * F_HEADS, PAGE_SIZE)
    suf_ref[...] = _dot01_right(x, after).reshape(PRE_PG, F_HEADS, PAGE_SIZE)
    tot_ref[...] = _dot01_right(x, ones).reshape(PRE_PG, F_HEADS, PAGE_SIZE)


def lf_suffix(lft):
    n_phys = lft.shape[0]
    spec = pl.BlockSpec((PRE_PG, F_HEADS, PAGE_SIZE), lambda i: (i, 0, 0))
    return pl.pallas_call(
        _lf_suffix_kernel,
        grid=(n_phys // PRE_PG,),
        in_specs=[spec],
        out_specs=[spec, spec],
        out_shape=[jax.ShapeDtypeStruct(lft.shape, F32)] * 2,
        compiler_params=pltpu.CompilerParams(dimension_semantics=("parallel",)),
        name="lf_suffix",
    )(lft)


def _fox_decode_kernel(pt_ref, zq_ref, zk_ref, zv_ref, gt_ref, *rest):
    k_refs = rest[0:DEC_PG]
    v_refs = rest[DEC_PG:2 * DEC_PG]
    suf_refs = rest[2 * DEC_PG:3 * DEC_PG]
    tot_refs = rest[3 * DEC_PG:4 * DEC_PG]
    o_ref = rest[4 * DEC_PG]
    q2_s, m_s, l_s, acc_s, run_s, nc_s = rest[4 * DEC_PG + 1:]
    j = pl.program_id(1)
    n_rows = DEC_SEQ * F_HEADS
    width = F_HEADS * F_DH

    def update(u, v, v_feature_major):
        m_old = m_s[...]
        m_new = jnp.maximum(m_old, jnp.max(u, axis=1, keepdims=True))
        alpha = jnp.exp(m_old - m_new)
        p = jnp.exp(u - m_new[:, 0:u.shape[1]])
        pv = _dot_nt(_bf(p), v) if v_feature_major else _dot(_bf(p), v)
        l_s[...] = alpha * l_s[...] + jnp.sum(p, axis=1, keepdims=True)
        acc_s[...] = jnp.concatenate([alpha] * (width // LANES), axis=1) * acc_s[...] + pv
        m_s[...] = m_new

    @pl.when(j == 0)
    def _():
        m_s[...] = jnp.full(m_s.shape, NEG, F32)
        l_s[...] = jnp.zeros(l_s.shape, F32)
        acc_s[...] = jnp.zeros(acc_s.shape, F32)
        run_s[...] = jnp.zeros(run_s.shape, F32)
        head_mask = _iota((F_HEADS, width), 1) // F_DH == _iota((F_HEADS, width), 0)
        q16 = zq_ref[...] * (F_DH ** -0.5)
        q2 = _bf(jnp.concatenate(
            [jnp.where(head_mask, q16[SPAD + t:SPAD + t + 1, :], 0.0) for t in range(DEC_SEQ)], axis=0))
        q2_s[...] = q2
        lf_new = gt_ref[F_HEADS:2 * F_HEADS, :]
        lane = _iota((1, SROWS), 1)
        cum = jnp.zeros((F_HEADS, SROWS), F32)
        for t in range(DEC_SEQ):
            cum = cum + jnp.where(lane >= SPAD + t, lf_new[:, SPAD + t:SPAD + t + 1], 0.0)
        nc = jnp.concatenate([cum[:, SPAD + t:SPAD + t + 1] for t in range(DEC_SEQ)], axis=0)
        nc_s[...] = jnp.broadcast_to(nc, nc_s.shape)
        cum4 = jnp.concatenate([cum] * DEC_SEQ, axis=0)
        qi = _iota((n_rows, SROWS), 0) // F_HEADS
        kj = _iota((n_rows, SROWS), 1) - SPAD
        u = jnp.where((kj >= 0) & (kj <= qi), _dot_nt(q2, _bf(zk_ref[...])) + (nc - cum4), NEG)
        update(u, _bf(zv_ref[...]), False)

    q2 = q2_s[...]
    nc = nc_s[...]
    run = run_s[...]
    us = []
    for i in range(DEC_PG):
        later = run + suf_refs[i][0]
        bias = jnp.concatenate([later] * DEC_SEQ, axis=0) + nc
        us.append(_dot(q2, _bf(k_refs[i][0].reshape(width, PAGE_SIZE))) + bias)
        run = run + tot_refs[i][0]
    run_s[...] = run
    u = jnp.concatenate(us, axis=1)
    m_old = m_s[...]
    m_new = jnp.maximum(m_old, jnp.max(u, axis=1, keepdims=True))
    alpha = jnp.exp(m_old - m_new)
    p = jnp.exp(u - jnp.concatenate([m_new] * DEC_PG, axis=1))
    l_s[...] = alpha * l_s[...] + jnp.sum(p, axis=1, keepdims=True)
    pv = _dot_nt(_bf(p[:, 0:PAGE_SIZE]), _bf(v_refs[0][0].reshape(width, PAGE_SIZE)))
    for i in range(1, DEC_PG):
        pv = pv + _dot_nt(_bf(p[:, i * PAGE_SIZE:(i + 1) * PAGE_SIZE]), _bf(v_refs[i][0].reshape(width, PAGE_SIZE)))
    acc_s[...] = jnp.concatenate([alpha] * (width // LANES), axis=1) * acc_s[...] + pv
    m_s[...] = m_new

    @pl.when(j == pl.num_programs(1) - 1)
    def _():
        head_mask = (_iota((n_rows, width), 1) // F_DH) == (_iota((n_rows, width), 0) % F_HEADS)
        o2 = jnp.where(head_mask, acc_s[...] / jnp.concatenate([l_s[...]] * (width // LANES), axis=1), 0.0)
        o_ref[...] = jnp.zeros(o_ref.shape, F32)
        for t in range(DEC_SEQ):
            o_ref[SPAD + t:SPAD + t + 1, :] = jnp.sum(o2[t * F_HEADS:(t + 1) * F_HEADS, :], axis=0, keepdims=True)


def fox_decode(page_table, zq, zk, zv, g2t_s, cache_kt, cache_vt, suf, tot):
    n_pages = page_table.shape[1]
    width = F_HEADS * F_DH

    def page_map(i):
        return lambda b, j, pt: (pt[b, n_pages - 1 - (j * DEC_PG + i)], 0, 0, 0)

    def page_map3(i):
        return lambda b, j, pt: (pt[b, n_pages - 1 - (j * DEC_PG + i)], 0, 0)

    in_specs = [pl.BlockSpec((SROWS, width), lambda b, j, pt: (b, 0)),
                pl.BlockSpec((SROWS, width), lambda b, j, pt: (b, 0)),
                pl.BlockSpec((SROWS, width), lambda b, j, pt: (b, 0)),
                pl.BlockSpec((None, 16, SROWS), lambda b, j, pt: (b, 0, 0))]
    in_specs += [pl.BlockSpec((1, F_HEADS, F_DH, PAGE_SIZE), page_map(i)) for i in range(DEC_PG)]
    in_specs += [pl.BlockSpec((1, F_HEADS, F_DH, PAGE_SIZE), page_map(i)) for i in range(DEC_PG)]
    in_specs += [pl.BlockSpec((1, F_HEADS, PAGE_SIZE), page_map3(i)) for i in range(DEC_PG)]
    in_specs += [pl.BlockSpec((1, F_HEADS, PAGE_SIZE), page_map3(i)) for i in range(DEC_PG)]
    args = [zq, zk, zv, g2t_s] + [cache_kt] * DEC_PG + [cache_vt] * DEC_PG + [suf] * DEC_PG + [tot] * DEC_PG
    n_rows = DEC_SEQ * F_HEADS
    grid_spec = pltpu.PrefetchScalarGridSpec(
        num_scalar_prefetch=1,
        grid=(DEC_BATCH, n_pages // DEC_PG),
        in_specs=in_specs,
        out_specs=pl.BlockSpec((SROWS, width), lambda b, j, pt: (b, 0)),
        scratch_shapes=[pltpu.VMEM((n_rows, width), BF16),
                        pltpu.VMEM((n_rows, LANES), F32),
                        pltpu.VMEM((n_rows, LANES), F32),
                        pltpu.VMEM((n_rows, width), F32),
                        pltpu.VMEM((F_HEADS, PAGE_SIZE), F32),
                        pltpu.VMEM((n_rows, LANES), F32)])
    return pl.pallas_call(
        _fox_decode_kernel,
        grid_spec=grid_spec,
        out_shape=jax.ShapeDtypeStruct((DEC_BATCH * SROWS, width), F32),
        compiler_params=pltpu.CompilerParams(dimension_semantics=("parallel", "arbitrary")),
        name="fox_decode",
    )(page_table, *args)


def _block_diag_pairs(w):
    z = jnp.zeros((LANES // 2, LANES // 2), w.dtype)
    pairs = [jnp.block([[w[2 * p], z], [z, w[2 * p + 1]]]) for p in range(w.shape[0] // 2)]
    return _bf(jnp.stack(pairs))


def _rope_tables(pos):
    half = HEAD_DIM // 2
    freq = ROPE_BASE ** (-jnp.arange(half, dtype=F32) / half)
    ang = pos.astype(F32)[:, None] * freq[None, :]
    cos, sin = jnp.cos(ang), jnp.sin(ang)
    return jnp.concatenate([cos, cos], axis=1), jnp.concatenate([-sin, sin], axis=1)


def _unpad_prompt(x):
    return x.reshape((BATCH, TP) + x.shape[1:])[:, PADF:]


def _unpad_sample(x):
    return x.reshape((DEC_BATCH, SROWS) + x.shape[1:])[:, SPAD:]


def kernel(x_prompt, x_sample, cache_fox_k, cache_fox_v, cache_fox_logf, state_mlstm_C, state_mlstm_n, state_mlstm_m, state_ret_S, state_lru_h, state_lru_conv, page_table, meta_tokens, w_in_even, b_mlstm_i, b_mlstm_f, b_fox_f, w_out_even, w_in_odd, ret_ln_g, conv_w, conv_b, lru_wa, lru_ba, lru_wx, lru_bx, lru_lambda, w_out_odd, norm_g, ffn_wg, ffn_wu, ffn_wd):
    n_pages = page_table.shape[1]
    past = n_pages * PAGE_SIZE
    n_phys = cache_fox_k.shape[0]

    head = jnp.concatenate([jnp.zeros((PADF, D_MODEL), F32), meta_tokens.astype(F32)], axis=0)
    hs = jnp.concatenate([jnp.zeros((DEC_BATCH, SPAD, D_MODEL), F32), x_sample.astype(F32)],
                         axis=1).reshape(DEC_BATCH * SROWS, D_MODEL)
    h_s = hs
    h_p = jnp.concatenate([jnp.broadcast_to(head[None], (BATCH, CHUNK, D_MODEL)), x_prompt.astype(F32)],
                          axis=1).reshape(SBASE, D_MODEL)

    c = np.cumsum([0, 512, 512, 512, 512, 4, 4, 512, 512, 512, 8])
    gate_w = jnp.concatenate([w_in_even[:, c[4]:c[6]], w_in_even[:, c[9]:c[10]],
                              jnp.zeros((D_MODEL, LANES - 16), F32)], axis=1)
    w_even = _bf(jnp.concatenate([w_in_even[:, c[0]:c[4]], w_in_even[:, c[6]:c[9]], gate_w], axis=1))
    gate_b = jnp.concatenate([b_mlstm_i, b_mlstm_f, b_fox_f]).astype(F32)
    brow = jnp.concatenate([gate_b, jnp.zeros((LANES - 16,), F32)]).reshape(1, LANES)
    bcol = gate_b.reshape(16, 1)
    w_odd = _bf(w_in_odd)
    wo_even = _bf(w_out_even)
    wo_odd = _bf(w_out_odd)
    wg = _bf(ffn_wg)
    wu = _bf(ffn_wu)
    wd = _bf(ffn_wd)
    ng = norm_g.astype(F32).reshape(2, 4, 1, D_MODEL)

    even_cols = (2048, 512, 512, 512, LANES)
    zm_p, zq_p, zk_p, zv_p, zg_p = norm_matmul(h_p, ng[0, 0], w_even, even_cols)
    zm_s, zq_s, zk_s, zv_s, zg_s = norm_matmul(h_s, ng[0, 0], w_even, even_cols)
    g2_p, g2t_p = gates_activate(zg_p, zg_p[:, :16].T, brow, bcol)
    g2_s, g2t_s = gates_activate(zg_s, zg_s[:, :16].T, brow, bcol)
    prompt3 = lambda x: x.reshape(BATCH, TP, x.shape[-1])
    sample3 = lambda x: x.reshape(DEC_BATCH, SROWS, x.shape[-1])
    flat2 = lambda x: x.reshape(-1, x.shape[-1])

    zeros_c = jnp.zeros((BATCH, M_HEADS, HEAD_DIM, HEAD_DIM), F32)
    zeros_n = jnp.zeros((BATCH, M_HEADS, HEAD_DIM), F32)
    hm_p, p_c, p_n, p_m = mlstm_call(prompt3(zm_p), prompt3(g2_p),
                                     g2t_p.reshape(16, BATCH, NCH, CHUNK).transpose(1, 2, 0, 3),
                                     zeros_c, zeros_n, zeros_n, row0_first=PADF, transposed=True)
    p_c = p_c.transpose(0, 1, 3, 2)
    m0_s = jnp.broadcast_to(state_mlstm_m.astype(F32)[:, :, None], (DEC_BATCH, M_HEADS, LANES))
    hm_s, s_c, s_n, s_m = mlstm_call(sample3(zm_s), sample3(g2_s),
                                     g2t_s.reshape(16, DEC_BATCH, 1, SROWS).transpose(1, 2, 0, 3),
                                     state_mlstm_C.astype(F32), state_mlstm_n.astype(F32), m0_s, row0_first=SPAD,
                                     transposed=False)

    fc, fr = fox_cumsum(g2_p, g2t_p)
    pairs = F_HEADS // 2
    fr_p = fr[:, 8:16].reshape(BATCH, pairs, 2, NCH, CHUNK).transpose(0, 1, 3, 2, 4)
    fc_p = fc[:, :, 8:16].reshape(BATCH, TP, pairs, 2).transpose(0, 2, 1, 3)
    hf_p = fox_prompt(zq_p, zk_p, zv_p, fr_p, fc_p)
    suf, tot = lf_suffix(cache_fox_logf.astype(F32).transpose(0, 2, 1))
    hf_s = fox_decode(page_table, zq_s, zk_s, zv_s, g2t_s.reshape(16, DEC_BATCH, SROWS).transpose(1, 0, 2),
                      cache_fox_k.astype(F32).transpose(0, 2, 3, 1), cache_fox_v.astype(F32).transpose(0, 2, 3, 1),
                      suf, tot)

    gains = norm_g.astype(F32)
    even_w = (wo_even[:512], wo_even[512:], gains[0, 1:4], wg[0], wu[0], wd[0])
    h_p = mix_ffn(flat2(hm_p), hf_p, h_p, *even_w)
    h_s = mix_ffn(flat2(hm_s), hf_s, h_s, *even_w)

    zr_p, zl_p = norm_matmul(h_p, ng[1, 0], w_odd, (2048, 1024))
    zr_s, zl_s = norm_matmul(h_s, ng[1, 0], w_odd, (2048, 1024))
    cos_p, sin_p = _rope_tables(jnp.arange(TP) - PADF)
    cos_s, sin_s = _rope_tables(past + jnp.arange(SROWS) - SPAD)
    lng = ret_ln_g.astype(F32)
    hr_p, p_s = retention_call(prompt3(zr_p), cos_p, sin_p, lng,
                               jnp.zeros((BATCH, R_HEADS, HEAD_DIM, HEAD_DIM), F32), rows=CHUNK, row0_first=PADF)
    hr_s, s_s = retention_call(sample3(zr_s), cos_s, sin_s, lng, state_ret_S.astype(F32),
                               rows=SROWS, row0_first=SPAD)

    lru_args = (conv_w.astype(F32), conv_b.astype(F32).reshape(1, LRU_WIDTH),
                _block_diag_pairs(lru_wa), lru_ba.astype(F32).reshape(1, LRU_WIDTH),
                _block_diag_pairs(lru_wx), lru_bx.astype(F32).reshape(1, LRU_WIDTH),
                lru_lambda.astype(F32).reshape(1, LRU_WIDTH))
    hl_p, p_h, p_tail = lru_call(prompt3(zl_p), jnp.zeros((1, CHUNK, LRU_WIDTH), F32),
                                 jnp.zeros((BATCH, 1, LRU_WIDTH), F32), *lru_args, rows=CHUNK, row0_first=PADF)
    pre_s = jnp.concatenate([jnp.zeros((DEC_BATCH, SPAD - (CONV_W - 1), LRU_WIDTH), F32),
                             state_lru_conv.astype(F32),
                             jnp.zeros((DEC_BATCH, DEC_SEQ, LRU_WIDTH), F32)], axis=1)
    hl_s, s_h, s_tail = lru_call(sample3(zl_s), pre_s, state_lru_h.astype(F32).reshape(DEC_BATCH, 1, LRU_WIDTH),
                                 *lru_args, rows=SROWS, row0_first=SPAD)

    odd_w = (wo_odd[:512], wo_odd[512:], gains[1, 1:4], wg[1], wu[1], wd[1])
    h_p = mix_ffn(flat2(hr_p), flat2(hl_p), h_p, *odd_w)
    h_s = mix_ffn(flat2(hr_s), flat2(hl_s), h_s, *odd_w)

    y_prompt = h_p.reshape(BATCH, TP, D_MODEL)[:, CHUNK:]
    y_sample = h_s.reshape(DEC_BATCH, SROWS, D_MODEL)[:, SPAD:]
    heads = lambda x: x.reshape(x.shape[:2] + (F_HEADS, F_DH))
    nback = CONV_W - 1
    return (y_prompt, y_sample,
            heads(_unpad_prompt(zk_p)), heads(_unpad_prompt(zv_p)), _unpad_prompt(g2_p[:, 8:16]),
            heads(_unpad_sample(zk_s)), heads(_unpad_sample(zv_s)), _unpad_sample(g2_s[:, 8:16]),
            p_c, p_n, p_m[:, :, 0], s_c, s_n, s_m[:, :, 0],
            p_s, s_s,
            p_h[:, 0], s_h[:, 0], p_tail[:, 8 - nback:], s_tail[:, 8 - nback:])
```

```python
import functools
import math

import jax
import jax.numpy as jnp
import numpy as np
from jax import lax
from jax.experimental import pallas as pl
from jax.experimental.pallas import tpu as pltpu

F32 = jnp.float32
BF16 = jnp.bfloat16

D_MODEL = 1024
BATCH = 4
SEQ = 4096
DEC_BATCH = 32
DEC_SEQ = 4
PAGE_SIZE = 128
N_META = 16
CHUNK = 128
M_HEADS = 4
F_HEADS = 8
F_DH = 64
R_HEADS = 4
HEAD_DIM = 128
ROPE_BASE = 10000.0
LRU_WIDTH = 512
LRU_C = 8.0
CONV_W = 4
D_FF = 2816
EPS = 1e-6
NEG = -1e30

LANES = 128
PADF = CHUNK - N_META
TP = PADF + N_META + SEQ
NCH = TP // CHUNK
SROWS = 16
SPAD = SROWS - DEC_SEQ
SBASE = BATCH * TP
NP = SBASE + DEC_BATCH * SROWS
TM = 512
SEQ_GROUP = 4
MXU_WIDTH = 256
MIX_FFN_VMEM_BYTES = 56 * 1024 * 1024
FOX_TQ = 512
FOX_CK = 512
LOG2E = math.log2(math.e)
DEC_PG = 32
PRE_PG = 128


def _bf(x):
    return x.astype(BF16)


def _dot(a, b):
    return jnp.dot(a, b, preferred_element_type=F32)


def _dot_nt(a, b):
    return lax.dot_general(a, b, (((1,), (1,)), ((), ())), preferred_element_type=F32)


def _dot_tn(a, b):
    return lax.dot_general(a, b, (((0,), (0,)), ((), ())), preferred_element_type=F32)


def _split3(x):
    hi = _bf(x)
    r1 = x - hi.astype(F32)
    mid = _bf(r1)
    lo = _bf(r1 - mid.astype(F32))
    return hi, mid, lo


def _dot01_right(x, m01):
    hi, mid, lo = _split3(x)
    return _dot(hi, m01) + _dot(mid, m01) + _dot(lo, m01)


def _dot01_left(m01, x):
    hi, mid, lo = _split3(x)
    return _dot(m01, hi) + _dot(m01, mid) + _dot(m01, lo)


def _iota(shape, dim):
    return lax.broadcasted_iota(jnp.int32, shape, dim)


def _rms(x, g):
    ms = jnp.mean(x * x, axis=-1, keepdims=True)
    return x * lax.rsqrt(ms + EPS) * g


def _softplus(x):
    return jnp.maximum(x, 0.0) + jnp.log1p(jnp.exp(-jnp.abs(x)))


def _log_sigmoid(x):
    return -_softplus(-x)


def _sigmoid(x):
    return 0.5 * jnp.tanh(0.5 * x) + 0.5


def _nm_kernel(x_ref, g_ref, w_ref, *o_refs, splits):
    xn = _bf(_rms(x_ref[...], g_ref[...]))
    off = 0
    for o_ref, n in zip(o_refs, splits):
        o_ref[...] = _dot(xn, w_ref[:, off:off + n])
        off += n


def norm_matmul(x, g, w, splits):
    n_rows, k = x.shape
    const = lambda i: (0, 0)
    return pl.pallas_call(
        functools.partial(_nm_kernel, splits=splits),
        grid=(n_rows // TM,),
        in_specs=[pl.BlockSpec((TM, k), lambda i: (i, 0)),
                  pl.BlockSpec((1, k), const),
                  pl.BlockSpec((k, sum(splits)), const, pipeline_mode=pl.Buffered(1))],
        out_specs=[pl.BlockSpec((TM, n), lambda i: (i, 0)) for n in splits],
        out_shape=[jax.ShapeDtypeStruct((n_rows, n), F32) for n in splits],
        compiler_params=pltpu.CompilerParams(dimension_semantics=("parallel",)),
        name="norm_matmul",
    )(x, g, w)


def _mix_ffn_kernel(a1_ref, a2_ref, h_ref, w1_ref, w2_ref, g_ref, wg_ref, wu_ref, wd_ref, o_ref):
    mix = _dot(_bf(a1_ref[...]), w1_ref[...]) + _dot(_bf(a2_ref[...]), w2_ref[...])
    h1 = h_ref[...] + _rms(mix, g_ref[0:1, :])
    xn = _bf(_rms(h1, g_ref[1:2, :]))
    acts = []
    for c in range(wg_ref.shape[1] // MXU_WIDTH):
        cols = slice(c * MXU_WIDTH, (c + 1) * MXU_WIDTH)
        gate = _dot(xn, wg_ref[:, cols])
        up = _dot(xn, wu_ref[:, cols])
        acts.append(_bf(gate * _sigmoid(gate) * up))
    ff = _dot(jnp.concatenate(acts, axis=1), wd_ref[...])
    o_ref[...] = h1 + _rms(ff, g_ref[2:3, :])


def mix_ffn(a1, a2, h, w1, w2, g, wg, wu, wd):
    k1 = a1.shape[1]
    k2 = a2.shape[1]
    n_rows, d = h.shape
    dff = wg.shape[1]
    assert dff % MXU_WIDTH == 0 and n_rows % TM == 0
    row_blk = lambda i: (i, 0)
    resident = lambda shape: pl.BlockSpec(shape, lambda i: (0, 0), pipeline_mode=pl.Buffered(1))
    return pl.pallas_call(
        _mix_ffn_kernel,
        grid=(n_rows // TM,),
        in_specs=[pl.BlockSpec((TM, k1), row_blk),
                  pl.BlockSpec((TM, k2), row_blk),
                  pl.BlockSpec((TM, d), row_blk),
                  resident((k1, d)),
                  resident((k2, d)),
                  resident((3, d)),
                  resident((d, dff)),
                  resident((d, dff)),
                  resident((dff, d))],
        out_specs=pl.BlockSpec((TM, d), row_blk),
        out_shape=jax.ShapeDtypeStruct((n_rows, d), F32),
        compiler_params=pltpu.CompilerParams(dimension_semantics=("parallel",),
                                             vmem_limit_bytes=MIX_FFN_VMEM_BYTES),
        name="mix_ffn",
    )(a1, a2, h, w1, w2, g, wg, wu, wd)


def _gates_kernel(z_ref, zt_ref, brow_ref, bcol_ref, o_ref, ot_ref):
    x = z_ref[...] + brow_ref[...]
    o_ref[...] = jnp.where(_iota(x.shape, 1) < M_HEADS, x, _log_sigmoid(x))
    xt = zt_ref[...] + bcol_ref[...]
    ot_ref[...] = jnp.where(_iota(xt.shape, 0) < M_HEADS, xt, _log_sigmoid(xt))


def gates_activate(zg, zgt, brow, bcol):
    n_rows = zg.shape[0]
    return pl.pallas_call(
        _gates_kernel,
        grid=(n_rows // TM,),
        in_specs=[pl.BlockSpec((TM, LANES), lambda i: (i, 0)),
                  pl.BlockSpec((16, TM), lambda i: (0, i)),
                  pl.BlockSpec((1, LANES), lambda i: (0, 0)),
                  pl.BlockSpec((16, 1), lambda i: (0, 0))],
        out_specs=[pl.BlockSpec((TM, LANES), lambda i: (i, 0)),
                   pl.BlockSpec((16, TM), lambda i: (0, i))],
        out_shape=[jax.ShapeDtypeStruct((n_rows, LANES), F32),
                   jax.ShapeDtypeStruct((16, n_rows), F32)],
        compiler_params=pltpu.CompilerParams(dimension_semantics=("parallel",)),
        name="gates_activate",
    )(zg, zgt, brow, bcol)


def _mlstm_chunk(q, k, v, logi_c, lf_c, logi_r, lf_r, c_state, n_state, m_state, row0, rows):
    ri = _iota((rows, rows), 0)
    ci = _iota((rows, rows), 1)
    valid_c = _iota((rows, 1), 0) >= row0
    valid_r = _iota((1, rows), 1) >= row0
    lf_c = jnp.where(valid_c, lf_c, 0.0)
    lf_r = jnp.where(valid_r, lf_r, 0.0)
    logi_c = jnp.where(valid_c, logi_c, NEG)
    logi_r = jnp.where(valid_r, logi_r, NEG)
    causal = ci <= ri
    b_c = jnp.sum(jnp.where(causal, lf_r, 0.0), axis=1, keepdims=True)
    b_r = jnp.sum(jnp.where(ri <= ci, lf_c, 0.0), axis=0, keepdims=True)
    log_d = jnp.where(causal, b_c - b_r + logi_r, -jnp.inf)
    m_inter = b_c + m_state
    m_t = jnp.maximum(m_inter, jnp.max(log_d, axis=1, keepdims=True))
    qb = _bf(q)
    kb = _bf(k)
    vb = _bf(v)
    w_intra = _dot_nt(qb, kb) * jnp.exp(log_d - m_t)
    w_inter = jnp.exp(m_inter - m_t)
    num = _dot(_bf(w_intra), vb) + w_inter * _dot(qb, _bf(c_state))
    den = jnp.sum(w_intra, axis=1, keepdims=True) + w_inter * jnp.sum(q * n_state, axis=1, keepdims=True)
    h = num / jnp.maximum(jnp.abs(den), jnp.exp(-m_t))
    m_new = m_t[rows - 1:rows, :]
    b_last = b_c[rows - 1:rows, :]
    w_src = jnp.exp(b_last - b_c + logi_c - m_new)
    decay = jnp.exp(b_last + m_state - m_new)
    ks = k * w_src
    c_new = decay * c_state + _dot_tn(_bf(ks), vb)
    n_new = decay * n_state + jnp.sum(ks, axis=0, keepdims=True)
    return h, c_new, n_new, m_new


def _mlstm_chunk_t(q, k, v, logi_r, lf_r, ct_state, n_state, m_state, row0, rows):
    ri = _iota((rows, rows), 0)
    ci = _iota((rows, rows), 1)
    valid_r = _iota((1, rows), 1) >= row0
    lf_r = jnp.where(valid_r, lf_r, 0.0)
    logi_r = jnp.where(valid_r, logi_r, NEG)
    upper = jnp.where(ri <= ci, 1.0, 0.0).astype(BF16)
    a = jnp.where(ci > ri, lf_r, jnp.where(ci == ri, logi_r, 0.0))
    pre = _dot01_right(jnp.concatenate([a, jnp.broadcast_to(lf_r, (8, rows))], axis=0), upper)
    b_r = pre[rows:rows + 1, :]
    log_d = jnp.where(ri <= ci, pre[0:rows, :], -jnp.inf)
    m_inter = b_r + m_state
    m_t = jnp.maximum(m_inter, jnp.max(log_d, axis=0, keepdims=True))
    qb = _bf(q)
    kb = _bf(k)
    w = _dot_nt(kb, qb) * jnp.exp(log_d - m_t)
    w_inter = jnp.exp(m_inter - m_t)
    vt = v.T
    num_t = _dot(_bf(vt), _bf(w)) + w_inter * _dot_nt(_bf(ct_state), qb)
    nq = _dot_nt(_bf(jnp.broadcast_to(n_state, (8, HEAD_DIM))), qb)[0:1, :]
    den = jnp.sum(w, axis=0, keepdims=True) + w_inter * nq
    h_t = num_t / jnp.maximum(jnp.abs(den), jnp.exp(-m_t))
    m_new = m_t[:, rows - 1:rows]
    b_last = b_r[:, rows - 1:rows]
    w_src = jnp.exp(b_last - b_r + logi_r - m_new)
    decay = jnp.exp(b_last + m_state - m_new)
    ct_new = decay * ct_state + _dot(_bf(vt * w_src), kb)
    n_new = decay * n_state + _dot(_bf(jnp.broadcast_to(w_src, (8, rows))), kb)[0:1, :]
    return h_t.T, ct_new, n_new, m_new


def _mlstm_kernel(z_ref, g_ref, gt_ref, c0_ref, n0_ref, m0_ref, o_ref, c_ref, n_ref, m_ref, *,
                  rows, row0_first, transposed):
    c = pl.program_id(1)

    @pl.when(c == 0)
    def _():
        c_ref[...] = c0_ref[...]
        n_ref[...] = n0_ref[...]
        m_ref[...] = m0_ref[...]

    row0 = jnp.where(c == 0, row0_first, 0)
    for s in range(SEQ_GROUP):
        g = g_ref[s]
        gt = gt_ref[s, 0]
        outs = []
        for h in range(M_HEADS):
            lo = h * HEAD_DIM
            q = z_ref[s, :, lo:lo + HEAD_DIM]
            k = z_ref[s, :, 512 + lo:512 + lo + HEAD_DIM] * (HEAD_DIM ** -0.5)
            v = z_ref[s, :, 1024 + lo:1024 + lo + HEAD_DIM]
            og = z_ref[s, :, 1536 + lo:1536 + lo + HEAD_DIM]
            state = (c_ref[s, h], n_ref[s, h:h + 1, :], m_ref[s, h:h + 1, 0:1])
            if transposed:
                hh, c_new, n_new, m_new = _mlstm_chunk_t(
                    q, k, v, gt[h:h + 1, :], gt[M_HEADS + h:M_HEADS + h + 1, :], *state, row0, rows)
            else:
                hh, c_new, n_new, m_new = _mlstm_chunk(
                    q, k, v,
                    g[:, h:h + 1], g[:, M_HEADS + h:M_HEADS + h + 1],
                    gt[h:h + 1, :], gt[M_HEADS + h:M_HEADS + h + 1, :], *state, row0, rows)
            c_ref[s, h] = c_new
            n_ref[s, h:h + 1, :] = n_new
            m_ref[s, h:h + 1, :] = jnp.broadcast_to(m_new, (1, LANES))
            outs.append(_sigmoid(og) * hh)
        o_ref[s] = jnp.concatenate(outs, axis=1)


def _state_specs(heads):
    return [pl.BlockSpec((SEQ_GROUP, heads, HEAD_DIM, HEAD_DIM), lambda b, c: (b, 0, 0, 0)),
            pl.BlockSpec((SEQ_GROUP, heads, HEAD_DIM), lambda b, c: (b, 0, 0)),
            pl.BlockSpec((SEQ_GROUP, heads, LANES), lambda b, c: (b, 0, 0))]


def mlstm_call(z, g2, g2t, c0, n0, m0, *, row0_first, transposed):
    nb, t_len, _ = z.shape
    n_chunks, rows = g2t.shape[1], g2t.shape[3]
    assert nb % SEQ_GROUP == 0 and n_chunks * rows == t_len
    blk = lambda b, c: (b, c, 0)
    return pl.pallas_call(
        functools.partial(_mlstm_kernel, rows=rows, row0_first=row0_first, transposed=transposed),
        grid=(nb // SEQ_GROUP, n_chunks),
        in_specs=[pl.BlockSpec((SEQ_GROUP, rows, 2048), blk),
                  pl.BlockSpec((SEQ_GROUP, rows, LANES), blk),
                  pl.BlockSpec((SEQ_GROUP, 1, 16, rows), lambda b, c: (b, c, 0, 0))] + _state_specs(M_HEADS),
        out_specs=[pl.BlockSpec((SEQ_GROUP, rows, 512), blk)] + _state_specs(M_HEADS),
        out_shape=[jax.ShapeDtypeStruct((nb, t_len, 512), F32),
                   jax.ShapeDtypeStruct((nb, M_HEADS, HEAD_DIM, HEAD_DIM), F32),
                   jax.ShapeDtypeStruct((nb, M_HEADS, HEAD_DIM), F32),
                   jax.ShapeDtypeStruct((nb, M_HEADS, LANES), F32)],
        compiler_params=pltpu.CompilerParams(dimension_semantics=("parallel", "arbitrary")),
        name="mlstm",
    )(z, g2, g2t, c0, n0, m0)


def _ret_log_gamma(h):
    return math.log1p(-(2.0 ** (-5.0 - h)))


def _ret_body(z_ref, cos_ref, sin_ref, lng_ref, o_ref, s_ref, *, rows, row0_first):
    c = pl.program_id(1)
    row0 = jnp.where(c == 0, row0_first, 0)
    n_valid = (rows - row0).astype(F32)
    cosf = cos_ref[...]
    sinf = sin_ref[...]
    ri = _iota((rows, rows), 0)
    ci = _iota((rows, rows), 1)
    diff = (ri - ci).astype(F32)
    rowi = _iota((rows, 1), 0)
    te = (rowi - row0).astype(F32)
    valid = rowi >= row0
    for h in range(R_HEADS):
        lg = _ret_log_gamma(h)
        lo = h * HEAD_DIM
        decay_m = jnp.where(diff >= 0, jnp.exp(diff * lg), 0.0)
        w_inter = jnp.exp((te + 1.0) * lg)
        w_src = jnp.exp((n_valid - 1.0 - te) * lg)
        s_decay = jnp.exp(n_valid * lg)
        for s in range(SEQ_GROUP):
            q = z_ref[s, :, lo:lo + HEAD_DIM]
            k = z_ref[s, :, 512 + lo:512 + lo + HEAD_DIM]
            v = jnp.where(valid, z_ref[s, :, 1024 + lo:1024 + lo + HEAD_DIM], 0.0)
            rg = z_ref[s, :, 1536 + lo:1536 + lo + HEAD_DIM]
            q = q * cosf + pltpu.roll(q, HEAD_DIM // 2, 1) * sinf
            k = (k * cosf + pltpu.roll(k, HEAD_DIM // 2, 1) * sinf) * (HEAD_DIM ** -0.5)
            s_state = s_ref[s, h]
            qb = _bf(q)
            vb = _bf(v)
            intra = _dot(_bf(_dot_nt(qb, _bf(k)) * decay_m), vb)
            inter = _dot(qb, _bf(s_state)) * w_inter
            s_ref[s, h] = s_decay * s_state + _dot_tn(_bf(k * w_src), vb)
            hr = intra + inter
            mu = jnp.mean(hr, axis=-1, keepdims=True)
            var = jnp.mean(jnp.square(hr - mu), axis=-1, keepdims=True)
            hr = (hr - mu) * lax.rsqrt(var + EPS) * lng_ref[h:h + 1, :]
            o_ref[s, :, lo:lo + HEAD_DIM] = rg * _sigmoid(rg) * hr


def _lru_body(z_ref, pre_ref, cw_ref, cb_ref, wa_ref, ba_ref, wx_ref, bx_ref, lam_ref,
              o_ref, hlast_ref, tail_ref, cbuf_s, a_s, b_s, *, rows, row0_first):
    c = pl.program_id(1)
    row0 = jnp.where(c == 0, row0_first, 0)
    rowi = _iota((rows, 1), 0)
    valid = rowi >= row0
    sp_lam = _softplus(-lam_ref[...])
    for s in range(SEQ_GROUP):
        lx = jnp.where(valid, z_ref[s, :, 0:LRU_WIDTH], pre_ref[min(s, pre_ref.shape[0] - 1)])
        cbuf_s[s, 0:8, :] = tail_ref[s]
        cbuf_s[s, 8:8 + rows, :] = lx
        xc = cb_ref[...] + cw_ref[CONV_W - 1:CONV_W, :] * lx
        for j in range(1, CONV_W):
            xc = xc + cw_ref[CONV_W - 1 - j:CONV_W - j, :] * cbuf_s[s, 8 - j:8 - j + rows, :]
        tail_ref[s] = lx[rows - 8:rows]
        xcb = _bf(xc)
        pre_a = []
        pre_x = []
        for p in range(LRU_WIDTH // LANES):
            xs = xcb[:, p * LANES:(p + 1) * LANES]
            pre_a.append(_dot(xs, wa_ref[p]))
            pre_x.append(_dot(xs, wx_ref[p]))
        r = _sigmoid(jnp.concatenate(pre_a, axis=1) + ba_ref[...])
        ig = _sigmoid(jnp.concatenate(pre_x, axis=1) + bx_ref[...])
        log_a = -LRU_C * r * sp_lam
        a = jnp.where(valid, jnp.exp(log_a), 1.0)
        one_minus_a2 = -jnp.tanh(log_a) * (jnp.exp(2.0 * log_a) + 1.0)
        bx = jnp.where(valid, jnp.sqrt(one_minus_a2) * (ig * xc), 0.0)
        a_s[s] = a
        b_s[s] = bx
    carries = [hlast_ref[s] for s in range(SEQ_GROUP)]
    for t in range(rows):
        for s in range(SEQ_GROUP):
            carries[s] = a_s[s, t:t + 1, :] * carries[s] + b_s[s, t:t + 1, :]
            b_s[s, t:t + 1, :] = carries[s]
    for s in range(SEQ_GROUP):
        hlast_ref[s] = carries[s]
        gate = z_ref[s, :, LRU_WIDTH:2 * LRU_WIDTH]
        gl = 0.5 * gate * (1.0 + jnp.tanh(math.sqrt(2.0 / math.pi) * (gate + 0.044715 * gate * gate * gate)))
        o_ref[s] = b_s[s] * gl


def _ret_lru_kernel(zr_ref, cos_ref, sin_ref, lng_ref, s0_ref, zl_ref, pre_ref, h0_ref,
                    cw_ref, cb_ref, wa_ref, ba_ref, wx_ref, bx_ref, lam_ref,
                    or_ref, s_ref, ol_ref, hlast_ref, tail_ref, cbuf_s, a_s, b_s, *, rows, row0_first):
    @pl.when(pl.program_id(1) == 0)
    def _():
        s_ref[...] = s0_ref[...]
        hlast_ref[...] = h0_ref[...]
        tail_ref[...] = jnp.zeros(tail_ref.shape, F32)

    _ret_body(zr_ref, cos_ref, sin_ref, lng_ref, or_ref, s_ref, rows=rows, row0_first=row0_first)
    _lru_body(zl_ref, pre_ref, cw_ref, cb_ref, wa_ref, ba_ref, wx_ref, bx_ref, lam_ref,
              ol_ref, hlast_ref, tail_ref, cbuf_s, a_s, b_s, rows=rows, row0_first=row0_first)


def ret_lru_call(zr, cosf, sinf, lng, s0, zl, pre, h0, cw, cb, wa2, ba, wx2, bx, lam, *, rows, row0_first):
    nb, t_len, _ = zr.shape
    assert nb % SEQ_GROUP == 0 and t_len % rows == 0
    blk = lambda b, c: (b, c, 0)
    per_seq = lambda b, c: (b, 0, 0)
    const2 = lambda b, c: (0, 0)
    const3 = lambda b, c: (0, 0, 0)
    sspec = pl.BlockSpec((SEQ_GROUP, R_HEADS, HEAD_DIM, HEAD_DIM), lambda b, c: (b, 0, 0, 0))
    shared_pre = pre.shape[0] == 1
    in_specs = [pl.BlockSpec((SEQ_GROUP, rows, 2048), blk),
                pl.BlockSpec((rows, HEAD_DIM), lambda b, c: (c, 0)),
                pl.BlockSpec((rows, HEAD_DIM), lambda b, c: (c, 0)),
                pl.BlockSpec((R_HEADS, HEAD_DIM), const2),
                sspec,
                pl.BlockSpec((SEQ_GROUP, rows, 1024), blk),
                pl.BlockSpec((1 if shared_pre else SEQ_GROUP, rows, LRU_WIDTH), const3 if shared_pre else per_seq),
                pl.BlockSpec((SEQ_GROUP, 1, LRU_WIDTH), per_seq),
                pl.BlockSpec((CONV_W, LRU_WIDTH), const2),
                pl.BlockSpec((1, LRU_WIDTH), const2),
                pl.BlockSpec((LRU_WIDTH // LANES, LANES, LANES), const3),
                pl.BlockSpec((1, LRU_WIDTH), const2),
                pl.BlockSpec((LRU_WIDTH // LANES, LANES, LANES), const3),
                pl.BlockSpec((1, LRU_WIDTH), const2),
                pl.BlockSpec((1, LRU_WIDTH), const2)]
    return pl.pallas_call(
        functools.partial(_ret_lru_kernel, rows=rows, row0_first=row0_first),
        grid=(nb // SEQ_GROUP, t_len // rows),
        in_specs=in_specs,
        out_specs=[pl.BlockSpec((SEQ_GROUP, rows, 512), blk), sspec,
                   pl.BlockSpec((SEQ_GROUP, rows, LRU_WIDTH), blk),
                   pl.BlockSpec((SEQ_GROUP, 1, LRU_WIDTH), per_seq),
                   pl.BlockSpec((SEQ_GROUP, 8, LRU_WIDTH), per_seq)],
        out_shape=[jax.ShapeDtypeStruct((nb, t_len, 512), F32),
                   jax.ShapeDtypeStruct((nb, R_HEADS, HEAD_DIM, HEAD_DIM), F32),
                   jax.ShapeDtypeStruct((nb, t_len, LRU_WIDTH), F32),
                   jax.ShapeDtypeStruct((nb, 1, LRU_WIDTH), F32),
                   jax.ShapeDtypeStruct((nb, 8, LRU_WIDTH), F32)],
        scratch_shapes=[pltpu.VMEM((SEQ_GROUP, rows + 8, LRU_WIDTH), F32),
                        pltpu.VMEM((SEQ_GROUP, rows, LRU_WIDTH), F32),
                        pltpu.VMEM((SEQ_GROUP, rows, LRU_WIDTH), F32)],
        compiler_params=pltpu.CompilerParams(dimension_semantics=("parallel", "arbitrary")),
        name="ret_lru",
    )(zr, cosf, sinf, lng, s0, zl, pre, h0, cw, cb, wa2, ba, wx2, bx, lam)


def _fox_cumsum_kernel(g_ref, gt_ref, fx_ref, fr_ref):
    ri = _iota((CHUNK, CHUNK), 0)
    ci = _iota((CHUNK, CHUNK), 1)
    lower = jnp.where(ci <= ri, 1.0, 0.0).astype(BF16)
    upper = jnp.where(ri <= ci, 1.0, 0.0).astype(BF16)
    pr = _iota((3 * LANES, F_HEADS // 2 * LANES), 0)
    pc = _iota((3 * LANES, F_HEADS // 2 * LANES), 1)
    term, gate = pr // LANES, pr % LANES
    pair, lane = pc // LANES, pc % LANES
    first_head = 2 * M_HEADS
    place = (((gate == first_head + 2 * pair) & (lane == F_DH + term))
             | ((gate == first_head + 2 * pair + 1) & (lane == term)))
    place = jnp.where(place, 1.0, 0.0).astype(BF16)
    carry_r = jnp.zeros((1, LANES), F32)
    carry_c = jnp.zeros((16, 1), F32)
    for blk in range(NCH):
        x = g_ref[blk * CHUNK:(blk + 1) * CHUNK, :]
        xt = gt_ref[:, blk * CHUNK:(blk + 1) * CHUNK]
        if blk == 0:
            x = jnp.where(_iota((CHUNK, 1), 0) >= PADF, x, 0.0)
            xt = jnp.where(_iota((1, CHUNK), 1) >= PADF, xt, 0.0)
        cs = _dot01_left(lower, x) + carry_r
        cst = _dot01_right(xt, upper) + carry_c
        carry_r = cs[CHUNK - 1:CHUNK, :]
        carry_c = cst[:, CHUNK - 1:CHUNK]
        fk = cs * LOG2E
        if blk == 0:
            cst = jnp.where(_iota((1, CHUNK), 1) >= PADF, cst, -NEG)
            fk = jnp.where(_iota((CHUNK, 1), 0) >= PADF, fk, -NEG)
        fx_ref[blk * CHUNK:(blk + 1) * CHUNK, :] = _dot(jnp.concatenate(_split3(fk), axis=1), place)
        fr_ref[0, :, blk * CHUNK:(blk + 1) * CHUNK] = cst


def fox_cumsum(g2, g2t):
    return pl.pallas_call(
        _fox_cumsum_kernel,
        grid=(BATCH,),
        in_specs=[pl.BlockSpec((TP, LANES), lambda b: (b, 0)),
                  pl.BlockSpec((16, TP), lambda b: (0, b))],
        out_specs=[pl.BlockSpec((TP, F_HEADS // 2 * LANES), lambda b: (b, 0)),
                   pl.BlockSpec((1, 16, TP), lambda b: (b, 0, 0))],
        out_shape=[jax.ShapeDtypeStruct((SBASE, F_HEADS // 2 * LANES), F32),
                   jax.ShapeDtypeStruct((BATCH, 16, TP), F32)],
        compiler_params=pltpu.CompilerParams(dimension_semantics=("parallel",)),
        name="fox_cumsum",
    )(g2, g2t)


def _fox_prompt_kernel(q_ref, k_ref, v_ref, fr_ref, fx_ref, o_ref,
                       qta_s, qtb_s, ka_s, kb_s, vt_s, m_s, l_s, acc_s, ua_s):
    i = pl.program_id(2)
    lane = _iota((1, LANES), 1)
    is_a = lane < F_DH
    row = _iota((LANES, 1), 0)
    row_a = row < F_DH
    k_heads = (ka_s, kb_s)
    qt_heads = (qta_s, qtb_s)

    @pl.when(i == 0)
    def _():
        k = k_ref[...]
        fx = fx_ref[...]
        ka_s[...] = _bf(jnp.where(is_a, k, fx))
        kb_s[...] = _bf(jnp.where(is_a, fx, k))

        minus_a = jnp.where((row >= F_DH) & (row < F_DH + 3), -1.0, 0.0)
        minus_b = jnp.where(row < 3, -1.0, 0.0)

        def fill(blk, carry):
            rows = pl.ds(pl.multiple_of(blk * CHUNK, CHUNK), CHUNK)
            qt = (q_ref[rows, :] * (F_DH ** -0.5 * LOG2E)).T
            qta_s[blk] = _bf(jnp.where(row_a, qt, minus_a))
            qtb_s[blk] = _bf(jnp.where(row_a, minus_b, qt))
            vt_s[blk] = _bf(v_ref[rows, :].T)
            return carry

        lax.fori_loop(0, NCH, fill, 0)

    def init():
        m_s[...] = jnp.full(m_s.shape, NEG, F32)
        l_s[...] = jnp.zeros(l_s.shape, F32)
        acc_s[...] = jnp.zeros(acc_s.shape, F32)

    def scores(hh, qt, kblk, nkb):
        ck = nkb * CHUNK
        k0 = kblk * CHUNK if isinstance(kblk, int) else pl.multiple_of(kblk * CHUNK, CHUNK)
        return _dot(k_heads[hh][pl.ds(k0, ck), :], qt[hh])

    def softmax(hh, u, qblk, nqb, kblk, nkb, causal):
        tq = nqb * CHUNK
        ck = nkb * CHUNK
        fq = jnp.concatenate([fr_ref[0, 0, qblk + t, hh:hh + 1, :] for t in range(nqb)], axis=1)
        fq = jnp.where(fq > 0.5 * -NEG, 0.0, fq * LOG2E)
        if causal:
            u = jnp.where(kblk * CHUNK + _iota((ck, tq), 0) <= qblk * CHUNK + _iota((ck, tq), 1), u, NEG)
        m_old = m_s[hh, :, 0:tq]
        m_new = jnp.maximum(m_old, jnp.max(u, axis=0, keepdims=True) + fq)
        m_s[hh, :, 0:tq] = m_new
        return _bf(jnp.exp2(u + (fq - m_new))), jnp.exp2(m_old - m_new)

    def accumulate(hh, p, alpha, kblk, nkb):
        ck, tq = p.shape
        feat = slice(hh * F_DH, (hh + 1) * F_DH)
        vt = jnp.concatenate([vt_s[kblk + t, feat, :] for t in range(nkb)], axis=1)
        pv = _dot(jnp.concatenate([vt, jnp.ones((16, ck), BF16)], axis=0), p)
        l_s[hh, :, 0:tq] = alpha * l_s[hh, :, 0:tq] + pv[F_DH:F_DH + 1, :]
        acc_s[feat, 0:tq] = alpha * acc_s[feat, 0:tq] + pv[0:F_DH, :]

    def step(hh, u, qblk, nqb, kblk, nkb, causal):
        p, alpha = softmax(hh, u, qblk, nqb, kblk, nkb, causal)
        accumulate(hh, p, alpha, kblk, nkb)

    def finalize(qblk, nqb):
        tq = nqb * CHUNK
        o_t = acc_s[:, 0:tq] / jnp.where(row_a, l_s[0, :, 0:tq], l_s[1, :, 0:tq])
        for t in range(nqb):
            rows = pl.ds(pl.multiple_of((qblk + t) * CHUNK, CHUNK), CHUNK)
            o_ref[rows, :] = o_t[:, t * CHUNK:(t + 1) * CHUNK].T

    @pl.when(i == 0)
    def _():
        init()
        qt0 = (qta_s[0], qtb_s[0])
        for hh in range(2):
            step(hh, scores(hh, qt0, 0, 1), 0, 1, 0, 1, True)
        finalize(0, 1)

    nck = FOX_CK // CHUNK
    nqb = FOX_TQ // CHUNK
    qblk = 1 + i * nqb
    qt = tuple(jnp.concatenate([qt_h[qblk + t] for t in range(nqb)], axis=1) for qt_h in qt_heads)
    init()
    for hh in range(2):
        step(hh, scores(hh, qt, 0, 1), qblk, nqb, 0, 1, False)

    ua_s[...] = scores(0, qt, 1, nck)

    def body(j, carry):
        kblk = 1 + j * nck
        ub = scores(1, qt, kblk, nck)
        step(0, ua_s[...], qblk, nqb, kblk, nck, False)
        ua_s[...] = scores(0, qt, kblk + nck, nck)
        step(1, ub, qblk, nqb, kblk, nck, False)
        return carry

    lax.fori_loop(0, i, body, 0)
    ub = scores(1, qt, qblk, nck)
    step(0, ua_s[...], qblk, nqb, qblk, nck, True)
    step(1, ub, qblk, nqb, qblk, nck, True)
    finalize(qblk, nqb)


def fox_prompt(zq, zk, zv, fr, fx):
    nq = (TP - CHUNK) // FOX_TQ
    pairs = F_HEADS // 2
    return pl.pallas_call(
        _fox_prompt_kernel,
        grid=(BATCH, pairs, nq),
        in_specs=[pl.BlockSpec((TP, LANES), lambda b, p, i: (b, p)),
                  pl.BlockSpec((TP, LANES), lambda b, p, i: (b, p)),
                  pl.BlockSpec((TP, LANES), lambda b, p, i: (b, p)),
                  pl.BlockSpec((1, 1, NCH, 2, LANES), lambda b, p, i: (b, p, 0, 0, 0)),
                  pl.BlockSpec((TP, LANES), lambda b, p, i: (b, p))],
        out_specs=pl.BlockSpec((TP, LANES), lambda b, p, i: (b, p)),
        out_shape=jax.ShapeDtypeStruct((SBASE, 512), F32),
        scratch_shapes=[pltpu.VMEM((NCH, LANES, CHUNK), BF16),
                        pltpu.VMEM((NCH, LANES, CHUNK), BF16),
                        pltpu.VMEM((TP, LANES), BF16),
                        pltpu.VMEM((TP, LANES), BF16),
                        pltpu.VMEM((NCH, LANES, CHUNK), BF16),
                        pltpu.VMEM((2, 1, FOX_TQ), F32),
                        pltpu.VMEM((2, 1, FOX_TQ), F32),
                        pltpu.VMEM((LANES, FOX_TQ), F32),
                        pltpu.VMEM((FOX_CK, FOX_TQ), F32)],
        compiler_params=pltpu.CompilerParams(dimension_semantics=("parallel", "parallel", "arbitrary")),
        name="fox_prompt",
    )(zq, zk, zv, fr, fx)


def _lf_suffix_kernel(x_ref, suf_ref, tot_ref):
    ri = _iota((PAGE_SIZE, PAGE_SIZE), 0)
    ci = _iota((PAGE_SIZE, PAGE_SIZE), 1)
    after = jnp.where(ri > ci, 1.0, 0.0).astype(BF16)
    ones = jnp.ones((PAGE_SIZE, PAGE_SIZE), BF16)
    x = x_ref[...].reshape(PRE_PG * F_HEADS, PAGE_SIZE)
    suf_ref[...] = _dot01_right(x, after).reshape(PRE_PG, F_HEADS, PAGE_SIZE)
    tot_ref[...] = _dot01_right(x, ones).reshape(PRE_PG, F_HEADS, PAGE_SIZE)


def lf_suffix(lft):
    n_phys = lft.shape[0]
    spec = pl.BlockSpec((PRE_PG, F_HEADS, PAGE_SIZE), lambda i: (i, 0, 0))
    return pl.pallas_call(
        _lf_suffix_kernel,
        grid=(n_phys // PRE_PG,),
        in_specs=[spec],
        out_specs=[spec, spec],
        out_shape=[jax.ShapeDtypeStruct(lft.shape, F32)] * 2,
        compiler_params=pltpu.CompilerParams(dimension_semantics=("parallel",)),
        name="lf_suffix",
    )(lft)


def _fox_decode_kernel(pt_ref, zq_ref, zk_ref, zv_ref, gt_ref, *rest):
    k_refs = rest[0:DEC_PG]
    v_refs = rest[DEC_PG:2 * DEC_PG]
    suf_refs = rest[2 * DEC_PG:3 * DEC_PG]
    tot_refs = rest[3 * DEC_PG:4 * DEC_PG]
    o_ref = rest[4 * DEC_PG]
    q2_s, m_s, l_s, acc_s, run_s, nc_s = rest[4 * DEC_PG + 1:]
    j = pl.program_id(1)
    n_rows = DEC_SEQ * F_HEADS
    width = F_HEADS * F_DH

    def update(u, v, v_feature_major):
        m_old = m_s[...]
        m_new = jnp.maximum(m_old, jnp.max(u, axis=1, keepdims=True))
        alpha = jnp.exp(m_old - m_new)
        p = jnp.exp(u - m_new[:, 0:u.shape[1]])
        pv = _dot_nt(_bf(p), v) if v_feature_major else _dot(_bf(p), v)
        l_s[...] = alpha * l_s[...] + jnp.sum(p, axis=1, keepdims=True)
        acc_s[...] = jnp.concatenate([alpha] * (width // LANES), axis=1) * acc_s[...] + pv
        m_s[...] = m_new

    @pl.when(j == 0)
    def _():
        m_s[...] = jnp.full(m_s.shape, NEG, F32)
        l_s[...] = jnp.zeros(l_s.shape, F32)
        acc_s[...] = jnp.zeros(acc_s.shape, F32)
        run_s[...] = jnp.zeros(run_s.shape, F32)
        head_mask = _iota((F_HEADS, width), 1) // F_DH == _iota((F_HEADS, width), 0)
        q16 = zq_ref[...] * (F_DH ** -0.5)
        q2 = _bf(jnp.concatenate(
            [jnp.where(head_mask, q16[SPAD + t:SPAD + t + 1, :], 0.0) for t in range(DEC_SEQ)], axis=0))
        q2_s[...] = q2
        lf_new = gt_ref[F_HEADS:2 * F_HEADS, :]
        lane = _iota((1, SROWS), 1)
        cum = jnp.zeros((F_HEADS, SROWS), F32)
        for t in range(DEC_SEQ):
            cum = cum + jnp.where(lane >= SPAD + t, lf_new[:, SPAD + t:SPAD + t + 1], 0.0)
        nc = jnp.concatenate([cum[:, SPAD + t:SPAD + t + 1] for t in range(DEC_SEQ)], axis=0)
        nc_s[...] = jnp.broadcast_to(nc, nc_s.shape)
        cum4 = jnp.concatenate([cum] * DEC_SEQ, axis=0)
        qi = _iota((n_rows, SROWS), 0) // F_HEADS
        kj = _iota((n_rows, SROWS), 1) - SPAD
        u = jnp.where((kj >= 0) & (kj <= qi), _dot_nt(q2, _bf(zk_ref[...])) + (nc - cum4), NEG)
        update(u, _bf(zv_ref[...]), False)

    q2 = q2_s[...]
    nc = nc_s[...]
    run = run_s[...]
    us = []
    for i in range(DEC_PG):
        later = run + suf_refs[i][0]
        bias = jnp.concatenate([later] * DEC_SEQ, axis=0) + nc
        us.append(_dot(q2, _bf(k_refs[i][0].reshape(width, PAGE_SIZE))) + bias)
        run = run + tot_refs[i][0]
    run_s[...] = run
    u = jnp.concatenate(us, axis=1)
    m_old = m_s[...]
    m_new = jnp.maximum(m_old, jnp.max(u, axis=1, keepdims=True))
    alpha = jnp.exp(m_old - m_new)
    p = jnp.exp(u - jnp.concatenate([m_new] * DEC_PG, axis=1))
    l_s[...] = alpha * l_s[...] + jnp.sum(p, axis=1, keepdims=True)
    pv = _dot_nt(_bf(p[:, 0:PAGE_SIZE]), _bf(v_refs[0][0].reshape(width, PAGE_SIZE)))
    for i in range(1, DEC_PG):
        pv = pv + _dot_nt(_bf(p[:, i * PAGE_SIZE:(i + 1) * PAGE_SIZE]), _bf(v_refs[i][0].reshape(width, PAGE_SIZE)))
    acc_s[...] = jnp.concatenate([alpha] * (width // LANES), axis=1) * acc_s[...] + pv
    m_s[...] = m_new

    @pl.when(j == pl.num_programs(1) - 1)
    def _():
        head_mask = (_iota((n_rows, width), 1) // F_DH) == (_iota((n_rows, width), 0) % F_HEADS)
        o2 = jnp.where(head_mask, acc_s[...] / jnp.concatenate([l_s[...]] * (width // LANES), axis=1), 0.0)
        o_ref[...] = jnp.zeros(o_ref.shape, F32)
        for t in range(DEC_SEQ):
            o_ref[SPAD + t:SPAD + t + 1, :] = jnp.sum(o2[t * F_HEADS:(t + 1) * F_HEADS, :], axis=0, keepdims=True)


def fox_decode(page_table, zq, zk, zv, g2t_s, cache_kt, cache_vt, suf, tot):
    n_pages = page_table.shape[1]
    width = F_HEADS * F_DH

    def page_map(i):
        return lambda b, j, pt: (pt[b, n_pages - 1 - (j * DEC_PG + i)], 0, 0, 0)

    def page_map3(i):
        return lambda b, j, pt: (pt[b, n_pages - 1 - (j * DEC_PG + i)], 0, 0)

    in_specs = [pl.BlockSpec((SROWS, width), lambda b, j, pt: (b, 0)),
                pl.BlockSpec((SROWS, width), lambda b, j, pt: (b, 0)),
                pl.BlockSpec((SROWS, width), lambda b, j, pt: (b, 0)),
                pl.BlockSpec((None, 16, SROWS), lambda b, j, pt: (b, 0, 0))]
    in_specs += [pl.BlockSpec((1, F_HEADS, F_DH, PAGE_SIZE), page_map(i)) for i in range(DEC_PG)]
    in_specs += [pl.BlockSpec((1, F_HEADS, F_DH, PAGE_SIZE), page_map(i)) for i in range(DEC_PG)]
    in_specs += [pl.BlockSpec((1, F_HEADS, PAGE_SIZE), page_map3(i)) for i in range(DEC_PG)]
    in_specs += [pl.BlockSpec((1, F_HEADS, PAGE_SIZE), page_map3(i)) for i in range(DEC_PG)]
    args = [zq, zk, zv, g2t_s] + [cache_kt] * DEC_PG + [cache_vt] * DEC_PG + [suf] * DEC_PG + [tot] * DEC_PG
    n_rows = DEC_SEQ * F_HEADS
    grid_spec = pltpu.PrefetchScalarGridSpec(
        num_scalar_prefetch=1,
        grid=(DEC_BATCH, n_pages // DEC_PG),
        in_specs=in_specs,
        out_specs=pl.BlockSpec((SROWS, width), lambda b, j, pt: (b, 0)),
        scratch_shapes=[pltpu.VMEM((n_rows, width), BF16),
                        pltpu.VMEM((n_rows, LANES), F32),
                        pltpu.VMEM((n_rows, LANES), F32),
                        pltpu.VMEM((n_rows, width), F32),
                        pltpu.VMEM((F_HEADS, PAGE_SIZE), F32),
                        pltpu.VMEM((n_rows, LANES), F32)])
    return pl.pallas_call(
        _fox_decode_kernel,
        grid_spec=grid_spec,
        out_shape=jax.ShapeDtypeStruct((DEC_BATCH * SROWS, width), F32),
        compiler_params=pltpu.CompilerParams(dimension_semantics=("parallel", "arbitrary")),
        name="fox_decode",
    )(page_table, *args)


def _block_diag_pairs(w):
    z = jnp.zeros((LANES // 2, LANES // 2), w.dtype)
    pairs = [jnp.block([[w[2 * p], z], [z, w[2 * p + 1]]]) for p in range(w.shape[0] // 2)]
    return _bf(jnp.stack(pairs))


def _rope_tables(pos):
    half = HEAD_DIM // 2
    freq = ROPE_BASE ** (-jnp.arange(half, dtype=F32) / half)
    ang = pos.astype(F32)[:, None] * freq[None, :]
    cos, sin = jnp.cos(ang), jnp.sin(ang)
    return jnp.concatenate([cos, cos], axis=1), jnp.concatenate([-sin, sin], axis=1)


def _unpad_prompt(x):
    return x.reshape((BATCH, TP) + x.shape[1:])[:, PADF:]


def _unpad_sample(x):
    return x.reshape((DEC_BATCH, SROWS) + x.shape[1:])[:, SPAD:]


def kernel(x_prompt, x_sample, cache_fox_k, cache_fox_v, cache_fox_logf, state_mlstm_C, state_mlstm_n, state_mlstm_m, state_ret_S, state_lru_h, state_lru_conv, page_table, meta_tokens, w_in_even, b_mlstm_i, b_mlstm_f, b_fox_f, w_out_even, w_in_odd, ret_ln_g, conv_w, conv_b, lru_wa, lru_ba, lru_wx, lru_bx, lru_lambda, w_out_odd, norm_g, ffn_wg, ffn_wu, ffn_wd):
    n_pages = page_table.shape[1]
    past = n_pages * PAGE_SIZE
    n_phys = cache_fox_k.shape[0]

    head = jnp.concatenate([jnp.zeros((PADF, D_MODEL), F32), meta_tokens.astype(F32)], axis=0)
    hs = jnp.concatenate([jnp.zeros((DEC_BATCH, SPAD, D_MODEL), F32), x_sample.astype(F32)],
                         axis=1).reshape(DEC_BATCH * SROWS, D_MODEL)
    h_s = hs
    h_p = jnp.concatenate([jnp.broadcast_to(head[None], (BATCH, CHUNK, D_MODEL)), x_prompt.astype(F32)],
                          axis=1).reshape(SBASE, D_MODEL)

    c = np.cumsum([0, 512, 512, 512, 512, 4, 4, 512, 512, 512, 8])
    gate_w = jnp.concatenate([w_in_even[:, c[4]:c[6]], w_in_even[:, c[9]:c[10]],
                              jnp.zeros((D_MODEL, LANES - 16), F32)], axis=1)
    w_even = _bf(jnp.concatenate([w_in_even[:, c[0]:c[4]], w_in_even[:, c[6]:c[9]], gate_w], axis=1))
    gate_b = jnp.concatenate([b_mlstm_i, b_mlstm_f, b_fox_f]).astype(F32)
    brow = jnp.concatenate([gate_b, jnp.zeros((LANES - 16,), F32)]).reshape(1, LANES)
    bcol = gate_b.reshape(16, 1)
    w_odd = _bf(w_in_odd)
    wo_even = _bf(w_out_even)
    wo_odd = _bf(w_out_odd)
    wg = _bf(ffn_wg)
    wu = _bf(ffn_wu)
    wd = _bf(ffn_wd)
    ng = norm_g.astype(F32).reshape(2, 4, 1, D_MODEL)

    even_cols = (2048, 512, 512, 512, LANES)
    zm_p, zq_p, zk_p, zv_p, zg_p = norm_matmul(h_p, ng[0, 0], w_even, even_cols)
    zm_s, zq_s, zk_s, zv_s, zg_s = norm_matmul(h_s, ng[0, 0], w_even, even_cols)
    g2_p, g2t_p = gates_activate(zg_p, zg_p[:, :16].T, brow, bcol)
    g2_s, g2t_s = gates_activate(zg_s, zg_s[:, :16].T, brow, bcol)
    prompt3 = lambda x: x.reshape(BATCH, TP, x.shape[-1])
    sample3 = lambda x: x.reshape(DEC_BATCH, SROWS, x.shape[-1])
    flat2 = lambda x: x.reshape(-1, x.shape[-1])

    zeros_c = jnp.zeros((BATCH, M_HEADS, HEAD_DIM, HEAD_DIM), F32)
    zeros_n = jnp.zeros((BATCH, M_HEADS, HEAD_DIM), F32)
    hm_p, p_c, p_n, p_m = mlstm_call(prompt3(zm_p), prompt3(g2_p),
                                     g2t_p.reshape(16, BATCH, NCH, CHUNK).transpose(1, 2, 0, 3),
                                     zeros_c, zeros_n, zeros_n, row0_first=PADF, transposed=True)
    p_c = p_c.transpose(0, 1, 3, 2)
    m0_s = jnp.broadcast_to(state_mlstm_m.astype(F32)[:, :, None], (DEC_BATCH, M_HEADS, LANES))
    hm_s, s_c, s_n, s_m = mlstm_call(sample3(zm_s), sample3(g2_s),
                                     g2t_s.reshape(16, DEC_BATCH, 1, SROWS).transpose(1, 2, 0, 3),
                                     state_mlstm_C.astype(F32), state_mlstm_n.astype(F32), m0_s, row0_first=SPAD,
                                     transposed=False)

    fx, fr = fox_cumsum(g2_p, g2t_p)
    pairs = F_HEADS // 2
    fr_p = fr[:, 8:16].reshape(BATCH, pairs, 2, NCH, CHUNK).transpose(0, 1, 3, 2, 4)
    hf_p = fox_prompt(zq_p, zk_p, zv_p, fr_p, fx)
    suf, tot = lf_suffix(cache_fox_logf.astype(F32).transpose(0, 2, 1))
    hf_s = fox_decode(page_table, zq_s, zk_s, zv_s, g2t_s.reshape(16, DEC_BATCH, SROWS).transpose(1, 0, 2),
                      cache_fox_k.astype(F32).transpose(0, 2, 3, 1), cache_fox_v.astype(F32).transpose(0, 2, 3, 1),
                      suf, tot)

    gains = norm_g.astype(F32)
    even_w = (wo_even[:512], wo_even[512:], gains[0, 1:4], wg[0], wu[0], wd[0])
    h_p = mix_ffn(flat2(hm_p), hf_p, h_p, *even_w)
    h_s = mix_ffn(flat2(hm_s), hf_s, h_s, *even_w)

    zr_p, zl_p = norm_matmul(h_p, ng[1, 0], w_odd, (2048, 1024))
    zr_s, zl_s = norm_matmul(h_s, ng[1, 0], w_odd, (2048, 1024))
    cos_p, sin_p = _rope_tables(jnp.arange(TP) - PADF)
    cos_s, sin_s = _rope_tables(past + jnp.arange(SROWS) - SPAD)
    lng = ret_ln_g.astype(F32)
    lru_args = (conv_w.astype(F32), conv_b.astype(F32).reshape(1, LRU_WIDTH),
                _block_diag_pairs(lru_wa), lru_ba.astype(F32).reshape(1, LRU_WIDTH),
                _block_diag_pairs(lru_wx), lru_bx.astype(F32).reshape(1, LRU_WIDTH),
                lru_lambda.astype(F32).reshape(1, LRU_WIDTH))
    hr_p, p_s, hl_p, p_h, p_tail = ret_lru_call(
        prompt3(zr_p), cos_p, sin_p, lng, jnp.zeros((BATCH, R_HEADS, HEAD_DIM, HEAD_DIM), F32),
        prompt3(zl_p), jnp.zeros((1, CHUNK, LRU_WIDTH), F32), jnp.zeros((BATCH, 1, LRU_WIDTH), F32), *lru_args,
        rows=CHUNK, row0_first=PADF)
    pre_s = jnp.concatenate([jnp.zeros((DEC_BATCH, SPAD - (CONV_W - 1), LRU_WIDTH), F32),
                             state_lru_conv.astype(F32),
                             jnp.zeros((DEC_BATCH, DEC_SEQ, LRU_WIDTH), F32)], axis=1)
    hr_s, s_s, hl_s, s_h, s_tail = ret_lru_call(
        sample3(zr_s), cos_s, sin_s, lng, state_ret_S.astype(F32),
        sample3(zl_s), pre_s, state_lru_h.astype(F32).reshape(DEC_BATCH, 1, LRU_WIDTH), *lru_args,
        rows=SROWS, row0_first=SPAD)

    odd_w = (wo_odd[:512], wo_odd[512:], gains[1, 1:4], wg[1], wu[1], wd[1])
    h_p = mix_ffn(flat2(hr_p), flat2(hl_p), h_p, *odd_w)
    h_s = mix_ffn(flat2(hr_s), flat2(hl_s), h_s, *odd_w)

    y_prompt = h_p.reshape(BATCH, TP, D_MODEL)[:, CHUNK:]
    y_sample = h_s.reshape(DEC_BATCH, SROWS, D_MODEL)[:, SPAD:]
    heads = lambda x: x.reshape(x.shape[:2] + (F_HEADS, F_DH))
    nback = CONV_W - 1
    return (y_prompt, y_sample,
            heads(_unpad_prompt(zk_p)), heads(_unpad_prompt(zv_p)), _unpad_prompt(g2_p[:, 8:16]),
            heads(_unpad_sample(zk_s)), heads(_unpad_sample(zv_s)), _unpad_sample(g2_s[:, 8:16]),
            p_c, p_n, p_m[:, :, 0], s_c, s_n, s_m[:, :, 0],
            p_s, s_s,
            p_h[:, 0], s_h[:, 0], p_tail[:, 8 - nback:], s_tail[:, 8 - nback:])
```

```python
import functools
import math

import jax
import jax.numpy as jnp
import numpy as np
from jax import lax
from jax.experimental import pallas as pl
from jax.experimental.pallas import tpu as pltpu

F32 = jnp.float32
BF16 = jnp.bfloat16

D_MODEL = 1024
BATCH = 4
SEQ = 4096
DEC_BATCH = 32
DEC_SEQ = 4
PAGE_SIZE = 128
N_META = 16
CHUNK = 128
M_HEADS = 4
F_HEADS = 8
F_DH = 64
R_HEADS = 4
HEAD_DIM = 128
ROPE_BASE = 10000.0
LRU_WIDTH = 512
LRU_C = 8.0
CONV_W = 4
D_FF = 2816
EPS = 1e-6
NEG = -1e30

LANES = 128
PADF = CHUNK - N_META
TP = PADF + N_META + SEQ
NCH = TP // CHUNK
SROWS = 16
SPAD = SROWS - DEC_SEQ
SBASE = BATCH * TP
NP = SBASE + DEC_BATCH * SROWS
TM = 512
SEQ_GROUP = 4
MXU_WIDTH = 256
MIX_FFN_VMEM_BYTES = 56 * 1024 * 1024
FOX_TQ = 512
FOX_CK = 512
LOG2E = math.log2(math.e)
DEC_PG = 32
PRE_PG = 128


def _bf(x):
    return x.astype(BF16)


def _dot(a, b):
    return jnp.dot(a, b, preferred_element_type=F32)


def _dot_nt(a, b):
    return lax.dot_general(a, b, (((1,), (1,)), ((), ())), preferred_element_type=F32)


def _dot_tn(a, b):
    return lax.dot_general(a, b, (((0,), (0,)), ((), ())), preferred_element_type=F32)


def _split3(x):
    hi = _bf(x)
    r1 = x - hi.astype(F32)
    mid = _bf(r1)
    lo = _bf(r1 - mid.astype(F32))
    return hi, mid, lo


def _dot01_right(x, m01):
    hi, mid, lo = _split3(x)
    return _dot(hi, m01) + _dot(mid, m01) + _dot(lo, m01)


def _dot01_left(m01, x):
    hi, mid, lo = _split3(x)
    return _dot(m01, hi) + _dot(m01, mid) + _dot(m01, lo)


def _iota(shape, dim):
    return lax.broadcasted_iota(jnp.int32, shape, dim)


def _rms(x, g):
    ms = jnp.mean(x * x, axis=-1, keepdims=True)
    return x * lax.rsqrt(ms + EPS) * g


def _softplus(x):
    return jnp.maximum(x, 0.0) + jnp.log1p(jnp.exp(-jnp.abs(x)))


def _log_sigmoid(x):
    return -_softplus(-x)


def _sigmoid(x):
    return 0.5 * jnp.tanh(0.5 * x) + 0.5


def _nm_kernel(x_ref, g_ref, w_ref, *rest, splits, gated):
    if gated:
        b_ref, o_refs = rest[0], rest[1:]
    else:
        o_refs = rest
    xn = _bf(_rms(x_ref[...], g_ref[...]))
    off = 0
    for k, n in enumerate(splits):
        res = _dot(xn, w_ref[:, off:off + n])
        off += n
        if gated and k == len(splits) - 1:
            res = res + b_ref[...]
            res = jnp.where(_iota(res.shape, 1) < M_HEADS, res, _log_sigmoid(res))
            o_refs[k + 1][...] = res.T[0:16, :]
        o_refs[k][...] = res


def norm_matmul(x, g, w, splits, gate_bias=None):
    n_rows, k = x.shape
    const = lambda i: (0, 0)
    gated = gate_bias is not None
    in_specs = [pl.BlockSpec((TM, k), lambda i: (i, 0)),
                pl.BlockSpec((1, k), const),
                pl.BlockSpec((k, sum(splits)), const, pipeline_mode=pl.Buffered(1))]
    out_specs = [pl.BlockSpec((TM, n), lambda i: (i, 0)) for n in splits]
    out_shape = [jax.ShapeDtypeStruct((n_rows, n), F32) for n in splits]
    args = [x, g, w]
    if gated:
        assert splits[-1] == LANES
        in_specs.append(pl.BlockSpec((1, LANES), const))
        out_specs.append(pl.BlockSpec((16, TM), lambda i: (0, i)))
        out_shape.append(jax.ShapeDtypeStruct((16, n_rows), F32))
        args.append(gate_bias)
    return pl.pallas_call(
        functools.partial(_nm_kernel, splits=splits, gated=gated),
        grid=(n_rows // TM,),
        in_specs=in_specs,
        out_specs=out_specs,
        out_shape=out_shape,
        compiler_params=pltpu.CompilerParams(dimension_semantics=("parallel",)),
        name="norm_matmul",
    )(*args)


def _mix_ffn_kernel(a1_ref, a2_ref, h_ref, w1_ref, w2_ref, g_ref, wg_ref, wu_ref, wd_ref, o_ref):
    mix = _dot(_bf(a1_ref[...]), w1_ref[...]) + _dot(_bf(a2_ref[...]), w2_ref[...])
    h1 = h_ref[...] + _rms(mix, g_ref[0:1, :])
    xn = _bf(_rms(h1, g_ref[1:2, :]))
    acts = []
    for c in range(wg_ref.shape[1] // MXU_WIDTH):
        cols = slice(c * MXU_WIDTH, (c + 1) * MXU_WIDTH)
        gate = _dot(xn, wg_ref[:, cols])
        up = _dot(xn, wu_ref[:, cols])
        acts.append(_bf(gate * _sigmoid(gate) * up))
    ff = _dot(jnp.concatenate(acts, axis=1), wd_ref[...])
    o_ref[...] = h1 + _rms(ff, g_ref[2:3, :])


def mix_ffn(a1, a2, h, w1, w2, g, wg, wu, wd):
    k1 = a1.shape[1]
    k2 = a2.shape[1]
    n_rows, d = h.shape
    dff = wg.shape[1]
    assert dff % MXU_WIDTH == 0 and n_rows % TM == 0
    row_blk = lambda i: (i, 0)
    resident = lambda shape: pl.BlockSpec(shape, lambda i: (0, 0), pipeline_mode=pl.Buffered(1))
    return pl.pallas_call(
        _mix_ffn_kernel,
        grid=(n_rows // TM,),
        in_specs=[pl.BlockSpec((TM, k1), row_blk),
                  pl.BlockSpec((TM, k2), row_blk),
                  pl.BlockSpec((TM, d), row_blk),
                  resident((k1, d)),
                  resident((k2, d)),
                  resident((3, d)),
                  resident((d, dff)),
                  resident((d, dff)),
                  resident((dff, d))],
        out_specs=pl.BlockSpec((TM, d), row_blk),
        out_shape=jax.ShapeDtypeStruct((n_rows, d), F32),
        compiler_params=pltpu.CompilerParams(dimension_semantics=("parallel",),
                                             vmem_limit_bytes=MIX_FFN_VMEM_BYTES),
        name="mix_ffn",
    )(a1, a2, h, w1, w2, g, wg, wu, wd)


def _mlstm_chunk(q, k, v, logi_c, lf_c, logi_r, lf_r, c_state, n_state, m_state, row0, rows):
    ri = _iota((rows, rows), 0)
    ci = _iota((rows, rows), 1)
    valid_c = _iota((rows, 1), 0) >= row0
    valid_r = _iota((1, rows), 1) >= row0
    lf_c = jnp.where(valid_c, lf_c, 0.0)
    lf_r = jnp.where(valid_r, lf_r, 0.0)
    logi_c = jnp.where(valid_c, logi_c, NEG)
    logi_r = jnp.where(valid_r, logi_r, NEG)
    causal = ci <= ri
    b_c = jnp.sum(jnp.where(causal, lf_r, 0.0), axis=1, keepdims=True)
    b_r = jnp.sum(jnp.where(ri <= ci, lf_c, 0.0), axis=0, keepdims=True)
    log_d = jnp.where(causal, b_c - b_r + logi_r, -jnp.inf)
    m_inter = b_c + m_state
    m_t = jnp.maximum(m_inter, jnp.max(log_d, axis=1, keepdims=True))
    qb = _bf(q)
    kb = _bf(k)
    vb = _bf(v)
    w_intra = _dot_nt(qb, kb) * jnp.exp(log_d - m_t)
    w_inter = jnp.exp(m_inter - m_t)
    num = _dot(_bf(w_intra), vb) + w_inter * _dot(qb, _bf(c_state))
    den = jnp.sum(w_intra, axis=1, keepdims=True) + w_inter * jnp.sum(q * n_state, axis=1, keepdims=True)
    h = num / jnp.maximum(jnp.abs(den), jnp.exp(-m_t))
    m_new = m_t[rows - 1:rows, :]
    b_last = b_c[rows - 1:rows, :]
    w_src = jnp.exp(b_last - b_c + logi_c - m_new)
    decay = jnp.exp(b_last + m_state - m_new)
    ks = k * w_src
    c_new = decay * c_state + _dot_tn(_bf(ks), vb)
    n_new = decay * n_state + jnp.sum(ks, axis=0, keepdims=True)
    return h, c_new, n_new, m_new


def _mlstm_chunk_t(q, k, v, logi_r, lf_r, ct_state, n_state, m_state, row0, rows):
    ri = _iota((rows, rows), 0)
    ci = _iota((rows, rows), 1)
    valid_r = _iota((1, rows), 1) >= row0
    lf_r = jnp.where(valid_r, lf_r, 0.0)
    logi_r = jnp.where(valid_r, logi_r, NEG)
    upper = jnp.where(ri <= ci, 1.0, 0.0).astype(BF16)
    a = jnp.where(ci > ri, lf_r, jnp.where(ci == ri, logi_r, 0.0))
    pre = _dot01_right(jnp.concatenate([a, jnp.broadcast_to(lf_r, (8, rows))], axis=0), upper)
    b_r = pre[rows:rows + 1, :]
    log_d = jnp.where(ri <= ci, pre[0:rows, :], -jnp.inf)
    m_inter = b_r + m_state
    m_t = jnp.maximum(m_inter, jnp.max(log_d, axis=0, keepdims=True))
    qb = _bf(q)
    kb = _bf(k)
    w = _dot_nt(kb, qb) * jnp.exp(log_d - m_t)
    w_inter = jnp.exp(m_inter - m_t)
    vt = v.T
    num_t = _dot(_bf(vt), _bf(w)) + w_inter * _dot_nt(_bf(ct_state), qb)
    nq = _dot_nt(_bf(jnp.broadcast_to(n_state, (8, HEAD_DIM))), qb)[0:1, :]
    den = jnp.sum(w, axis=0, keepdims=True) + w_inter * nq
    h_t = num_t / jnp.maximum(jnp.abs(den), jnp.exp(-m_t))
    m_new = m_t[:, rows - 1:rows]
    b_last = b_r[:, rows - 1:rows]
    w_src = jnp.exp(b_last - b_r + logi_r - m_new)
    decay = jnp.exp(b_last + m_state - m_new)
    ct_new = decay * ct_state + _dot(_bf(vt * w_src), kb)
    n_new = decay * n_state + _dot(_bf(jnp.broadcast_to(w_src, (8, rows))), kb)[0:1, :]
    return h_t.T, ct_new, n_new, m_new


def _mlstm_kernel(z_ref, g_ref, gt_ref, c0_ref, n0_ref, m0_ref, o_ref, c_ref, n_ref, m_ref, *,
                  rows, row0_first, transposed):
    c = pl.program_id(1)

    @pl.when(c == 0)
    def _():
        c_ref[...] = c0_ref[...]
        n_ref[...] = n0_ref[...]
        m_ref[...] = m0_ref[...]

    row0 = jnp.where(c == 0, row0_first, 0)
    for s in range(SEQ_GROUP):
        g = g_ref[s]
        gt = gt_ref[s, 0]
        outs = []
        for h in range(M_HEADS):
            lo = h * HEAD_DIM
            q = z_ref[s, :, lo:lo + HEAD_DIM]
            k = z_ref[s, :, 512 + lo:512 + lo + HEAD_DIM] * (HEAD_DIM ** -0.5)
            v = z_ref[s, :, 1024 + lo:1024 + lo + HEAD_DIM]
            og = z_ref[s, :, 1536 + lo:1536 + lo + HEAD_DIM]
            state = (c_ref[s, h], n_ref[s, h:h + 1, :], m_ref[s, h:h + 1, 0:1])
            if transposed:
                hh, c_new, n_new, m_new = _mlstm_chunk_t(
                    q, k, v, gt[h:h + 1, :], gt[M_HEADS + h:M_HEADS + h + 1, :], *state, row0, rows)
            else:
                hh, c_new, n_new, m_new = _mlstm_chunk(
                    q, k, v,
                    g[:, h:h + 1], g[:, M_HEADS + h:M_HEADS + h + 1],
                    gt[h:h + 1, :], gt[M_HEADS + h:M_HEADS + h + 1, :], *state, row0, rows)
            c_ref[s, h] = c_new
            n_ref[s, h:h + 1, :] = n_new
            m_ref[s, h:h + 1, :] = jnp.broadcast_to(m_new, (1, LANES))
            outs.append(_sigmoid(og) * hh)
        o_ref[s] = jnp.concatenate(outs, axis=1)


def _state_specs(heads):
    return [pl.BlockSpec((SEQ_GROUP, heads, HEAD_DIM, HEAD_DIM), lambda b, c: (b, 0, 0, 0)),
            pl.BlockSpec((SEQ_GROUP, heads, HEAD_DIM), lambda b, c: (b, 0, 0)),
            pl.BlockSpec((SEQ_GROUP, heads, LANES), lambda b, c: (b, 0, 0))]


def mlstm_call(z, g2, g2t, c0, n0, m0, *, row0_first, transposed):
    nb, t_len, _ = z.shape
    n_chunks, rows = g2t.shape[1], g2t.shape[3]
    assert nb % SEQ_GROUP == 0 and n_chunks * rows == t_len
    blk = lambda b, c: (b, c, 0)
    return pl.pallas_call(
        functools.partial(_mlstm_kernel, rows=rows, row0_first=row0_first, transposed=transposed),
        grid=(nb // SEQ_GROUP, n_chunks),
        in_specs=[pl.BlockSpec((SEQ_GROUP, rows, 2048), blk),
                  pl.BlockSpec((SEQ_GROUP, rows, LANES), blk),
                  pl.BlockSpec((SEQ_GROUP, 1, 16, rows), lambda b, c: (b, c, 0, 0))] + _state_specs(M_HEADS),
        out_specs=[pl.BlockSpec((SEQ_GROUP, rows, 512), blk)] + _state_specs(M_HEADS),
        out_shape=[jax.ShapeDtypeStruct((nb, t_len, 512), F32),
                   jax.ShapeDtypeStruct((nb, M_HEADS, HEAD_DIM, HEAD_DIM), F32),
                   jax.ShapeDtypeStruct((nb, M_HEADS, HEAD_DIM), F32),
                   jax.ShapeDtypeStruct((nb, M_HEADS, LANES), F32)],
        compiler_params=pltpu.CompilerParams(dimension_semantics=("parallel", "arbitrary")),
        name="mlstm",
    )(z, g2, g2t, c0, n0, m0)


def _ret_log_gamma(h):
    return math.log1p(-(2.0 ** (-5.0 - h)))


def _ret_body(z_ref, cos_ref, sin_ref, lng_ref, o_ref, s_ref, *, rows, row0_first):
    c = pl.program_id(1)
    row0 = jnp.where(c == 0, row0_first, 0)
    n_valid = (rows - row0).astype(F32)
    cosf = cos_ref[...]
    sinf = sin_ref[...]
    ri = _iota((rows, rows), 0)
    ci = _iota((rows, rows), 1)
    diff = (ri - ci).astype(F32)
    rowi = _iota((rows, 1), 0)
    te = (rowi - row0).astype(F32)
    valid = rowi >= row0
    for h in range(R_HEADS):
        lg = _ret_log_gamma(h)
        lo = h * HEAD_DIM
        decay_m = jnp.where(diff >= 0, jnp.exp(diff * lg), 0.0)
        w_inter = jnp.exp((te + 1.0) * lg)
        w_src = jnp.exp((n_valid - 1.0 - te) * lg)
        s_decay = jnp.exp(n_valid * lg)
        for s in range(SEQ_GROUP):
            q = z_ref[s, :, lo:lo + HEAD_DIM]
            k = z_ref[s, :, 512 + lo:512 + lo + HEAD_DIM]
            v = jnp.where(valid, z_ref[s, :, 1024 + lo:1024 + lo + HEAD_DIM], 0.0)
            rg = z_ref[s, :, 1536 + lo:1536 + lo + HEAD_DIM]
            q = q * cosf + pltpu.roll(q, HEAD_DIM // 2, 1) * sinf
            k = (k * cosf + pltpu.roll(k, HEAD_DIM // 2, 1) * sinf) * (HEAD_DIM ** -0.5)
            s_state = s_ref[s, h]
            qb = _bf(q)
            vb = _bf(v)
            intra = _dot(_bf(_dot_nt(qb, _bf(k)) * decay_m), vb)
            inter = _dot(qb, _bf(s_state)) * w_inter
            s_ref[s, h] = s_decay * s_state + _dot_tn(_bf(k * w_src), vb)
            hr = intra + inter
            mu = jnp.mean(hr, axis=-1, keepdims=True)
            var = jnp.mean(jnp.square(hr - mu), axis=-1, keepdims=True)
            hr = (hr - mu) * lax.rsqrt(var + EPS) * lng_ref[h:h + 1, :]
            o_ref[s, :, lo:lo + HEAD_DIM] = rg * _sigmoid(rg) * hr


def _lru_body(z_ref, pre_ref, cw_ref, cb_ref, wa_ref, ba_ref, wx_ref, bx_ref, lam_ref,
              o_ref, hlast_ref, tail_ref, cbuf_s, a_s, b_s, *, rows, row0_first):
    c = pl.program_id(1)
    row0 = jnp.where(c == 0, row0_first, 0)
    rowi = _iota((rows, 1), 0)
    valid = rowi >= row0
    sp_lam = _softplus(-lam_ref[...])
    for s in range(SEQ_GROUP):
        lx = jnp.where(valid, z_ref[s, :, 0:LRU_WIDTH], pre_ref[min(s, pre_ref.shape[0] - 1)])
        cbuf_s[s, 0:8, :] = tail_ref[s]
        cbuf_s[s, 8:8 + rows, :] = lx
        xc = cb_ref[...] + cw_ref[CONV_W - 1:CONV_W, :] * lx
        for j in range(1, CONV_W):
            xc = xc + cw_ref[CONV_W - 1 - j:CONV_W - j, :] * cbuf_s[s, 8 - j:8 - j + rows, :]
        tail_ref[s] = lx[rows - 8:rows]
        xcb = _bf(xc)
        pre_a = []
        pre_x = []
        for p in range(LRU_WIDTH // LANES):
            xs = xcb[:, p * LANES:(p + 1) * LANES]
            pre_a.append(_dot(xs, wa_ref[p]))
            pre_x.append(_dot(xs, wx_ref[p]))
        r = _sigmoid(jnp.concatenate(pre_a, axis=1) + ba_ref[...])
        ig = _sigmoid(jnp.concatenate(pre_x, axis=1) + bx_ref[...])
        log_a = -LRU_C * r * sp_lam
        a = jnp.where(valid, jnp.exp(log_a), 1.0)
        one_minus_a2 = -jnp.tanh(log_a) * (jnp.exp(2.0 * log_a) + 1.0)
        bx = jnp.where(valid, jnp.sqrt(one_minus_a2) * (ig * xc), 0.0)
        a_s[s] = a
        b_s[s] = bx
    carries = [hlast_ref[s] for s in range(SEQ_GROUP)]
    for t in range(rows):
        for s in range(SEQ_GROUP):
            carries[s] = a_s[s, t:t + 1, :] * carries[s] + b_s[s, t:t + 1, :]
            b_s[s, t:t + 1, :] = carries[s]
    for s in range(SEQ_GROUP):
        hlast_ref[s] = carries[s]
        gate = z_ref[s, :, LRU_WIDTH:2 * LRU_WIDTH]
        gl = 0.5 * gate * (1.0 + jnp.tanh(math.sqrt(2.0 / math.pi) * (gate + 0.044715 * gate * gate * gate)))
        o_ref[s] = b_s[s] * gl


def _ret_lru_kernel(zr_ref, cos_ref, sin_ref, lng_ref, s0_ref, zl_ref, pre_ref, h0_ref,
                    cw_ref, cb_ref, wa_ref, ba_ref, wx_ref, bx_ref, lam_ref,
                    or_ref, s_ref, ol_ref, hlast_ref, tail_ref, cbuf_s, a_s, b_s, *, rows, row0_first):
    @pl.when(pl.program_id(1) == 0)
    def _():
        s_ref[...] = s0_ref[...]
        hlast_ref[...] = h0_ref[...]
        tail_ref[...] = jnp.zeros(tail_ref.shape, F32)

    _ret_body(zr_ref, cos_ref, sin_ref, lng_ref, or_ref, s_ref, rows=rows, row0_first=row0_first)
    _lru_body(zl_ref, pre_ref, cw_ref, cb_ref, wa_ref, ba_ref, wx_ref, bx_ref, lam_ref,
              ol_ref, hlast_ref, tail_ref, cbuf_s, a_s, b_s, rows=rows, row0_first=row0_first)


def ret_lru_call(zr, cosf, sinf, lng, s0, zl, pre, h0, cw, cb, wa2, ba, wx2, bx, lam, *, rows, row0_first):
    nb, t_len, _ = zr.shape
    assert nb % SEQ_GROUP == 0 and t_len % rows == 0
    blk = lambda b, c: (b, c, 0)
    per_seq = lambda b, c: (b, 0, 0)
    const2 = lambda b, c: (0, 0)
    const3 = lambda b, c: (0, 0, 0)
    sspec = pl.BlockSpec((SEQ_GROUP, R_HEADS, HEAD_DIM, HEAD_DIM), lambda b, c: (b, 0, 0, 0))
    shared_pre = pre.shape[0] == 1
    in_specs = [pl.BlockSpec((SEQ_GROUP, rows, 2048), blk),
                pl.BlockSpec((rows, HEAD_DIM), lambda b, c: (c, 0)),
                pl.BlockSpec((rows, HEAD_DIM), lambda b, c: (c, 0)),
                pl.BlockSpec((R_HEADS, HEAD_DIM), const2),
                sspec,
                pl.BlockSpec((SEQ_GROUP, rows, 1024), blk),
                pl.BlockSpec((1 if shared_pre else SEQ_GROUP, rows, LRU_WIDTH), const3 if shared_pre else per_seq),
                pl.BlockSpec((SEQ_GROUP, 1, LRU_WIDTH), per_seq),
                pl.BlockSpec((CONV_W, LRU_WIDTH), const2),
                pl.BlockSpec((1, LRU_WIDTH), const2),
                pl.BlockSpec((LRU_WIDTH // LANES, LANES, LANES), const3),
                pl.BlockSpec((1, LRU_WIDTH), const2),
                pl.BlockSpec((LRU_WIDTH // LANES, LANES, LANES), const3),
                pl.BlockSpec((1, LRU_WIDTH), const2),
                pl.BlockSpec((1, LRU_WIDTH), const2)]
    return pl.pallas_call(
        functools.partial(_ret_lru_kernel, rows=rows, row0_first=row0_first),
        grid=(nb // SEQ_GROUP, t_len // rows),
        in_specs=in_specs,
        out_specs=[pl.BlockSpec((SEQ_GROUP, rows, 512), blk), sspec,
                   pl.BlockSpec((SEQ_GROUP, rows, LRU_WIDTH), blk),
                   pl.BlockSpec((SEQ_GROUP, 1, LRU_WIDTH), per_seq),
                   pl.BlockSpec((SEQ_GROUP, 8, LRU_WIDTH), per_seq)],
        out_shape=[jax.ShapeDtypeStruct((nb, t_len, 512), F32),
                   jax.ShapeDtypeStruct((nb, R_HEADS, HEAD_DIM, HEAD_DIM), F32),
                   jax.ShapeDtypeStruct((nb, t_len, LRU_WIDTH), F32),
                   jax.ShapeDtypeStruct((nb, 1, LRU_WIDTH), F32),
                   jax.ShapeDtypeStruct((nb, 8, LRU_WIDTH), F32)],
        scratch_shapes=[pltpu.VMEM((SEQ_GROUP, rows + 8, LRU_WIDTH), F32),
                        pltpu.VMEM((SEQ_GROUP, rows, LRU_WIDTH), F32),
                        pltpu.VMEM((SEQ_GROUP, rows, LRU_WIDTH), F32)],
        compiler_params=pltpu.CompilerParams(dimension_semantics=("parallel", "arbitrary")),
        name="ret_lru",
    )(zr, cosf, sinf, lng, s0, zl, pre, h0, cw, cb, wa2, ba, wx2, bx, lam)


def _fox_cumsum_kernel(g_ref, gt_ref, fx_ref, fr_ref):
    ri = _iota((CHUNK, CHUNK), 0)
    ci = _iota((CHUNK, CHUNK), 1)
    lower = jnp.where(ci <= ri, 1.0, 0.0).astype(BF16)
    upper = jnp.where(ri <= ci, 1.0, 0.0).astype(BF16)
    pr = _iota((3 * LANES, F_HEADS // 2 * LANES), 0)
    pc = _iota((3 * LANES, F_HEADS // 2 * LANES), 1)
    term, gate = pr // LANES, pr % LANES
    pair, lane = pc // LANES, pc % LANES
    first_head = 2 * M_HEADS
    place = (((gate == first_head + 2 * pair) & (lane == F_DH + term))
             | ((gate == first_head + 2 * pair + 1) & (lane == term)))
    place = jnp.where(place, 1.0, 0.0).astype(BF16)
    carry_r = jnp.zeros((1, LANES), F32)
    carry_c = jnp.zeros((16, 1), F32)
    for blk in range(NCH):
        x = g_ref[blk * CHUNK:(blk + 1) * CHUNK, :]
        xt = gt_ref[:, blk * CHUNK:(blk + 1) * CHUNK]
        if blk == 0:
            x = jnp.where(_iota((CHUNK, 1), 0) >= PADF, x, 0.0)
            xt = jnp.where(_iota((1, CHUNK), 1) >= PADF, xt, 0.0)
        cs = _dot01_left(lower, x) + carry_r
        cst = _dot01_right(xt, upper) + carry_c
        carry_r = cs[CHUNK - 1:CHUNK, :]
        carry_c = cst[:, CHUNK - 1:CHUNK]
        fk = cs * LOG2E
        if blk == 0:
            cst = jnp.where(_iota((1, CHUNK), 1) >= PADF, cst, -NEG)
            fk = jnp.where(_iota((CHUNK, 1), 0) >= PADF, fk, -NEG)
        fx_ref[blk * CHUNK:(blk + 1) * CHUNK, :] = _dot(jnp.concatenate(_split3(fk), axis=1), place)
        fr_ref[0, :, blk * CHUNK:(blk + 1) * CHUNK] = cst


def fox_cumsum(g2, g2t):
    return pl.pallas_call(
        _fox_cumsum_kernel,
        grid=(BATCH,),
        in_specs=[pl.BlockSpec((TP, LANES), lambda b: (b, 0)),
                  pl.BlockSpec((16, TP), lambda b: (0, b))],
        out_specs=[pl.BlockSpec((TP, F_HEADS // 2 * LANES), lambda b: (b, 0)),
                   pl.BlockSpec((1, 16, TP), lambda b: (b, 0, 0))],
        out_shape=[jax.ShapeDtypeStruct((SBASE, F_HEADS // 2 * LANES), F32),
                   jax.ShapeDtypeStruct((BATCH, 16, TP), F32)],
        compiler_params=pltpu.CompilerParams(dimension_semantics=("parallel",)),
        name="fox_cumsum",
    )(g2, g2t)


def _fox_prompt_kernel(q_ref, k_ref, v_ref, fr_ref, fx_ref, o_ref,
                       qta_s, qtb_s, ka_s, kb_s, vt_s, m_s, l_s, acc_s, ua_s):
    i = pl.program_id(2)
    lane = _iota((1, LANES), 1)
    is_a = lane < F_DH
    row = _iota((LANES, 1), 0)
    row_a = row < F_DH
    k_heads = (ka_s, kb_s)
    qt_heads = (qta_s, qtb_s)

    @pl.when(i == 0)
    def _():
        k = k_ref[...]
        fx = fx_ref[...]
        ka_s[...] = _bf(jnp.where(is_a, k, fx))
        kb_s[...] = _bf(jnp.where(is_a, fx, k))

        minus_a = jnp.where((row >= F_DH) & (row < F_DH + 3), -1.0, 0.0)
        minus_b = jnp.where(row < 3, -1.0, 0.0)

        def fill(blk, carry):
            rows = pl.ds(pl.multiple_of(blk * CHUNK, CHUNK), CHUNK)
            qt = (q_ref[rows, :] * (F_DH ** -0.5 * LOG2E)).T
            qta_s[blk] = _bf(jnp.where(row_a, qt, minus_a))
            qtb_s[blk] = _bf(jnp.where(row_a, minus_b, qt))
            vt_s[blk] = _bf(v_ref[rows, :].T)
            return carry

        lax.fori_loop(0, NCH, fill, 0)

    def init():
        m_s[...] = jnp.full(m_s.shape, NEG, F32)
        l_s[...] = jnp.zeros(l_s.shape, F32)
        acc_s[...] = jnp.zeros(acc_s.shape, F32)

    def scores(hh, qt, kblk, nkb):
        ck = nkb * CHUNK
        k0 = kblk * CHUNK if isinstance(kblk, int) else pl.multiple_of(kblk * CHUNK, CHUNK)
        return _dot(k_heads[hh][pl.ds(k0, ck), :], qt[hh])

    def step(hh, u, qblk, nqb, kblks, diagonal):
        tq = nqb * CHUNK
        ck = len(kblks) * CHUNK
        fq = jnp.concatenate([fr_ref[0, 0, qblk + t, hh:hh + 1, :] for t in range(nqb)], axis=1)
        fq = jnp.where(fq > 0.5 * -NEG, 0.0, fq * LOG2E)
        if diagonal:
            lead = ck - tq
            ri = _iota((ck, tq), 0)
            u = jnp.where((ri < lead) | (ri - lead <= _iota((ck, tq), 1)), u, NEG)
        m_old = m_s[hh, :, 0:tq]
        m_new = jnp.maximum(m_old, jnp.max(u, axis=0, keepdims=True) + fq)
        m_s[hh, :, 0:tq] = m_new
        alpha = jnp.exp2(m_old - m_new)
        p = _bf(jnp.exp2(u + (fq - m_new)))
        feat = slice(hh * F_DH, (hh + 1) * F_DH)
        vt = jnp.concatenate([vt_s[kb, feat, :] for kb in kblks], axis=1)
        pv = _dot(jnp.concatenate([vt, jnp.ones((16, ck), BF16)], axis=0), p)
        l_s[hh, :, 0:tq] = alpha * l_s[hh, :, 0:tq] + pv[F_DH:F_DH + 1, :]
        acc_s[feat, 0:tq] = alpha * acc_s[feat, 0:tq] + pv[0:F_DH, :]

    def finalize(qblk, nqb):
        tq = nqb * CHUNK
        o_t = acc_s[:, 0:tq] / jnp.where(row_a, l_s[0, :, 0:tq], l_s[1, :, 0:tq])
        for t in range(nqb):
            rows = pl.ds(pl.multiple_of((qblk + t) * CHUNK, CHUNK), CHUNK)
            o_ref[rows, :] = o_t[:, t * CHUNK:(t + 1) * CHUNK].T

    @pl.when(i == 0)
    def _():
        init()
        qt0 = (qta_s[0], qtb_s[0])
        for hh in range(2):
            step(hh, scores(hh, qt0, 0, 1), 0, 1, [0], True)
        finalize(0, 1)

    nck = FOX_CK // CHUNK
    nqb = FOX_TQ // CHUNK
    qblk = 1 + i * nqb
    qt = tuple(jnp.concatenate([qt_h[qblk + t] for t in range(nqb)], axis=1) for qt_h in qt_heads)
    init()

    ua_s[...] = scores(0, qt, 1, nck)

    def body(j, carry):
        kblk = 1 + j * nck
        kblks = [kblk + t for t in range(nck)]
        ub = scores(1, qt, kblk, nck)
        step(0, ua_s[...], qblk, nqb, kblks, False)
        ua_s[...] = scores(0, qt, kblk + nck, nck)
        step(1, ub, qblk, nqb, kblks, False)
        return carry

    lax.fori_loop(0, i, body, 0)
    last = [0] + [qblk + t for t in range(nck)]
    ua = jnp.concatenate([scores(0, qt, 0, 1), ua_s[...]], axis=0)
    ub = jnp.concatenate([scores(1, qt, 0, 1), scores(1, qt, qblk, nck)], axis=0)
    step(0, ua, qblk, nqb, last, True)
    step(1, ub, qblk, nqb, last, True)
    finalize(qblk, nqb)


def fox_prompt(zq, zk, zv, fr, fx):
    nq = (TP - CHUNK) // FOX_TQ
    pairs = F_HEADS // 2
    return pl.pallas_call(
        _fox_prompt_kernel,
        grid=(BATCH, pairs, nq),
        in_specs=[pl.BlockSpec((TP, LANES), lambda b, p, i: (b, p)),
                  pl.BlockSpec((TP, LANES), lambda b, p, i: (b, p)),
                  pl.BlockSpec((TP, LANES), lambda b, p, i: (b, p)),
                  pl.BlockSpec((1, 1, NCH, 2, LANES), lambda b, p, i: (b, p, 0, 0, 0)),
                  pl.BlockSpec((TP, LANES), lambda b, p, i: (b, p))],
        out_specs=pl.BlockSpec((TP, LANES), lambda b, p, i: (b, p)),
        out_shape=jax.ShapeDtypeStruct((SBASE, 512), F32),
        scratch_shapes=[pltpu.VMEM((NCH, LANES, CHUNK), BF16),
                        pltpu.VMEM((NCH, LANES, CHUNK), BF16),
                        pltpu.VMEM((TP, LANES), BF16),
                        pltpu.VMEM((TP, LANES), BF16),
                        pltpu.VMEM((NCH, LANES, CHUNK), BF16),
                        pltpu.VMEM((2, 1, FOX_TQ), F32),
                        pltpu.VMEM((2, 1, FOX_TQ), F32),
                        pltpu.VMEM((LANES, FOX_TQ), F32),
                        pltpu.VMEM((FOX_CK, FOX_TQ), F32)],
        compiler_params=pltpu.CompilerParams(dimension_semantics=("parallel", "parallel", "arbitrary")),
        name="fox_prompt",
    )(zq, zk, zv, fr, fx)


def _lf_suffix_kernel(x_ref, suf_ref, tot_ref):
    ri = _iota((PAGE_SIZE, PAGE_SIZE), 0)
    ci = _iota((PAGE_SIZE, PAGE_SIZE), 1)
    after = jnp.where(ri > ci, 1.0, 0.0).astype(BF16)
    ones = jnp.ones((PAGE_SIZE, PAGE_SIZE), BF16)
    x = x_ref[...].reshape(PRE_PG * F_HEADS, PAGE_SIZE)
    suf_ref[...] = _dot01_right(x, after).reshape(PRE_PG, F_HEADS, PAGE_SIZE)
    tot_ref[...] = _dot01_right(x, ones).reshape(PRE_PG, F_HEADS, PAGE_SIZE)


def lf_suffix(lft):
    n_phys = lft.shape[0]
    spec = pl.BlockSpec((PRE_PG, F_HEADS, PAGE_SIZE), lambda i: (i, 0, 0))
    return pl.pallas_call(
        _lf_suffix_kernel,
        grid=(n_phys // PRE_PG,),
        in_specs=[spec],
        out_specs=[spec, spec],
        out_shape=[jax.ShapeDtypeStruct(lft.shape, F32)] * 2,
        compiler_params=pltpu.CompilerParams(dimension_semantics=("parallel",)),
        name="lf_suffix",
    )(lft)


def _fox_decode_kernel(pt_ref, zq_ref, zk_ref, zv_ref, gt_ref, *rest):
    k_refs = rest[0:DEC_PG]
    v_refs = rest[DEC_PG:2 * DEC_PG]
    suf_refs = rest[2 * DEC_PG:3 * DEC_PG]
    tot_refs = rest[3 * DEC_PG:4 * DEC_PG]
    o_ref = rest[4 * DEC_PG]
    q2_s, m_s, l_s, acc_s, run_s, nc_s = rest[4 * DEC_PG + 1:]
    j = pl.program_id(1)
    n_rows = DEC_SEQ * F_HEADS
    width = F_HEADS * F_DH

    def update(u, v, v_feature_major):
        m_old = m_s[...]
        m_new = jnp.maximum(m_old, jnp.max(u, axis=1, keepdims=True))
        alpha = jnp.exp(m_old - m_new)
        p = jnp.exp(u - m_new[:, 0:u.shape[1]])
        pv = _dot_nt(_bf(p), v) if v_feature_major else _dot(_bf(p), v)
        l_s[...] = alpha * l_s[...] + jnp.sum(p, axis=1, keepdims=True)
        acc_s[...] = jnp.concatenate([alpha] * (width // LANES), axis=1) * acc_s[...] + pv
        m_s[...] = m_new

    @pl.when(j == 0)
    def _():
        m_s[...] = jnp.full(m_s.shape, NEG, F32)
        l_s[...] = jnp.zeros(l_s.shape, F32)
        acc_s[...] = jnp.zeros(acc_s.shape, F32)
        run_s[...] = jnp.zeros(run_s.shape, F32)
        head_mask = _iota((F_HEADS, width), 1) // F_DH == _iota((F_HEADS, width), 0)
        q16 = zq_ref[...] * (F_DH ** -0.5)
        q2 = _bf(jnp.concatenate(
            [jnp.where(head_mask, q16[SPAD + t:SPAD + t + 1, :], 0.0) for t in range(DEC_SEQ)], axis=0))
        q2_s[...] = q2
        lf_new = gt_ref[F_HEADS:2 * F_HEADS, :]
        lane = _iota((1, SROWS), 1)
        cum = jnp.zeros((F_HEADS, SROWS), F32)
        for t in range(DEC_SEQ):
            cum = cum + jnp.where(lane >= SPAD + t, lf_new[:, SPAD + t:SPAD + t + 1], 0.0)
        nc = jnp.concatenate([cum[:, SPAD + t:SPAD + t + 1] for t in range(DEC_SEQ)], axis=0)
        nc_s[...] = jnp.broadcast_to(nc, nc_s.shape)
        cum4 = jnp.concatenate([cum] * DEC_SEQ, axis=0)
        qi = _iota((n_rows, SROWS), 0) // F_HEADS
        kj = _iota((n_rows, SROWS), 1) - SPAD
        u = jnp.where((kj >= 0) & (kj <= qi), _dot_nt(q2, _bf(zk_ref[...])) + (nc - cum4), NEG)
        update(u, _bf(zv_ref[...]), False)

    q2 = q2_s[...]
    nc = nc_s[...]
    run = run_s[...]
    us = []
    for i in range(DEC_PG):
        later = run + suf_refs[i][0]
        bias = jnp.concatenate([later] * DEC_SEQ, axis=0) + nc
        us.append(_dot(q2, _bf(k_refs[i][0].reshape(width, PAGE_SIZE))) + bias)
        run = run + tot_refs[i][0]
    run_s[...] = run
    u = jnp.concatenate(us, axis=1)
    m_old = m_s[...]
    m_new = jnp.maximum(m_old, jnp.max(u, axis=1, keepdims=True))
    alpha = jnp.exp(m_old - m_new)
    p = jnp.exp(u - jnp.concatenate([m_new] * DEC_PG, axis=1))
    l_s[...] = alpha * l_s[...] + jnp.sum(p, axis=1, keepdims=True)
    pv = _dot_nt(_bf(p[:, 0:PAGE_SIZE]), _bf(v_refs[0][0].reshape(width, PAGE_SIZE)))
    for i in range(1, DEC_PG):
        pv = pv + _dot_nt(_bf(p[:, i * PAGE_SIZE:(i + 1) * PAGE_SIZE]), _bf(v_refs[i][0].reshape(width, PAGE_SIZE)))
    acc_s[...] = jnp.concatenate([alpha] * (width // LANES), axis=1) * acc_s[...] + pv
    m_s[...] = m_new

    @pl.when(j == pl.num_programs(1) - 1)
    def _():
        head_mask = (_iota((n_rows, width), 1) // F_DH) == (_iota((n_rows, width), 0) % F_HEADS)
        o2 = jnp.where(head_mask, acc_s[...] / jnp.concatenate([l_s[...]] * (width // LANES), axis=1), 0.0)
        o_ref[...] = jnp.zeros(o_ref.shape, F32)
        for t in range(DEC_SEQ):
            o_ref[SPAD + t:SPAD + t + 1, :] = jnp.sum(o2[t * F_HEADS:(t + 1) * F_HEADS, :], axis=0, keepdims=True)


def fox_decode(page_table, zq, zk, zv, g2t_s, cache_kt, cache_vt, suf, tot):
    n_pages = page_table.shape[1]
    width = F_HEADS * F_DH

    def page_map(i):
        return lambda b, j, pt: (pt[b, n_pages - 1 - (j * DEC_PG + i)], 0, 0, 0)

    def page_map3(i):
        return lambda b, j, pt: (pt[b, n_pages - 1 - (j * DEC_PG + i)], 0, 0)

    in_specs = [pl.BlockSpec((SROWS, width), lambda b, j, pt: (b, 0)),
                pl.BlockSpec((SROWS, width), lambda b, j, pt: (b, 0)),
                pl.BlockSpec((SROWS, width), lambda b, j, pt: (b, 0)),
                pl.BlockSpec((None, 16, SROWS), lambda b, j, pt: (b, 0, 0))]
    in_specs += [pl.BlockSpec((1, F_HEADS, F_DH, PAGE_SIZE), page_map(i)) for i in range(DEC_PG)]
    in_specs += [pl.BlockSpec((1, F_HEADS, F_DH, PAGE_SIZE), page_map(i)) for i in range(DEC_PG)]
    in_specs += [pl.BlockSpec((1, F_HEADS, PAGE_SIZE), page_map3(i)) for i in range(DEC_PG)]
    in_specs += [pl.BlockSpec((1, F_HEADS, PAGE_SIZE), page_map3(i)) for i in range(DEC_PG)]
    args = [zq, zk, zv, g2t_s] + [cache_kt] * DEC_PG + [cache_vt] * DEC_PG + [suf] * DEC_PG + [tot] * DEC_PG
    n_rows = DEC_SEQ * F_HEADS
    grid_spec = pltpu.PrefetchScalarGridSpec(
        num_scalar_prefetch=1,
        grid=(DEC_BATCH, n_pages // DEC_PG),
        in_specs=in_specs,
        out_specs=pl.BlockSpec((SROWS, width), lambda b, j, pt: (b, 0)),
        scratch_shapes=[pltpu.VMEM((n_rows, width), BF16),
                        pltpu.VMEM((n_rows, LANES), F32),
                        pltpu.VMEM((n_rows, LANES), F32),
                        pltpu.VMEM((n_rows, width), F32),
                        pltpu.VMEM((F_HEADS, PAGE_SIZE), F32),
                        pltpu.VMEM((n_rows, LANES), F32)])
    return pl.pallas_call(
        _fox_decode_kernel,
        grid_spec=grid_spec,
        out_shape=jax.ShapeDtypeStruct((DEC_BATCH * SROWS, width), F32),
        compiler_params=pltpu.CompilerParams(dimension_semantics=("parallel", "arbitrary")),
        name="fox_decode",
    )(page_table, *args)


def _block_diag_pairs(w):
    z = jnp.zeros((LANES // 2, LANES // 2), w.dtype)
    pairs = [jnp.block([[w[2 * p], z], [z, w[2 * p + 1]]]) for p in range(w.shape[0] // 2)]
    return _bf(jnp.stack(pairs))


def _rope_tables(pos):
    half = HEAD_DIM // 2
    freq = ROPE_BASE ** (-jnp.arange(half, dtype=F32) / half)
    ang = pos.astype(F32)[:, None] * freq[None, :]
    cos, sin = jnp.cos(ang), jnp.sin(ang)
    return jnp.concatenate([cos, cos], axis=1), jnp.concatenate([-sin, sin], axis=1)


def _unpad_prompt(x):
    return x.reshape((BATCH, TP) + x.shape[1:])[:, PADF:]


def _unpad_sample(x):
    return x.reshape((DEC_BATCH, SROWS) + x.shape[1:])[:, SPAD:]


def kernel(x_prompt, x_sample, cache_fox_k, cache_fox_v, cache_fox_logf, state_mlstm_C, state_mlstm_n, state_mlstm_m, state_ret_S, state_lru_h, state_lru_conv, page_table, meta_tokens, w_in_even, b_mlstm_i, b_mlstm_f, b_fox_f, w_out_even, w_in_odd, ret_ln_g, conv_w, conv_b, lru_wa, lru_ba, lru_wx, lru_bx, lru_lambda, w_out_odd, norm_g, ffn_wg, ffn_wu, ffn_wd):
    n_pages = page_table.shape[1]
    past = n_pages * PAGE_SIZE
    n_phys = cache_fox_k.shape[0]

    head = jnp.concatenate([jnp.zeros((PADF, D_MODEL), F32), meta_tokens.astype(F32)], axis=0)
    hs = jnp.concatenate([jnp.zeros((DEC_BATCH, SPAD, D_MODEL), F32), x_sample.astype(F32)],
                         axis=1).reshape(DEC_BATCH * SROWS, D_MODEL)
    h_s = hs
    h_p = jnp.concatenate([jnp.broadcast_to(head[None], (BATCH, CHUNK, D_MODEL)), x_prompt.astype(F32)],
                          axis=1).reshape(SBASE, D_MODEL)

    c = np.cumsum([0, 512, 512, 512, 512, 4, 4, 512, 512, 512, 8])
    gate_w = jnp.concatenate([w_in_even[:, c[4]:c[6]], w_in_even[:, c[9]:c[10]],
                              jnp.zeros((D_MODEL, LANES - 16), F32)], axis=1)
    w_even = _bf(jnp.concatenate([w_in_even[:, c[0]:c[4]], w_in_even[:, c[6]:c[9]], gate_w], axis=1))
    gate_b = jnp.concatenate([b_mlstm_i, b_mlstm_f, b_fox_f]).astype(F32)
    brow = jnp.concatenate([gate_b, jnp.zeros((LANES - 16,), F32)]).reshape(1, LANES)
    w_odd = _bf(w_in_odd)
    wo_even = _bf(w_out_even)
    wo_odd = _bf(w_out_odd)
    wg = _bf(ffn_wg)
    wu = _bf(ffn_wu)
    wd = _bf(ffn_wd)
    ng = norm_g.astype(F32).reshape(2, 4, 1, D_MODEL)

    even_cols = (2048, 512, 512, 512, LANES)
    zm_p, zq_p, zk_p, zv_p, g2_p, g2t_p = norm_matmul(h_p, ng[0, 0], w_even, even_cols, gate_bias=brow)
    zm_s, zq_s, zk_s, zv_s, g2_s, g2t_s = norm_matmul(h_s, ng[0, 0], w_even, even_cols, gate_bias=brow)
    prompt3 = lambda x: x.reshape(BATCH, TP, x.shape[-1])
    sample3 = lambda x: x.reshape(DEC_BATCH, SROWS, x.shape[-1])
    flat2 = lambda x: x.reshape(-1, x.shape[-1])

    zeros_c = jnp.zeros((BATCH, M_HEADS, HEAD_DIM, HEAD_DIM), F32)
    zeros_n = jnp.zeros((BATCH, M_HEADS, HEAD_DIM), F32)
    hm_p, p_c, p_n, p_m = mlstm_call(prompt3(zm_p), prompt3(g2_p),
                                     g2t_p.reshape(16, BATCH, NCH, CHUNK).transpose(1, 2, 0, 3),
                                     zeros_c, zeros_n, zeros_n, row0_first=PADF, transposed=True)
    p_c = p_c.transpose(0, 1, 3, 2)
    m0_s = jnp.broadcast_to(state_mlstm_m.astype(F32)[:, :, None], (DEC_BATCH, M_HEADS, LANES))
    hm_s, s_c, s_n, s_m = mlstm_call(sample3(zm_s), sample3(g2_s),
                                     g2t_s.reshape(16, DEC_BATCH, 1, SROWS).transpose(1, 2, 0, 3),
                                     state_mlstm_C.astype(F32), state_mlstm_n.astype(F32), m0_s, row0_first=SPAD,
                                     transposed=False)

    fx, fr = fox_cumsum(g2_p, g2t_p)
    pairs = F_HEADS // 2
    fr_p = fr[:, 8:16].reshape(BATCH, pairs, 2, NCH, CHUNK).transpose(0, 1, 3, 2, 4)
    hf_p = fox_prompt(zq_p, zk_p, zv_p, fr_p, fx)
    suf, tot = lf_suffix(cache_fox_logf.astype(F32).transpose(0, 2, 1))
    hf_s = fox_decode(page_table, zq_s, zk_s, zv_s, g2t_s.reshape(16, DEC_BATCH, SROWS).transpose(1, 0, 2),
                      cache_fox_k.astype(F32).transpose(0, 2, 3, 1), cache_fox_v.astype(F32).transpose(0, 2, 3, 1),
                      suf, tot)

    gains = norm_g.astype(F32)
    even_w = (wo_even[:512], wo_even[512:], gains[0, 1:4], wg[0], wu[0], wd[0])
    h_p = mix_ffn(flat2(hm_p), hf_p, h_p, *even_w)
    h_s = mix_ffn(flat2(hm_s), hf_s, h_s, *even_w)

    zr_p, zl_p = norm_matmul(h_p, ng[1, 0], w_odd, (2048, 1024))
    zr_s, zl_s = norm_matmul(h_s, ng[1, 0], w_odd, (2048, 1024))
    cos_p, sin_p = _rope_tables(jnp.arange(TP) - PADF)
    cos_s, sin_s = _rope_tables(past + jnp.arange(SROWS) - SPAD)
    lng = ret_ln_g.astype(F32)
    lru_args = (conv_w.astype(F32), conv_b.astype(F32).reshape(1, LRU_WIDTH),
                _block_diag_pairs(lru_wa), lru_ba.astype(F32).reshape(1, LRU_WIDTH),
                _block_diag_pairs(lru_wx), lru_bx.astype(F32).reshape(1, LRU_WIDTH),
                lru_lambda.astype(F32).reshape(1, LRU_WIDTH))
    hr_p, p_s, hl_p, p_h, p_tail = ret_lru_call(
        prompt3(zr_p), cos_p, sin_p, lng, jnp.zeros((BATCH, R_HEADS, HEAD_DIM, HEAD_DIM), F32),
        prompt3(zl_p), jnp.zeros((1, CHUNK, LRU_WIDTH), F32), jnp.zeros((BATCH, 1, LRU_WIDTH), F32), *lru_args,
        rows=CHUNK, row0_first=PADF)
    pre_s = jnp.concatenate([jnp.zeros((DEC_BATCH, SPAD - (CONV_W - 1), LRU_WIDTH), F32),
                             state_lru_conv.astype(F32),
                             jnp.zeros((DEC_BATCH, DEC_SEQ, LRU_WIDTH), F32)], axis=1)
    hr_s, s_s, hl_s, s_h, s_tail = ret_lru_call(
        sample3(zr_s), cos_s, sin_s, lng, state_ret_S.astype(F32),
        sample3(zl_s), pre_s, state_lru_h.astype(F32).reshape(DEC_BATCH, 1, LRU_WIDTH), *lru_args,
        rows=SROWS, row0_first=SPAD)

    odd_w = (wo_odd[:512], wo_odd[512:], gains[1, 1:4], wg[1], wu[1], wd[1])
    h_p = mix_ffn(flat2(hr_p), flat2(hl_p), h_p, *odd_w)
    h_s = mix_ffn(flat2(hr_s), flat2(hl_s), h_s, *odd_w)

    y_prompt = h_p.reshape(BATCH, TP, D_MODEL)[:, CHUNK:]
    y_sample = h_s.reshape(DEC_BATCH, SROWS, D_MODEL)[:, SPAD:]
    heads = lambda x: x.reshape(x.shape[:2] + (F_HEADS, F_DH))
    nback = CONV_W - 1
    return (y_prompt, y_sample,
            heads(_unpad_prompt(zk_p)), heads(_unpad_prompt(zv_p)), _unpad_prompt(g2_p[:, 8:16]),
            heads(_unpad_sample(zk_s)), heads(_unpad_sample(zv_s)), _unpad_sample(g2_s[:, 8:16]),
            p_c, p_n, p_m[:, :, 0], s_c, s_n, s_m[:, :, 0],
            p_s, s_s,
            p_h[:, 0], s_h[:, 0], p_tail[:, 8 - nback:], s_tail[:, 8 - nback:])
```

```python
import functools
import math

import jax
import jax.numpy as jnp
import numpy as np
from jax import lax
from jax.experimental import pallas as pl
from jax.experimental.pallas import tpu as pltpu

F32 = jnp.float32
BF16 = jnp.bfloat16

D_MODEL = 1024
BATCH = 4
SEQ = 4096
DEC_BATCH = 32
DEC_SEQ = 4
PAGE_SIZE = 128
N_META = 16
CHUNK = 128
M_HEADS = 4
F_HEADS = 8
F_DH = 64
R_HEADS = 4
HEAD_DIM = 128
ROPE_BASE = 10000.0
LRU_WIDTH = 512
LRU_C = 8.0
CONV_W = 4
D_FF = 2816
EPS = 1e-6
NEG = -1e30

LANES = 128
PADF = CHUNK - N_META
TP = PADF + N_META + SEQ
NCH = TP // CHUNK
SROWS = 16
SPAD = SROWS - DEC_SEQ
SBASE = BATCH * TP
NP = SBASE + DEC_BATCH * SROWS
TM = 512
SEQ_GROUP = 4
MXU_WIDTH = 256
MIX_FFN_VMEM_BYTES = 56 * 1024 * 1024
FOX_TQ = 512
FOX_CK = 512
LOG2E = math.log2(math.e)
DEC_PG = 32
PRE_PG = 128


def _bf(x):
    return x.astype(BF16)


def _dot(a, b):
    return jnp.dot(a, b, preferred_element_type=F32)


def _dot_nt(a, b):
    return lax.dot_general(a, b, (((1,), (1,)), ((), ())), preferred_element_type=F32)


def _dot_tn(a, b):
    return lax.dot_general(a, b, (((0,), (0,)), ((), ())), preferred_element_type=F32)


def _split3(x):
    hi = _bf(x)
    r1 = x - hi.astype(F32)
    mid = _bf(r1)
    lo = _bf(r1 - mid.astype(F32))
    return hi, mid, lo


def _dot01_right(x, m01):
    hi, mid, lo = _split3(x)
    return _dot(hi, m01) + _dot(mid, m01) + _dot(lo, m01)


def _dot01_left(m01, x):
    hi, mid, lo = _split3(x)
    return _dot(m01, hi) + _dot(m01, mid) + _dot(m01, lo)


def _iota(shape, dim):
    return lax.broadcasted_iota(jnp.int32, shape, dim)


def _rms(x, g):
    ms = jnp.mean(x * x, axis=-1, keepdims=True)
    return x * lax.rsqrt(ms + EPS) * g


def _softplus(x):
    return jnp.maximum(x, 0.0) + jnp.log1p(jnp.exp(-jnp.abs(x)))


def _log_sigmoid(x):
    return -_softplus(-x)


def _sigmoid(x):
    return 0.5 * jnp.tanh(0.5 * x) + 0.5


def _nm_kernel(x_ref, g_ref, w_ref, *rest, splits, gated):
    if gated:
        b_ref, o_refs = rest[0], rest[1:]
    else:
        o_refs = rest
    xn = _bf(_rms(x_ref[...], g_ref[...]))
    off = 0
    for k, n in enumerate(splits):
        res = _dot(xn, w_ref[:, off:off + n])
        off += n
        if gated and k == len(splits) - 1:
            res = res + b_ref[...]
            res = jnp.where(_iota(res.shape, 1) < M_HEADS, res, _log_sigmoid(res))
            o_refs[k + 1][...] = res.T[0:16, :]
        o_refs[k][...] = res


def norm_matmul(x, g, w, splits, gate_bias=None):
    n_rows, k = x.shape
    const = lambda i: (0, 0)
    gated = gate_bias is not None
    in_specs = [pl.BlockSpec((TM, k), lambda i: (i, 0)),
                pl.BlockSpec((1, k), const),
                pl.BlockSpec((k, sum(splits)), const, pipeline_mode=pl.Buffered(1))]
    out_specs = [pl.BlockSpec((TM, n), lambda i: (i, 0)) for n in splits]
    out_shape = [jax.ShapeDtypeStruct((n_rows, n), F32) for n in splits]
    args = [x, g, w]
    if gated:
        assert splits[-1] == LANES
        in_specs.append(pl.BlockSpec((1, LANES), const))
        out_specs.append(pl.BlockSpec((16, TM), lambda i: (0, i)))
        out_shape.append(jax.ShapeDtypeStruct((16, n_rows), F32))
        args.append(gate_bias)
    return pl.pallas_call(
        functools.partial(_nm_kernel, splits=splits, gated=gated),
        grid=(n_rows // TM,),
        in_specs=in_specs,
        out_specs=out_specs,
        out_shape=out_shape,
        compiler_params=pltpu.CompilerParams(dimension_semantics=("parallel",)),
        name="norm_matmul",
    )(*args)


def _mix_ffn_kernel(*refs, parts):
    def rows(piece_refs):
        if parts == 1:
            return piece_refs[0][...]
        return jnp.concatenate([r[0] for r in piece_refs], axis=0)

    a1 = rows(refs[0:parts])
    a2 = rows(refs[parts:2 * parts])
    h = rows(refs[2 * parts:3 * parts])
    w1_ref, w2_ref, g_ref, wg_ref, wu_ref, wd_ref, o_ref = refs[3 * parts:]
    mix = _dot(_bf(a1), w1_ref[...]) + _dot(_bf(a2), w2_ref[...])
    h1 = h + _rms(mix, g_ref[0:1, :])
    xn = _bf(_rms(h1, g_ref[1:2, :]))
    acts = []
    for c in range(wg_ref.shape[1] // MXU_WIDTH):
        cols = slice(c * MXU_WIDTH, (c + 1) * MXU_WIDTH)
        gate = _dot(xn, wg_ref[:, cols])
        up = _dot(xn, wu_ref[:, cols])
        acts.append(_bf(gate * _sigmoid(gate) * up))
    ff = _dot(jnp.concatenate(acts, axis=1), wd_ref[...])
    res = h1 + _rms(ff, g_ref[2:3, :])
    o_ref[...] = res if parts == 1 else res[None]


def mix_ffn(a1, a2, h, w1, w2, g, wg, wu, wd, main_only=False):
    k1 = a1.shape[-1]
    k2 = a2.shape[-1]
    d = h.shape[-1]
    dff = wg.shape[1]
    assert dff % MXU_WIDTH == 0
    resident = lambda shape: pl.BlockSpec(shape, lambda *_: (0,) * len(shape), pipeline_mode=pl.Buffered(1))
    weights = [resident((k1, d)), resident((k2, d)), resident((3, d)),
               resident((d, dff)), resident((d, dff)), resident((dff, d))]
    if main_only:
        parts = TM // CHUNK
        nb, t_len = h.shape[0], h.shape[1] - CHUNK
        assert t_len % TM == 0
        piece = lambda k, t: pl.BlockSpec((1, CHUNK, k), lambda b, j: (b, 1 + j * parts + t, 0))
        in_specs = ([piece(k1, t) for t in range(parts)] + [piece(k2, t) for t in range(parts)]
                    + [piece(d, t) for t in range(parts)] + weights)
        args = [a1] * parts + [a2] * parts + [h] * parts
        grid = (nb, t_len // TM)
        out_specs = pl.BlockSpec((1, TM, d), lambda b, j: (b, j, 0))
        out_shape = jax.ShapeDtypeStruct((nb, t_len, d), F32)
    else:
        parts = 1
        n_rows = h.shape[0]
        assert n_rows % TM == 0
        row_blk = lambda i: (i, 0)
        in_specs = [pl.BlockSpec((TM, k1), row_blk), pl.BlockSpec((TM, k2), row_blk),
                    pl.BlockSpec((TM, d), row_blk)] + weights
        args = [a1, a2, h]
        grid = (n_rows // TM,)
        out_specs = pl.BlockSpec((TM, d), row_blk)
        out_shape = jax.ShapeDtypeStruct((n_rows, d), F32)
    return pl.pallas_call(
        functools.partial(_mix_ffn_kernel, parts=parts),
        grid=grid,
        in_specs=in_specs,
        out_specs=out_specs,
        out_shape=out_shape,
        compiler_params=pltpu.CompilerParams(dimension_semantics=("parallel",) * len(grid),
                                             vmem_limit_bytes=MIX_FFN_VMEM_BYTES),
        name="mix_ffn",
    )(*args, w1, w2, g, wg, wu, wd)


def _mlstm_chunk(q, k, v, logi_c, lf_c, logi_r, lf_r, c_state, n_state, m_state, row0, rows):
    ri = _iota((rows, rows), 0)
    ci = _iota((rows, rows), 1)
    valid_c = _iota((rows, 1), 0) >= row0
    valid_r = _iota((1, rows), 1) >= row0
    lf_c = jnp.where(valid_c, lf_c, 0.0)
    lf_r = jnp.where(valid_r, lf_r, 0.0)
    logi_c = jnp.where(valid_c, logi_c, NEG)
    logi_r = jnp.where(valid_r, logi_r, NEG)
    causal = ci <= ri
    b_c = jnp.sum(jnp.where(causal, lf_r, 0.0), axis=1, keepdims=True)
    b_r = jnp.sum(jnp.where(ri <= ci, lf_c, 0.0), axis=0, keepdims=True)
    log_d = jnp.where(causal, b_c - b_r + logi_r, -jnp.inf)
    m_inter = b_c + m_state
    m_t = jnp.maximum(m_inter, jnp.max(log_d, axis=1, keepdims=True))
    qb = _bf(q)
    kb = _bf(k)
    vb = _bf(v)
    w_intra = _dot_nt(qb, kb) * jnp.exp(log_d - m_t)
    w_inter = jnp.exp(m_inter - m_t)
    num = _dot(_bf(w_intra), vb) + w_inter * _dot(qb, _bf(c_state))
    den = jnp.sum(w_intra, axis=1, keepdims=True) + w_inter * jnp.sum(q * n_state, axis=1, keepdims=True)
    h = num / jnp.maximum(jnp.abs(den), jnp.exp(-m_t))
    m_new = m_t[rows - 1:rows, :]
    b_last = b_c[rows - 1:rows, :]
    w_src = jnp.exp(b_last - b_c + logi_c - m_new)
    decay = jnp.exp(b_last + m_state - m_new)
    ks = k * w_src
    c_new = decay * c_state + _dot_tn(_bf(ks), vb)
    n_new = decay * n_state + jnp.sum(ks, axis=0, keepdims=True)
    return h, c_new, n_new, m_new


def _mlstm_chunk_t(q, k, v, logi_r, lf_r, ct_state, n_state, m_state, row0, rows):
    ri = _iota((rows, rows), 0)
    ci = _iota((rows, rows), 1)
    valid_r = _iota((1, rows), 1) >= row0
    lf_r = jnp.where(valid_r, lf_r, 0.0)
    logi_r = jnp.where(valid_r, logi_r, NEG)
    upper = jnp.where(ri <= ci, 1.0, 0.0).astype(BF16)
    a = jnp.where(ci > ri, lf_r, jnp.where(ci == ri, logi_r, 0.0))
    pre = _dot01_right(jnp.concatenate([a, jnp.broadcast_to(lf_r, (8, rows))], axis=0), upper)
    b_r = pre[rows:rows + 1, :]
    log_d = jnp.where(ri <= ci, pre[0:rows, :], -jnp.inf)
    m_inter = b_r + m_state
    m_t = jnp.maximum(m_inter, jnp.max(log_d, axis=0, keepdims=True))
    qb = _bf(q)
    kb = _bf(k)
    w = _dot_nt(kb, qb) * jnp.exp(log_d - m_t)
    w_inter = jnp.exp(m_inter - m_t)
    vt = v.T
    num_t = _dot(_bf(vt), _bf(w)) + w_inter * _dot_nt(_bf(ct_state), qb)
    nq = _dot_nt(_bf(jnp.broadcast_to(n_state, (8, HEAD_DIM))), qb)[0:1, :]
    den = jnp.sum(w, axis=0, keepdims=True) + w_inter * nq
    h_t = num_t / jnp.maximum(jnp.abs(den), jnp.exp(-m_t))
    m_new = m_t[:, rows - 1:rows]
    b_last = b_r[:, rows - 1:rows]
    w_src = jnp.exp(b_last - b_r + logi_r - m_new)
    decay = jnp.exp(b_last + m_state - m_new)
    ct_new = decay * ct_state + _dot(_bf(vt * w_src), kb)
    n_new = decay * n_state + _dot(_bf(jnp.broadcast_to(w_src, (8, rows))), kb)[0:1, :]
    return h_t.T, ct_new, n_new, m_new


def _mlstm_kernel(z_ref, g_ref, gt_ref, c0_ref, n0_ref, m0_ref, o_ref, c_ref, n_ref, m_ref, *,
                  rows, row0_first, transposed):
    c = pl.program_id(1)

    @pl.when(c == 0)
    def _():
        c_ref[...] = c0_ref[...]
        n_ref[...] = n0_ref[...]
        m_ref[...] = m0_ref[...]

    row0 = jnp.where(c == 0, row0_first, 0)
    for s in range(SEQ_GROUP):
        g = g_ref[s]
        gt = gt_ref[s, 0]
        outs = []
        for h in range(M_HEADS):
            lo = h * HEAD_DIM
            q = z_ref[s, :, lo:lo + HEAD_DIM]
            k = z_ref[s, :, 512 + lo:512 + lo + HEAD_DIM] * (HEAD_DIM ** -0.5)
            v = z_ref[s, :, 1024 + lo:1024 + lo + HEAD_DIM]
            og = z_ref[s, :, 1536 + lo:1536 + lo + HEAD_DIM]
            state = (c_ref[s, h], n_ref[s, h:h + 1, :], m_ref[s, h:h + 1, 0:1])
            if transposed:
                hh, c_new, n_new, m_new = _mlstm_chunk_t(
                    q, k, v, gt[h:h + 1, :], gt[M_HEADS + h:M_HEADS + h + 1, :], *state, row0, rows)
            else:
                hh, c_new, n_new, m_new = _mlstm_chunk(
                    q, k, v,
                    g[:, h:h + 1], g[:, M_HEADS + h:M_HEADS + h + 1],
                    gt[h:h + 1, :], gt[M_HEADS + h:M_HEADS + h + 1, :], *state, row0, rows)
            c_ref[s, h] = c_new
            n_ref[s, h:h + 1, :] = n_new
            m_ref[s, h:h + 1, :] = jnp.broadcast_to(m_new, (1, LANES))
            outs.append(_sigmoid(og) * hh)
        o_ref[s] = jnp.concatenate(outs, axis=1)


def _state_specs(heads):
    return [pl.BlockSpec((SEQ_GROUP, heads, HEAD_DIM, HEAD_DIM), lambda b, c: (b, 0, 0, 0)),
            pl.BlockSpec((SEQ_GROUP, heads, HEAD_DIM), lambda b, c: (b, 0, 0)),
            pl.BlockSpec((SEQ_GROUP, heads, LANES), lambda b, c: (b, 0, 0))]


def mlstm_call(z, g2, g2t, c0, n0, m0, *, row0_first, transposed):
    nb, t_len, _ = z.shape
    n_chunks, rows = g2t.shape[1], g2t.shape[3]
    assert nb % SEQ_GROUP == 0 and n_chunks * rows == t_len
    blk = lambda b, c: (b, c, 0)
    return pl.pallas_call(
        functools.partial(_mlstm_kernel, rows=rows, row0_first=row0_first, transposed=transposed),
        grid=(nb // SEQ_GROUP, n_chunks),
        in_specs=[pl.BlockSpec((SEQ_GROUP, rows, 2048), blk),
                  pl.BlockSpec((SEQ_GROUP, rows, LANES), blk),
                  pl.BlockSpec((SEQ_GROUP, 1, 16, rows), lambda b, c: (b, c, 0, 0))] + _state_specs(M_HEADS),
        out_specs=[pl.BlockSpec((SEQ_GROUP, rows, 512), blk)] + _state_specs(M_HEADS),
        out_shape=[jax.ShapeDtypeStruct((nb, t_len, 512), F32),
                   jax.ShapeDtypeStruct((nb, M_HEADS, HEAD_DIM, HEAD_DIM), F32),
                   jax.ShapeDtypeStruct((nb, M_HEADS, HEAD_DIM), F32),
                   jax.ShapeDtypeStruct((nb, M_HEADS, LANES), F32)],
        compiler_params=pltpu.CompilerParams(dimension_semantics=("parallel", "arbitrary")),
        name="mlstm",
    )(z, g2, g2t, c0, n0, m0)


def _ret_log_gamma(h):
    return math.log1p(-(2.0 ** (-5.0 - h)))


def _ret_body(z_ref, cos_ref, sin_ref, lng_ref, o_ref, s_ref, *, rows, row0_first):
    c = pl.program_id(1)
    row0 = jnp.where(c == 0, row0_first, 0)
    n_valid = (rows - row0).astype(F32)
    cosf = cos_ref[...]
    sinf = sin_ref[...]
    ri = _iota((rows, rows), 0)
    ci = _iota((rows, rows), 1)
    diff = (ri - ci).astype(F32)
    rowi = _iota((rows, 1), 0)
    te = (rowi - row0).astype(F32)
    valid = rowi >= row0
    for h in range(R_HEADS):
        lg = _ret_log_gamma(h)
        lo = h * HEAD_DIM
        decay_m = jnp.where(diff >= 0, jnp.exp(diff * lg), 0.0)
        w_inter = jnp.exp((te + 1.0) * lg)
        w_src = jnp.exp((n_valid - 1.0 - te) * lg)
        s_decay = jnp.exp(n_valid * lg)
        for s in range(SEQ_GROUP):
            q = z_ref[s, :, lo:lo + HEAD_DIM]
            k = z_ref[s, :, 512 + lo:512 + lo + HEAD_DIM]
            v = jnp.where(valid, z_ref[s, :, 1024 + lo:1024 + lo + HEAD_DIM], 0.0)
            rg = z_ref[s, :, 1536 + lo:1536 + lo + HEAD_DIM]
            q = q * cosf + pltpu.roll(q, HEAD_DIM // 2, 1) * sinf
            k = (k * cosf + pltpu.roll(k, HEAD_DIM // 2, 1) * sinf) * (HEAD_DIM ** -0.5)
            s_state = s_ref[s, h]
            qb = _bf(q)
            vb = _bf(v)
            intra = _dot(_bf(_dot_nt(qb, _bf(k)) * decay_m), vb)
            inter = _dot(qb, _bf(s_state)) * w_inter
            s_ref[s, h] = s_decay * s_state + _dot_tn(_bf(k * w_src), vb)
            hr = intra + inter
            mu = jnp.mean(hr, axis=-1, keepdims=True)
            var = jnp.mean(jnp.square(hr - mu), axis=-1, keepdims=True)
            hr = (hr - mu) * lax.rsqrt(var + EPS) * lng_ref[h:h + 1, :]
            o_ref[s, :, lo:lo + HEAD_DIM] = rg * _sigmoid(rg) * hr


def _lru_body(z_ref, pre_ref, cw_ref, cb_ref, wa_ref, ba_ref, wx_ref, bx_ref, lam_ref,
              o_ref, hlast_ref, tail_ref, cbuf_s, a_s, b_s, *, rows, row0_first):
    c = pl.program_id(1)
    row0 = jnp.where(c == 0, row0_first, 0)
    rowi = _iota((rows, 1), 0)
    valid = rowi >= row0
    sp_lam = _softplus(-lam_ref[...])
    for s in range(SEQ_GROUP):
        lx = jnp.where(valid, z_ref[s, :, 0:LRU_WIDTH], pre_ref[min(s, pre_ref.shape[0] - 1)])
        cbuf_s[s, 0:8, :] = tail_ref[s]
        cbuf_s[s, 8:8 + rows, :] = lx
        xc = cb_ref[...] + cw_ref[CONV_W - 1:CONV_W, :] * lx
        for j in range(1, CONV_W):
            xc = xc + cw_ref[CONV_W - 1 - j:CONV_W - j, :] * cbuf_s[s, 8 - j:8 - j + rows, :]
        tail_ref[s] = lx[rows - 8:rows]
        xcb = _bf(xc)
        pre_a = []
        pre_x = []
        for p in range(LRU_WIDTH // LANES):
            xs = xcb[:, p * LANES:(p + 1) * LANES]
            pre_a.append(_dot(xs, wa_ref[p]))
            pre_x.append(_dot(xs, wx_ref[p]))
        r = _sigmoid(jnp.concatenate(pre_a, axis=1) + ba_ref[...])
        ig = _sigmoid(jnp.concatenate(pre_x, axis=1) + bx_ref[...])
        log_a = -LRU_C * r * sp_lam
        a = jnp.where(valid, jnp.exp(log_a), 1.0)
        one_minus_a2 = -jnp.tanh(log_a) * (jnp.exp(2.0 * log_a) + 1.0)
        bx = jnp.where(valid, jnp.sqrt(one_minus_a2) * (ig * xc), 0.0)
        a_s[s] = a
        b_s[s] = bx
    carries = [hlast_ref[s] for s in range(SEQ_GROUP)]
    for t in range(rows):
        for s in range(SEQ_GROUP):
            carries[s] = a_s[s, t:t + 1, :] * carries[s] + b_s[s, t:t + 1, :]
            b_s[s, t:t + 1, :] = carries[s]
    for s in range(SEQ_GROUP):
        hlast_ref[s] = carries[s]
        gate = z_ref[s, :, LRU_WIDTH:2 * LRU_WIDTH]
        gl = 0.5 * gate * (1.0 + jnp.tanh(math.sqrt(2.0 / math.pi) * (gate + 0.044715 * gate * gate * gate)))
        o_ref[s] = b_s[s] * gl


def _ret_lru_kernel(zr_ref, cos_ref, sin_ref, lng_ref, s0_ref, zl_ref, pre_ref, h0_ref,
                    cw_ref, cb_ref, wa_ref, ba_ref, wx_ref, bx_ref, lam_ref,
                    or_ref, s_ref, ol_ref, hlast_ref, tail_ref, cbuf_s, a_s, b_s, *, rows, row0_first):
    @pl.when(pl.program_id(1) == 0)
    def _():
        s_ref[...] = s0_ref[...]
        hlast_ref[...] = h0_ref[...]
        tail_ref[...] = jnp.zeros(tail_ref.shape, F32)

    _ret_body(zr_ref, cos_ref, sin_ref, lng_ref, or_ref, s_ref, rows=rows, row0_first=row0_first)
    _lru_body(zl_ref, pre_ref, cw_ref, cb_ref, wa_ref, ba_ref, wx_ref, bx_ref, lam_ref,
              ol_ref, hlast_ref, tail_ref, cbuf_s, a_s, b_s, rows=rows, row0_first=row0_first)


def ret_lru_call(zr, cosf, sinf, lng, s0, zl, pre, h0, cw, cb, wa2, ba, wx2, bx, lam, *, rows, row0_first):
    nb, t_len, _ = zr.shape
    assert nb % SEQ_GROUP == 0 and t_len % rows == 0
    blk = lambda b, c: (b, c, 0)
    per_seq = lambda b, c: (b, 0, 0)
    const2 = lambda b, c: (0, 0)
    const3 = lambda b, c: (0, 0, 0)
    sspec = pl.BlockSpec((SEQ_GROUP, R_HEADS, HEAD_DIM, HEAD_DIM), lambda b, c: (b, 0, 0, 0))
    shared_pre = pre.shape[0] == 1
    in_specs = [pl.BlockSpec((SEQ_GROUP, rows, 2048), blk),
                pl.BlockSpec((rows, HEAD_DIM), lambda b, c: (c, 0)),
                pl.BlockSpec((rows, HEAD_DIM), lambda b, c: (c, 0)),
                pl.BlockSpec((R_HEADS, HEAD_DIM), const2),
                sspec,
                pl.BlockSpec((SEQ_GROUP, rows, 1024), blk),
                pl.BlockSpec((1 if shared_pre else SEQ_GROUP, rows, LRU_WIDTH), const3 if shared_pre else per_seq),
                pl.BlockSpec((SEQ_GROUP, 1, LRU_WIDTH), per_seq),
                pl.BlockSpec((CONV_W, LRU_WIDTH), const2),
                pl.BlockSpec((1, LRU_WIDTH), const2),
                pl.BlockSpec((LRU_WIDTH // LANES, LANES, LANES), const3),
                pl.BlockSpec((1, LRU_WIDTH), const2),
                pl.BlockSpec((LRU_WIDTH // LANES, LANES, LANES), const3),
                pl.BlockSpec((1, LRU_WIDTH), const2),
                pl.BlockSpec((1, LRU_WIDTH), const2)]
    return pl.pallas_call(
        functools.partial(_ret_lru_kernel, rows=rows, row0_first=row0_first),
        grid=(nb // SEQ_GROUP, t_len // rows),
        in_specs=in_specs,
        out_specs=[pl.BlockSpec((SEQ_GROUP, rows, 512), blk), sspec,
                   pl.BlockSpec((SEQ_GROUP, rows, LRU_WIDTH), blk),
                   pl.BlockSpec((SEQ_GROUP, 1, LRU_WIDTH), per_seq),
                   pl.BlockSpec((SEQ_GROUP, 8, LRU_WIDTH), per_seq)],
        out_shape=[jax.ShapeDtypeStruct((nb, t_len, 512), F32),
                   jax.ShapeDtypeStruct((nb, R_HEADS, HEAD_DIM, HEAD_DIM), F32),
                   jax.ShapeDtypeStruct((nb, t_len, LRU_WIDTH), F32),
                   jax.ShapeDtypeStruct((nb, 1, LRU_WIDTH), F32),
                   jax.ShapeDtypeStruct((nb, 8, LRU_WIDTH), F32)],
        scratch_shapes=[pltpu.VMEM((SEQ_GROUP, rows + 8, LRU_WIDTH), F32),
                        pltpu.VMEM((SEQ_GROUP, rows, LRU_WIDTH), F32),
                        pltpu.VMEM((SEQ_GROUP, rows, LRU_WIDTH), F32)],
        compiler_params=pltpu.CompilerParams(dimension_semantics=("parallel", "arbitrary")),
        name="ret_lru",
    )(zr, cosf, sinf, lng, s0, zl, pre, h0, cw, cb, wa2, ba, wx2, bx, lam)


def _fox_cumsum_kernel(g_ref, gt_ref, fx_ref, fr_ref):
    ri = _iota((CHUNK, CHUNK), 0)
    ci = _iota((CHUNK, CHUNK), 1)
    lower = jnp.where(ci <= ri, 1.0, 0.0).astype(BF16)
    upper = jnp.where(ri <= ci, 1.0, 0.0).astype(BF16)
    pr = _iota((3 * LANES, F_HEADS // 2 * LANES), 0)
    pc = _iota((3 * LANES, F_HEADS // 2 * LANES), 1)
    term, gate = pr // LANES, pr % LANES
    pair, lane = pc // LANES, pc % LANES
    first_head = 2 * M_HEADS
    place = (((gate == first_head + 2 * pair) & (lane == F_DH + term))
             | ((gate == first_head + 2 * pair + 1) & (lane == term)))
    place = jnp.where(place, 1.0, 0.0).astype(BF16)
    carry_r = jnp.zeros((1, LANES), F32)
    carry_c = jnp.zeros((16, 1), F32)
    for blk in range(NCH):
        x = g_ref[blk * CHUNK:(blk + 1) * CHUNK, :]
        xt = gt_ref[:, blk * CHUNK:(blk + 1) * CHUNK]
        if blk == 0:
            x = jnp.where(_iota((CHUNK, 1), 0) >= PADF, x, 0.0)
            xt = jnp.where(_iota((1, CHUNK), 1) >= PADF, xt, 0.0)
        cs = _dot01_left(lower, x) + carry_r
        cst = _dot01_right(xt, upper) + carry_c
        carry_r = cs[CHUNK - 1:CHUNK, :]
        carry_c = cst[:, CHUNK - 1:CHUNK]
        fk = cs * LOG2E
        if blk == 0:
            cst = jnp.where(_iota((1, CHUNK), 1) >= PADF, cst, -NEG)
            fk = jnp.where(_iota((CHUNK, 1), 0) >= PADF, fk, -NEG)
        fx_ref[blk * CHUNK:(blk + 1) * CHUNK, :] = _dot(jnp.concatenate(_split3(fk), axis=1), place)
        fr_ref[0, :, blk * CHUNK:(blk + 1) * CHUNK] = cst


def fox_cumsum(g2, g2t):
    return pl.pallas_call(
        _fox_cumsum_kernel,
        grid=(BATCH,),
        in_specs=[pl.BlockSpec((TP, LANES), lambda b: (b, 0)),
                  pl.BlockSpec((16, TP), lambda b: (0, b))],
        out_specs=[pl.BlockSpec((TP, F_HEADS // 2 * LANES), lambda b: (b, 0)),
                   pl.BlockSpec((1, 16, TP), lambda b: (b, 0, 0))],
        out_shape=[jax.ShapeDtypeStruct((SBASE, F_HEADS // 2 * LANES), F32),
                   jax.ShapeDtypeStruct((BATCH, 16, TP), F32)],
        compiler_params=pltpu.CompilerParams(dimension_semantics=("parallel",)),
        name="fox_cumsum",
    )(g2, g2t)


def _fox_prompt_kernel(q_ref, k_ref, v_ref, fr_ref, fx_ref, o_ref,
                       qta_s, qtb_s, ka_s, kb_s, vt_s, m_s, l_s, acc_s, ua_s):
    i = pl.program_id(2)
    lane = _iota((1, LANES), 1)
    is_a = lane < F_DH
    row = _iota((LANES, 1), 0)
    row_a = row < F_DH
    k_heads = (ka_s, kb_s)
    qt_heads = (qta_s, qtb_s)

    @pl.when(i == 0)
    def _():
        k = k_ref[...]
        fx = fx_ref[...]
        ka_s[...] = _bf(jnp.where(is_a, k, fx))
        kb_s[...] = _bf(jnp.where(is_a, fx, k))

        minus_a = jnp.where((row >= F_DH) & (row < F_DH + 3), -1.0, 0.0)
        minus_b = jnp.where(row < 3, -1.0, 0.0)

        def fill(blk, carry):
            rows = pl.ds(pl.multiple_of(blk * CHUNK, CHUNK), CHUNK)
            qt = (q_ref[rows, :] * (F_DH ** -0.5 * LOG2E)).T
            qta_s[blk] = _bf(jnp.where(row_a, qt, minus_a))
            qtb_s[blk] = _bf(jnp.where(row_a, minus_b, qt))
            vt_s[blk] = _bf(v_ref[rows, :].T)
            return carry

        lax.fori_loop(0, NCH, fill, 0)

    def init():
        m_s[...] = jnp.full(m_s.shape, NEG, F32)
        l_s[...] = jnp.zeros(l_s.shape, F32)
        acc_s[...] = jnp.zeros(acc_s.shape, F32)

    def scores(hh, qt, kblk, nkb):
        ck = nkb * CHUNK
        k0 = kblk * CHUNK if isinstance(kblk, int) else pl.multiple_of(kblk * CHUNK, CHUNK)
        return _dot(k_heads[hh][pl.ds(k0, ck), :], qt[hh])

    def step(hh, u, qblk, nqb, kblks, diagonal):
        tq = nqb * CHUNK
        ck = len(kblks) * CHUNK
        fq = jnp.concatenate([fr_ref[0, 0, qblk + t, hh:hh + 1, :] for t in range(nqb)], axis=1)
        fq = jnp.where(fq > 0.5 * -NEG, 0.0, fq * LOG2E)
        if diagonal:
            lead = ck - tq
            ri = _iota((ck, tq), 0)
            u = jnp.where((ri < lead) | (ri - lead <= _iota((ck, tq), 1)), u, NEG)
        m_old = m_s[hh, :, 0:tq]
        m_new = jnp.maximum(m_old, jnp.max(u, axis=0, keepdims=True) + fq)
        m_s[hh, :, 0:tq] = m_new
        alpha = jnp.exp2(m_old - m_new)
        p = _bf(jnp.exp2(u + (fq - m_new)))
        feat = slice(hh * F_DH, (hh + 1) * F_DH)
        vt = jnp.concatenate([vt_s[kb, feat, :] for kb in kblks], axis=1)
        pv = _dot(jnp.concatenate([vt, jnp.ones((16, ck), BF16)], axis=0), p)
        l_s[hh, :, 0:tq] = alpha * l_s[hh, :, 0:tq] + pv[F_DH:F_DH + 1, :]
        acc_s[feat, 0:tq] = alpha * acc_s[feat, 0:tq] + pv[0:F_DH, :]

    def finalize(qblk, nqb):
        tq = nqb * CHUNK
        o_t = acc_s[:, 0:tq] / jnp.where(row_a, l_s[0, :, 0:tq], l_s[1, :, 0:tq])
        for t in range(nqb):
            rows = pl.ds(pl.multiple_of((qblk + t) * CHUNK, CHUNK), CHUNK)
            o_ref[rows, :] = o_t[:, t * CHUNK:(t + 1) * CHUNK].T

    @pl.when(i == 0)
    def _():
        init()
        qt0 = (qta_s[0], qtb_s[0])
        for hh in range(2):
            step(hh, scores(hh, qt0, 0, 1), 0, 1, [0], True)
        finalize(0, 1)

    nck = FOX_CK // CHUNK
    nqb = FOX_TQ // CHUNK
    qblk = 1 + i * nqb
    qt = tuple(jnp.concatenate([qt_h[qblk + t] for t in range(nqb)], axis=1) for qt_h in qt_heads)
    init()

    ua_s[...] = scores(0, qt, 1, nck)

    def body(j, carry):
        kblk = 1 + j * nck
        kblks = [kblk + t for t in range(nck)]
        ub = scores(1, qt, kblk, nck)
        step(0, ua_s[...], qblk, nqb, kblks, False)
        ua_s[...] = scores(0, qt, kblk + nck, nck)
        step(1, ub, qblk, nqb, kblks, False)
        return carry

    lax.fori_loop(0, i, body, 0)
    last = [0] + [qblk + t for t in range(nck)]
    ua = jnp.concatenate([scores(0, qt, 0, 1), ua_s[...]], axis=0)
    ub = jnp.concatenate([scores(1, qt, 0, 1), scores(1, qt, qblk, nck)], axis=0)
    step(0, ua, qblk, nqb, last, True)
    step(1, ub, qblk, nqb, last, True)
    finalize(qblk, nqb)


def fox_prompt(zq, zk, zv, fr, fx):
    nq = (TP - CHUNK) // FOX_TQ
    pairs = F_HEADS // 2
    return pl.pallas_call(
        _fox_prompt_kernel,
        grid=(BATCH, pairs, nq),
        in_specs=[pl.BlockSpec((TP, LANES), lambda b, p, i: (b, p)),
                  pl.BlockSpec((TP, LANES), lambda b, p, i: (b, p)),
                  pl.BlockSpec((TP, LANES), lambda b, p, i: (b, p)),
                  pl.BlockSpec((1, 1, NCH, 2, LANES), lambda b, p, i: (b, p, 0, 0, 0)),
                  pl.BlockSpec((TP, LANES), lambda b, p, i: (b, p))],
        out_specs=pl.BlockSpec((TP, LANES), lambda b, p, i: (b, p)),
        out_shape=jax.ShapeDtypeStruct((SBASE, 512), F32),
        scratch_shapes=[pltpu.VMEM((NCH, LANES, CHUNK), BF16),
                        pltpu.VMEM((NCH, LANES, CHUNK), BF16),
                        pltpu.VMEM((TP, LANES), BF16),
                        pltpu.VMEM((TP, LANES), BF16),
                        pltpu.VMEM((NCH, LANES, CHUNK), BF16),
                        pltpu.VMEM((2, 1, FOX_TQ), F32),
                        pltpu.VMEM((2, 1, FOX_TQ), F32),
                        pltpu.VMEM((LANES, FOX_TQ), F32),
                        pltpu.VMEM((FOX_CK, FOX_TQ), F32)],
        compiler_params=pltpu.CompilerParams(dimension_semantics=("parallel", "parallel", "arbitrary")),
        name="fox_prompt",
    )(zq, zk, zv, fr, fx)


def _lf_suffix_kernel(x_ref, suf_ref, tot_ref):
    ri = _iota((PAGE_SIZE, PAGE_SIZE), 0)
    ci = _iota((PAGE_SIZE, PAGE_SIZE), 1)
    after = jnp.where(ri > ci, 1.0, 0.0).astype(BF16)
    ones = jnp.ones((PAGE_SIZE, PAGE_SIZE), BF16)
    x = x_ref[...].reshape(PRE_PG * F_HEADS, PAGE_SIZE)
    suf_ref[...] = _dot01_right(x, after).reshape(PRE_PG, F_HEADS, PAGE_SIZE)
    tot_ref[...] = _dot01_right(x, ones).reshape(PRE_PG, F_HEADS, PAGE_SIZE)


def lf_suffix(lft):
    n_phys = lft.shape[0]
    spec = pl.BlockSpec((PRE_PG, F_HEADS, PAGE_SIZE), lambda i: (i, 0, 0))
    return pl.pallas_call(
        _lf_suffix_kernel,
        grid=(n_phys // PRE_PG,),
        in_specs=[spec],
        out_specs=[spec, spec],
        out_shape=[jax.ShapeDtypeStruct(lft.shape, F32)] * 2,
        compiler_params=pltpu.CompilerParams(dimension_semantics=("parallel",)),
        name="lf_suffix",
    )(lft)


def _fox_decode_kernel(pt_ref, zq_ref, zk_ref, zv_ref, gt_ref, *rest):
    k_refs = rest[0:DEC_PG]
    v_refs = rest[DEC_PG:2 * DEC_PG]
    suf_refs = rest[2 * DEC_PG:3 * DEC_PG]
    tot_refs = rest[3 * DEC_PG:4 * DEC_PG]
    o_ref = rest[4 * DEC_PG]
    q2_s, m_s, l_s, acc_s, run_s, nc_s = rest[4 * DEC_PG + 1:]
    j = pl.program_id(1)
    n_rows = DEC_SEQ * F_HEADS
    width = F_HEADS * F_DH

    def update(u, v, v_feature_major):
        m_old = m_s[...]
        m_new = jnp.maximum(m_old, jnp.max(u, axis=1, keepdims=True))
        alpha = jnp.exp(m_old - m_new)
        p = jnp.exp(u - m_new[:, 0:u.shape[1]])
        pv = _dot_nt(_bf(p), v) if v_feature_major else _dot(_bf(p), v)
        l_s[...] = alpha * l_s[...] + jnp.sum(p, axis=1, keepdims=True)
        acc_s[...] = jnp.concatenate([alpha] * (width // LANES), axis=1) * acc_s[...] + pv
        m_s[...] = m_new

    @pl.when(j == 0)
    def _():
        m_s[...] = jnp.full(m_s.shape, NEG, F32)
        l_s[...] = jnp.zeros(l_s.shape, F32)
        acc_s[...] = jnp.zeros(acc_s.shape, F32)
        run_s[...] = jnp.zeros(run_s.shape, F32)
        head_mask = _iota((F_HEADS, width), 1) // F_DH == _iota((F_HEADS, width), 0)
        q16 = zq_ref[...] * (F_DH ** -0.5)
        q2 = _bf(jnp.concatenate(
            [jnp.where(head_mask, q16[SPAD + t:SPAD + t + 1, :], 0.0) for t in range(DEC_SEQ)], axis=0))
        q2_s[...] = q2
        lf_new = gt_ref[F_HEADS:2 * F_HEADS, :]
        lane = _iota((1, SROWS), 1)
        cum = jnp.zeros((F_HEADS, SROWS), F32)
        for t in range(DEC_SEQ):
            cum = cum + jnp.where(lane >= SPAD + t, lf_new[:, SPAD + t:SPAD + t + 1], 0.0)
        nc = jnp.concatenate([cum[:, SPAD + t:SPAD + t + 1] for t in range(DEC_SEQ)], axis=0)
        nc_s[...] = jnp.broadcast_to(nc, nc_s.shape)
        cum4 = jnp.concatenate([cum] * DEC_SEQ, axis=0)
        qi = _iota((n_rows, SROWS), 0) // F_HEADS
        kj = _iota((n_rows, SROWS), 1) - SPAD
        u = jnp.where((kj >= 0) & (kj <= qi), _dot_nt(q2, _bf(zk_ref[...])) + (nc - cum4), NEG)
        update(u, _bf(zv_ref[...]), False)

    q2 = q2_s[...]
    nc = nc_s[...]
    run = run_s[...]
    us = []
    for i in range(DEC_PG):
        later = run + suf_refs[i][0]
        bias = jnp.concatenate([later] * DEC_SEQ, axis=0) + nc
        us.append(_dot(q2, _bf(k_refs[i][0].reshape(width, PAGE_SIZE))) + bias)
        run = run + tot_refs[i][0]
    run_s[...] = run
    u = jnp.concatenate(us, axis=1)
    m_old = m_s[...]
    m_new = jnp.maximum(m_old, jnp.max(u, axis=1, keepdims=True))
    alpha = jnp.exp(m_old - m_new)
    p = jnp.exp(u - jnp.concatenate([m_new] * DEC_PG, axis=1))
    l_s[...] = alpha * l_s[...] + jnp.sum(p, axis=1, keepdims=True)
    pv = _dot_nt(_bf(p[:, 0:PAGE_SIZE]), _bf(v_refs[0][0].reshape(width, PAGE_SIZE)))
    for i in range(1, DEC_PG):
        pv = pv + _dot_nt(_bf(p[:, i * PAGE_SIZE:(i + 1) * PAGE_SIZE]), _bf(v_refs[i][0].reshape(width, PAGE_SIZE)))
    acc_s[...] = jnp.concatenate([alpha] * (width // LANES), axis=1) * acc_s[...] + pv
    m_s[...] = m_new

    @pl.when(j == pl.num_programs(1) - 1)
    def _():
        head_mask = (_iota((n_rows, width), 1) // F_DH) == (_iota((n_rows, width), 0) % F_HEADS)
        o2 = jnp.where(head_mask, acc_s[...] / jnp.concatenate([l_s[...]] * (width // LANES), axis=1), 0.0)
        o_ref[...] = jnp.zeros(o_ref.shape, F32)
        for t in range(DEC_SEQ):
            o_ref[SPAD + t:SPAD + t + 1, :] = jnp.sum(o2[t * F_HEADS:(t + 1) * F_HEADS, :], axis=0, keepdims=True)


def fox_decode(page_table, zq, zk, zv, g2t_s, cache_kt, cache_vt, suf, tot):
    n_pages = page_table.shape[1]
    width = F_HEADS * F_DH

    def page_map(i):
        return lambda b, j, pt: (pt[b, n_pages - 1 - (j * DEC_PG + i)], 0, 0, 0)

    def page_map3(i):
        return lambda b, j, pt: (pt[b, n_pages - 1 - (j * DEC_PG + i)], 0, 0)

    in_specs = [pl.BlockSpec((SROWS, width), lambda b, j, pt: (b, 0)),
                pl.BlockSpec((SROWS, width), lambda b, j, pt: (b, 0)),
                pl.BlockSpec((SROWS, width), lambda b, j, pt: (b, 0)),
                pl.BlockSpec((None, 16, SROWS), lambda b, j, pt: (b, 0, 0))]
    in_specs += [pl.BlockSpec((1, F_HEADS, F_DH, PAGE_SIZE), page_map(i)) for i in range(DEC_PG)]
    in_specs += [pl.BlockSpec((1, F_HEADS, F_DH, PAGE_SIZE), page_map(i)) for i in range(DEC_PG)]
    in_specs += [pl.BlockSpec((1, F_HEADS, PAGE_SIZE), page_map3(i)) for i in range(DEC_PG)]
    in_specs += [pl.BlockSpec((1, F_HEADS, PAGE_SIZE), page_map3(i)) for i in range(DEC_PG)]
    args = [zq, zk, zv, g2t_s] + [cache_kt] * DEC_PG + [cache_vt] * DEC_PG + [suf] * DEC_PG + [tot] * DEC_PG
    n_rows = DEC_SEQ * F_HEADS
    grid_spec = pltpu.PrefetchScalarGridSpec(
        num_scalar_prefetch=1,
        grid=(DEC_BATCH, n_pages // DEC_PG),
        in_specs=in_specs,
        out_specs=pl.BlockSpec((SROWS, width), lambda b, j, pt: (b, 0)),
        scratch_shapes=[pltpu.VMEM((n_rows, width), BF16),
                        pltpu.VMEM((n_rows, LANES), F32),
                        pltpu.VMEM((n_rows, LANES), F32),
                        pltpu.VMEM((n_rows, width), F32),
                        pltpu.VMEM((F_HEADS, PAGE_SIZE), F32),
                        pltpu.VMEM((n_rows, LANES), F32)])
    return pl.pallas_call(
        _fox_decode_kernel,
        grid_spec=grid_spec,
        out_shape=jax.ShapeDtypeStruct((DEC_BATCH * SROWS, width), F32),
        compiler_params=pltpu.CompilerParams(dimension_semantics=("parallel", "arbitrary")),
        name="fox_decode",
    )(page_table, *args)


def _block_diag_pairs(w):
    z = jnp.zeros((LANES // 2, LANES // 2), w.dtype)
    pairs = [jnp.block([[w[2 * p], z], [z, w[2 * p + 1]]]) for p in range(w.shape[0] // 2)]
    return _bf(jnp.stack(pairs))


def _rope_tables(pos):
    half = HEAD_DIM // 2
    freq = ROPE_BASE ** (-jnp.arange(half, dtype=F32) / half)
    ang = pos.astype(F32)[:, None] * freq[None, :]
    cos, sin = jnp.cos(ang), jnp.sin(ang)
    return jnp.concatenate([cos, cos], axis=1), jnp.concatenate([-sin, sin], axis=1)


def _unpad_prompt(x):
    return x.reshape((BATCH, TP) + x.shape[1:])[:, PADF:]


def _unpad_sample(x):
    return x.reshape((DEC_BATCH, SROWS) + x.shape[1:])[:, SPAD:]


def kernel(x_prompt, x_sample, cache_fox_k, cache_fox_v, cache_fox_logf, state_mlstm_C, state_mlstm_n, state_mlstm_m, state_ret_S, state_lru_h, state_lru_conv, page_table, meta_tokens, w_in_even, b_mlstm_i, b_mlstm_f, b_fox_f, w_out_even, w_in_odd, ret_ln_g, conv_w, conv_b, lru_wa, lru_ba, lru_wx, lru_bx, lru_lambda, w_out_odd, norm_g, ffn_wg, ffn_wu, ffn_wd):
    n_pages = page_table.shape[1]
    past = n_pages * PAGE_SIZE
    n_phys = cache_fox_k.shape[0]

    head = jnp.concatenate([jnp.zeros((PADF, D_MODEL), F32), meta_tokens.astype(F32)], axis=0)
    hs = jnp.concatenate([jnp.zeros((DEC_BATCH, SPAD, D_MODEL), F32), x_sample.astype(F32)],
                         axis=1).reshape(DEC_BATCH * SROWS, D_MODEL)
    h_s = hs
    h_p = jnp.concatenate([jnp.broadcast_to(head[None], (BATCH, CHUNK, D_MODEL)), x_prompt.astype(F32)],
                          axis=1).reshape(SBASE, D_MODEL)

    c = np.cumsum([0, 512, 512, 512, 512, 4, 4, 512, 512, 512, 8])
    gate_w = jnp.concatenate([w_in_even[:, c[4]:c[6]], w_in_even[:, c[9]:c[10]],
                              jnp.zeros((D_MODEL, LANES - 16), F32)], axis=1)
    w_even = _bf(jnp.concatenate([w_in_even[:, c[0]:c[4]], w_in_even[:, c[6]:c[9]], gate_w], axis=1))
    gate_b = jnp.concatenate([b_mlstm_i, b_mlstm_f, b_fox_f]).astype(F32)
    brow = jnp.concatenate([gate_b, jnp.zeros((LANES - 16,), F32)]).reshape(1, LANES)
    w_odd = _bf(w_in_odd)
    wo_even = _bf(w_out_even)
    wo_odd = _bf(w_out_odd)
    wg = _bf(ffn_wg)
    wu = _bf(ffn_wu)
    wd = _bf(ffn_wd)
    ng = norm_g.astype(F32).reshape(2, 4, 1, D_MODEL)

    even_cols = (2048, 512, 512, 512, LANES)
    zm_p, zq_p, zk_p, zv_p, g2_p, g2t_p = norm_matmul(h_p, ng[0, 0], w_even, even_cols, gate_bias=brow)
    zm_s, zq_s, zk_s, zv_s, g2_s, g2t_s = norm_matmul(h_s, ng[0, 0], w_even, even_cols, gate_bias=brow)
    prompt3 = lambda x: x.reshape(BATCH, TP, x.shape[-1])
    sample3 = lambda x: x.reshape(DEC_BATCH, SROWS, x.shape[-1])
    flat2 = lambda x: x.reshape(-1, x.shape[-1])

    zeros_c = jnp.zeros((BATCH, M_HEADS, HEAD_DIM, HEAD_DIM), F32)
    zeros_n = jnp.zeros((BATCH, M_HEADS, HEAD_DIM), F32)
    hm_p, p_c, p_n, p_m = mlstm_call(prompt3(zm_p), prompt3(g2_p),
                                     g2t_p.reshape(16, BATCH, NCH, CHUNK).transpose(1, 2, 0, 3),
                                     zeros_c, zeros_n, zeros_n, row0_first=PADF, transposed=True)
    p_c = p_c.transpose(0, 1, 3, 2)
    m0_s = jnp.broadcast_to(state_mlstm_m.astype(F32)[:, :, None], (DEC_BATCH, M_HEADS, LANES))
    hm_s, s_c, s_n, s_m = mlstm_call(sample3(zm_s), sample3(g2_s),
                                     g2t_s.reshape(16, DEC_BATCH, 1, SROWS).transpose(1, 2, 0, 3),
                                     state_mlstm_C.astype(F32), state_mlstm_n.astype(F32), m0_s, row0_first=SPAD,
                                     transposed=False)

    fx, fr = fox_cumsum(g2_p, g2t_p)
    pairs = F_HEADS // 2
    fr_p = fr[:, 8:16].reshape(BATCH, pairs, 2, NCH, CHUNK).transpose(0, 1, 3, 2, 4)
    hf_p = fox_prompt(zq_p, zk_p, zv_p, fr_p, fx)
    suf, tot = lf_suffix(cache_fox_logf.astype(F32).transpose(0, 2, 1))
    hf_s = fox_decode(page_table, zq_s, zk_s, zv_s, g2t_s.reshape(16, DEC_BATCH, SROWS).transpose(1, 0, 2),
                      cache_fox_k.astype(F32).transpose(0, 2, 3, 1), cache_fox_v.astype(F32).transpose(0, 2, 3, 1),
                      suf, tot)

    gains = norm_g.astype(F32)
    even_w = (wo_even[:512], wo_even[512:], gains[0, 1:4], wg[0], wu[0], wd[0])
    h_p = mix_ffn(flat2(hm_p), hf_p, h_p, *even_w)
    h_s = mix_ffn(flat2(hm_s), hf_s, h_s, *even_w)

    zr_p, zl_p = norm_matmul(h_p, ng[1, 0], w_odd, (2048, 1024))
    zr_s, zl_s = norm_matmul(h_s, ng[1, 0], w_odd, (2048, 1024))
    cos_p, sin_p = _rope_tables(jnp.arange(TP) - PADF)
    cos_s, sin_s = _rope_tables(past + jnp.arange(SROWS) - SPAD)
    lng = ret_ln_g.astype(F32)
    lru_args = (conv_w.astype(F32), conv_b.astype(F32).reshape(1, LRU_WIDTH),
                _block_diag_pairs(lru_wa), lru_ba.astype(F32).reshape(1, LRU_WIDTH),
                _block_diag_pairs(lru_wx), lru_bx.astype(F32).reshape(1, LRU_WIDTH),
                lru_lambda.astype(F32).reshape(1, LRU_WIDTH))
    hr_p, p_s, hl_p, p_h, p_tail = ret_lru_call(
        prompt3(zr_p), cos_p, sin_p, lng, jnp.zeros((BATCH, R_HEADS, HEAD_DIM, HEAD_DIM), F32),
        prompt3(zl_p), jnp.zeros((1, CHUNK, LRU_WIDTH), F32), jnp.zeros((BATCH, 1, LRU_WIDTH), F32), *lru_args,
        rows=CHUNK, row0_first=PADF)
    pre_s = jnp.concatenate([jnp.zeros((DEC_BATCH, SPAD - (CONV_W - 1), LRU_WIDTH), F32),
                             state_lru_conv.astype(F32),
                             jnp.zeros((DEC_BATCH, DEC_SEQ, LRU_WIDTH), F32)], axis=1)
    hr_s, s_s, hl_s, s_h, s_tail = ret_lru_call(
        sample3(zr_s), cos_s, sin_s, lng, state_ret_S.astype(F32),
        sample3(zl_s), pre_s, state_lru_h.astype(F32).reshape(DEC_BATCH, 1, LRU_WIDTH), *lru_args,
        rows=SROWS, row0_first=SPAD)

    odd_w = (wo_odd[:512], wo_odd[512:], gains[1, 1:4], wg[1], wu[1], wd[1])
    y_prompt = mix_ffn(hr_p, hl_p, prompt3(h_p), *odd_w, main_only=True)
    h_s = mix_ffn(flat2(hr_s), flat2(hl_s), h_s, *odd_w)

    y_sample = h_s.reshape(DEC_BATCH, SROWS, D_MODEL)[:, SPAD:]
    heads = lambda x: x.reshape(x.shape[:2] + (F_HEADS, F_DH))
    nback = CONV_W - 1
    return (y_prompt, y_sample,
            heads(_unpad_prompt(zk_p)), heads(_unpad_prompt(zv_p)), _unpad_prompt(g2_p[:, 8:16]),
            heads(_unpad_sample(zk_s)), heads(_unpad_sample(zv_s)), _unpad_sample(g2_s[:, 8:16]),
            p_c, p_n, p_m[:, :, 0], s_c, s_n, s_m[:, :, 0],
            p_s, s_s,
            p_h[:, 0], s_h[:, 0], p_tail[:, 8 - nback:], s_tail[:, 8 - nback:])
```

```python
import functools
import math

import jax
import jax.numpy as jnp
import numpy as np
from jax import lax
from jax.experimental import pallas as pl
from jax.experimental.pallas import tpu as pltpu

F32 = jnp.float32
BF16 = jnp.bfloat16

D_MODEL = 1024
BATCH = 4
SEQ = 4096
DEC_BATCH = 32
DEC_SEQ = 4
PAGE_SIZE = 128
N_META = 16
CHUNK = 128
M_HEADS = 4
F_HEADS = 8
F_DH = 64
R_HEADS = 4
HEAD_DIM = 128
ROPE_BASE = 10000.0
LRU_WIDTH = 512
LRU_C = 8.0
CONV_W = 4
D_FF = 2816
EPS = 1e-6
NEG = -1e30

LANES = 128
PADF = CHUNK - N_META
TP = PADF + N_META + SEQ
NCH = TP // CHUNK
SROWS = 16
SPAD = SROWS - DEC_SEQ
SBASE = BATCH * TP
NP = SBASE + DEC_BATCH * SROWS
TM = 512
SEQ_GROUP = 4
MXU_WIDTH = 256
MIX_FFN_VMEM_BYTES = 56 * 1024 * 1024
FOX_TQ = 512
FOX_CK = 512
LOG2E = math.log2(math.e)
DEC_PG = 32
PRE_PG = 128


def _bf(x):
    return x.astype(BF16)


def _dot(a, b):
    return jnp.dot(a, b, preferred_element_type=F32)


def _dot_nt(a, b):
    return lax.dot_general(a, b, (((1,), (1,)), ((), ())), preferred_element_type=F32)


def _dot_tn(a, b):
    return lax.dot_general(a, b, (((0,), (0,)), ((), ())), preferred_element_type=F32)


def _split3(x):
    hi = _bf(x)
    r1 = x - hi.astype(F32)
    mid = _bf(r1)
    lo = _bf(r1 - mid.astype(F32))
    return hi, mid, lo


def _dot01_right(x, m01):
    hi, mid, lo = _split3(x)
    return _dot(hi, m01) + _dot(mid, m01) + _dot(lo, m01)


def _dot01_left(m01, x):
    hi, mid, lo = _split3(x)
    return _dot(m01, hi) + _dot(m01, mid) + _dot(m01, lo)


def _iota(shape, dim):
    return lax.broadcasted_iota(jnp.int32, shape, dim)


def _rms(x, g):
    ms = jnp.mean(x * x, axis=-1, keepdims=True)
    return x * lax.rsqrt(ms + EPS) * g


def _softplus(x):
    return jnp.maximum(x, 0.0) + jnp.log1p(jnp.exp(-jnp.abs(x)))


def _log_sigmoid(x):
    return -_softplus(-x)


def _sigmoid(x):
    return 0.5 * jnp.tanh(0.5 * x) + 0.5


def _nm_kernel(x_ref, g_ref, w_ref, *rest, splits, gated):
    if gated:
        b_ref, o_refs = rest[0], rest[1:]
    else:
        o_refs = rest
    xn = _bf(_rms(x_ref[...], g_ref[...]))
    off = 0
    for k, n in enumerate(splits):
        res = _dot(xn, w_ref[:, off:off + n])
        off += n
        if gated and k == len(splits) - 1:
            res = res + b_ref[...]
            res = jnp.where(_iota(res.shape, 1) < M_HEADS, res, _log_sigmoid(res))
            o_refs[k + 1][...] = res.T[0:16, :]
        o_refs[k][...] = res


def norm_matmul(x, g, w, splits, gate_bias=None):
    n_rows, k = x.shape
    const = lambda i: (0, 0)
    gated = gate_bias is not None
    in_specs = [pl.BlockSpec((TM, k), lambda i: (i, 0)),
                pl.BlockSpec((1, k), const),
                pl.BlockSpec((k, sum(splits)), const, pipeline_mode=pl.Buffered(1))]
    out_specs = [pl.BlockSpec((TM, n), lambda i: (i, 0)) for n in splits]
    out_shape = [jax.ShapeDtypeStruct((n_rows, n), F32) for n in splits]
    args = [x, g, w]
    if gated:
        assert splits[-1] == LANES
        in_specs.append(pl.BlockSpec((1, LANES), const))
        out_specs.append(pl.BlockSpec((16, TM), lambda i: (0, i)))
        out_shape.append(jax.ShapeDtypeStruct((16, n_rows), F32))
        args.append(gate_bias)
    return pl.pallas_call(
        functools.partial(_nm_kernel, splits=splits, gated=gated),
        grid=(n_rows // TM,),
        in_specs=in_specs,
        out_specs=out_specs,
        out_shape=out_shape,
        compiler_params=pltpu.CompilerParams(dimension_semantics=("parallel",)),
        name="norm_matmul",
    )(*args)


def _mix_ffn_kernel(*refs, parts):
    def rows(piece_refs):
        if parts == 1:
            return piece_refs[0][...]
        return jnp.concatenate([r[0] for r in piece_refs], axis=0)

    a1 = rows(refs[0:parts])
    a2 = rows(refs[parts:2 * parts])
    h = rows(refs[2 * parts:3 * parts])
    w1_ref, w2_ref, g_ref, wg_ref, wu_ref, wd_ref, o_ref = refs[3 * parts:]
    mix = _dot(_bf(a1), w1_ref[...]) + _dot(_bf(a2), w2_ref[...])
    h1 = h + _rms(mix, g_ref[0:1, :])
    xn = _bf(_rms(h1, g_ref[1:2, :]))
    acts = []
    for c in range(wg_ref.shape[1] // MXU_WIDTH):
        cols = slice(c * MXU_WIDTH, (c + 1) * MXU_WIDTH)
        gate = _dot(xn, wg_ref[:, cols])
        up = _dot(xn, wu_ref[:, cols])
        acts.append(_bf(gate * _sigmoid(gate) * up))
    ff = _dot(jnp.concatenate(acts, axis=1), wd_ref[...])
    res = h1 + _rms(ff, g_ref[2:3, :])
    o_ref[...] = res if parts == 1 else res[None]


def mix_ffn(a1, a2, h, w1, w2, g, wg, wu, wd, main_only=False):
    k1 = a1.shape[-1]
    k2 = a2.shape[-1]
    d = h.shape[-1]
    dff = wg.shape[1]
    assert dff % MXU_WIDTH == 0
    resident = lambda shape: pl.BlockSpec(shape, lambda *_: (0,) * len(shape), pipeline_mode=pl.Buffered(1))
    weights = [resident((k1, d)), resident((k2, d)), resident((3, d)),
               resident((d, dff)), resident((d, dff)), resident((dff, d))]
    if main_only:
        parts = TM // CHUNK
        nb, t_len = h.shape[0], h.shape[1] - CHUNK
        assert t_len % TM == 0
        piece = lambda k, t: pl.BlockSpec((1, CHUNK, k), lambda b, j: (b, 1 + j * parts + t, 0))
        in_specs = ([piece(k1, t) for t in range(parts)] + [piece(k2, t) for t in range(parts)]
                    + [piece(d, t) for t in range(parts)] + weights)
        args = [a1] * parts + [a2] * parts + [h] * parts
        grid = (nb, t_len // TM)
        out_specs = pl.BlockSpec((1, TM, d), lambda b, j: (b, j, 0))
        out_shape = jax.ShapeDtypeStruct((nb, t_len, d), F32)
    else:
        parts = 1
        n_rows = h.shape[0]
        assert n_rows % TM == 0
        row_blk = lambda i: (i, 0)
        in_specs = [pl.BlockSpec((TM, k1), row_blk), pl.BlockSpec((TM, k2), row_blk),
                    pl.BlockSpec((TM, d), row_blk)] + weights
        args = [a1, a2, h]
        grid = (n_rows // TM,)
        out_specs = pl.BlockSpec((TM, d), row_blk)
        out_shape = jax.ShapeDtypeStruct((n_rows, d), F32)
    return pl.pallas_call(
        functools.partial(_mix_ffn_kernel, parts=parts),
        grid=grid,
        in_specs=in_specs,
        out_specs=out_specs,
        out_shape=out_shape,
        compiler_params=pltpu.CompilerParams(dimension_semantics=("parallel",) * len(grid),
                                             vmem_limit_bytes=MIX_FFN_VMEM_BYTES),
        name="mix_ffn",
    )(*args, w1, w2, g, wg, wu, wd)


def _mlstm_chunk(q, k, v, logi_c, lf_c, logi_r, lf_r, c_state, n_state, m_state, row0, rows):
    ri = _iota((rows, rows), 0)
    ci = _iota((rows, rows), 1)
    valid_c = _iota((rows, 1), 0) >= row0
    valid_r = _iota((1, rows), 1) >= row0
    lf_c = jnp.where(valid_c, lf_c, 0.0)
    lf_r = jnp.where(valid_r, lf_r, 0.0)
    logi_c = jnp.where(valid_c, logi_c, NEG)
    logi_r = jnp.where(valid_r, logi_r, NEG)
    causal = ci <= ri
    b_c = jnp.sum(jnp.where(causal, lf_r, 0.0), axis=1, keepdims=True)
    b_r = jnp.sum(jnp.where(ri <= ci, lf_c, 0.0), axis=0, keepdims=True)
    log_d = jnp.where(causal, b_c - b_r + logi_r, -jnp.inf)
    m_inter = b_c + m_state
    m_t = jnp.maximum(m_inter, jnp.max(log_d, axis=1, keepdims=True))
    qb = _bf(q)
    kb = _bf(k)
    vb = _bf(v)
    w_intra = _dot_nt(qb, kb) * jnp.exp(log_d - m_t)
    w_inter = jnp.exp(m_inter - m_t)
    num = _dot(_bf(w_intra), vb) + w_inter * _dot(qb, _bf(c_state))
    den = jnp.sum(w_intra, axis=1, keepdims=True) + w_inter * jnp.sum(q * n_state, axis=1, keepdims=True)
    h = num / jnp.maximum(jnp.abs(den), jnp.exp(-m_t))
    m_new = m_t[rows - 1:rows, :]
    b_last = b_c[rows - 1:rows, :]
    w_src = jnp.exp(b_last - b_c + logi_c - m_new)
    decay = jnp.exp(b_last + m_state - m_new)
    ks = k * w_src
    c_new = decay * c_state + _dot_tn(_bf(ks), vb)
    n_new = decay * n_state + jnp.sum(ks, axis=0, keepdims=True)
    return h, c_new, n_new, m_new


def _mlstm_chunk_t(q, k, v, logi_r, lf_r, ct_state, n_state, m_state, row0, rows):
    ri = _iota((rows, rows), 0)
    ci = _iota((rows, rows), 1)
    valid_r = _iota((1, rows), 1) >= row0
    lf_r = jnp.where(valid_r, lf_r, 0.0)
    logi_r = jnp.where(valid_r, logi_r, NEG)
    upper = jnp.where(ri <= ci, 1.0, 0.0).astype(BF16)
    a = jnp.where(ci > ri, lf_r, jnp.where(ci == ri, logi_r, 0.0))
    pre = _dot01_right(jnp.concatenate([a, jnp.broadcast_to(lf_r, (8, rows))], axis=0), upper)
    b_r = pre[rows:rows + 1, :]
    log_d = jnp.where(ri <= ci, pre[0:rows, :], -jnp.inf)
    m_inter = b_r + m_state
    m_t = jnp.maximum(m_inter, jnp.max(log_d, axis=0, keepdims=True))
    qb = _bf(q)
    kb = _bf(k)
    w = _dot_nt(kb, qb) * jnp.exp(log_d - m_t)
    w_inter = jnp.exp(m_inter - m_t)
    vt = v.T
    num_t = _dot(_bf(vt), _bf(w)) + w_inter * _dot_nt(_bf(ct_state), qb)
    nq = _dot_nt(_bf(jnp.broadcast_to(n_state, (8, HEAD_DIM))), qb)[0:1, :]
    den = jnp.sum(w, axis=0, keepdims=True) + w_inter * nq
    h_t = num_t / jnp.maximum(jnp.abs(den), jnp.exp(-m_t))
    m_new = m_t[:, rows - 1:rows]
    b_last = b_r[:, rows - 1:rows]
    w_src = jnp.exp(b_last - b_r + logi_r - m_new)
    decay = jnp.exp(b_last + m_state - m_new)
    ct_new = decay * ct_state + _dot(_bf(vt * w_src), kb)
    n_new = decay * n_state + _dot(_bf(jnp.broadcast_to(w_src, (8, rows))), kb)[0:1, :]
    return h_t.T, ct_new, n_new, m_new


def _mlstm_kernel(z_ref, g_ref, gt_ref, c0_ref, n0_ref, m0_ref, o_ref, c_ref, n_ref, m_ref, *,
                  rows, row0_first, transposed):
    c = pl.program_id(1)

    @pl.when(c == 0)
    def _():
        c_ref[...] = c0_ref[...]
        n_ref[...] = n0_ref[...]
        m_ref[...] = m0_ref[...]

    row0 = jnp.where(c == 0, row0_first, 0)
    for s in range(SEQ_GROUP):
        g = g_ref[s]
        gt = gt_ref[s, 0]
        outs = []
        for h in range(M_HEADS):
            lo = h * HEAD_DIM
            q = z_ref[s, :, lo:lo + HEAD_DIM]
            k = z_ref[s, :, 512 + lo:512 + lo + HEAD_DIM] * (HEAD_DIM ** -0.5)
            v = z_ref[s, :, 1024 + lo:1024 + lo + HEAD_DIM]
            og = z_ref[s, :, 1536 + lo:1536 + lo + HEAD_DIM]
            state = (c_ref[s, h], n_ref[s, h:h + 1, :], m_ref[s, h:h + 1, 0:1])
            if transposed:
                hh, c_new, n_new, m_new = _mlstm_chunk_t(
                    q, k, v, gt[h:h + 1, :], gt[M_HEADS + h:M_HEADS + h + 1, :], *state, row0, rows)
            else:
                hh, c_new, n_new, m_new = _mlstm_chunk(
                    q, k, v,
                    g[:, h:h + 1], g[:, M_HEADS + h:M_HEADS + h + 1],
                    gt[h:h + 1, :], gt[M_HEADS + h:M_HEADS + h + 1, :], *state, row0, rows)
            c_ref[s, h] = c_new
            n_ref[s, h:h + 1, :] = n_new
            m_ref[s, h:h + 1, :] = jnp.broadcast_to(m_new, (1, LANES))
            outs.append(_sigmoid(og) * hh)
        o_ref[s] = jnp.concatenate(outs, axis=1)


def _state_specs(heads):
    return [pl.BlockSpec((SEQ_GROUP, heads, HEAD_DIM, HEAD_DIM), lambda b, c: (b, 0, 0, 0)),
            pl.BlockSpec((SEQ_GROUP, heads, HEAD_DIM), lambda b, c: (b, 0, 0)),
            pl.BlockSpec((SEQ_GROUP, heads, LANES), lambda b, c: (b, 0, 0))]


def mlstm_call(z, g2, g2t, c0, n0, m0, *, row0_first, transposed):
    nb, t_len, _ = z.shape
    n_chunks, rows = g2t.shape[1], g2t.shape[3]
    assert nb % SEQ_GROUP == 0 and n_chunks * rows == t_len
    blk = lambda b, c: (b, c, 0)
    return pl.pallas_call(
        functools.partial(_mlstm_kernel, rows=rows, row0_first=row0_first, transposed=transposed),
        grid=(nb // SEQ_GROUP, n_chunks),
        in_specs=[pl.BlockSpec((SEQ_GROUP, rows, 2048), blk),
                  pl.BlockSpec((SEQ_GROUP, rows, LANES), blk),
                  pl.BlockSpec((SEQ_GROUP, 1, 16, rows), lambda b, c: (b, c, 0, 0))] + _state_specs(M_HEADS),
        out_specs=[pl.BlockSpec((SEQ_GROUP, rows, 512), blk)] + _state_specs(M_HEADS),
        out_shape=[jax.ShapeDtypeStruct((nb, t_len, 512), F32),
                   jax.ShapeDtypeStruct((nb, M_HEADS, HEAD_DIM, HEAD_DIM), F32),
                   jax.ShapeDtypeStruct((nb, M_HEADS, HEAD_DIM), F32),
                   jax.ShapeDtypeStruct((nb, M_HEADS, LANES), F32)],
        compiler_params=pltpu.CompilerParams(dimension_semantics=("parallel", "arbitrary")),
        name="mlstm",
    )(z, g2, g2t, c0, n0, m0)


def _ret_log_gamma(h):
    return math.log1p(-(2.0 ** (-5.0 - h)))


def _ret_body(z_ref, cos_ref, sin_ref, lng_ref, o_ref, s_ref, *, rows, row0_first):
    c = pl.program_id(1)
    row0 = jnp.where(c == 0, row0_first, 0)
    n_valid = (rows - row0).astype(F32)
    cosf = cos_ref[...]
    sinf = sin_ref[...]
    ri = _iota((rows, rows), 0)
    ci = _iota((rows, rows), 1)
    diff = (ri - ci).astype(F32)
    rowi = _iota((rows, 1), 0)
    te = (rowi - row0).astype(F32)
    valid = rowi >= row0
    for h in range(R_HEADS):
        lg = _ret_log_gamma(h)
        lo = h * HEAD_DIM
        decay_m = jnp.where(diff >= 0, jnp.exp(diff * lg), 0.0)
        w_inter = jnp.exp((te + 1.0) * lg)
        w_src = jnp.exp((n_valid - 1.0 - te) * lg)
        s_decay = jnp.exp(n_valid * lg)
        for s in range(SEQ_GROUP):
            q = z_ref[s, :, lo:lo + HEAD_DIM]
            k = z_ref[s, :, 512 + lo:512 + lo + HEAD_DIM]
            v = jnp.where(valid, z_ref[s, :, 1024 + lo:1024 + lo + HEAD_DIM], 0.0)
            rg = z_ref[s, :, 1536 + lo:1536 + lo + HEAD_DIM]
            q = q * cosf + pltpu.roll(q, HEAD_DIM // 2, 1) * sinf
            k = (k * cosf + pltpu.roll(k, HEAD_DIM // 2, 1) * sinf) * (HEAD_DIM ** -0.5)
            s_state = s_ref[s, h]
            qb = _bf(q)
            vb = _bf(v)
            intra = _dot(_bf(_dot_nt(qb, _bf(k)) * decay_m), vb)
            inter = _dot(qb, _bf(s_state)) * w_inter
            s_ref[s, h] = s_decay * s_state + _dot_tn(_bf(k * w_src), vb)
            hr = intra + inter
            mu = jnp.mean(hr, axis=-1, keepdims=True)
            var = jnp.mean(jnp.square(hr - mu), axis=-1, keepdims=True)
            hr = (hr - mu) * lax.rsqrt(var + EPS) * lng_ref[h:h + 1, :]
            o_ref[s, :, lo:lo + HEAD_DIM] = rg * _sigmoid(rg) * hr


def _lru_body(z_ref, pre_ref, cw_ref, cb_ref, wa_ref, ba_ref, wx_ref, bx_ref, lam_ref,
              o_ref, hlast_ref, tail_ref, cbuf_s, a_s, b_s, *, rows, row0_first):
    c = pl.program_id(1)
    row0 = jnp.where(c == 0, row0_first, 0)
    rowi = _iota((rows, 1), 0)
    valid = rowi >= row0
    sp_lam = _softplus(-lam_ref[...])
    for s in range(SEQ_GROUP):
        lx = jnp.where(valid, z_ref[s, :, 0:LRU_WIDTH], pre_ref[min(s, pre_ref.shape[0] - 1)])
        cbuf_s[s, 0:8, :] = tail_ref[s]
        cbuf_s[s, 8:8 + rows, :] = lx
        xc = cb_ref[...] + cw_ref[CONV_W - 1:CONV_W, :] * lx
        for j in range(1, CONV_W):
            xc = xc + cw_ref[CONV_W - 1 - j:CONV_W - j, :] * cbuf_s[s, 8 - j:8 - j + rows, :]
        tail_ref[s] = lx[rows - 8:rows]
        xcb = _bf(xc)
        pre_a = []
        pre_x = []
        for p in range(LRU_WIDTH // LANES):
            xs = xcb[:, p * LANES:(p + 1) * LANES]
            pre_a.append(_dot(xs, wa_ref[p]))
            pre_x.append(_dot(xs, wx_ref[p]))
        r = _sigmoid(jnp.concatenate(pre_a, axis=1) + ba_ref[...])
        ig = _sigmoid(jnp.concatenate(pre_x, axis=1) + bx_ref[...])
        log_a = -LRU_C * r * sp_lam
        a = jnp.where(valid, jnp.exp(log_a), 1.0)
        one_minus_a2 = -jnp.tanh(log_a) * (jnp.exp(2.0 * log_a) + 1.0)
        bx = jnp.where(valid, jnp.sqrt(one_minus_a2) * (ig * xc), 0.0)
        a_s[s] = a
        b_s[s] = bx
    carries = [hlast_ref[s] for s in range(SEQ_GROUP)]
    for t in range(rows):
        for s in range(SEQ_GROUP):
            carries[s] = a_s[s, t:t + 1, :] * carries[s] + b_s[s, t:t + 1, :]
            b_s[s, t:t + 1, :] = carries[s]
    for s in range(SEQ_GROUP):
        hlast_ref[s] = carries[s]
        gate = z_ref[s, :, LRU_WIDTH:2 * LRU_WIDTH]
        gl = 0.5 * gate * (1.0 + jnp.tanh(math.sqrt(2.0 / math.pi) * (gate + 0.044715 * gate * gate * gate)))
        o_ref[s] = b_s[s] * gl


def _ret_lru_kernel(zr_ref, cos_ref, sin_ref, lng_ref, s0_ref, zl_ref, pre_ref, h0_ref,
                    cw_ref, cb_ref, wa_ref, ba_ref, wx_ref, bx_ref, lam_ref,
                    or_ref, s_ref, ol_ref, hlast_ref, tail_ref, cbuf_s, a_s, b_s, *, rows, row0_first):
    @pl.when(pl.program_id(1) == 0)
    def _():
        s_ref[...] = s0_ref[...]
        hlast_ref[...] = h0_ref[...]
        tail_ref[...] = jnp.zeros(tail_ref.shape, F32)

    _ret_body(zr_ref, cos_ref, sin_ref, lng_ref, or_ref, s_ref, rows=rows, row0_first=row0_first)
    _lru_body(zl_ref, pre_ref, cw_ref, cb_ref, wa_ref, ba_ref, wx_ref, bx_ref, lam_ref,
              ol_ref, hlast_ref, tail_ref, cbuf_s, a_s, b_s, rows=rows, row0_first=row0_first)


def ret_lru_call(zr, cosf, sinf, lng, s0, zl, pre, h0, cw, cb, wa2, ba, wx2, bx, lam, *, rows, row0_first):
    nb, t_len, _ = zr.shape
    assert nb % SEQ_GROUP == 0 and t_len % rows == 0
    blk = lambda b, c: (b, c, 0)
    per_seq = lambda b, c: (b, 0, 0)
    const2 = lambda b, c: (0, 0)
    const3 = lambda b, c: (0, 0, 0)
    sspec = pl.BlockSpec((SEQ_GROUP, R_HEADS, HEAD_DIM, HEAD_DIM), lambda b, c: (b, 0, 0, 0))
    shared_pre = pre.shape[0] == 1
    in_specs = [pl.BlockSpec((SEQ_GROUP, rows, 2048), blk),
                pl.BlockSpec((rows, HEAD_DIM), lambda b, c: (c, 0)),
                pl.BlockSpec((rows, HEAD_DIM), lambda b, c: (c, 0)),
                pl.BlockSpec((R_HEADS, HEAD_DIM), const2),
                sspec,
                pl.BlockSpec((SEQ_GROUP, rows, 1024), blk),
                pl.BlockSpec((1 if shared_pre else SEQ_GROUP, rows, LRU_WIDTH), const3 if shared_pre else per_seq),
                pl.BlockSpec((SEQ_GROUP, 1, LRU_WIDTH), per_seq),
                pl.BlockSpec((CONV_W, LRU_WIDTH), const2),
                pl.BlockSpec((1, LRU_WIDTH), const2),
                pl.BlockSpec((LRU_WIDTH // LANES, LANES, LANES), const3),
                pl.BlockSpec((1, LRU_WIDTH), const2),
                pl.BlockSpec((LRU_WIDTH // LANES, LANES, LANES), const3),
                pl.BlockSpec((1, LRU_WIDTH), const2),
                pl.BlockSpec((1, LRU_WIDTH), const2)]
    return pl.pallas_call(
        functools.partial(_ret_lru_kernel, rows=rows, row0_first=row0_first),
        grid=(nb // SEQ_GROUP, t_len // rows),
        in_specs=in_specs,
        out_specs=[pl.BlockSpec((SEQ_GROUP, rows, 512), blk), sspec,
                   pl.BlockSpec((SEQ_GROUP, rows, LRU_WIDTH), blk),
                   pl.BlockSpec((SEQ_GROUP, 1, LRU_WIDTH), per_seq),
                   pl.BlockSpec((SEQ_GROUP, 8, LRU_WIDTH), per_seq)],
        out_shape=[jax.ShapeDtypeStruct((nb, t_len, 512), F32),
                   jax.ShapeDtypeStruct((nb, R_HEADS, HEAD_DIM, HEAD_DIM), F32),
                   jax.ShapeDtypeStruct((nb, t_len, LRU_WIDTH), F32),
                   jax.ShapeDtypeStruct((nb, 1, LRU_WIDTH), F32),
                   jax.ShapeDtypeStruct((nb, 8, LRU_WIDTH), F32)],
        scratch_shapes=[pltpu.VMEM((SEQ_GROUP, rows + 8, LRU_WIDTH), F32),
                        pltpu.VMEM((SEQ_GROUP, rows, LRU_WIDTH), F32),
                        pltpu.VMEM((SEQ_GROUP, rows, LRU_WIDTH), F32)],
        compiler_params=pltpu.CompilerParams(dimension_semantics=("parallel", "arbitrary")),
        name="ret_lru",
    )(zr, cosf, sinf, lng, s0, zl, pre, h0, cw, cb, wa2, ba, wx2, bx, lam)


def _fox_cumsum_kernel(g_ref, gt_ref, fx_ref, fr_ref):
    ri = _iota((CHUNK, CHUNK), 0)
    ci = _iota((CHUNK, CHUNK), 1)
    lower = jnp.where(ci <= ri, 1.0, 0.0).astype(BF16)
    upper = jnp.where(ri <= ci, 1.0, 0.0).astype(BF16)
    pr = _iota((3 * LANES, F_HEADS // 2 * LANES), 0)
    pc = _iota((3 * LANES, F_HEADS // 2 * LANES), 1)
    term, gate = pr // LANES, pr % LANES
    pair, lane = pc // LANES, pc % LANES
    first_head = 2 * M_HEADS
    place = (((gate == first_head + 2 * pair) & (lane == F_DH + term))
             | ((gate == first_head + 2 * pair + 1) & (lane == term)))
    place = jnp.where(place, 1.0, 0.0).astype(BF16)
    carry_r = jnp.zeros((1, LANES), F32)
    carry_c = jnp.zeros((16, 1), F32)
    for blk in range(NCH):
        x = g_ref[blk * CHUNK:(blk + 1) * CHUNK, :]
        xt = gt_ref[:, blk * CHUNK:(blk + 1) * CHUNK]
        if blk == 0:
            x = jnp.where(_iota((CHUNK, 1), 0) >= PADF, x, 0.0)
            xt = jnp.where(_iota((1, CHUNK), 1) >= PADF, xt, 0.0)
        cs = _dot01_left(lower, x) + carry_r
        cst = _dot01_right(xt, upper) + carry_c
        carry_r = cs[CHUNK - 1:CHUNK, :]
        carry_c = cst[:, CHUNK - 1:CHUNK]
        fk = cs * LOG2E
        if blk == 0:
            cst = jnp.where(_iota((1, CHUNK), 1) >= PADF, cst, -NEG)
            fk = jnp.where(_iota((CHUNK, 1), 0) >= PADF, fk, -NEG)
        fx_ref[blk * CHUNK:(blk + 1) * CHUNK, :] = _dot(jnp.concatenate(_split3(fk), axis=1), place)
        fr_ref[0, :, blk * CHUNK:(blk + 1) * CHUNK] = cst


def fox_cumsum(g2, g2t):
    return pl.pallas_call(
        _fox_cumsum_kernel,
        grid=(BATCH,),
        in_specs=[pl.BlockSpec((TP, LANES), lambda b: (b, 0)),
                  pl.BlockSpec((16, TP), lambda b: (0, b))],
        out_specs=[pl.BlockSpec((TP, F_HEADS // 2 * LANES), lambda b: (b, 0)),
                   pl.BlockSpec((1, 16, TP), lambda b: (b, 0, 0))],
        out_shape=[jax.ShapeDtypeStruct((SBASE, F_HEADS // 2 * LANES), F32),
                   jax.ShapeDtypeStruct((BATCH, 16, TP), F32)],
        compiler_params=pltpu.CompilerParams(dimension_semantics=("parallel",)),
        name="fox_cumsum",
    )(g2, g2t)


def _fox_prompt_kernel(q_ref, k_ref, v_ref, fr_ref, fx_ref, o_ref,
                       qta_s, qtb_s, ka_s, kb_s, vt_s, m_s, l_s, acc_s, ua_s):
    i = pl.program_id(2)
    lane = _iota((1, LANES), 1)
    is_a = lane < F_DH
    row = _iota((LANES, 1), 0)
    row_a = row < F_DH
    k_heads = (ka_s, kb_s)
    qt_heads = (qta_s, qtb_s)

    @pl.when(i == 0)
    def _():
        k = k_ref[...]
        fx = fx_ref[...]
        ka_s[...] = _bf(jnp.where(is_a, k, fx))
        kb_s[...] = _bf(jnp.where(is_a, fx, k))

        minus_a = jnp.where((row >= F_DH) & (row < F_DH + 3), -1.0, 0.0)
        minus_b = jnp.where(row < 3, -1.0, 0.0)

        def fill(blk, carry):
            rows = pl.ds(pl.multiple_of(blk * CHUNK, CHUNK), CHUNK)
            qt = (q_ref[rows, :] * (F_DH ** -0.5 * LOG2E)).T
            qta_s[blk] = _bf(jnp.where(row_a, qt, minus_a))
            qtb_s[blk] = _bf(jnp.where(row_a, minus_b, qt))
            vt_s[blk] = _bf(v_ref[rows, :].T)
            return carry

        lax.fori_loop(0, NCH, fill, 0)

    def init():
        m_s[...] = jnp.full(m_s.shape, NEG, F32)
        l_s[...] = jnp.zeros(l_s.shape, F32)
        acc_s[...] = jnp.zeros(acc_s.shape, F32)

    def scores(hh, qt, kblk, nkb):
        ck = nkb * CHUNK
        k0 = kblk * CHUNK if isinstance(kblk, int) else pl.multiple_of(kblk * CHUNK, CHUNK)
        return _dot(k_heads[hh][pl.ds(k0, ck), :], qt[hh])

    def step(hh, u, qblk, nqb, kblks, diagonal):
        tq = nqb * CHUNK
        ck = len(kblks) * CHUNK
        fq = jnp.concatenate([fr_ref[0, 0, qblk + t, hh:hh + 1, :] for t in range(nqb)], axis=1)
        fq = jnp.where(fq > 0.5 * -NEG, 0.0, fq * LOG2E)
        if diagonal:
            lead = ck - tq
            ri = _iota((ck, tq), 0)
            u = jnp.where((ri < lead) | (ri - lead <= _iota((ck, tq), 1)), u, NEG)
        m_old = m_s[hh, :, 0:tq]
        m_new = jnp.maximum(m_old, jnp.max(u, axis=0, keepdims=True) + fq)
        m_s[hh, :, 0:tq] = m_new
        alpha = jnp.exp2(m_old - m_new)
        p = _bf(jnp.exp2(u + (fq - m_new)))
        feat = slice(hh * F_DH, (hh + 1) * F_DH)
        vt = jnp.concatenate([vt_s[kb, feat, :] for kb in kblks], axis=1)
        pv = _dot(jnp.concatenate([vt, jnp.ones((16, ck), BF16)], axis=0), p)
        l_s[hh, :, 0:tq] = alpha * l_s[hh, :, 0:tq] + pv[F_DH:F_DH + 1, :]
        acc_s[feat, 0:tq] = alpha * acc_s[feat, 0:tq] + pv[0:F_DH, :]

    def finalize(qblk, nqb):
        tq = nqb * CHUNK
        o_t = acc_s[:, 0:tq] / jnp.where(row_a, l_s[0, :, 0:tq], l_s[1, :, 0:tq])
        for t in range(nqb):
            rows = pl.ds(pl.multiple_of((qblk + t) * CHUNK, CHUNK), CHUNK)
            o_ref[rows, :] = o_t[:, t * CHUNK:(t + 1) * CHUNK].T

    @pl.when(i == 0)
    def _():
        init()
        qt0 = (qta_s[0], qtb_s[0])
        for hh in range(2):
            step(hh, scores(hh, qt0, 0, 1), 0, 1, [0], True)
        finalize(0, 1)

    nck = FOX_CK // CHUNK
    nqb = FOX_TQ // CHUNK
    qblk = 1 + i * nqb
    qt = tuple(jnp.concatenate([qt_h[qblk + t] for t in range(nqb)], axis=1) for qt_h in qt_heads)
    init()

    ua_s[...] = scores(0, qt, 1, nck)

    def body(j, carry):
        kblk = 1 + j * nck
        kblks = [kblk + t for t in range(nck)]
        ub = scores(1, qt, kblk, nck)
        step(0, ua_s[...], qblk, nqb, kblks, False)
        ua_s[...] = scores(0, qt, kblk + nck, nck)
        step(1, ub, qblk, nqb, kblks, False)
        return carry

    lax.fori_loop(0, i, body, 0)
    last = [0] + [qblk + t for t in range(nck)]
    ua = jnp.concatenate([scores(0, qt, 0, 1), ua_s[...]], axis=0)
    ub = jnp.concatenate([scores(1, qt, 0, 1), scores(1, qt, qblk, nck)], axis=0)
    step(0, ua, qblk, nqb, last, True)
    step(1, ub, qblk, nqb, last, True)
    finalize(qblk, nqb)


def fox_prompt(zq, zk, zv, fr, fx):
    nq = (TP - CHUNK) // FOX_TQ
    pairs = F_HEADS // 2
    return pl.pallas_call(
        _fox_prompt_kernel,
        grid=(BATCH, pairs, nq),
        in_specs=[pl.BlockSpec((TP, LANES), lambda b, p, i: (b, p)),
                  pl.BlockSpec((TP, LANES), lambda b, p, i: (b, p)),
                  pl.BlockSpec((TP, LANES), lambda b, p, i: (b, p)),
                  pl.BlockSpec((1, 1, NCH, 2, LANES), lambda b, p, i: (b, p, 0, 0, 0)),
                  pl.BlockSpec((TP, LANES), lambda b, p, i: (b, p))],
        out_specs=pl.BlockSpec((TP, LANES), lambda b, p, i: (b, p)),
        out_shape=jax.ShapeDtypeStruct((SBASE, 512), F32),
        scratch_shapes=[pltpu.VMEM((NCH, LANES, CHUNK), BF16),
                        pltpu.VMEM((NCH, LANES, CHUNK), BF16),
                        pltpu.VMEM((TP, LANES), BF16),
                        pltpu.VMEM((TP, LANES), BF16),
                        pltpu.VMEM((NCH, LANES, CHUNK), BF16),
                        pltpu.VMEM((2, 1, FOX_TQ), F32),
                        pltpu.VMEM((2, 1, FOX_TQ), F32),
                        pltpu.VMEM((LANES, FOX_TQ), F32),
                        pltpu.VMEM((FOX_CK, FOX_TQ), F32)],
        compiler_params=pltpu.CompilerParams(dimension_semantics=("parallel", "parallel", "arbitrary")),
        name="fox_prompt",
    )(zq, zk, zv, fr, fx)


def _lf_suffix_kernel(x_ref, o_ref):
    ri = _iota((PAGE_SIZE, PAGE_SIZE), 0)
    ci = _iota((PAGE_SIZE, PAGE_SIZE), 1)
    after = jnp.where(ri > ci, 1.0, 0.0).astype(BF16)
    ones = jnp.ones((PAGE_SIZE, PAGE_SIZE), BF16)
    x = x_ref[...].reshape(PRE_PG * F_HEADS, PAGE_SIZE)
    o_ref[:, 0:F_HEADS, :] = _dot01_right(x, after).reshape(PRE_PG, F_HEADS, PAGE_SIZE)
    o_ref[:, F_HEADS:2 * F_HEADS, :] = _dot01_right(x, ones).reshape(PRE_PG, F_HEADS, PAGE_SIZE)


def lf_suffix(lft):
    n_phys = lft.shape[0]
    return pl.pallas_call(
        _lf_suffix_kernel,
        grid=(n_phys // PRE_PG,),
        in_specs=[pl.BlockSpec((PRE_PG, F_HEADS, PAGE_SIZE), lambda i: (i, 0, 0))],
        out_specs=pl.BlockSpec((PRE_PG, 2 * F_HEADS, PAGE_SIZE), lambda i: (i, 0, 0)),
        out_shape=jax.ShapeDtypeStruct((n_phys, 2 * F_HEADS, PAGE_SIZE), F32),
        compiler_params=pltpu.CompilerParams(dimension_semantics=("parallel",)),
        name="lf_suffix",
    )(lft)


def _fox_decode_kernel(pt_ref, zq_ref, zk_ref, zv_ref, gt_ref, *rest):
    k_refs = rest[0:DEC_PG]
    v_refs = rest[DEC_PG:2 * DEC_PG]
    lf_refs = rest[2 * DEC_PG:3 * DEC_PG]
    o_ref = rest[3 * DEC_PG]
    q2_s, m_s, l_s, acc_s, run_s, nc_s = rest[3 * DEC_PG + 1:]
    j = pl.program_id(1)
    n_rows = DEC_SEQ * F_HEADS
    width = F_HEADS * F_DH

    def update(u, v, v_feature_major):
        m_old = m_s[...]
        m_new = jnp.maximum(m_old, jnp.max(u, axis=1, keepdims=True))
        alpha = jnp.exp(m_old - m_new)
        p = jnp.exp(u - m_new[:, 0:u.shape[1]])
        pv = _dot_nt(_bf(p), v) if v_feature_major else _dot(_bf(p), v)
        l_s[...] = alpha * l_s[...] + jnp.sum(p, axis=1, keepdims=True)
        acc_s[...] = jnp.concatenate([alpha] * (width // LANES), axis=1) * acc_s[...] + pv
        m_s[...] = m_new

    @pl.when(j == 0)
    def _():
        m_s[...] = jnp.full(m_s.shape, NEG, F32)
        l_s[...] = jnp.zeros(l_s.shape, F32)
        acc_s[...] = jnp.zeros(acc_s.shape, F32)
        run_s[...] = jnp.zeros(run_s.shape, F32)
        head_mask = _iota((F_HEADS, width), 1) // F_DH == _iota((F_HEADS, width), 0)
        q16 = zq_ref[...] * (F_DH ** -0.5)
        q2 = _bf(jnp.concatenate(
            [jnp.where(head_mask, q16[SPAD + t:SPAD + t + 1, :], 0.0) for t in range(DEC_SEQ)], axis=0))
        q2_s[...] = q2
        lf_new = gt_ref[F_HEADS:2 * F_HEADS, :]
        lane = _iota((1, SROWS), 1)
        cum = jnp.zeros((F_HEADS, SROWS), F32)
        for t in range(DEC_SEQ):
            cum = cum + jnp.where(lane >= SPAD + t, lf_new[:, SPAD + t:SPAD + t + 1], 0.0)
        nc = jnp.concatenate([cum[:, SPAD + t:SPAD + t + 1] for t in range(DEC_SEQ)], axis=0)
        nc_s[...] = jnp.broadcast_to(nc, nc_s.shape)
        cum4 = jnp.concatenate([cum] * DEC_SEQ, axis=0)
        qi = _iota((n_rows, SROWS), 0) // F_HEADS
        kj = _iota((n_rows, SROWS), 1) - SPAD
        u = jnp.where((kj >= 0) & (kj <= qi), _dot_nt(q2, _bf(zk_ref[...])) + (nc - cum4), NEG)
        update(u, _bf(zv_ref[...]), False)

    q2 = q2_s[...]
    nc = nc_s[...]
    run = run_s[...]
    us = []
    for i in range(DEC_PG):
        later = run + lf_refs[i][0, 0:F_HEADS, :]
        bias = jnp.concatenate([later] * DEC_SEQ, axis=0) + nc
        us.append(_dot(q2, _bf(k_refs[i][0].reshape(width, PAGE_SIZE))) + bias)
        run = run + lf_refs[i][0, F_HEADS:2 * F_HEADS, :]
    run_s[...] = run
    u = jnp.concatenate(us, axis=1)
    m_old = m_s[...]
    m_new = jnp.maximum(m_old, jnp.max(u, axis=1, keepdims=True))
    alpha = jnp.exp(m_old - m_new)
    p = jnp.exp(u - jnp.concatenate([m_new] * DEC_PG, axis=1))
    l_s[...] = alpha * l_s[...] + jnp.sum(p, axis=1, keepdims=True)
    pv = _dot_nt(_bf(p[:, 0:PAGE_SIZE]), _bf(v_refs[0][0].reshape(width, PAGE_SIZE)))
    for i in range(1, DEC_PG):
        pv = pv + _dot_nt(_bf(p[:, i * PAGE_SIZE:(i + 1) * PAGE_SIZE]), _bf(v_refs[i][0].reshape(width, PAGE_SIZE)))
    acc_s[...] = jnp.concatenate([alpha] * (width // LANES), axis=1) * acc_s[...] + pv
    m_s[...] = m_new

    @pl.when(j == pl.num_programs(1) - 1)
    def _():
        head_mask = (_iota((n_rows, width), 1) // F_DH) == (_iota((n_rows, width), 0) % F_HEADS)
        o2 = jnp.where(head_mask, acc_s[...] / jnp.concatenate([l_s[...]] * (width // LANES), axis=1), 0.0)
        o_ref[...] = jnp.zeros(o_ref.shape, F32)
        for t in range(DEC_SEQ):
            o_ref[SPAD + t:SPAD + t + 1, :] = jnp.sum(o2[t * F_HEADS:(t + 1) * F_HEADS, :], axis=0, keepdims=True)


def fox_decode(page_table, zq, zk, zv, g2t_s, cache_kt, cache_vt, lf_sums):
    n_pages = page_table.shape[1]
    width = F_HEADS * F_DH

    def page_map(i):
        return lambda b, j, pt: (pt[b, n_pages - 1 - (j * DEC_PG + i)], 0, 0, 0)

    def page_map3(i):
        return lambda b, j, pt: (pt[b, n_pages - 1 - (j * DEC_PG + i)], 0, 0)

    in_specs = [pl.BlockSpec((SROWS, width), lambda b, j, pt: (b, 0)),
                pl.BlockSpec((SROWS, width), lambda b, j, pt: (b, 0)),
                pl.BlockSpec((SROWS, width), lambda b, j, pt: (b, 0)),
                pl.BlockSpec((None, 16, SROWS), lambda b, j, pt: (b, 0, 0))]
    in_specs += [pl.BlockSpec((1, F_HEADS, F_DH, PAGE_SIZE), page_map(i)) for i in range(DEC_PG)]
    in_specs += [pl.BlockSpec((1, F_HEADS, F_DH, PAGE_SIZE), page_map(i)) for i in range(DEC_PG)]
    in_specs += [pl.BlockSpec((1, 2 * F_HEADS, PAGE_SIZE), page_map3(i)) for i in range(DEC_PG)]
    args = [zq, zk, zv, g2t_s] + [cache_kt] * DEC_PG + [cache_vt] * DEC_PG + [lf_sums] * DEC_PG
    n_rows = DEC_SEQ * F_HEADS
    grid_spec = pltpu.PrefetchScalarGridSpec(
        num_scalar_prefetch=1,
        grid=(DEC_BATCH, n_pages // DEC_PG),
        in_specs=in_specs,
        out_specs=pl.BlockSpec((SROWS, width), lambda b, j, pt: (b, 0)),
        scratch_shapes=[pltpu.VMEM((n_rows, width), BF16),
                        pltpu.VMEM((n_rows, LANES), F32),
                        pltpu.VMEM((n_rows, LANES), F32),
                        pltpu.VMEM((n_rows, width), F32),
                        pltpu.VMEM((F_HEADS, PAGE_SIZE), F32),
                        pltpu.VMEM((n_rows, LANES), F32)])
    return pl.pallas_call(
        _fox_decode_kernel,
        grid_spec=grid_spec,
        out_shape=jax.ShapeDtypeStruct((DEC_BATCH * SROWS, width), F32),
        compiler_params=pltpu.CompilerParams(dimension_semantics=("parallel", "arbitrary")),
        name="fox_decode",
    )(page_table, *args)


def _block_diag_pairs(w):
    z = jnp.zeros((LANES // 2, LANES // 2), w.dtype)
    pairs = [jnp.block([[w[2 * p], z], [z, w[2 * p + 1]]]) for p in range(w.shape[0] // 2)]
    return _bf(jnp.stack(pairs))


def _rope_tables(pos):
    half = HEAD_DIM // 2
    freq = ROPE_BASE ** (-jnp.arange(half, dtype=F32) / half)
    ang = pos.astype(F32)[:, None] * freq[None, :]
    cos, sin = jnp.cos(ang), jnp.sin(ang)
    return jnp.concatenate([cos, cos], axis=1), jnp.concatenate([-sin, sin], axis=1)


def _unpad_prompt(x):
    return x.reshape((BATCH, TP) + x.shape[1:])[:, PADF:]


def _unpad_sample(x):
    return x.reshape((DEC_BATCH, SROWS) + x.shape[1:])[:, SPAD:]


def kernel(x_prompt, x_sample, cache_fox_k, cache_fox_v, cache_fox_logf, state_mlstm_C, state_mlstm_n, state_mlstm_m, state_ret_S, state_lru_h, state_lru_conv, page_table, meta_tokens, w_in_even, b_mlstm_i, b_mlstm_f, b_fox_f, w_out_even, w_in_odd, ret_ln_g, conv_w, conv_b, lru_wa, lru_ba, lru_wx, lru_bx, lru_lambda, w_out_odd, norm_g, ffn_wg, ffn_wu, ffn_wd):
    n_pages = page_table.shape[1]
    past = n_pages * PAGE_SIZE
    n_phys = cache_fox_k.shape[0]

    head = jnp.concatenate([jnp.zeros((PADF, D_MODEL), F32), meta_tokens.astype(F32)], axis=0)
    hs = jnp.concatenate([jnp.zeros((DEC_BATCH, SPAD, D_MODEL), F32), x_sample.astype(F32)],
                         axis=1).reshape(DEC_BATCH * SROWS, D_MODEL)
    h_s = hs
    h_p = jnp.concatenate([jnp.broadcast_to(head[None], (BATCH, CHUNK, D_MODEL)), x_prompt.astype(F32)],
                          axis=1).reshape(SBASE, D_MODEL)

    c = np.cumsum([0, 512, 512, 512, 512, 4, 4, 512, 512, 512, 8])
    gate_w = jnp.concatenate([w_in_even[:, c[4]:c[6]], w_in_even[:, c[9]:c[10]],
                              jnp.zeros((D_MODEL, LANES - 16), F32)], axis=1)
    w_even = _bf(jnp.concatenate([w_in_even[:, c[0]:c[4]], w_in_even[:, c[6]:c[9]], gate_w], axis=1))
    gate_b = jnp.concatenate([b_mlstm_i, b_mlstm_f, b_fox_f]).astype(F32)
    brow = jnp.concatenate([gate_b, jnp.zeros((LANES - 16,), F32)]).reshape(1, LANES)
    w_odd = _bf(w_in_odd)
    wo_even = _bf(w_out_even)
    wo_odd = _bf(w_out_odd)
    wg = _bf(ffn_wg)
    wu = _bf(ffn_wu)
    wd = _bf(ffn_wd)
    ng = norm_g.astype(F32).reshape(2, 4, 1, D_MODEL)

    even_cols = (2048, 512, 512, 512, LANES)
    zm_p, zq_p, zk_p, zv_p, g2_p, g2t_p = norm_matmul(h_p, ng[0, 0], w_even, even_cols, gate_bias=brow)
    zm_s, zq_s, zk_s, zv_s, g2_s, g2t_s = norm_matmul(h_s, ng[0, 0], w_even, even_cols, gate_bias=brow)
    prompt3 = lambda x: x.reshape(BATCH, TP, x.shape[-1])
    sample3 = lambda x: x.reshape(DEC_BATCH, SROWS, x.shape[-1])
    flat2 = lambda x: x.reshape(-1, x.shape[-1])

    zeros_c = jnp.zeros((BATCH, M_HEADS, HEAD_DIM, HEAD_DIM), F32)
    zeros_n = jnp.zeros((BATCH, M_HEADS, HEAD_DIM), F32)
    hm_p, p_c, p_n, p_m = mlstm_call(prompt3(zm_p), prompt3(g2_p),
                                     g2t_p.reshape(16, BATCH, NCH, CHUNK).transpose(1, 2, 0, 3),
                                     zeros_c, zeros_n, zeros_n, row0_first=PADF, transposed=True)
    p_c = p_c.transpose(0, 1, 3, 2)
    m0_s = jnp.broadcast_to(state_mlstm_m.astype(F32)[:, :, None], (DEC_BATCH, M_HEADS, LANES))
    hm_s, s_c, s_n, s_m = mlstm_call(sample3(zm_s), sample3(g2_s),
                                     g2t_s.reshape(16, DEC_BATCH, 1, SROWS).transpose(1, 2, 0, 3),
                                     state_mlstm_C.astype(F32), state_mlstm_n.astype(F32), m0_s, row0_first=SPAD,
                                     transposed=False)

    fx, fr = fox_cumsum(g2_p, g2t_p)
    pairs = F_HEADS // 2
    fr_p = fr[:, 8:16].reshape(BATCH, pairs, 2, NCH, CHUNK).transpose(0, 1, 3, 2, 4)
    hf_p = fox_prompt(zq_p, zk_p, zv_p, fr_p, fx)
    lf_sums = lf_suffix(cache_fox_logf.astype(F32).transpose(0, 2, 1))
    hf_s = fox_decode(page_table, zq_s, zk_s, zv_s, g2t_s.reshape(16, DEC_BATCH, SROWS).transpose(1, 0, 2),
                      cache_fox_k.astype(F32).transpose(0, 2, 3, 1), cache_fox_v.astype(F32).transpose(0, 2, 3, 1),
                      lf_sums)

    gains = norm_g.astype(F32)
    even_w = (wo_even[:512], wo_even[512:], gains[0, 1:4], wg[0], wu[0], wd[0])
    h_p = mix_ffn(flat2(hm_p), hf_p, h_p, *even_w)
    h_s = mix_ffn(flat2(hm_s), hf_s, h_s, *even_w)

    zr_p, zl_p = norm_matmul(h_p, ng[1, 0], w_odd, (2048, 1024))
    zr_s, zl_s = norm_matmul(h_s, ng[1, 0], w_odd, (2048, 1024))
    cos_p, sin_p = _rope_tables(jnp.arange(TP) - PADF)
    cos_s, sin_s = _rope_tables(past + jnp.arange(SROWS) - SPAD)
    lng = ret_ln_g.astype(F32)
    lru_args = (conv_w.astype(F32), conv_b.astype(F32).reshape(1, LRU_WIDTH),
                _block_diag_pairs(lru_wa), lru_ba.astype(F32).reshape(1, LRU_WIDTH),
                _block_diag_pairs(lru_wx), lru_bx.astype(F32).reshape(1, LRU_WIDTH),
                lru_lambda.astype(F32).reshape(1, LRU_WIDTH))
    hr_p, p_s, hl_p, p_h, p_tail = ret_lru_call(
        prompt3(zr_p), cos_p, sin_p, lng, jnp.zeros((BATCH, R_HEADS, HEAD_DIM, HEAD_DIM), F32),
        prompt3(zl_p), jnp.zeros((1, CHUNK, LRU_WIDTH), F32), jnp.zeros((BATCH, 1, LRU_WIDTH), F32), *lru_args,
        rows=CHUNK, row0_first=PADF)
    pre_s = jnp.concatenate([jnp.zeros((DEC_BATCH, SPAD - (CONV_W - 1), LRU_WIDTH), F32),
                             state_lru_conv.astype(F32),
                             jnp.zeros((DEC_BATCH, DEC_SEQ, LRU_WIDTH), F32)], axis=1)
    hr_s, s_s, hl_s, s_h, s_tail = ret_lru_call(
        sample3(zr_s), cos_s, sin_s, lng, state_ret_S.astype(F32),
        sample3(zl_s), pre_s, state_lru_h.astype(F32).reshape(DEC_BATCH, 1, LRU_WIDTH), *lru_args,
        rows=SROWS, row0_first=SPAD)

    odd_w = (wo_odd[:512], wo_odd[512:], gains[1, 1:4], wg[1], wu[1], wd[1])
    y_prompt = mix_ffn(hr_p, hl_p, prompt3(h_p), *odd_w, main_only=True)
    h_s = mix_ffn(flat2(hr_s), flat2(hl_s), h_s, *odd_w)

    y_sample = h_s.reshape(DEC_BATCH, SROWS, D_MODEL)[:, SPAD:]
    heads = lambda x: x.reshape(x.shape[:2] + (F_HEADS, F_DH))
    nback = CONV_W - 1
    return (y_prompt, y_sample,
            heads(_unpad_prompt(zk_p)), heads(_unpad_prompt(zv_p)), _unpad_prompt(g2_p[:, 8:16]),
            heads(_unpad_sample(zk_s)), heads(_unpad_sample(zv_s)), _unpad_sample(g2_s[:, 8:16]),
            p_c, p_n, p_m[:, :, 0], s_c, s_n, s_m[:, :, 0],
            p_s, s_s,
            p_h[:, 0], s_h[:, 0], p_tail[:, 8 - nback:], s_tail[:, 8 - nback:])
```

```python
import functools
import math

import jax
import jax.numpy as jnp
import numpy as np
from jax import lax
from jax.experimental import pallas as pl
from jax.experimental.pallas import tpu as pltpu

F32 = jnp.float32
BF16 = jnp.bfloat16

D_MODEL = 1024
BATCH = 4
SEQ = 4096
DEC_BATCH = 32
DEC_SEQ = 4
PAGE_SIZE = 128
N_META = 16
CHUNK = 128
M_HEADS = 4
F_HEADS = 8
F_DH = 64
R_HEADS = 4
HEAD_DIM = 128
ROPE_BASE = 10000.0
LRU_WIDTH = 512
LRU_C = 8.0
CONV_W = 4
D_FF = 2816
EPS = 1e-6
NEG = -1e30

LANES = 128
PADF = CHUNK - N_META
TP = PADF + N_META + SEQ
NCH = TP // CHUNK
SROWS = 16
SPAD = SROWS - DEC_SEQ
SBASE = BATCH * TP
NP = SBASE + DEC_BATCH * SROWS
TM = 512
SEQ_GROUP = 4
MXU_WIDTH = 256
MIX_FFN_VMEM_BYTES = 56 * 1024 * 1024
FOX_TQ = 512
FOX_CK = 512
LOG2E = math.log2(math.e)
DEC_PG = 32
PRE_PG = 128


def _bf(x):
    return x.astype(BF16)


def _dot(a, b):
    return jnp.dot(a, b, preferred_element_type=F32)


def _dot_nt(a, b):
    return lax.dot_general(a, b, (((1,), (1,)), ((), ())), preferred_element_type=F32)


def _dot_tn(a, b):
    return lax.dot_general(a, b, (((0,), (0,)), ((), ())), preferred_element_type=F32)


def _split3(x):
    hi = _bf(x)
    r1 = x - hi.astype(F32)
    mid = _bf(r1)
    lo = _bf(r1 - mid.astype(F32))
    return hi, mid, lo


def _dot01_right(x, m01):
    hi, mid, lo = _split3(x)
    return _dot(hi, m01) + _dot(mid, m01) + _dot(lo, m01)


def _dot01_left(m01, x):
    hi, mid, lo = _split3(x)
    return _dot(m01, hi) + _dot(m01, mid) + _dot(m01, lo)


def _iota(shape, dim):
    return lax.broadcasted_iota(jnp.int32, shape, dim)


def _rms(x, g):
    ms = jnp.mean(x * x, axis=-1, keepdims=True)
    return x * lax.rsqrt(ms + EPS) * g


def _softplus(x):
    return jnp.maximum(x, 0.0) + jnp.log1p(jnp.exp(-jnp.abs(x)))


def _log_sigmoid(x):
    return -_softplus(-x)


def _sigmoid(x):
    return 0.5 * jnp.tanh(0.5 * x) + 0.5


def _nm_kernel(x_ref, g_ref, w_ref, *rest, splits, gated):
    if gated:
        b_ref, o_refs = rest[0], rest[1:]
    else:
        o_refs = rest
    xn = _bf(_rms(x_ref[...], g_ref[...]))
    off = 0
    for k, n in enumerate(splits):
        res = _dot(xn, w_ref[:, off:off + n])
        off += n
        if gated and k == len(splits) - 1:
            res = res + b_ref[...]
            res = jnp.where(_iota(res.shape, 1) < M_HEADS, res, _log_sigmoid(res))
            o_refs[k + 1][...] = res.T[0:16, :]
        o_refs[k][...] = res


def norm_matmul(x, g, w, splits, gate_bias=None):
    n_rows, k = x.shape
    const = lambda i: (0, 0)
    gated = gate_bias is not None
    in_specs = [pl.BlockSpec((TM, k), lambda i: (i, 0)),
                pl.BlockSpec((1, k), const),
                pl.BlockSpec((k, sum(splits)), const, pipeline_mode=pl.Buffered(1))]
    out_specs = [pl.BlockSpec((TM, n), lambda i: (i, 0)) for n in splits]
    out_shape = [jax.ShapeDtypeStruct((n_rows, n), F32) for n in splits]
    args = [x, g, w]
    if gated:
        assert splits[-1] == LANES
        in_specs.append(pl.BlockSpec((1, LANES), const))
        out_specs.append(pl.BlockSpec((16, TM), lambda i: (0, i)))
        out_shape.append(jax.ShapeDtypeStruct((16, n_rows), F32))
        args.append(gate_bias)
    return pl.pallas_call(
        functools.partial(_nm_kernel, splits=splits, gated=gated),
        grid=(n_rows // TM,),
        in_specs=in_specs,
        out_specs=out_specs,
        out_shape=out_shape,
        compiler_params=pltpu.CompilerParams(dimension_semantics=("parallel",)),
        name="norm_matmul",
    )(*args)


def _mix_ffn_kernel(*refs, parts):
    def rows(piece_refs):
        if parts == 1:
            return piece_refs[0][...]
        return jnp.concatenate([r[0] for r in piece_refs], axis=0)

    a1 = rows(refs[0:parts])
    a2 = rows(refs[parts:2 * parts])
    h = rows(refs[2 * parts:3 * parts])
    w1_ref, w2_ref, g_ref, wg_ref, wu_ref, wd_ref, o_ref = refs[3 * parts:]
    mix = _dot(_bf(a1), w1_ref[...]) + _dot(_bf(a2), w2_ref[...])
    h1 = h + _rms(mix, g_ref[0:1, :])
    xn = _bf(_rms(h1, g_ref[1:2, :]))
    acts = []
    for c in range(wg_ref.shape[1] // MXU_WIDTH):
        cols = slice(c * MXU_WIDTH, (c + 1) * MXU_WIDTH)
        gate = _dot(xn, wg_ref[:, cols])
        up = _dot(xn, wu_ref[:, cols])
        acts.append(_bf(gate * _sigmoid(gate) * up))
    ff = _dot(jnp.concatenate(acts, axis=1), wd_ref[...])
    res = h1 + _rms(ff, g_ref[2:3, :])
    o_ref[...] = res if parts == 1 else res[None]


def mix_ffn(a1, a2, h, w1, w2, g, wg, wu, wd, main_only=False):
    k1 = a1.shape[-1]
    k2 = a2.shape[-1]
    d = h.shape[-1]
    dff = wg.shape[1]
    assert dff % MXU_WIDTH == 0
    resident = lambda shape: pl.BlockSpec(shape, lambda *_: (0,) * len(shape), pipeline_mode=pl.Buffered(1))
    weights = [resident((k1, d)), resident((k2, d)), resident((3, d)),
               resident((d, dff)), resident((d, dff)), resident((dff, d))]
    if main_only:
        parts = TM // CHUNK
        nb, t_len = h.shape[0], h.shape[1] - CHUNK
        assert t_len % TM == 0
        piece = lambda k, t: pl.BlockSpec((1, CHUNK, k), lambda b, j: (b, 1 + j * parts + t, 0))
        in_specs = ([piece(k1, t) for t in range(parts)] + [piece(k2, t) for t in range(parts)]
                    + [piece(d, t) for t in range(parts)] + weights)
        args = [a1] * parts + [a2] * parts + [h] * parts
        grid = (nb, t_len // TM)
        out_specs = pl.BlockSpec((1, TM, d), lambda b, j: (b, j, 0))
        out_shape = jax.ShapeDtypeStruct((nb, t_len, d), F32)
    else:
        parts = 1
        n_rows = h.shape[0]
        assert n_rows % TM == 0
        row_blk = lambda i: (i, 0)
        in_specs = [pl.BlockSpec((TM, k1), row_blk), pl.BlockSpec((TM, k2), row_blk),
                    pl.BlockSpec((TM, d), row_blk)] + weights
        args = [a1, a2, h]
        grid = (n_rows // TM,)
        out_specs = pl.BlockSpec((TM, d), row_blk)
        out_shape = jax.ShapeDtypeStruct((n_rows, d), F32)
    return pl.pallas_call(
        functools.partial(_mix_ffn_kernel, parts=parts),
        grid=grid,
        in_specs=in_specs,
        out_specs=out_specs,
        out_shape=out_shape,
        compiler_params=pltpu.CompilerParams(dimension_semantics=("parallel",) * len(grid),
                                             vmem_limit_bytes=MIX_FFN_VMEM_BYTES),
        name="mix_ffn",
    )(*args, w1, w2, g, wg, wu, wd)


def _mlstm_chunk(q, k, v, logi_r, lf_r, ct_state, n_state, m_state, row0, rows):
    ri = _iota((rows, rows), 0)
    ci = _iota((rows, rows), 1)
    valid_r = _iota((1, rows), 1) >= row0
    lf_r = jnp.where(valid_r, lf_r, 0.0)
    logi_r = jnp.where(valid_r, logi_r, NEG)
    upper = jnp.where(ri <= ci, 1.0, 0.0).astype(BF16)
    a = jnp.where(ci > ri, lf_r, jnp.where(ci == ri, logi_r, 0.0))
    pre = _dot01_right(jnp.concatenate([a, jnp.broadcast_to(lf_r, (8, rows))], axis=0), upper)
    b_r = pre[rows:rows + 1, :]
    log_d = jnp.where(ri <= ci, pre[0:rows, :], -jnp.inf)
    m_inter = b_r + m_state
    m_t = jnp.maximum(m_inter, jnp.max(log_d, axis=0, keepdims=True))
    qb = _bf(q)
    kb = _bf(k)
    w = _dot_nt(kb, qb) * jnp.exp(log_d - m_t)
    w_inter = jnp.exp(m_inter - m_t)
    vt = v.T
    num_t = _dot(_bf(vt), _bf(w)) + w_inter * _dot_nt(_bf(ct_state), qb)
    nq = _dot_nt(_bf(jnp.broadcast_to(n_state, (8, HEAD_DIM))), qb)[0:1, :]
    den = jnp.sum(w, axis=0, keepdims=True) + w_inter * nq
    h_t = num_t / jnp.maximum(jnp.abs(den), jnp.exp(-m_t))
    m_new = m_t[:, rows - 1:rows]
    b_last = b_r[:, rows - 1:rows]
    w_src = jnp.exp(b_last - b_r + logi_r - m_new)
    decay = jnp.exp(b_last + m_state - m_new)
    ct_new = decay * ct_state + _dot(_bf(vt * w_src), kb)
    n_new = decay * n_state + _dot(_bf(jnp.broadcast_to(w_src, (8, rows))), kb)[0:1, :]
    return h_t.T, ct_new, n_new, m_new


def _mlstm_kernel(z_ref, gt_ref, c0_ref, n0_ref, m0_ref, o_ref, c_ref, n_ref, m_ref, *, rows, row0_first):
    c = pl.program_id(1)

    @pl.when(c == 0)
    def _():
        c_ref[...] = c0_ref[...]
        n_ref[...] = n0_ref[...]
        m_ref[...] = m0_ref[...]

    row0 = jnp.where(c == 0, row0_first, 0)
    for s in range(SEQ_GROUP):
        gt = gt_ref[s, 0]
        outs = []
        for h in range(M_HEADS):
            lo = h * HEAD_DIM
            q = z_ref[s, :, lo:lo + HEAD_DIM]
            k = z_ref[s, :, 512 + lo:512 + lo + HEAD_DIM] * (HEAD_DIM ** -0.5)
            v = z_ref[s, :, 1024 + lo:1024 + lo + HEAD_DIM]
            og = z_ref[s, :, 1536 + lo:1536 + lo + HEAD_DIM]
            hh, c_new, n_new, m_new = _mlstm_chunk(
                q, k, v, gt[h:h + 1, :], gt[M_HEADS + h:M_HEADS + h + 1, :],
                c_ref[s, h], n_ref[s, h:h + 1, :], m_ref[s, h:h + 1, 0:1], row0, rows)
            c_ref[s, h] = c_new
            n_ref[s, h:h + 1, :] = n_new
            m_ref[s, h:h + 1, :] = jnp.broadcast_to(m_new, (1, LANES))
            outs.append(_sigmoid(og) * hh)
        o_ref[s] = jnp.concatenate(outs, axis=1)


def _state_specs(heads):
    return [pl.BlockSpec((SEQ_GROUP, heads, HEAD_DIM, HEAD_DIM), lambda b, c: (b, 0, 0, 0)),
            pl.BlockSpec((SEQ_GROUP, heads, HEAD_DIM), lambda b, c: (b, 0, 0)),
            pl.BlockSpec((SEQ_GROUP, heads, LANES), lambda b, c: (b, 0, 0))]


def mlstm_call(z, g2t, c0, n0, m0, *, row0_first):
    nb, t_len, _ = z.shape
    n_chunks, rows = g2t.shape[1], g2t.shape[3]
    assert nb % SEQ_GROUP == 0 and n_chunks * rows == t_len
    blk = lambda b, c: (b, c, 0)
    return pl.pallas_call(
        functools.partial(_mlstm_kernel, rows=rows, row0_first=row0_first),
        grid=(nb // SEQ_GROUP, n_chunks),
        in_specs=[pl.BlockSpec((SEQ_GROUP, rows, 2048), blk),
                  pl.BlockSpec((SEQ_GROUP, 1, 16, rows), lambda b, c: (b, c, 0, 0))] + _state_specs(M_HEADS),
        out_specs=[pl.BlockSpec((SEQ_GROUP, rows, 512), blk)] + _state_specs(M_HEADS),
        out_shape=[jax.ShapeDtypeStruct((nb, t_len, 512), F32),
                   jax.ShapeDtypeStruct((nb, M_HEADS, HEAD_DIM, HEAD_DIM), F32),
                   jax.ShapeDtypeStruct((nb, M_HEADS, HEAD_DIM), F32),
                   jax.ShapeDtypeStruct((nb, M_HEADS, LANES), F32)],
        compiler_params=pltpu.CompilerParams(dimension_semantics=("parallel", "arbitrary")),
        name="mlstm",
    )(z, g2t, c0, n0, m0)


def _ret_log_gamma(h):
    return math.log1p(-(2.0 ** (-5.0 - h)))


def _ret_body(z_ref, cos_ref, sin_ref, lng_ref, o_ref, s_ref, *, rows, row0_first):
    c = pl.program_id(1)
    row0 = jnp.where(c == 0, row0_first, 0)
    n_valid = (rows - row0).astype(F32)
    cosf = cos_ref[...]
    sinf = sin_ref[...]
    ri = _iota((rows, rows), 0)
    ci = _iota((rows, rows), 1)
    diff = (ri - ci).astype(F32)
    rowi = _iota((rows, 1), 0)
    te = (rowi - row0).astype(F32)
    valid = rowi >= row0
    for h in range(R_HEADS):
        lg = _ret_log_gamma(h)
        lo = h * HEAD_DIM
        decay_m = jnp.where(diff >= 0, jnp.exp(diff * lg), 0.0)
        w_inter = jnp.exp((te + 1.0) * lg)
        w_src = jnp.exp((n_valid - 1.0 - te) * lg)
        s_decay = jnp.exp(n_valid * lg)
        for s in range(SEQ_GROUP):
            q = z_ref[s, :, lo:lo + HEAD_DIM]
            k = z_ref[s, :, 512 + lo:512 + lo + HEAD_DIM]
            v = jnp.where(valid, z_ref[s, :, 1024 + lo:1024 + lo + HEAD_DIM], 0.0)
            rg = z_ref[s, :, 1536 + lo:1536 + lo + HEAD_DIM]
            q = q * cosf + pltpu.roll(q, HEAD_DIM // 2, 1) * sinf
            k = (k * cosf + pltpu.roll(k, HEAD_DIM // 2, 1) * sinf) * (HEAD_DIM ** -0.5)
            s_state = s_ref[s, h]
            qb = _bf(q)
            vb = _bf(v)
            intra = _dot(_bf(_dot_nt(qb, _bf(k)) * decay_m), vb)
            inter = _dot(qb, _bf(s_state)) * w_inter
            s_ref[s, h] = s_decay * s_state + _dot_tn(_bf(k * w_src), vb)
            hr = intra + inter
            mu = jnp.mean(hr, axis=-1, keepdims=True)
            var = jnp.mean(jnp.square(hr - mu), axis=-1, keepdims=True)
            hr = (hr - mu) * lax.rsqrt(var + EPS) * lng_ref[h:h + 1, :]
            o_ref[s, :, lo:lo + HEAD_DIM] = rg * _sigmoid(rg) * hr


def _lru_body(z_ref, pre_ref, cw_ref, cb_ref, wa_ref, ba_ref, wx_ref, bx_ref, lam_ref,
              o_ref, hlast_ref, tail_ref, cbuf_s, a_s, b_s, *, rows, row0_first):
    c = pl.program_id(1)
    row0 = jnp.where(c == 0, row0_first, 0)
    rowi = _iota((rows, 1), 0)
    valid = rowi >= row0
    sp_lam = _softplus(-lam_ref[...])
    for s in range(SEQ_GROUP):
        lx = jnp.where(valid, z_ref[s, :, 0:LRU_WIDTH], pre_ref[min(s, pre_ref.shape[0] - 1)])
        cbuf_s[s, 0:8, :] = tail_ref[s]
        cbuf_s[s, 8:8 + rows, :] = lx
        xc = cb_ref[...] + cw_ref[CONV_W - 1:CONV_W, :] * lx
        for j in range(1, CONV_W):
            xc = xc + cw_ref[CONV_W - 1 - j:CONV_W - j, :] * cbuf_s[s, 8 - j:8 - j + rows, :]
        tail_ref[s] = lx[rows - 8:rows]
        xcb = _bf(xc)
        pre_a = []
        pre_x = []
        for p in range(LRU_WIDTH // LANES):
            xs = xcb[:, p * LANES:(p + 1) * LANES]
            pre_a.append(_dot(xs, wa_ref[p]))
            pre_x.append(_dot(xs, wx_ref[p]))
        r = _sigmoid(jnp.concatenate(pre_a, axis=1) + ba_ref[...])
        ig = _sigmoid(jnp.concatenate(pre_x, axis=1) + bx_ref[...])
        log_a = -LRU_C * r * sp_lam
        a = jnp.where(valid, jnp.exp(log_a), 1.0)
        one_minus_a2 = -jnp.tanh(log_a) * (jnp.exp(2.0 * log_a) + 1.0)
        bx = jnp.where(valid, jnp.sqrt(one_minus_a2) * (ig * xc), 0.0)
        a_s[s] = a
        b_s[s] = bx
    carries = [hlast_ref[s] for s in range(SEQ_GROUP)]
    for t in range(rows):
        for s in range(SEQ_GROUP):
            carries[s] = a_s[s, t:t + 1, :] * carries[s] + b_s[s, t:t + 1, :]
            b_s[s, t:t + 1, :] = carries[s]
    for s in range(SEQ_GROUP):
        hlast_ref[s] = carries[s]
        gate = z_ref[s, :, LRU_WIDTH:2 * LRU_WIDTH]
        gl = 0.5 * gate * (1.0 + jnp.tanh(math.sqrt(2.0 / math.pi) * (gate + 0.044715 * gate * gate * gate)))
        o_ref[s] = b_s[s] * gl


def _ret_lru_kernel(zr_ref, cos_ref, sin_ref, lng_ref, s0_ref, zl_ref, pre_ref, h0_ref,
                    cw_ref, cb_ref, wa_ref, ba_ref, wx_ref, bx_ref, lam_ref,
                    or_ref, s_ref, ol_ref, hlast_ref, tail_ref, cbuf_s, a_s, b_s, *, rows, row0_first):
    @pl.when(pl.program_id(1) == 0)
    def _():
        s_ref[...] = s0_ref[...]
        hlast_ref[...] = h0_ref[...]
        tail_ref[...] = jnp.zeros(tail_ref.shape, F32)

    _ret_body(zr_ref, cos_ref, sin_ref, lng_ref, or_ref, s_ref, rows=rows, row0_first=row0_first)
    _lru_body(zl_ref, pre_ref, cw_ref, cb_ref, wa_ref, ba_ref, wx_ref, bx_ref, lam_ref,
              ol_ref, hlast_ref, tail_ref, cbuf_s, a_s, b_s, rows=rows, row0_first=row0_first)


def ret_lru_call(zr, cosf, sinf, lng, s0, zl, pre, h0, cw, cb, wa2, ba, wx2, bx, lam, *, rows, row0_first):
    nb, t_len, _ = zr.shape
    assert nb % SEQ_GROUP == 0 and t_len % rows == 0
    blk = lambda b, c: (b, c, 0)
    per_seq = lambda b, c: (b, 0, 0)
    const2 = lambda b, c: (0, 0)
    const3 = lambda b, c: (0, 0, 0)
    sspec = pl.BlockSpec((SEQ_GROUP, R_HEADS, HEAD_DIM, HEAD_DIM), lambda b, c: (b, 0, 0, 0))
    shared_pre = pre.shape[0] == 1
    in_specs = [pl.BlockSpec((SEQ_GROUP, rows, 2048), blk),
                pl.BlockSpec((rows, HEAD_DIM), lambda b, c: (c, 0)),
                pl.BlockSpec((rows, HEAD_DIM), lambda b, c: (c, 0)),
                pl.BlockSpec((R_HEADS, HEAD_DIM), const2),
                sspec,
                pl.BlockSpec((SEQ_GROUP, rows, 1024), blk),
                pl.BlockSpec((1 if shared_pre else SEQ_GROUP, rows, LRU_WIDTH), const3 if shared_pre else per_seq),
                pl.BlockSpec((SEQ_GROUP, 1, LRU_WIDTH), per_seq),
                pl.BlockSpec((CONV_W, LRU_WIDTH), const2),
                pl.BlockSpec((1, LRU_WIDTH), const2),
                pl.BlockSpec((LRU_WIDTH // LANES, LANES, LANES), const3),
                pl.BlockSpec((1, LRU_WIDTH), const2),
                pl.BlockSpec((LRU_WIDTH // LANES, LANES, LANES), const3),
                pl.BlockSpec((1, LRU_WIDTH), const2),
                pl.BlockSpec((1, LRU_WIDTH), const2)]
    return pl.pallas_call(
        functools.partial(_ret_lru_kernel, rows=rows, row0_first=row0_first),
        grid=(nb // SEQ_GROUP, t_len // rows),
        in_specs=in_specs,
        out_specs=[pl.BlockSpec((SEQ_GROUP, rows, 512), blk), sspec,
                   pl.BlockSpec((SEQ_GROUP, rows, LRU_WIDTH), blk),
                   pl.BlockSpec((SEQ_GROUP, 1, LRU_WIDTH), per_seq),
                   pl.BlockSpec((SEQ_GROUP, 8, LRU_WIDTH), per_seq)],
        out_shape=[jax.ShapeDtypeStruct((nb, t_len, 512), F32),
                   jax.ShapeDtypeStruct((nb, R_HEADS, HEAD_DIM, HEAD_DIM), F32),
                   jax.ShapeDtypeStruct((nb, t_len, LRU_WIDTH), F32),
                   jax.ShapeDtypeStruct((nb, 1, LRU_WIDTH), F32),
                   jax.ShapeDtypeStruct((nb, 8, LRU_WIDTH), F32)],
        scratch_shapes=[pltpu.VMEM((SEQ_GROUP, rows + 8, LRU_WIDTH), F32),
                        pltpu.VMEM((SEQ_GROUP, rows, LRU_WIDTH), F32),
                        pltpu.VMEM((SEQ_GROUP, rows, LRU_WIDTH), F32)],
        compiler_params=pltpu.CompilerParams(dimension_semantics=("parallel", "arbitrary")),
        name="ret_lru",
    )(zr, cosf, sinf, lng, s0, zl, pre, h0, cw, cb, wa2, ba, wx2, bx, lam)


def _fox_cumsum_kernel(g_ref, gt_ref, fx_ref, fr_ref):
    ri = _iota((CHUNK, CHUNK), 0)
    ci = _iota((CHUNK, CHUNK), 1)
    lower = jnp.where(ci <= ri, 1.0, 0.0).astype(BF16)
    upper = jnp.where(ri <= ci, 1.0, 0.0).astype(BF16)
    pr = _iota((3 * LANES, F_HEADS // 2 * LANES), 0)
    pc = _iota((3 * LANES, F_HEADS // 2 * LANES), 1)
    term, gate = pr // LANES, pr % LANES
    pair, lane = pc // LANES, pc % LANES
    first_head = 2 * M_HEADS
    place = (((gate == first_head + 2 * pair) & (lane == F_DH + term))
             | ((gate == first_head + 2 * pair + 1) & (lane == term)))
    place = jnp.where(place, 1.0, 0.0).astype(BF16)
    carry_r = jnp.zeros((1, LANES), F32)
    carry_c = jnp.zeros((16, 1), F32)
    for blk in range(NCH):
        x = g_ref[blk * CHUNK:(blk + 1) * CHUNK, :]
        xt = gt_ref[:, blk * CHUNK:(blk + 1) * CHUNK]
        if blk == 0:
            x = jnp.where(_iota((CHUNK, 1), 0) >= PADF, x, 0.0)
            xt = jnp.where(_iota((1, CHUNK), 1) >= PADF, xt, 0.0)
        cs = _dot01_left(lower, x) + carry_r
        cst = _dot01_right(xt, upper) + carry_c
        carry_r = cs[CHUNK - 1:CHUNK, :]
        carry_c = cst[:, CHUNK - 1:CHUNK]
        fk = cs * LOG2E
        if blk == 0:
            cst = jnp.where(_iota((1, CHUNK), 1) >= PADF, cst, -NEG)
            fk = jnp.where(_iota((CHUNK, 1), 0) >= PADF, fk, -NEG)
        fx_ref[blk * CHUNK:(blk + 1) * CHUNK, :] = _dot(jnp.concatenate(_split3(fk), axis=1), place)
        fr_ref[0, :, blk * CHUNK:(blk + 1) * CHUNK] = cst


def fox_cumsum(g2, g2t):
    return pl.pallas_call(
        _fox_cumsum_kernel,
        grid=(BATCH,),
        in_specs=[pl.BlockSpec((TP, LANES), lambda b: (b, 0)),
                  pl.BlockSpec((16, TP), lambda b: (0, b))],
        out_specs=[pl.BlockSpec((TP, F_HEADS // 2 * LANES), lambda b: (b, 0)),
                   pl.BlockSpec((1, 16, TP), lambda b: (b, 0, 0))],
        out_shape=[jax.ShapeDtypeStruct((SBASE, F_HEADS // 2 * LANES), F32),
                   jax.ShapeDtypeStruct((BATCH, 16, TP), F32)],
        compiler_params=pltpu.CompilerParams(dimension_semantics=("parallel",)),
        name="fox_cumsum",
    )(g2, g2t)


def _fox_prompt_kernel(q_ref, k_ref, v_ref, fr_ref, fx_ref, o_ref,
                       qta_s, qtb_s, ka_s, kb_s, vt_s, m_s, l_s, acc_s, ua_s):
    i = pl.program_id(2)
    lane = _iota((1, LANES), 1)
    is_a = lane < F_DH
    row = _iota((LANES, 1), 0)
    row_a = row < F_DH
    k_heads = (ka_s, kb_s)
    qt_heads = (qta_s, qtb_s)

    @pl.when(i == 0)
    def _():
        k = k_ref[...]
        fx = fx_ref[...]
        ka_s[...] = _bf(jnp.where(is_a, k, fx))
        kb_s[...] = _bf(jnp.where(is_a, fx, k))

        minus_a = jnp.where((row >= F_DH) & (row < F_DH + 3), -1.0, 0.0)
        minus_b = jnp.where(row < 3, -1.0, 0.0)

        def fill(blk, carry):
            rows = pl.ds(pl.multiple_of(blk * CHUNK, CHUNK), CHUNK)
            qt = (q_ref[rows, :] * (F_DH ** -0.5 * LOG2E)).T
            qta_s[blk] = _bf(jnp.where(row_a, qt, minus_a))
            qtb_s[blk] = _bf(jnp.where(row_a, minus_b, qt))
            vt_s[blk] = _bf(v_ref[rows, :].T)
            return carry

        lax.fori_loop(0, NCH, fill, 0)

    def init():
        m_s[...] = jnp.full(m_s.shape, NEG, F32)
        l_s[...] = jnp.zeros(l_s.shape, F32)
        acc_s[...] = jnp.zeros(acc_s.shape, F32)

    def scores(hh, qt, kblk, nkb):
        ck = nkb * CHUNK
        k0 = kblk * CHUNK if isinstance(kblk, int) else pl.multiple_of(kblk * CHUNK, CHUNK)
        return _dot(k_heads[hh][pl.ds(k0, ck), :], qt[hh])

    def step(hh, u, qblk, nqb, kblks, diagonal):
        tq = nqb * CHUNK
        ck = len(kblks) * CHUNK
        fq = jnp.concatenate([fr_ref[0, 0, qblk + t, hh:hh + 1, :] for t in range(nqb)], axis=1)
        fq = jnp.where(fq > 0.5 * -NEG, 0.0, fq * LOG2E)
        if diagonal:
            lead = ck - tq
            ri = _iota((ck, tq), 0)
            u = jnp.where((ri < lead) | (ri - lead <= _iota((ck, tq), 1)), u, NEG)
        m_old = m_s[hh, :, 0:tq]
        m_new = jnp.maximum(m_old, jnp.max(u, axis=0, keepdims=True) + fq)
        m_s[hh, :, 0:tq] = m_new
        alpha = jnp.exp2(m_old - m_new)
        p = _bf(jnp.exp2(u + (fq - m_new)))
        feat = slice(hh * F_DH, (hh + 1) * F_DH)
        vt = jnp.concatenate([vt_s[kb, feat, :] for kb in kblks], axis=1)
        pv = _dot(jnp.concatenate([vt, jnp.ones((16, ck), BF16)], axis=0), p)
        l_s[hh, :, 0:tq] = alpha * l_s[hh, :, 0:tq] + pv[F_DH:F_DH + 1, :]
        acc_s[feat, 0:tq] = alpha * acc_s[feat, 0:tq] + pv[0:F_DH, :]

    def finalize(qblk, nqb):
        tq = nqb * CHUNK
        o_t = acc_s[:, 0:tq] / jnp.where(row_a, l_s[0, :, 0:tq], l_s[1, :, 0:tq])
        for t in range(nqb):
            rows = pl.ds(pl.multiple_of((qblk + t) * CHUNK, CHUNK), CHUNK)
            o_ref[rows, :] = o_t[:, t * CHUNK:(t + 1) * CHUNK].T

    @pl.when(i == 0)
    def _():
        init()
        qt0 = (qta_s[0], qtb_s[0])
        for hh in range(2):
            step(hh, scores(hh, qt0, 0, 1), 0, 1, [0], True)
        finalize(0, 1)

    nck = FOX_CK // CHUNK
    nqb = FOX_TQ // CHUNK
    qblk = 1 + i * nqb
    qt = tuple(jnp.concatenate([qt_h[qblk + t] for t in range(nqb)], axis=1) for qt_h in qt_heads)
    init()

    ua_s[...] = scores(0, qt, 1, nck)

    def body(j, carry):
        kblk = 1 + j * nck
        kblks = [kblk + t for t in range(nck)]
        ub = scores(1, qt, kblk, nck)
        step(0, ua_s[...], qblk, nqb, kblks, False)
        ua_s[...] = scores(0, qt, kblk + nck, nck)
        step(1, ub, qblk, nqb, kblks, False)
        return carry

    lax.fori_loop(0, i, body, 0)
    last = [0] + [qblk + t for t in range(nck)]
    ua = jnp.concatenate([scores(0, qt, 0, 1), ua_s[...]], axis=0)
    ub = jnp.concatenate([scores(1, qt, 0, 1), scores(1, qt, qblk, nck)], axis=0)
    step(0, ua, qblk, nqb, last, True)
    step(1, ub, qblk, nqb, last, True)
    finalize(qblk, nqb)


def fox_prompt(zq, zk, zv, fr, fx):
    nq = (TP - CHUNK) // FOX_TQ
    pairs = F_HEADS // 2
    return pl.pallas_call(
        _fox_prompt_kernel,
        grid=(BATCH, pairs, nq),
        in_specs=[pl.BlockSpec((TP, LANES), lambda b, p, i: (b, p)),
                  pl.BlockSpec((TP, LANES), lambda b, p, i: (b, p)),
                  pl.BlockSpec((TP, LANES), lambda b, p, i: (b, p)),
                  pl.BlockSpec((1, 1, NCH, 2, LANES), lambda b, p, i: (b, p, 0, 0, 0)),
                  pl.BlockSpec((TP, LANES), lambda b, p, i: (b, p))],
        out_specs=pl.BlockSpec((TP, LANES), lambda b, p, i: (b, p)),
        out_shape=jax.ShapeDtypeStruct((SBASE, 512), F32),
        scratch_shapes=[pltpu.VMEM((NCH, LANES, CHUNK), BF16),
                        pltpu.VMEM((NCH, LANES, CHUNK), BF16),
                        pltpu.VMEM((TP, LANES), BF16),
                        pltpu.VMEM((TP, LANES), BF16),
                        pltpu.VMEM((NCH, LANES, CHUNK), BF16),
                        pltpu.VMEM((2, 1, FOX_TQ), F32),
                        pltpu.VMEM((2, 1, FOX_TQ), F32),
                        pltpu.VMEM((LANES, FOX_TQ), F32),
                        pltpu.VMEM((FOX_CK, FOX_TQ), F32)],
        compiler_params=pltpu.CompilerParams(dimension_semantics=("parallel", "parallel", "arbitrary")),
        name="fox_prompt",
    )(zq, zk, zv, fr, fx)


def _lf_suffix_kernel(x_ref, o_ref):
    ri = _iota((PAGE_SIZE, PAGE_SIZE), 0)
    ci = _iota((PAGE_SIZE, PAGE_SIZE), 1)
    after = jnp.where(ri > ci, 1.0, 0.0).astype(BF16)
    ones = jnp.ones((PAGE_SIZE, PAGE_SIZE), BF16)
    x = x_ref[...].reshape(PRE_PG * F_HEADS, PAGE_SIZE)
    o_ref[:, 0:F_HEADS, :] = _dot01_right(x, after).reshape(PRE_PG, F_HEADS, PAGE_SIZE)
    o_ref[:, F_HEADS:2 * F_HEADS, :] = _dot01_right(x, ones).reshape(PRE_PG, F_HEADS, PAGE_SIZE)


def lf_suffix(lft):
    n_phys = lft.shape[0]
    return pl.pallas_call(
        _lf_suffix_kernel,
        grid=(n_phys // PRE_PG,),
        in_specs=[pl.BlockSpec((PRE_PG, F_HEADS, PAGE_SIZE), lambda i: (i, 0, 0))],
        out_specs=pl.BlockSpec((PRE_PG, 2 * F_HEADS, PAGE_SIZE), lambda i: (i, 0, 0)),
        out_shape=jax.ShapeDtypeStruct((n_phys, 2 * F_HEADS, PAGE_SIZE), F32),
        compiler_params=pltpu.CompilerParams(dimension_semantics=("parallel",)),
        name="lf_suffix",
    )(lft)


def _fox_decode_kernel(pt_ref, zq_ref, zk_ref, zv_ref, gt_ref, *rest):
    k_refs = rest[0:DEC_PG]
    v_refs = rest[DEC_PG:2 * DEC_PG]
    lf_refs = rest[2 * DEC_PG:3 * DEC_PG]
    o_ref = rest[3 * DEC_PG]
    q2_s, m_s, l_s, acc_s, run_s, nc_s = rest[3 * DEC_PG + 1:]
    j = pl.program_id(1)
    n_rows = DEC_SEQ * F_HEADS
    width = F_HEADS * F_DH

    def update(u, v, v_feature_major):
        m_old = m_s[...]
        m_new = jnp.maximum(m_old, jnp.max(u, axis=1, keepdims=True))
        alpha = jnp.exp(m_old - m_new)
        p = jnp.exp(u - m_new[:, 0:u.shape[1]])
        pv = _dot_nt(_bf(p), v) if v_feature_major else _dot(_bf(p), v)
        l_s[...] = alpha * l_s[...] + jnp.sum(p, axis=1, keepdims=True)
        acc_s[...] = jnp.concatenate([alpha] * (width // LANES), axis=1) * acc_s[...] + pv
        m_s[...] = m_new

    @pl.when(j == 0)
    def _():
        m_s[...] = jnp.full(m_s.shape, NEG, F32)
        l_s[...] = jnp.zeros(l_s.shape, F32)
        acc_s[...] = jnp.zeros(acc_s.shape, F32)
        run_s[...] = jnp.zeros(run_s.shape, F32)
        head_mask = _iota((F_HEADS, width), 1) // F_DH == _iota((F_HEADS, width), 0)
        q16 = zq_ref[...] * (F_DH ** -0.5)
        q2 = _bf(jnp.concatenate(
            [jnp.where(head_mask, q16[SPAD + t:SPAD + t + 1, :], 0.0) for t in range(DEC_SEQ)], axis=0))
        q2_s[...] = q2
        lf_new = gt_ref[F_HEADS:2 * F_HEADS, :]
        lane = _iota((1, SROWS), 1)
        cum = jnp.zeros((F_HEADS, SROWS), F32)
        for t in range(DEC_SEQ):
            cum = cum + jnp.where(lane >= SPAD + t, lf_new[:, SPAD + t:SPAD + t + 1], 0.0)
        nc = jnp.concatenate([cum[:, SPAD + t:SPAD + t + 1] for t in range(DEC_SEQ)], axis=0)
        nc_s[...] = jnp.broadcast_to(nc, nc_s.shape)
        cum4 = jnp.concatenate([cum] * DEC_SEQ, axis=0)
        qi = _iota((n_rows, SROWS), 0) // F_HEADS
        kj = _iota((n_rows, SROWS), 1) - SPAD
        u = jnp.where((kj >= 0) & (kj <= qi), _dot_nt(q2, _bf(zk_ref[...])) + (nc - cum4), NEG)
        update(u, _bf(zv_ref[...]), False)

    q2 = q2_s[...]
    nc = nc_s[...]
    run = run_s[...]
    us = []
    for i in range(DEC_PG):
        later = run + lf_refs[i][0, 0:F_HEADS, :]
        bias = jnp.concatenate([later] * DEC_SEQ, axis=0) + nc
        us.append(_dot(q2, _bf(k_refs[i][0].reshape(width, PAGE_SIZE))) + bias)
        run = run + lf_refs[i][0, F_HEADS:2 * F_HEADS, :]
    run_s[...] = run
    u = jnp.concatenate(us, axis=1)
    m_old = m_s[...]
    m_new = jnp.maximum(m_old, jnp.max(u, axis=1, keepdims=True))
    alpha = jnp.exp(m_old - m_new)
    p = jnp.exp(u - jnp.concatenate([m_new] * DEC_PG, axis=1))
    l_s[...] = alpha * l_s[...] + jnp.sum(p, axis=1, keepdims=True)
    pv = _dot_nt(_bf(p[:, 0:PAGE_SIZE]), _bf(v_refs[0][0].reshape(width, PAGE_SIZE)))
    for i in range(1, DEC_PG):
        pv = pv + _dot_nt(_bf(p[:, i * PAGE_SIZE:(i + 1) * PAGE_SIZE]), _bf(v_refs[i][0].reshape(width, PAGE_SIZE)))
    acc_s[...] = jnp.concatenate([alpha] * (width // LANES), axis=1) * acc_s[...] + pv
    m_s[...] = m_new

    @pl.when(j == pl.num_programs(1) - 1)
    def _():
        head_mask = (_iota((n_rows, width), 1) // F_DH) == (_iota((n_rows, width), 0) % F_HEADS)
        o2 = jnp.where(head_mask, acc_s[...] / jnp.concatenate([l_s[...]] * (width // LANES), axis=1), 0.0)
        o_ref[...] = jnp.zeros(o_ref.shape, F32)
        for t in range(DEC_SEQ):
            o_ref[SPAD + t:SPAD + t + 1, :] = jnp.sum(o2[t * F_HEADS:(t + 1) * F_HEADS, :], axis=0, keepdims=True)


def fox_decode(page_table, zq, zk, zv, g2t_s, cache_kt, cache_vt, lf_sums):
    n_pages = page_table.shape[1]
    width = F_HEADS * F_DH

    def page_map(i):
        return lambda b, j, pt: (pt[b, n_pages - 1 - (j * DEC_PG + i)], 0, 0, 0)

    def page_map3(i):
        return lambda b, j, pt: (pt[b, n_pages - 1 - (j * DEC_PG + i)], 0, 0)

    in_specs = [pl.BlockSpec((SROWS, width), lambda b, j, pt: (b, 0)),
                pl.BlockSpec((SROWS, width), lambda b, j, pt: (b, 0)),
                pl.BlockSpec((SROWS, width), lambda b, j, pt: (b, 0)),
                pl.BlockSpec((None, 16, SROWS), lambda b, j, pt: (b, 0, 0))]
    in_specs += [pl.BlockSpec((1, F_HEADS, F_DH, PAGE_SIZE), page_map(i)) for i in range(DEC_PG)]
    in_specs += [pl.BlockSpec((1, F_HEADS, F_DH, PAGE_SIZE), page_map(i)) for i in range(DEC_PG)]
    in_specs += [pl.BlockSpec((1, 2 * F_HEADS, PAGE_SIZE), page_map3(i)) for i in range(DEC_PG)]
    args = [zq, zk, zv, g2t_s] + [cache_kt] * DEC_PG + [cache_vt] * DEC_PG + [lf_sums] * DEC_PG
    n_rows = DEC_SEQ * F_HEADS
    grid_spec = pltpu.PrefetchScalarGridSpec(
        num_scalar_prefetch=1,
        grid=(DEC_BATCH, n_pages // DEC_PG),
        in_specs=in_specs,
        out_specs=pl.BlockSpec((SROWS, width), lambda b, j, pt: (b, 0)),
        scratch_shapes=[pltpu.VMEM((n_rows, width), BF16),
                        pltpu.VMEM((n_rows, LANES), F32),
                        pltpu.VMEM((n_rows, LANES), F32),
                        pltpu.VMEM((n_rows, width), F32),
                        pltpu.VMEM((F_HEADS, PAGE_SIZE), F32),
                        pltpu.VMEM((n_rows, LANES), F32)])
    return pl.pallas_call(
        _fox_decode_kernel,
        grid_spec=grid_spec,
        out_shape=jax.ShapeDtypeStruct((DEC_BATCH * SROWS, width), F32),
        compiler_params=pltpu.CompilerParams(dimension_semantics=("parallel", "arbitrary")),
        name="fox_decode",
    )(page_table, *args)


def _block_diag_pairs(w):
    z = jnp.zeros((LANES // 2, LANES // 2), w.dtype)
    pairs = [jnp.block([[w[2 * p], z], [z, w[2 * p + 1]]]) for p in range(w.shape[0] // 2)]
    return _bf(jnp.stack(pairs))


def _rope_tables(pos):
    half = HEAD_DIM // 2
    freq = ROPE_BASE ** (-jnp.arange(half, dtype=F32) / half)
    ang = pos.astype(F32)[:, None] * freq[None, :]
    cos, sin = jnp.cos(ang), jnp.sin(ang)
    return jnp.concatenate([cos, cos], axis=1), jnp.concatenate([-sin, sin], axis=1)


def _unpad_prompt(x):
    return x.reshape((BATCH, TP) + x.shape[1:])[:, PADF:]


def _unpad_sample(x):
    return x.reshape((DEC_BATCH, SROWS) + x.shape[1:])[:, SPAD:]


def kernel(x_prompt, x_sample, cache_fox_k, cache_fox_v, cache_fox_logf, state_mlstm_C, state_mlstm_n, state_mlstm_m, state_ret_S, state_lru_h, state_lru_conv, page_table, meta_tokens, w_in_even, b_mlstm_i, b_mlstm_f, b_fox_f, w_out_even, w_in_odd, ret_ln_g, conv_w, conv_b, lru_wa, lru_ba, lru_wx, lru_bx, lru_lambda, w_out_odd, norm_g, ffn_wg, ffn_wu, ffn_wd):
    n_pages = page_table.shape[1]
    past = n_pages * PAGE_SIZE
    n_phys = cache_fox_k.shape[0]

    head = jnp.concatenate([jnp.zeros((PADF, D_MODEL), F32), meta_tokens.astype(F32)], axis=0)
    hs = jnp.concatenate([jnp.zeros((DEC_BATCH, SPAD, D_MODEL), F32), x_sample.astype(F32)],
                         axis=1).reshape(DEC_BATCH * SROWS, D_MODEL)
    h_s = hs
    h_p = jnp.concatenate([jnp.broadcast_to(head[None], (BATCH, CHUNK, D_MODEL)), x_prompt.astype(F32)],
                          axis=1).reshape(SBASE, D_MODEL)

    c = np.cumsum([0, 512, 512, 512, 512, 4, 4, 512, 512, 512, 8])
    gate_w = jnp.concatenate([w_in_even[:, c[4]:c[6]], w_in_even[:, c[9]:c[10]],
                              jnp.zeros((D_MODEL, LANES - 16), F32)], axis=1)
    w_even = _bf(jnp.concatenate([w_in_even[:, c[0]:c[4]], w_in_even[:, c[6]:c[9]], gate_w], axis=1))
    gate_b = jnp.concatenate([b_mlstm_i, b_mlstm_f, b_fox_f]).astype(F32)
    brow = jnp.concatenate([gate_b, jnp.zeros((LANES - 16,), F32)]).reshape(1, LANES)
    w_odd = _bf(w_in_odd)
    wo_even = _bf(w_out_even)
    wo_odd = _bf(w_out_odd)
    wg = _bf(ffn_wg)
    wu = _bf(ffn_wu)
    wd = _bf(ffn_wd)
    ng = norm_g.astype(F32).reshape(2, 4, 1, D_MODEL)

    even_cols = (2048, 512, 512, 512, LANES)
    zm_p, zq_p, zk_p, zv_p, g2_p, g2t_p = norm_matmul(h_p, ng[0, 0], w_even, even_cols, gate_bias=brow)
    zm_s, zq_s, zk_s, zv_s, g2_s, g2t_s = norm_matmul(h_s, ng[0, 0], w_even, even_cols, gate_bias=brow)
    prompt3 = lambda x: x.reshape(BATCH, TP, x.shape[-1])
    sample3 = lambda x: x.reshape(DEC_BATCH, SROWS, x.shape[-1])
    flat2 = lambda x: x.reshape(-1, x.shape[-1])

    zeros_c = jnp.zeros((BATCH, M_HEADS, HEAD_DIM, HEAD_DIM), F32)
    zeros_n = jnp.zeros((BATCH, M_HEADS, HEAD_DIM), F32)
    hm_p, p_ct, p_n, p_m = mlstm_call(prompt3(zm_p), g2t_p.reshape(16, BATCH, NCH, CHUNK).transpose(1, 2, 0, 3),
                                      zeros_c, zeros_n, zeros_n, row0_first=PADF)
    p_c = p_ct.transpose(0, 1, 3, 2)
    m0_s = jnp.broadcast_to(state_mlstm_m.astype(F32)[:, :, None], (DEC_BATCH, M_HEADS, LANES))
    hm_s, s_ct, s_n, s_m = mlstm_call(sample3(zm_s), g2t_s.reshape(16, DEC_BATCH, 1, SROWS).transpose(1, 2, 0, 3),
                                      state_mlstm_C.astype(F32).transpose(0, 1, 3, 2), state_mlstm_n.astype(F32),
                                      m0_s, row0_first=SPAD)
    s_c = s_ct.transpose(0, 1, 3, 2)

    fx, fr = fox_cumsum(g2_p, g2t_p)
    pairs = F_HEADS // 2
    fr_p = fr[:, 8:16].reshape(BATCH, pairs, 2, NCH, CHUNK).transpose(0, 1, 3, 2, 4)
    hf_p = fox_prompt(zq_p, zk_p, zv_p, fr_p, fx)
    lf_sums = lf_suffix(cache_fox_logf.astype(F32).transpose(0, 2, 1))
    hf_s = fox_decode(page_table, zq_s, zk_s, zv_s, g2t_s.reshape(16, DEC_BATCH, SROWS).transpose(1, 0, 2),
                      cache_fox_k.astype(F32).transpose(0, 2, 3, 1), cache_fox_v.astype(F32).transpose(0, 2, 3, 1),
                      lf_sums)

    gains = norm_g.astype(F32)
    even_w = (wo_even[:512], wo_even[512:], gains[0, 1:4], wg[0], wu[0], wd[0])
    h_p = mix_ffn(flat2(hm_p), hf_p, h_p, *even_w)
    h_s = mix_ffn(flat2(hm_s), hf_s, h_s, *even_w)

    zr_p, zl_p = norm_matmul(h_p, ng[1, 0], w_odd, (2048, 1024))
    zr_s, zl_s = norm_matmul(h_s, ng[1, 0], w_odd, (2048, 1024))
    cos_p, sin_p = _rope_tables(jnp.arange(TP) - PADF)
    cos_s, sin_s = _rope_tables(past + jnp.arange(SROWS) - SPAD)
    lng = ret_ln_g.astype(F32)
    lru_args = (conv_w.astype(F32), conv_b.astype(F32).reshape(1, LRU_WIDTH),
                _block_diag_pairs(lru_wa), lru_ba.astype(F32).reshape(1, LRU_WIDTH),
                _block_diag_pairs(lru_wx), lru_bx.astype(F32).reshape(1, LRU_WIDTH),
                lru_lambda.astype(F32).reshape(1, LRU_WIDTH))
    hr_p, p_s, hl_p, p_h, p_tail = ret_lru_call(
        prompt3(zr_p), cos_p, sin_p, lng, jnp.zeros((BATCH, R_HEADS, HEAD_DIM, HEAD_DIM), F32),
        prompt3(zl_p), jnp.zeros((1, CHUNK, LRU_WIDTH), F32), jnp.zeros((BATCH, 1, LRU_WIDTH), F32), *lru_args,
        rows=CHUNK, row0_first=PADF)
    pre_s = jnp.concatenate([jnp.zeros((DEC_BATCH, SPAD - (CONV_W - 1), LRU_WIDTH), F32),
                             state_lru_conv.astype(F32),
                             jnp.zeros((DEC_BATCH, DEC_SEQ, LRU_WIDTH), F32)], axis=1)
    hr_s, s_s, hl_s, s_h, s_tail = ret_lru_call(
        sample3(zr_s), cos_s, sin_s, lng, state_ret_S.astype(F32),
        sample3(zl_s), pre_s, state_lru_h.astype(F32).reshape(DEC_BATCH, 1, LRU_WIDTH), *lru_args,
        rows=SROWS, row0_first=SPAD)

    odd_w = (wo_odd[:512], wo_odd[512:], gains[1, 1:4], wg[1], wu[1], wd[1])
    y_prompt = mix_ffn(hr_p, hl_p, prompt3(h_p), *odd_w, main_only=True)
    h_s = mix_ffn(flat2(hr_s), flat2(hl_s), h_s, *odd_w)

    y_sample = h_s.reshape(DEC_BATCH, SROWS, D_MODEL)[:, SPAD:]
    heads = lambda x: x.reshape(x.shape[:2] + (F_HEADS, F_DH))
    nback = CONV_W - 1
    return (y_prompt, y_sample,
            heads(_unpad_prompt(zk_p)), heads(_unpad_prompt(zv_p)), _unpad_prompt(g2_p[:, 8:16]),
            heads(_unpad_sample(zk_s)), heads(_unpad_sample(zv_s)), _unpad_sample(g2_s[:, 8:16]),
            p_c, p_n, p_m[:, :, 0], s_c, s_n, s_m[:, :, 0],
            p_s, s_s,
            p_h[:, 0], s_h[:, 0], p_tail[:, 8 - nback:], s_tail[:, 8 - nback:])
```

```python
import functools
import math

import jax
import jax.numpy as jnp
import numpy as np
from jax import lax
from jax.experimental import pallas as pl
from jax.experimental.pallas import tpu as pltpu

F32 = jnp.float32
BF16 = jnp.bfloat16

D_MODEL = 1024
BATCH = 4
SEQ = 4096
DEC_BATCH = 32
DEC_SEQ = 4
PAGE_SIZE = 128
N_META = 16
CHUNK = 128
M_HEADS = 4
F_HEADS = 8
F_DH = 64
R_HEADS = 4
HEAD_DIM = 128
ROPE_BASE = 10000.0
LRU_WIDTH = 512
LRU_C = 8.0
CONV_W = 4
D_FF = 2816
EPS = 1e-6
NEG = -1e30

LANES = 128
PADF = CHUNK - N_META
TP = PADF + N_META + SEQ
NCH = TP // CHUNK
SROWS = 16
SPAD = SROWS - DEC_SEQ
SBASE = BATCH * TP
NP = SBASE + DEC_BATCH * SROWS
TM = 512
SEQ_GROUP = 4
MXU_WIDTH = 256
MIX_FFN_VMEM_BYTES = 56 * 1024 * 1024
FOX_TQ = 512
FOX_CK = 512
LOG2E = math.log2(math.e)
DEC_PG = 32
PRE_PG = 128


def _bf(x):
    return x.astype(BF16)


def _dot(a, b):
    return jnp.dot(a, b, preferred_element_type=F32)


def _dot_nt(a, b):
    return lax.dot_general(a, b, (((1,), (1,)), ((), ())), preferred_element_type=F32)


def _dot_tn(a, b):
    return lax.dot_general(a, b, (((0,), (0,)), ((), ())), preferred_element_type=F32)


def _split3(x):
    hi = _bf(x)
    r1 = x - hi.astype(F32)
    mid = _bf(r1)
    lo = _bf(r1 - mid.astype(F32))
    return hi, mid, lo


def _dot01_right(x, m01):
    hi, mid, lo = _split3(x)
    return _dot(hi, m01) + _dot(mid, m01) + _dot(lo, m01)


def _dot01_left(m01, x):
    hi, mid, lo = _split3(x)
    return _dot(m01, hi) + _dot(m01, mid) + _dot(m01, lo)


def _iota(shape, dim):
    return lax.broadcasted_iota(jnp.int32, shape, dim)


def _rms(x, g):
    ms = jnp.mean(x * x, axis=-1, keepdims=True)
    return x * lax.rsqrt(ms + EPS) * g


def _softplus(x):
    return jnp.maximum(x, 0.0) + jnp.log1p(jnp.exp(-jnp.abs(x)))


def _log_sigmoid(x):
    return -_softplus(-x)


def _sigmoid(x):
    return 0.5 * jnp.tanh(0.5 * x) + 0.5


def _nm_kernel(x_ref, g_ref, w_ref, *rest, splits, gated):
    if gated:
        b_ref, o_refs = rest[0], rest[1:]
    else:
        o_refs = rest
    xn = _bf(_rms(x_ref[...], g_ref[...]))
    off = 0
    for k, n in enumerate(splits):
        res = _dot(xn, w_ref[:, off:off + n])
        off += n
        if gated and k == len(splits) - 1:
            res = res + b_ref[...]
            res = jnp.where(_iota(res.shape, 1) < M_HEADS, res, _log_sigmoid(res))
            o_refs[k + 1][...] = res.T[0:16, :]
        o_refs[k][...] = res


def norm_matmul(x, g, w, splits, gate_bias=None):
    n_rows, k = x.shape
    const = lambda i: (0, 0)
    gated = gate_bias is not None
    in_specs = [pl.BlockSpec((TM, k), lambda i: (i, 0)),
                pl.BlockSpec((1, k), const),
                pl.BlockSpec((k, sum(splits)), const, pipeline_mode=pl.Buffered(1))]
    out_specs = [pl.BlockSpec((TM, n), lambda i: (i, 0)) for n in splits]
    out_shape = [jax.ShapeDtypeStruct((n_rows, n), F32) for n in splits]
    args = [x, g, w]
    if gated:
        assert splits[-1] == LANES
        in_specs.append(pl.BlockSpec((1, LANES), const))
        out_specs.append(pl.BlockSpec((16, TM), lambda i: (0, i)))
        out_shape.append(jax.ShapeDtypeStruct((16, n_rows), F32))
        args.append(gate_bias)
    return pl.pallas_call(
        functools.partial(_nm_kernel, splits=splits, gated=gated),
        grid=(n_rows // TM,),
        in_specs=in_specs,
        out_specs=out_specs,
        out_shape=out_shape,
        compiler_params=pltpu.CompilerParams(dimension_semantics=("parallel",)),
        name="norm_matmul",
    )(*args)


def _mix_ffn_kernel(*refs, parts):
    def rows(piece_refs):
        if parts == 1:
            return piece_refs[0][...]
        return jnp.concatenate([r[0] for r in piece_refs], axis=0)

    a1 = rows(refs[0:parts])
    a2 = rows(refs[parts:2 * parts])
    h = rows(refs[2 * parts:3 * parts])
    w1_ref, w2_ref, g_ref, wg_ref, wu_ref, wd_ref, o_ref = refs[3 * parts:]
    mix = _dot(_bf(a1), w1_ref[...]) + _dot(_bf(a2), w2_ref[...])
    h1 = h + _rms(mix, g_ref[0:1, :])
    xn = _bf(_rms(h1, g_ref[1:2, :]))
    acts = []
    for c in range(wg_ref.shape[1] // MXU_WIDTH):
        cols = slice(c * MXU_WIDTH, (c + 1) * MXU_WIDTH)
        gate = _dot(xn, wg_ref[:, cols])
        up = _dot(xn, wu_ref[:, cols])
        acts.append(_bf(gate * _sigmoid(gate) * up))
    ff = _dot(jnp.concatenate(acts, axis=1), wd_ref[...])
    res = h1 + _rms(ff, g_ref[2:3, :])
    o_ref[...] = res if parts == 1 else res[None]


def mix_ffn(a1, a2, h, w1, w2, g, wg, wu, wd, main_only=False):
    k1 = a1.shape[-1]
    k2 = a2.shape[-1]
    d = h.shape[-1]
    dff = wg.shape[1]
    assert dff % MXU_WIDTH == 0
    resident = lambda shape: pl.BlockSpec(shape, lambda *_: (0,) * len(shape), pipeline_mode=pl.Buffered(1))
    weights = [resident((k1, d)), resident((k2, d)), resident((3, d)),
               resident((d, dff)), resident((d, dff)), resident((dff, d))]
    if main_only:
        parts = TM // CHUNK
        nb, t_len = h.shape[0], h.shape[1] - CHUNK
        assert t_len % TM == 0
        piece = lambda k, t: pl.BlockSpec((1, CHUNK, k), lambda b, j: (b, 1 + j * parts + t, 0))
        in_specs = ([piece(k1, t) for t in range(parts)] + [piece(k2, t) for t in range(parts)]
                    + [piece(d, t) for t in range(parts)] + weights)
        args = [a1] * parts + [a2] * parts + [h] * parts
        grid = (nb, t_len // TM)
        out_specs = pl.BlockSpec((1, TM, d), lambda b, j: (b, j, 0))
        out_shape = jax.ShapeDtypeStruct((nb, t_len, d), F32)
    else:
        parts = 1
        n_rows = h.shape[0]
        assert n_rows % TM == 0
        row_blk = lambda i: (i, 0)
        in_specs = [pl.BlockSpec((TM, k1), row_blk), pl.BlockSpec((TM, k2), row_blk),
                    pl.BlockSpec((TM, d), row_blk)] + weights
        args = [a1, a2, h]
        grid = (n_rows // TM,)
        out_specs = pl.BlockSpec((TM, d), row_blk)
        out_shape = jax.ShapeDtypeStruct((n_rows, d), F32)
    return pl.pallas_call(
        functools.partial(_mix_ffn_kernel, parts=parts),
        grid=grid,
        in_specs=in_specs,
        out_specs=out_specs,
        out_shape=out_shape,
        compiler_params=pltpu.CompilerParams(dimension_semantics=("parallel",) * len(grid),
                                             vmem_limit_bytes=MIX_FFN_VMEM_BYTES),
        name="mix_ffn",
    )(*args, w1, w2, g, wg, wu, wd)


def _mlstm_chunk(q, k, v, logi_r, lf_r, ct_state, n_state, m_state, row0, rows):
    ri = _iota((rows, rows), 0)
    ci = _iota((rows, rows), 1)
    valid_r = _iota((1, rows), 1) >= row0
    lf_r = jnp.where(valid_r, lf_r, 0.0)
    logi_r = jnp.where(valid_r, logi_r, NEG)
    upper = jnp.where(ri <= ci, 1.0, 0.0).astype(BF16)
    a = jnp.where(ci > ri, lf_r, jnp.where(ci == ri, logi_r, 0.0))
    pre = _dot01_right(jnp.concatenate([a, jnp.broadcast_to(lf_r, (8, rows))], axis=0), upper)
    b_r = pre[rows:rows + 1, :]
    log_d = jnp.where(ri <= ci, pre[0:rows, :], -jnp.inf)
    m_inter = b_r + m_state
    m_t = jnp.maximum(m_inter, jnp.max(log_d, axis=0, keepdims=True))
    qb = _bf(q)
    kb = _bf(k)
    w = _dot_nt(kb, qb) * jnp.exp(log_d - m_t)
    w_inter = jnp.exp(m_inter - m_t)
    vt = v.T
    num_t = _dot(_bf(vt), _bf(w)) + w_inter * _dot_nt(_bf(ct_state), qb)
    nq = _dot_nt(_bf(jnp.broadcast_to(n_state, (8, HEAD_DIM))), qb)[0:1, :]
    den = jnp.sum(w, axis=0, keepdims=True) + w_inter * nq
    h_t = num_t / jnp.maximum(jnp.abs(den), jnp.exp(-m_t))
    m_new = m_t[:, rows - 1:rows]
    b_last = b_r[:, rows - 1:rows]
    w_src = jnp.exp(b_last - b_r + logi_r - m_new)
    decay = jnp.exp(b_last + m_state - m_new)
    ct_new = decay * ct_state + _dot(_bf(vt * w_src), kb)
    n_new = decay * n_state + _dot(_bf(jnp.broadcast_to(w_src, (8, rows))), kb)[0:1, :]
    return h_t.T, ct_new, n_new, m_new


def _mlstm_kernel(z_ref, gt_ref, c0_ref, n0_ref, m0_ref, o_ref, c_ref, n_ref, m_ref, *, rows, row0_first):
    c = pl.program_id(1)

    @pl.when(c == 0)
    def _():
        c_ref[...] = c0_ref[...]
        n_ref[...] = n0_ref[...]
        m_ref[...] = m0_ref[...]

    row0 = jnp.where(c == 0, row0_first, 0)
    for s in range(SEQ_GROUP):
        gt = gt_ref[s, 0]
        outs = []
        for h in range(M_HEADS):
            lo = h * HEAD_DIM
            q = z_ref[s, :, lo:lo + HEAD_DIM]
            k = z_ref[s, :, 512 + lo:512 + lo + HEAD_DIM] * (HEAD_DIM ** -0.5)
            v = z_ref[s, :, 1024 + lo:1024 + lo + HEAD_DIM]
            og = z_ref[s, :, 1536 + lo:1536 + lo + HEAD_DIM]
            hh, c_new, n_new, m_new = _mlstm_chunk(
                q, k, v, gt[h:h + 1, :], gt[M_HEADS + h:M_HEADS + h + 1, :],
                c_ref[s, h], n_ref[s, h:h + 1, :], m_ref[s, h:h + 1, 0:1], row0, rows)
            c_ref[s, h] = c_new
            n_ref[s, h:h + 1, :] = n_new
            m_ref[s, h:h + 1, :] = jnp.broadcast_to(m_new, (1, LANES))
            outs.append(_sigmoid(og) * hh)
        o_ref[s] = jnp.concatenate(outs, axis=1)


def _state_specs(heads):
    return [pl.BlockSpec((SEQ_GROUP, heads, HEAD_DIM, HEAD_DIM), lambda b, c: (b, 0, 0, 0)),
            pl.BlockSpec((SEQ_GROUP, heads, HEAD_DIM), lambda b, c: (b, 0, 0)),
            pl.BlockSpec((SEQ_GROUP, heads, LANES), lambda b, c: (b, 0, 0))]


def mlstm_call(z, g2t, c0, n0, m0, *, row0_first):
    nb, t_len, _ = z.shape
    n_chunks, rows = g2t.shape[1], g2t.shape[3]
    assert nb % SEQ_GROUP == 0 and n_chunks * rows == t_len
    blk = lambda b, c: (b, c, 0)
    return pl.pallas_call(
        functools.partial(_mlstm_kernel, rows=rows, row0_first=row0_first),
        grid=(nb // SEQ_GROUP, n_chunks),
        in_specs=[pl.BlockSpec((SEQ_GROUP, rows, 2048), blk),
                  pl.BlockSpec((SEQ_GROUP, 1, 16, rows), lambda b, c: (b, c, 0, 0))] + _state_specs(M_HEADS),
        out_specs=[pl.BlockSpec((SEQ_GROUP, rows, 512), blk)] + _state_specs(M_HEADS),
        out_shape=[jax.ShapeDtypeStruct((nb, t_len, 512), F32),
                   jax.ShapeDtypeStruct((nb, M_HEADS, HEAD_DIM, HEAD_DIM), F32),
                   jax.ShapeDtypeStruct((nb, M_HEADS, HEAD_DIM), F32),
                   jax.ShapeDtypeStruct((nb, M_HEADS, LANES), F32)],
        compiler_params=pltpu.CompilerParams(dimension_semantics=("parallel", "arbitrary")),
        name="mlstm",
    )(z, g2t, c0, n0, m0)


def _ret_log_gamma(h):
    return math.log1p(-(2.0 ** (-5.0 - h)))


def _ret_body(z_ref, cos_ref, sin_ref, lng_ref, o_ref, s_ref, *, rows, row0_first):
    c = pl.program_id(1)
    row0 = jnp.where(c == 0, row0_first, 0)
    n_valid = (rows - row0).astype(F32)
    cosf = cos_ref[...]
    sinf = sin_ref[...]
    ri = _iota((rows, rows), 0)
    ci = _iota((rows, rows), 1)
    diff = (ri - ci).astype(F32)
    rowi = _iota((rows, 1), 0)
    te = (rowi - row0).astype(F32)
    valid = rowi >= row0
    for h in range(R_HEADS):
        lg = _ret_log_gamma(h)
        lo = h * HEAD_DIM
        decay_m = jnp.where(diff >= 0, jnp.exp(diff * lg), 0.0)
        w_inter = jnp.exp((te + 1.0) * lg)
        w_src = jnp.exp((n_valid - 1.0 - te) * lg)
        s_decay = jnp.exp(n_valid * lg)
        for s in range(SEQ_GROUP):
            q = z_ref[s, :, lo:lo + HEAD_DIM]
            k = z_ref[s, :, 512 + lo:512 + lo + HEAD_DIM]
            v = jnp.where(valid, z_ref[s, :, 1024 + lo:1024 + lo + HEAD_DIM], 0.0)
            rg = z_ref[s, :, 1536 + lo:1536 + lo + HEAD_DIM]
            q = q * cosf + pltpu.roll(q, HEAD_DIM // 2, 1) * sinf
            k = (k * cosf + pltpu.roll(k, HEAD_DIM // 2, 1) * sinf) * (HEAD_DIM ** -0.5)
            s_state = s_ref[s, h]
            qb = _bf(q)
            vb = _bf(v)
            intra = _dot(_bf(_dot_nt(qb, _bf(k)) * decay_m), vb)
            inter = _dot(qb, _bf(s_state)) * w_inter
            s_ref[s, h] = s_decay * s_state + _dot_tn(_bf(k * w_src), vb)
            hr = intra + inter
            mu = jnp.mean(hr, axis=-1, keepdims=True)
            var = jnp.mean(jnp.square(hr - mu), axis=-1, keepdims=True)
            hr = (hr - mu) * lax.rsqrt(var + EPS) * lng_ref[h:h + 1, :]
            o_ref[s, :, lo:lo + HEAD_DIM] = rg * _sigmoid(rg) * hr


def _lru_body(z_ref, pre_ref, cw_ref, cb_ref, wa_ref, ba_ref, wx_ref, bx_ref, lam_ref,
              o_ref, hlast_ref, tail_ref, cbuf_s, a_s, b_s, *, rows, row0_first):
    c = pl.program_id(1)
    row0 = jnp.where(c == 0, row0_first, 0)
    rowi = _iota((rows, 1), 0)
    valid = rowi >= row0
    sp_lam = _softplus(-lam_ref[...])
    for s in range(SEQ_GROUP):
        lx = jnp.where(valid, z_ref[s, :, 0:LRU_WIDTH], pre_ref[min(s, pre_ref.shape[0] - 1)])
        cbuf_s[s, 0:8, :] = tail_ref[s]
        cbuf_s[s, 8:8 + rows, :] = lx
        xc = cb_ref[...] + cw_ref[CONV_W - 1:CONV_W, :] * lx
        for j in range(1, CONV_W):
            xc = xc + cw_ref[CONV_W - 1 - j:CONV_W - j, :] * cbuf_s[s, 8 - j:8 - j + rows, :]
        tail_ref[s] = lx[rows - 8:rows]
        xcb = _bf(xc)
        pre_a = []
        pre_x = []
        for p in range(LRU_WIDTH // LANES):
            xs = xcb[:, p * LANES:(p + 1) * LANES]
            pre_a.append(_dot(xs, wa_ref[p]))
            pre_x.append(_dot(xs, wx_ref[p]))
        r = _sigmoid(jnp.concatenate(pre_a, axis=1) + ba_ref[...])
        ig = _sigmoid(jnp.concatenate(pre_x, axis=1) + bx_ref[...])
        log_a = -LRU_C * r * sp_lam
        a = jnp.where(valid, jnp.exp(log_a), 1.0)
        one_minus_a2 = -jnp.tanh(log_a) * (jnp.exp(2.0 * log_a) + 1.0)
        bx = jnp.where(valid, jnp.sqrt(one_minus_a2) * (ig * xc), 0.0)
        a_s[s] = a
        b_s[s] = bx
    carries = [hlast_ref[s] for s in range(SEQ_GROUP)]
    for t in range(rows):
        for s in range(SEQ_GROUP):
            carries[s] = a_s[s, t:t + 1, :] * carries[s] + b_s[s, t:t + 1, :]
            b_s[s, t:t + 1, :] = carries[s]
    for s in range(SEQ_GROUP):
        hlast_ref[s] = carries[s]
        gate = z_ref[s, :, LRU_WIDTH:2 * LRU_WIDTH]
        gl = 0.5 * gate * (1.0 + jnp.tanh(math.sqrt(2.0 / math.pi) * (gate + 0.044715 * gate * gate * gate)))
        o_ref[s] = b_s[s] * gl


def _ret_lru_kernel(zr_ref, cos_ref, sin_ref, lng_ref, s0_ref, zl_ref, pre_ref, h0_ref,
                    cw_ref, cb_ref, wa_ref, ba_ref, wx_ref, bx_ref, lam_ref,
                    or_ref, s_ref, ol_ref, hlast_ref, tail_ref, cbuf_s, a_s, b_s, *, rows, row0_first):
    @pl.when(pl.program_id(1) == 0)
    def _():
        s_ref[...] = s0_ref[...]
        hlast_ref[...] = h0_ref[...]
        tail_ref[...] = jnp.zeros(tail_ref.shape, F32)

    _ret_body(zr_ref, cos_ref, sin_ref, lng_ref, or_ref, s_ref, rows=rows, row0_first=row0_first)
    _lru_body(zl_ref, pre_ref, cw_ref, cb_ref, wa_ref, ba_ref, wx_ref, bx_ref, lam_ref,
              ol_ref, hlast_ref, tail_ref, cbuf_s, a_s, b_s, rows=rows, row0_first=row0_first)


def ret_lru_call(zr, cosf, sinf, lng, s0, zl, pre, h0, cw, cb, wa2, ba, wx2, bx, lam, *, rows, row0_first):
    nb, t_len, _ = zr.shape
    assert nb % SEQ_GROUP == 0 and t_len % rows == 0
    blk = lambda b, c: (b, c, 0)
    per_seq = lambda b, c: (b, 0, 0)
    const2 = lambda b, c: (0, 0)
    const3 = lambda b, c: (0, 0, 0)
    sspec = pl.BlockSpec((SEQ_GROUP, R_HEADS, HEAD_DIM, HEAD_DIM), lambda b, c: (b, 0, 0, 0))
    shared_pre = pre.shape[0] == 1
    in_specs = [pl.BlockSpec((SEQ_GROUP, rows, 2048), blk),
                pl.BlockSpec((rows, HEAD_DIM), lambda b, c: (c, 0)),
                pl.BlockSpec((rows, HEAD_DIM), lambda b, c: (c, 0)),
                pl.BlockSpec((R_HEADS, HEAD_DIM), const2),
                sspec,
                pl.BlockSpec((SEQ_GROUP, rows, 1024), blk),
                pl.BlockSpec((1 if shared_pre else SEQ_GROUP, rows, LRU_WIDTH), const3 if shared_pre else per_seq),
                pl.BlockSpec((SEQ_GROUP, 1, LRU_WIDTH), per_seq),
                pl.BlockSpec((CONV_W, LRU_WIDTH), const2),
                pl.BlockSpec((1, LRU_WIDTH), const2),
                pl.BlockSpec((LRU_WIDTH // LANES, LANES, LANES), const3),
                pl.BlockSpec((1, LRU_WIDTH), const2),
                pl.BlockSpec((LRU_WIDTH // LANES, LANES, LANES), const3),
                pl.BlockSpec((1, LRU_WIDTH), const2),
                pl.BlockSpec((1, LRU_WIDTH), const2)]
    return pl.pallas_call(
        functools.partial(_ret_lru_kernel, rows=rows, row0_first=row0_first),
        grid=(nb // SEQ_GROUP, t_len // rows),
        in_specs=in_specs,
        out_specs=[pl.BlockSpec((SEQ_GROUP, rows, 512), blk), sspec,
                   pl.BlockSpec((SEQ_GROUP, rows, LRU_WIDTH), blk),
                   pl.BlockSpec((SEQ_GROUP, 1, LRU_WIDTH), per_seq),
                   pl.BlockSpec((SEQ_GROUP, 8, LRU_WIDTH), per_seq)],
        out_shape=[jax.ShapeDtypeStruct((nb, t_len, 512), F32),
                   jax.ShapeDtypeStruct((nb, R_HEADS, HEAD_DIM, HEAD_DIM), F32),
                   jax.ShapeDtypeStruct((nb, t_len, LRU_WIDTH), F32),
                   jax.ShapeDtypeStruct((nb, 1, LRU_WIDTH), F32),
                   jax.ShapeDtypeStruct((nb, 8, LRU_WIDTH), F32)],
        scratch_shapes=[pltpu.VMEM((SEQ_GROUP, rows + 8, LRU_WIDTH), F32),
                        pltpu.VMEM((SEQ_GROUP, rows, LRU_WIDTH), F32),
                        pltpu.VMEM((SEQ_GROUP, rows, LRU_WIDTH), F32)],
        compiler_params=pltpu.CompilerParams(dimension_semantics=("parallel", "arbitrary")),
        name="ret_lru",
    )(zr, cosf, sinf, lng, s0, zl, pre, h0, cw, cb, wa2, ba, wx2, bx, lam)


def _fox_cumsum_kernel(g_ref, gt_ref, fx_ref, fr_ref):
    ri = _iota((CHUNK, CHUNK), 0)
    ci = _iota((CHUNK, CHUNK), 1)
    lower = jnp.where(ci <= ri, 1.0, 0.0).astype(BF16)
    upper = jnp.where(ri <= ci, 1.0, 0.0).astype(BF16)
    pr = _iota((3 * LANES, F_HEADS // 2 * LANES), 0)
    pc = _iota((3 * LANES, F_HEADS // 2 * LANES), 1)
    term, gate = pr // LANES, pr % LANES
    pair, lane = pc // LANES, pc % LANES
    first_head = 2 * M_HEADS
    place = (((gate == first_head + 2 * pair) & (lane == F_DH + term))
             | ((gate == first_head + 2 * pair + 1) & (lane == term)))
    place = jnp.where(place, 1.0, 0.0).astype(BF16)
    carry_r = jnp.zeros((1, LANES), F32)
    carry_c = jnp.zeros((16, 1), F32)
    for blk in range(NCH):
        x = g_ref[blk * CHUNK:(blk + 1) * CHUNK, :]
        xt = gt_ref[:, blk * CHUNK:(blk + 1) * CHUNK]
        if blk == 0:
            x = jnp.where(_iota((CHUNK, 1), 0) >= PADF, x, 0.0)
            xt = jnp.where(_iota((1, CHUNK), 1) >= PADF, xt, 0.0)
        cs = _dot01_left(lower, x) + carry_r
        cst = _dot01_right(xt, upper) + carry_c
        carry_r = cs[CHUNK - 1:CHUNK, :]
        carry_c = cst[:, CHUNK - 1:CHUNK]
        fk = cs * LOG2E
        if blk == 0:
            cst = jnp.where(_iota((1, CHUNK), 1) >= PADF, cst, -NEG)
            fk = jnp.where(_iota((CHUNK, 1), 0) >= PADF, fk, -NEG)
        fx_ref[blk * CHUNK:(blk + 1) * CHUNK, :] = _dot(jnp.concatenate(_split3(fk), axis=1), place)
        fr_ref[0, :, blk * CHUNK:(blk + 1) * CHUNK] = cst


def fox_cumsum(g2, g2t):
    return pl.pallas_call(
        _fox_cumsum_kernel,
        grid=(BATCH,),
        in_specs=[pl.BlockSpec((TP, LANES), lambda b: (b, 0)),
                  pl.BlockSpec((16, TP), lambda b: (0, b))],
        out_specs=[pl.BlockSpec((TP, F_HEADS // 2 * LANES), lambda b: (b, 0)),
                   pl.BlockSpec((1, 16, TP), lambda b: (b, 0, 0))],
        out_shape=[jax.ShapeDtypeStruct((SBASE, F_HEADS // 2 * LANES), F32),
                   jax.ShapeDtypeStruct((BATCH, 16, TP), F32)],
        compiler_params=pltpu.CompilerParams(dimension_semantics=("parallel",)),
        name="fox_cumsum",
    )(g2, g2t)


def _fox_prompt_kernel(q_ref, k_ref, v_ref, fr_ref, fx_ref, o_ref,
                       qta_s, qtb_s, ka_s, kb_s, vt_s, m_s, l_s, acc_s, ua_s):
    i = pl.program_id(2)
    lane = _iota((1, LANES), 1)
    is_a = lane < F_DH
    row = _iota((LANES, 1), 0)
    row_a = row < F_DH
    k_heads = (ka_s, kb_s)
    qt_heads = (qta_s, qtb_s)

    @pl.when(i == 0)
    def _():
        k = k_ref[...]
        fx = fx_ref[...]
        ka_s[...] = _bf(jnp.where(is_a, k, fx))
        kb_s[...] = _bf(jnp.where(is_a, fx, k))

        minus_a = jnp.where((row >= F_DH) & (row < F_DH + 3), -1.0, 0.0)
        minus_b = jnp.where(row < 3, -1.0, 0.0)

        def fill(blk, carry):
            rows = pl.ds(pl.multiple_of(blk * CHUNK, CHUNK), CHUNK)
            qt = (q_ref[rows, :] * (F_DH ** -0.5 * LOG2E)).T
            qta_s[blk] = _bf(jnp.where(row_a, qt, minus_a))
            qtb_s[blk] = _bf(jnp.where(row_a, minus_b, qt))
            vt_s[blk] = _bf(v_ref[rows, :].T)
            return carry

        lax.fori_loop(0, NCH, fill, 0)

    def init():
        m_s[...] = jnp.full(m_s.shape, NEG, F32)
        l_s[...] = jnp.zeros(l_s.shape, F32)
        acc_s[...] = jnp.zeros(acc_s.shape, F32)

    def scores(hh, qt, kblk, nkb):
        ck = nkb * CHUNK
        k0 = kblk * CHUNK if isinstance(kblk, int) else pl.multiple_of(kblk * CHUNK, CHUNK)
        return _dot(k_heads[hh][pl.ds(k0, ck), :], qt[hh])

    def step(hh, u, qblk, nqb, kblks, diagonal):
        tq = nqb * CHUNK
        ck = len(kblks) * CHUNK
        fq = jnp.concatenate([fr_ref[0, 0, qblk + t, hh:hh + 1, :] for t in range(nqb)], axis=1)
        fq = jnp.where(fq > 0.5 * -NEG, 0.0, fq * LOG2E)
        if diagonal:
            lead = ck - tq
            ri = _iota((ck, tq), 0)
            u = jnp.where((ri < lead) | (ri - lead <= _iota((ck, tq), 1)), u, NEG)
        m_old = m_s[hh, :, 0:tq]
        m_new = jnp.maximum(m_old, jnp.max(u, axis=0, keepdims=True) + fq)
        m_s[hh, :, 0:tq] = m_new
        alpha = jnp.exp2(m_old - m_new)
        p = _bf(jnp.exp2(u + (fq - m_new)))
        feat = slice(hh * F_DH, (hh + 1) * F_DH)
        vt = jnp.concatenate([vt_s[kb, feat, :] for kb in kblks], axis=1)
        pv = _dot(jnp.concatenate([vt, jnp.ones((16, ck), BF16)], axis=0), p)
        l_s[hh, :, 0:tq] = alpha * l_s[hh, :, 0:tq] + pv[F_DH:F_DH + 1, :]
        acc_s[feat, 0:tq] = alpha * acc_s[feat, 0:tq] + pv[0:F_DH, :]

    def finalize(qblk, nqb):
        tq = nqb * CHUNK
        o_t = acc_s[:, 0:tq] / jnp.where(row_a, l_s[0, :, 0:tq], l_s[1, :, 0:tq])
        for t in range(nqb):
            rows = pl.ds(pl.multiple_of((qblk + t) * CHUNK, CHUNK), CHUNK)
            o_ref[rows, :] = o_t[:, t * CHUNK:(t + 1) * CHUNK].T

    @pl.when(i == 0)
    def _():
        init()
        qt0 = (qta_s[0], qtb_s[0])
        for hh in range(2):
            step(hh, scores(hh, qt0, 0, 1), 0, 1, [0], True)
        finalize(0, 1)

    nck = FOX_CK // CHUNK
    nqb = FOX_TQ // CHUNK
    qblk = 1 + i * nqb
    qt = tuple(jnp.concatenate([qt_h[qblk + t] for t in range(nqb)], axis=1) for qt_h in qt_heads)
    init()

    ua_s[...] = scores(0, qt, 1, nck)

    def body(j, carry):
        kblk = 1 + j * nck
        kblks = [kblk + t for t in range(nck)]
        ub = scores(1, qt, kblk, nck)
        step(0, ua_s[...], qblk, nqb, kblks, False)
        ua_s[...] = scores(0, qt, kblk + nck, nck)
        step(1, ub, qblk, nqb, kblks, False)
        return carry

    lax.fori_loop(0, i, body, 0)
    last = [0] + [qblk + t for t in range(nck)]
    ua = jnp.concatenate([scores(0, qt, 0, 1), ua_s[...]], axis=0)
    ub = jnp.concatenate([scores(1, qt, 0, 1), scores(1, qt, qblk, nck)], axis=0)
    step(0, ua, qblk, nqb, last, True)
    step(1, ub, qblk, nqb, last, True)
    finalize(qblk, nqb)


def fox_prompt(zq, zk, zv, fr, fx):
    nq = (TP - CHUNK) // FOX_TQ
    pairs = F_HEADS // 2
    return pl.pallas_call(
        _fox_prompt_kernel,
        grid=(BATCH, pairs, nq),
        in_specs=[pl.BlockSpec((TP, LANES), lambda b, p, i: (b, p)),
                  pl.BlockSpec((TP, LANES), lambda b, p, i: (b, p)),
                  pl.BlockSpec((TP, LANES), lambda b, p, i: (b, p)),
                  pl.BlockSpec((1, 1, NCH, 2, LANES), lambda b, p, i: (b, p, 0, 0, 0)),
                  pl.BlockSpec((TP, LANES), lambda b, p, i: (b, p))],
        out_specs=pl.BlockSpec((TP, LANES), lambda b, p, i: (b, p)),
        out_shape=jax.ShapeDtypeStruct((SBASE, 512), F32),
        scratch_shapes=[pltpu.VMEM((NCH, LANES, CHUNK), BF16),
                        pltpu.VMEM((NCH, LANES, CHUNK), BF16),
                        pltpu.VMEM((TP, LANES), BF16),
                        pltpu.VMEM((TP, LANES), BF16),
                        pltpu.VMEM((NCH, LANES, CHUNK), BF16),
                        pltpu.VMEM((2, 1, FOX_TQ), F32),
                        pltpu.VMEM((2, 1, FOX_TQ), F32),
                        pltpu.VMEM((LANES, FOX_TQ), F32),
                        pltpu.VMEM((FOX_CK, FOX_TQ), F32)],
        compiler_params=pltpu.CompilerParams(dimension_semantics=("parallel", "parallel", "arbitrary")),
        name="fox_prompt",
    )(zq, zk, zv, fr, fx)


def _lf_suffix_kernel(x_ref, o_ref):
    ri = _iota((PAGE_SIZE, 2 * PAGE_SIZE), 0)
    ci = _iota((PAGE_SIZE, 2 * PAGE_SIZE), 1)
    after_or_all = jnp.where((ri > ci) | (ci >= PAGE_SIZE), 1.0, 0.0).astype(BF16)
    x = x_ref[...].reshape(PRE_PG * F_HEADS, PAGE_SIZE)
    o_ref[...] = _dot01_right(x, after_or_all).reshape(PRE_PG, F_HEADS, 2 * PAGE_SIZE)


def lf_suffix(lft):
    n_phys = lft.shape[0]
    return pl.pallas_call(
        _lf_suffix_kernel,
        grid=(n_phys // PRE_PG,),
        in_specs=[pl.BlockSpec((PRE_PG, F_HEADS, PAGE_SIZE), lambda i: (i, 0, 0))],
        out_specs=pl.BlockSpec((PRE_PG, F_HEADS, 2 * PAGE_SIZE), lambda i: (i, 0, 0)),
        out_shape=jax.ShapeDtypeStruct((n_phys, F_HEADS, 2 * PAGE_SIZE), F32),
        compiler_params=pltpu.CompilerParams(dimension_semantics=("parallel",)),
        name="lf_suffix",
    )(lft)


def _fox_decode_kernel(pt_ref, zq_ref, zk_ref, zv_ref, gt_ref, *rest):
    k_refs = rest[0:DEC_PG]
    v_refs = rest[DEC_PG:2 * DEC_PG]
    lf_refs = rest[2 * DEC_PG:3 * DEC_PG]
    o_ref = rest[3 * DEC_PG]
    q2_s, m_s, l_s, acc_s, run_s, nc_s = rest[3 * DEC_PG + 1:]
    j = pl.program_id(1)
    n_rows = DEC_SEQ * F_HEADS
    width = F_HEADS * F_DH

    def update(u, v, v_feature_major):
        m_old = m_s[...]
        m_new = jnp.maximum(m_old, jnp.max(u, axis=1, keepdims=True))
        alpha = jnp.exp(m_old - m_new)
        p = jnp.exp(u - m_new[:, 0:u.shape[1]])
        pv = _dot_nt(_bf(p), v) if v_feature_major else _dot(_bf(p), v)
        l_s[...] = alpha * l_s[...] + jnp.sum(p, axis=1, keepdims=True)
        acc_s[...] = jnp.concatenate([alpha] * (width // LANES), axis=1) * acc_s[...] + pv
        m_s[...] = m_new

    @pl.when(j == 0)
    def _():
        m_s[...] = jnp.full(m_s.shape, NEG, F32)
        l_s[...] = jnp.zeros(l_s.shape, F32)
        acc_s[...] = jnp.zeros(acc_s.shape, F32)
        run_s[...] = jnp.zeros(run_s.shape, F32)
        head_mask = _iota((F_HEADS, width), 1) // F_DH == _iota((F_HEADS, width), 0)
        q16 = zq_ref[...] * (F_DH ** -0.5)
        q2 = _bf(jnp.concatenate(
            [jnp.where(head_mask, q16[SPAD + t:SPAD + t + 1, :], 0.0) for t in range(DEC_SEQ)], axis=0))
        q2_s[...] = q2
        lf_new = gt_ref[F_HEADS:2 * F_HEADS, :]
        lane = _iota((1, SROWS), 1)
        cum = jnp.zeros((F_HEADS, SROWS), F32)
        for t in range(DEC_SEQ):
            cum = cum + jnp.where(lane >= SPAD + t, lf_new[:, SPAD + t:SPAD + t + 1], 0.0)
        nc = jnp.concatenate([cum[:, SPAD + t:SPAD + t + 1] for t in range(DEC_SEQ)], axis=0)
        nc_s[...] = jnp.broadcast_to(nc, nc_s.shape)
        cum4 = jnp.concatenate([cum] * DEC_SEQ, axis=0)
        qi = _iota((n_rows, SROWS), 0) // F_HEADS
        kj = _iota((n_rows, SROWS), 1) - SPAD
        u = jnp.where((kj >= 0) & (kj <= qi), _dot_nt(q2, _bf(zk_ref[...])) + (nc - cum4), NEG)
        update(u, _bf(zv_ref[...]), False)

    q2 = q2_s[...]
    nc = nc_s[...]
    run = run_s[...]
    us = []
    for i in range(DEC_PG):
        later = run + lf_refs[i][0, :, 0:PAGE_SIZE]
        bias = jnp.concatenate([later] * DEC_SEQ, axis=0) + nc
        us.append(_dot(q2, _bf(k_refs[i][0].reshape(width, PAGE_SIZE))) + bias)
        run = run + lf_refs[i][0, :, PAGE_SIZE:2 * PAGE_SIZE]
    run_s[...] = run
    u = jnp.concatenate(us, axis=1)
    m_old = m_s[...]
    m_new = jnp.maximum(m_old, jnp.max(u, axis=1, keepdims=True))
    alpha = jnp.exp(m_old - m_new)
    p = jnp.exp(u - jnp.concatenate([m_new] * DEC_PG, axis=1))
    l_s[...] = alpha * l_s[...] + jnp.sum(p, axis=1, keepdims=True)
    pv = _dot_nt(_bf(p[:, 0:PAGE_SIZE]), _bf(v_refs[0][0].reshape(width, PAGE_SIZE)))
    for i in range(1, DEC_PG):
        pv = pv + _dot_nt(_bf(p[:, i * PAGE_SIZE:(i + 1) * PAGE_SIZE]), _bf(v_refs[i][0].reshape(width, PAGE_SIZE)))
    acc_s[...] = jnp.concatenate([alpha] * (width // LANES), axis=1) * acc_s[...] + pv
    m_s[...] = m_new

    @pl.when(j == pl.num_programs(1) - 1)
    def _():
        head_mask = (_iota((n_rows, width), 1) // F_DH) == (_iota((n_rows, width), 0) % F_HEADS)
        o2 = jnp.where(head_mask, acc_s[...] / jnp.concatenate([l_s[...]] * (width // LANES), axis=1), 0.0)
        o_ref[...] = jnp.zeros(o_ref.shape, F32)
        for t in range(DEC_SEQ):
            o_ref[SPAD + t:SPAD + t + 1, :] = jnp.sum(o2[t * F_HEADS:(t + 1) * F_HEADS, :], axis=0, keepdims=True)


def fox_decode(page_table, zq, zk, zv, g2t_s, cache_kt, cache_vt, lf_sums):
    n_pages = page_table.shape[1]
    width = F_HEADS * F_DH

    def page_map(i):
        return lambda b, j, pt: (pt[b, n_pages - 1 - (j * DEC_PG + i)], 0, 0, 0)

    def page_map3(i):
        return lambda b, j, pt: (pt[b, n_pages - 1 - (j * DEC_PG + i)], 0, 0)

    in_specs = [pl.BlockSpec((SROWS, width), lambda b, j, pt: (b, 0)),
                pl.BlockSpec((SROWS, width), lambda b, j, pt: (b, 0)),
                pl.BlockSpec((SROWS, width), lambda b, j, pt: (b, 0)),
                pl.BlockSpec((None, 16, SROWS), lambda b, j, pt: (b, 0, 0))]
    in_specs += [pl.BlockSpec((1, F_HEADS, F_DH, PAGE_SIZE), page_map(i)) for i in range(DEC_PG)]
    in_specs += [pl.BlockSpec((1, F_HEADS, F_DH, PAGE_SIZE), page_map(i)) for i in range(DEC_PG)]
    in_specs += [pl.BlockSpec((1, F_HEADS, 2 * PAGE_SIZE), page_map3(i)) for i in range(DEC_PG)]
    args = [zq, zk, zv, g2t_s] + [cache_kt] * DEC_PG + [cache_vt] * DEC_PG + [lf_sums] * DEC_PG
    n_rows = DEC_SEQ * F_HEADS
    grid_spec = pltpu.PrefetchScalarGridSpec(
        num_scalar_prefetch=1,
        grid=(DEC_BATCH, n_pages // DEC_PG),
        in_specs=in_specs,
        out_specs=pl.BlockSpec((SROWS, width), lambda b, j, pt: (b, 0)),
        scratch_shapes=[pltpu.VMEM((n_rows, width), BF16),
                        pltpu.VMEM((n_rows, LANES), F32),
                        pltpu.VMEM((n_rows, LANES), F32),
                        pltpu.VMEM((n_rows, width), F32),
                        pltpu.VMEM((F_HEADS, PAGE_SIZE), F32),
                        pltpu.VMEM((n_rows, LANES), F32)])
    return pl.pallas_call(
        _fox_decode_kernel,
        grid_spec=grid_spec,
        out_shape=jax.ShapeDtypeStruct((DEC_BATCH * SROWS, width), F32),
        compiler_params=pltpu.CompilerParams(dimension_semantics=("parallel", "arbitrary")),
        name="fox_decode",
    )(page_table, *args)


def _block_diag_pairs(w):
    z = jnp.zeros((LANES // 2, LANES // 2), w.dtype)
    pairs = [jnp.block([[w[2 * p], z], [z, w[2 * p + 1]]]) for p in range(w.shape[0] // 2)]
    return _bf(jnp.stack(pairs))


def _rope_tables(pos):
    half = HEAD_DIM // 2
    freq = ROPE_BASE ** (-jnp.arange(half, dtype=F32) / half)
    ang = pos.astype(F32)[:, None] * freq[None, :]
    cos, sin = jnp.cos(ang), jnp.sin(ang)
    return jnp.concatenate([cos, cos], axis=1), jnp.concatenate([-sin, sin], axis=1)


def _unpad_prompt(x):
    return x.reshape((BATCH, TP) + x.shape[1:])[:, PADF:]


def _unpad_sample(x):
    return x.reshape((DEC_BATCH, SROWS) + x.shape[1:])[:, SPAD:]


def kernel(x_prompt, x_sample, cache_fox_k, cache_fox_v, cache_fox_logf, state_mlstm_C, state_mlstm_n, state_mlstm_m, state_ret_S, state_lru_h, state_lru_conv, page_table, meta_tokens, w_in_even, b_mlstm_i, b_mlstm_f, b_fox_f, w_out_even, w_in_odd, ret_ln_g, conv_w, conv_b, lru_wa, lru_ba, lru_wx, lru_bx, lru_lambda, w_out_odd, norm_g, ffn_wg, ffn_wu, ffn_wd):
    n_pages = page_table.shape[1]
    past = n_pages * PAGE_SIZE
    n_phys = cache_fox_k.shape[0]

    head = jnp.concatenate([jnp.zeros((PADF, D_MODEL), F32), meta_tokens.astype(F32)], axis=0)
    hs = jnp.concatenate([jnp.zeros((DEC_BATCH, SPAD, D_MODEL), F32), x_sample.astype(F32)],
                         axis=1).reshape(DEC_BATCH * SROWS, D_MODEL)
    h_s = hs
    h_p = jnp.concatenate([jnp.broadcast_to(head[None], (BATCH, CHUNK, D_MODEL)), x_prompt.astype(F32)],
                          axis=1).reshape(SBASE, D_MODEL)

    c = np.cumsum([0, 512, 512, 512, 512, 4, 4, 512, 512, 512, 8])
    gate_w = jnp.concatenate([w_in_even[:, c[4]:c[6]], w_in_even[:, c[9]:c[10]],
                              jnp.zeros((D_MODEL, LANES - 16), F32)], axis=1)
    w_even = _bf(jnp.concatenate([w_in_even[:, c[0]:c[4]], w_in_even[:, c[6]:c[9]], gate_w], axis=1))
    gate_b = jnp.concatenate([b_mlstm_i, b_mlstm_f, b_fox_f]).astype(F32)
    brow = jnp.concatenate([gate_b, jnp.zeros((LANES - 16,), F32)]).reshape(1, LANES)
    w_odd = _bf(w_in_odd)
    wo_even = _bf(w_out_even)
    wo_odd = _bf(w_out_odd)
    wg = _bf(ffn_wg)
    wu = _bf(ffn_wu)
    wd = _bf(ffn_wd)
    ng = norm_g.astype(F32).reshape(2, 4, 1, D_MODEL)

    even_cols = (2048, 512, 512, 512, LANES)
    zm_p, zq_p, zk_p, zv_p, g2_p, g2t_p = norm_matmul(h_p, ng[0, 0], w_even, even_cols, gate_bias=brow)
    zm_s, zq_s, zk_s, zv_s, g2_s, g2t_s = norm_matmul(h_s, ng[0, 0], w_even, even_cols, gate_bias=brow)
    prompt3 = lambda x: x.reshape(BATCH, TP, x.shape[-1])
    sample3 = lambda x: x.reshape(DEC_BATCH, SROWS, x.shape[-1])
    flat2 = lambda x: x.reshape(-1, x.shape[-1])

    zeros_c = jnp.zeros((BATCH, M_HEADS, HEAD_DIM, HEAD_DIM), F32)
    zeros_n = jnp.zeros((BATCH, M_HEADS, HEAD_DIM), F32)
    hm_p, p_ct, p_n, p_m = mlstm_call(prompt3(zm_p), g2t_p.reshape(16, BATCH, NCH, CHUNK).transpose(1, 2, 0, 3),
                                      zeros_c, zeros_n, zeros_n, row0_first=PADF)
    p_c = p_ct.transpose(0, 1, 3, 2)
    m0_s = jnp.broadcast_to(state_mlstm_m.astype(F32)[:, :, None], (DEC_BATCH, M_HEADS, LANES))
    hm_s, s_ct, s_n, s_m = mlstm_call(sample3(zm_s), g2t_s.reshape(16, DEC_BATCH, 1, SROWS).transpose(1, 2, 0, 3),
                                      state_mlstm_C.astype(F32).transpose(0, 1, 3, 2), state_mlstm_n.astype(F32),
                                      m0_s, row0_first=SPAD)
    s_c = s_ct.transpose(0, 1, 3, 2)

    fx, fr = fox_cumsum(g2_p, g2t_p)
    pairs = F_HEADS // 2
    fr_p = fr[:, 8:16].reshape(BATCH, pairs, 2, NCH, CHUNK).transpose(0, 1, 3, 2, 4)
    hf_p = fox_prompt(zq_p, zk_p, zv_p, fr_p, fx)
    lf_sums = lf_suffix(cache_fox_logf.astype(F32).transpose(0, 2, 1))
    hf_s = fox_decode(page_table, zq_s, zk_s, zv_s, g2t_s.reshape(16, DEC_BATCH, SROWS).transpose(1, 0, 2),
                      cache_fox_k.astype(F32).transpose(0, 2, 3, 1), cache_fox_v.astype(F32).transpose(0, 2, 3, 1),
                      lf_sums)

    gains = norm_g.astype(F32)
    even_w = (wo_even[:512], wo_even[512:], gains[0, 1:4], wg[0], wu[0], wd[0])
    h_p = mix_ffn(flat2(hm_p), hf_p, h_p, *even_w)
    h_s = mix_ffn(flat2(hm_s), hf_s, h_s, *even_w)

    zr_p, zl_p = norm_matmul(h_p, ng[1, 0], w_odd, (2048, 1024))
    zr_s, zl_s = norm_matmul(h_s, ng[1, 0], w_odd, (2048, 1024))
    cos_p, sin_p = _rope_tables(jnp.arange(TP) - PADF)
    cos_s, sin_s = _rope_tables(past + jnp.arange(SROWS) - SPAD)
    lng = ret_ln_g.astype(F32)
    lru_args = (conv_w.astype(F32), conv_b.astype(F32).reshape(1, LRU_WIDTH),
                _block_diag_pairs(lru_wa), lru_ba.astype(F32).reshape(1, LRU_WIDTH),
                _block_diag_pairs(lru_wx), lru_bx.astype(F32).reshape(1, LRU_WIDTH),
                lru_lambda.astype(F32).reshape(1, LRU_WIDTH))
    hr_p, p_s, hl_p, p_h, p_tail = ret_lru_call(
        prompt3(zr_p), cos_p, sin_p, lng, jnp.zeros((BATCH, R_HEADS, HEAD_DIM, HEAD_DIM), F32),
        prompt3(zl_p), jnp.zeros((1, CHUNK, LRU_WIDTH), F32), jnp.zeros((BATCH, 1, LRU_WIDTH), F32), *lru_args,
        rows=CHUNK, row0_first=PADF)
    pre_s = jnp.concatenate([jnp.zeros((DEC_BATCH, SPAD - (CONV_W - 1), LRU_WIDTH), F32),
                             state_lru_conv.astype(F32),
                             jnp.zeros((DEC_BATCH, DEC_SEQ, LRU_WIDTH), F32)], axis=1)
    hr_s, s_s, hl_s, s_h, s_tail = ret_lru_call(
        sample3(zr_s), cos_s, sin_s, lng, state_ret_S.astype(F32),
        sample3(zl_s), pre_s, state_lru_h.astype(F32).reshape(DEC_BATCH, 1, LRU_WIDTH), *lru_args,
        rows=SROWS, row0_first=SPAD)

    odd_w = (wo_odd[:512], wo_odd[512:], gains[1, 1:4], wg[1], wu[1], wd[1])
    y_prompt = mix_ffn(hr_p, hl_p, prompt3(h_p), *odd_w, main_only=True)
    h_s = mix_ffn(flat2(hr_s), flat2(hl_s), h_s, *odd_w)

    y_sample = h_s.reshape(DEC_BATCH, SROWS, D_MODEL)[:, SPAD:]
    heads = lambda x: x.reshape(x.shape[:2] + (F_HEADS, F_DH))
    nback = CONV_W - 1
    return (y_prompt, y_sample,
            heads(_unpad_prompt(zk_p)), heads(_unpad_prompt(zv_p)), _unpad_prompt(g2_p[:, 8:16]),
            heads(_unpad_sample(zk_s)), heads(_unpad_sample(zv_s)), _unpad_sample(g2_s[:, 8:16]),
            p_c, p_n, p_m[:, :, 0], s_c, s_n, s_m[:, :, 0],
            p_s, s_s,
            p_h[:, 0], s_h[:, 0], p_tail[:, 8 - nback:], s_tail[:, 8 - nback:])
```
